```python
import math
import jax, jax.numpy as jnp
from jax import lax
import numpy as np

D_MODEL = 1024
BATCH = 32
SEQ = 256
DEPTH = 4
DEC_BATCH = 8
DEC_SEQ = 1024
PAST_LEN = 256

GRID_W = 64
N_EVEN = (DEPTH + 1) // 2
N_ODD = DEPTH // 2
EPS = 1e-6
N_MOD = 9
GLA_HEADS = 4
GLA_DK = 64
GLA_DV = 128
GLA_QK = GLA_HEADS * GLA_DK
GLA_VW = GLA_HEADS * GLA_DV
GLA_GATE_RANK = 16
GLA_GATE_NORM = 16.0
GLA_CHUNK = 64
MLA_HEADS = 8
MLA_NOPE = 64
MLA_ROPE = 32
MLA_V = 64
MLA_QK_DIM = MLA_NOPE + MLA_ROPE
MLA_Q_RANK = 384
MLA_KV_RANK = 256
ROPE_BASE = 10000.0
ROPE_AXIS_PAIRS = MLA_ROPE // 4
ATTN_BLOCK = 128
EVEN_SPLIT = (GLA_QK, GLA_QK, GLA_VW, GLA_VW, GLA_GATE_RANK, GLA_GATE_RANK, MLA_Q_RANK, MLA_KV_RANK, MLA_ROPE)
EVEN_IN_WIDTH = 2 * GLA_QK + 2 * GLA_VW + 2 * GLA_GATE_RANK + MLA_Q_RANK + MLA_KV_RANK + MLA_ROPE
EVEN_OUT_WIDTH = GLA_VW + MLA_HEADS * MLA_V
CMLP_CHUNK = 128
CMLP_GROUPS = 4
CMLP_WIDTH = 2 * D_MODEL
FFN_HIDDEN = 2816

kernel_name = 'hybrid_gla_mla_gmlp_diffusion_step'


def rms_norm(x, g):
    xf = x.astype(jnp.float32)
    y = xf * lax.rsqrt(jnp.mean(xf * xf, axis=-1, keepdims=True) + EPS)
    return (y * g.astype(jnp.float32)).astype(x.dtype)


def split_cols(h, sizes):
    out, o = [], 0
    for s in sizes:
        out.append(h[..., o:o + s])
        o += s
    return out


def to_heads(x, n_heads):
    b, t, w = x.shape
    return x.reshape(b, t, n_heads, w // n_heads).transpose(0, 2, 1, 3)


def from_heads(x):
    b, h, t, d = x.shape
    return x.transpose(0, 2, 1, 3).reshape(b, t, h * d)


def modulation(cond, w_ada, b_ada):
    m = jax.nn.silu(cond) @ w_ada + b_ada
    return jnp.split(m[:, None, :], N_MOD, axis=-1)


def modulate(x, g, shift, scale):
    return rms_norm(x, g) * (1 + scale) + shift


def swiglu(h, w_gate, w_up, w_down):
    return (jax.nn.silu(h @ w_gate) * (h @ w_up)) @ w_down


def axial_rope_tables(n):
    rows = n // GRID_W
    row = jnp.repeat(jnp.arange(rows, dtype=jnp.float32), GRID_W)
    col = jnp.tile(jnp.arange(GRID_W, dtype=jnp.float32), rows)
    inv = ROPE_BASE ** (-jnp.arange(ROPE_AXIS_PAIRS, dtype=jnp.float32) / ROPE_AXIS_PAIRS)
    ang = jnp.concatenate([row[:, None] * inv, col[:, None] * inv], axis=-1)
    return jnp.cos(ang), jnp.sin(ang)


def apply_axial_rope(x, cos, sin):
    cos = cos.astype(x.dtype)
    sin = sin.astype(x.dtype)
    x1, x2 = x[..., 0::2], x[..., 1::2]
    return jnp.stack([x1 * cos - x2 * sin, x1 * sin + x2 * cos], axis=-1).reshape(x.shape)


def rope_tail(x, rope):
    if rope is None:
        return x
    cos, sin = rope
    return jnp.concatenate([x[..., :MLA_NOPE], apply_axial_rope(x[..., MLA_NOPE:], cos, sin)], axis=-1)


def block_attention(q, k, v):
    b, h, n, dh = q.shape
    scale = dh ** -0.5
    qb = q.reshape(b, h, n // ATTN_BLOCK, ATTN_BLOCK, dh).transpose(2, 0, 1, 3, 4)

    def one_block(qi):
        s = jnp.einsum('bhqd,bhkd->bhqk', qi, k).astype(jnp.float32) * scale
        p = jax.nn.softmax(s, axis=-1).astype(v.dtype)
        return jnp.einsum('bhqk,bhkd->bhqd', p, v)

    o = lax.map(one_block, qb)
    return o.transpose(1, 2, 0, 3, 4).reshape(b, h, n, v.shape[-1])


def gla_log_decay(g_low, w2, bias):
    z = (g_low @ w2 + bias).astype(jnp.float32)
    return jax.nn.log_sigmoid(z) / GLA_GATE_NORM


def gla_chunk_scan(q, k, v, log_a, s0):
    b, h, t, dk = q.shape
    dv = v.shape[-1]
    nc = t // GLA_CHUNK

    def to_chunks(x):
        return x.reshape(b, h, nc, GLA_CHUNK, x.shape[-1]).transpose(2, 0, 1, 3, 4)

    lower = jnp.tril(jnp.ones((GLA_CHUNK, GLA_CHUNK), dtype=bool))[:, :, None]

    def step(s, xs):
        qc, kc, vc, lc = xs
        cum = jnp.cumsum(lc, axis=2)
        o_inter = jnp.einsum('bhcd,bhde->bhce', qc * jnp.exp(cum), s)
        diff = jnp.where(lower, cum[:, :, :, None, :] - cum[:, :, None, :, :], -jnp.inf)
        att = jnp.einsum('bhtd,bhsd,bhtsd->bhts', qc, kc, jnp.exp(diff))
        o = o_inter + jnp.einsum('bhts,bhse->bhte', att, vc)
        last = cum[:, :, -1:, :]
        s_new = jnp.exp(last[:, :, 0, :, None]) * s + jnp.einsum('bhcd,bhce->bhde', kc * jnp.exp(last - cum), vc)
        return s_new, o

    s_fin, o = lax.scan(step, s0, (to_chunks(q), to_chunks(k), to_chunks(v), to_chunks(log_a)))
    return o.transpose(1, 2, 0, 3, 4).reshape(b, h, t, dv), s_fin


def gla_bidirectional(q, k, v, la_f, la_b, s0_f, s0_b):
    o_f, s_f = gla_chunk_scan(q, k, v, la_f, s0_f)
    flip = lambda x: jnp.flip(x, axis=2)
    o_b, s_b = gla_chunk_scan(flip(q), flip(k), flip(v), flip(la_b), s0_b)
    return o_f + flip(o_b), s_f, s_b


def mla_queries(cq_raw, qa_g, qb_w, qn_g, rope):
    cq = rms_norm(cq_raw, qa_g)
    q = jnp.einsum('bnr,rhe->bhne', cq, qb_w)
    return rope_tail(rms_norm(q, qn_g), rope)


def mla_keys_values(ckv, krope, kvb_w, kn_g, rope):
    b, m, _ = ckv.shape
    kv = jnp.einsum('bmr,rhe->bhme', ckv, kvb_w)
    k_nope, v = kv[..., :MLA_NOPE], kv[..., MLA_NOPE:]
    k_r = jnp.broadcast_to(krope[:, None], (b, MLA_HEADS, m, MLA_ROPE))
    k = rms_norm(jnp.concatenate([k_nope, k_r], axis=-1), kn_g)
    return rope_tail(k, rope), v


def even_mixer(h, w_in, w_out, gate_w2, gate_b, gla_g, qa_g, qb_w, kva_g, kvb_w, qn_g, kn_g, ctx_cache):
    b, n, _ = h.shape
    f32 = jnp.float32
    q, k, v, og, gl_f, gl_b, cq_raw, ckv_raw, krope = split_cols(h @ w_in, EVEN_SPLIT)
    qh = to_heads(q, GLA_HEADS).astype(f32) * GLA_DK ** -0.5
    kh = to_heads(k, GLA_HEADS).astype(f32)
    vh = to_heads(v, GLA_HEADS).astype(f32)
    la_f = to_heads(gla_log_decay(gl_f, gate_w2[0], gate_b[0]), GLA_HEADS)
    la_b = to_heads(gla_log_decay(gl_b, gate_w2[1], gate_b[1]), GLA_HEADS)
    if ctx_cache is None:
        s0_f = jnp.zeros((b, GLA_HEADS, GLA_DK, GLA_DV), f32)
        s0_b = s0_f
        rope = None
    else:
        ckv_c, krope_c, s_c = ctx_cache
        s0_f = s_c[:, 0].astype(f32)
        s0_b = s_c[:, 1].astype(f32)
        rope = axial_rope_tables(n)
    o, s_f, s_b = gla_bidirectional(qh, kh, vh, la_f, la_b, s0_f, s0_b)
    gla_out = from_heads(rms_norm(o, gla_g)).astype(h.dtype) * jax.nn.silu(og)
    ckv = rms_norm(ckv_raw, kva_g)
    qm = mla_queries(cq_raw, qa_g, qb_w, qn_g, rope)
    km, vm = mla_keys_values(ckv, krope, kvb_w, kn_g, rope)
    if ctx_cache is not None:
        kc, vc = mla_keys_values(ckv_c, krope_c, kvb_w, kn_g, None)
        km = jnp.concatenate([kc, km], axis=2)
        vm = jnp.concatenate([vc, vm], axis=2)
    mla_out = from_heads(block_attention(qm, km, vm))
    out = jnp.concatenate([gla_out, mla_out], axis=-1) @ w_out
    if ctx_cache is None:
        return out, (ckv, krope, jnp.stack([s_f, s_b], axis=1))
    return out, None


def odd_mixer(h, w_in, v_g, w_s, b_s, w_out):
    b, n, _ = h.shape
    uv = jax.nn.gelu(h @ w_in)
    u, v = uv[..., :CMLP_WIDTH], uv[..., CMLP_WIDTH:]
    v = rms_norm(v, v_g)
    vg = v.reshape(b, n // CMLP_CHUNK, CMLP_CHUNK, CMLP_GROUPS, CMLP_WIDTH // CMLP_GROUPS)
    mixed = jnp.einsum('gpq,bcqge->bcpge', w_s, vg) + b_s.T[:, :, None]
    return (u * mixed.reshape(b, n, CMLP_WIDTH)) @ w_out


def setup_inputs(seed: int = 0) -> dict:
    key = jax.random.key(seed)
    ks = iter(jax.random.split(key, 40))

    def nrm(shape, scale=1.0):
        return jax.random.normal(next(ks), shape, jnp.float32) * scale

    D, F = D_MODEL, FFN_HIDDEN
    return {
        'x_prompt': nrm((BATCH, SEQ, D)),
        'x_sample': nrm((DEC_BATCH, DEC_SEQ, D)),
        'c': nrm((DEC_BATCH, D)),
        'c_ctx': nrm((D,)),
        'cache_ckv': nrm((DEC_BATCH, N_EVEN, PAST_LEN, MLA_KV_RANK)),
        'cache_krope': nrm((DEC_BATCH, N_EVEN, PAST_LEN, MLA_ROPE)),
        'state_gla': nrm((DEC_BATCH, N_EVEN, 2, GLA_HEADS, GLA_DK, GLA_DV)),
        'ada_w': nrm((DEPTH, D, N_MOD * D), D ** -0.5),
        'ada_b': nrm((DEPTH, N_MOD * D), 0.01),
        'norm_g': 1.0 + nrm((DEPTH, 3, D), 0.02),
        'ffn1_wg': nrm((DEPTH, D, F), D ** -0.5),
        'ffn1_wu': nrm((DEPTH, D, F), D ** -0.5),
        'ffn1_wd': nrm((DEPTH, F, D), F ** -0.5),
        'ffn2_wg': nrm((DEPTH, D, F), D ** -0.5),
        'ffn2_wu': nrm((DEPTH, D, F), D ** -0.5),
        'ffn2_wd': nrm((DEPTH, F, D), F ** -0.5),
        'even_w_in': nrm((N_EVEN, D, EVEN_IN_WIDTH), D ** -0.5),
        'even_w_out': nrm((N_EVEN, EVEN_OUT_WIDTH, D), EVEN_OUT_WIDTH ** -0.5),
        'gla_gate_w2': nrm((N_EVEN, 2, GLA_GATE_RANK, GLA_QK), GLA_GATE_RANK ** -0.5),
        'gla_gate_b': nrm((N_EVEN, 2, GLA_QK), 0.1),
        'gla_norm_g': 1.0 + nrm((N_EVEN, GLA_DV), 0.02),
        'mla_qa_g': 1.0 + nrm((N_EVEN, MLA_Q_RANK), 0.02),
        'mla_qb_w': nrm((N_EVEN, MLA_Q_RANK, MLA_HEADS, MLA_QK_DIM), MLA_Q_RANK ** -0.5),
        'mla_kva_g': 1.0 + nrm((N_EVEN, MLA_KV_RANK), 0.02),
        'mla_kvb_w': nrm((N_EVEN, MLA_KV_RANK, MLA_HEADS, MLA_NOPE + MLA_V), MLA_KV_RANK ** -0.5),
        'mla_qn_g': 1.0 + nrm((N_EVEN, MLA_QK_DIM), 0.02),
        'mla_kn_g': 1.0 + nrm((N_EVEN, MLA_QK_DIM), 0.02),
        'odd_w_in': nrm((N_ODD, D, 2 * CMLP_WIDTH), D ** -0.5),
        'odd_v_g': 1.0 + nrm((N_ODD, CMLP_WIDTH), 0.02),
        'odd_ws': nrm((N_ODD, CMLP_GROUPS, CMLP_CHUNK, CMLP_CHUNK), CMLP_CHUNK ** -0.5),
        'odd_bs': 1.0 + nrm((N_ODD, CMLP_GROUPS, CMLP_CHUNK), 0.1),
        'odd_w_out': nrm((N_ODD, CMLP_WIDTH, D), CMLP_WIDTH ** -0.5),
    }


def reference(x_prompt, x_sample, c, c_ctx, cache_ckv, cache_krope, state_gla, ada_w, ada_b, norm_g,
              ffn1_wg, ffn1_wu, ffn1_wd, ffn2_wg, ffn2_wu, ffn2_wd, even_w_in, even_w_out,
              gla_gate_w2, gla_gate_b, gla_norm_g, mla_qa_g, mla_qb_w, mla_kva_g, mla_kvb_w,
              mla_qn_g, mla_kn_g, odd_w_in, odd_v_g, odd_ws, odd_bs, odd_w_out):

    def run_trunk(x, cond, ctx_caches):
        new_ckv, new_krope, new_gla = [], [], []
        for i in range(DEPTH):
            sh1, sc1, g1, sh2, sc2, g2, sh3, sc3, g3 = modulation(cond, ada_w[i], ada_b[i])
            x = x + 0.5 * g1 * swiglu(modulate(x, norm_g[i, 0], sh1, sc1), ffn1_wg[i], ffn1_wu[i], ffn1_wd[i])
            hm = modulate(x, norm_g[i, 1], sh2, sc2)
            j = i // 2
            if i % 2 == 0:
                cache_j = None if ctx_caches is None else (ctx_caches[0][:, j], ctx_caches[1][:, j], ctx_caches[2][:, j])
                mix, st = even_mixer(hm, even_w_in[j], even_w_out[j], gla_gate_w2[j], gla_gate_b[j], gla_norm_g[j],
                                     mla_qa_g[j], mla_qb_w[j], mla_kva_g[j], mla_kvb_w[j], mla_qn_g[j], mla_kn_g[j],
                                     cache_j)
                if st is not None:
                    new_ckv.append(st[0])
                    new_krope.append(st[1])
                    new_gla.append(st[2].astype(x.dtype))
            else:
                mix = odd_mixer(hm, odd_w_in[j], odd_v_g[j], odd_ws[j], odd_bs[j], odd_w_out[j])
            x = x + g2 * mix
            x = x + 0.5 * g3 * swiglu(modulate(x, norm_g[i, 2], sh3, sc3), ffn2_wg[i], ffn2_wu[i], ffn2_wd[i])
        return x, new_ckv, new_krope, new_gla

    y_prompt, ckv_list, krope_list, gla_list = run_trunk(x_prompt, c_ctx[None, :], None)
    new_ckv = jnp.stack(ckv_list, axis=1)
    new_krope = jnp.stack(krope_list, axis=1)
    new_gla = jnp.stack(gla_list, axis=1)
    y_sample, _, _, _ = run_trunk(x_sample, c, (cache_ckv, cache_krope, state_gla))
    return (y_prompt, y_sample, new_ckv, new_krope, new_gla)
```

```python
import functools
import math

import numpy as np
import jax
import jax.numpy as jnp
from jax import lax
from jax.experimental import pallas as pl
from jax.experimental.pallas import tpu as pltpu

F32 = jnp.float32
BF16 = jnp.bfloat16

EPS = 1e-6
N_MOD = 9
GRID_W = 64
ROPE_BASE = 10000.0
GLA_HEADS = 4
GLA_DK = 64
GLA_DV = 128
GLA_QK = GLA_HEADS * GLA_DK
GLA_VW = GLA_HEADS * GLA_DV
GLA_GATE_RANK = 16
GLA_GATE_NORM = 16.0
GLA_CHUNK = 64
MLA_HEADS = 8
MLA_NOPE = 64
MLA_ROPE = 32
MLA_V = 64
MLA_QK_DIM = MLA_NOPE + MLA_ROPE
MLA_Q_RANK = 384
MLA_KV_RANK = 256
HEAD_PAD = 128
CMLP_CHUNK = 128
CMLP_GROUPS = 4
COND_PAD = 16
NEG_BIG = -1e30

VMEM_LIMIT = 56 * 1024 * 1024


def _cparams(*sem):
    return pltpu.CompilerParams(dimension_semantics=sem, vmem_limit_bytes=VMEM_LIMIT)


def _resident(shape, index_map):
    return pl.BlockSpec(shape, index_map, pipeline_mode=pl.Buffered(1))


def _const_map(nd):
    return lambda *_: (0,) * nd


def _silu(x):
    return x * jax.nn.sigmoid(x)


def _rms(x, g):
    return x * lax.rsqrt(jnp.mean(x * x, axis=-1, keepdims=True) + EPS) * g


def _modulate(x, g, shift, scale):
    return _rms(x, g) * (1.0 + scale) + shift


def _dot(a, b):
    return jnp.dot(a, b, preferred_element_type=F32)


def _dot_nt(a, b):
    return lax.dot_general(a, b, (((1,), (1,)), ((), ())), preferred_element_type=F32)


def _mod_kernel(c_ref, w_ref, b_ref, o_ref):
    s = _silu(c_ref[...]).astype(BF16)
    o_ref[...] = _dot(s, w_ref[...].astype(BF16)) + b_ref[...]


def _modulation_all(cond, ada_w, ada_b):
    depth, d, n = ada_w.shape
    tn = n // 4
    return pl.pallas_call(
        _mod_kernel,
        out_shape=jax.ShapeDtypeStruct((depth, COND_PAD, n), F32),
        grid=(depth, n // tn),
        in_specs=[
            pl.BlockSpec((COND_PAD, d), lambda l, j: (0, 0)),
            pl.BlockSpec((None, d, tn), lambda l, j: (l, 0, j)),
            pl.BlockSpec((None, 1, tn), lambda l, j: (l, 0, j)),
        ],
        out_specs=pl.BlockSpec((None, COND_PAD, tn), lambda l, j: (l, 0, j)),
        compiler_params=_cparams("arbitrary", "arbitrary"),
        name="adaln_modulation",
    )(cond, ada_w, ada_b.reshape(depth, 1, n))


class _Rows:
    def __init__(self, m_ctx, dec_batch, dec_seq, tm):
        assert m_ctx % tm == 0 and dec_seq % tm == 0
        self.tm = tm
        self.m = m_ctx + dec_batch * dec_seq
        self.nb_ctx = m_ctx // tm
        self.nb_seq = dec_seq // tm
        self.nblocks = self.m // tm

    def cond(self, i):
        return jnp.where(i < self.nb_ctx, 0, 1 + (i - self.nb_ctx) // self.nb_seq)

    def rope_block(self, i):
        return jnp.where(i < self.nb_ctx, 0, 1 + (i - self.nb_ctx) % self.nb_seq)

    def mod_spec(self, layer, d):
        return pl.BlockSpec((None, None, N_MOD, d), lambda i: (layer, self.cond(i), 0, 0))

    def row_spec(self, width):
        return pl.BlockSpec((self.tm, width), lambda i: (i, 0))


def _ffn_kernel(x_ref, mod_ref, g_ref, wg_ref, wu_ref, wd_ref, o_ref, *, row0):
    x = x_ref[...]
    shift = mod_ref[row0:row0 + 1, :]
    scale = mod_ref[row0 + 1:row0 + 2, :]
    gate = mod_ref[row0 + 2:row0 + 3, :]
    h = _modulate(x, g_ref[...], shift, scale).astype(BF16)
    a = _dot(h, wg_ref[...])
    u = _dot(h, wu_ref[...])
    act = (_silu(a) * u).astype(BF16)
    y = _dot(act, wd_ref[...])
    o_ref[...] = x + (0.5 * gate) * y


def _ffn(x, mod, layer, row0, g, wg, wu, wd, rows):
    m, d = x.shape
    f = wg.shape[1]
    return pl.pallas_call(
        functools.partial(_ffn_kernel, row0=row0),
        out_shape=jax.ShapeDtypeStruct((m, d), F32),
        grid=(rows.nblocks,),
        in_specs=[
            rows.row_spec(d),
            rows.mod_spec(layer, d),
            _resident((1, d), _const_map(2)),
            _resident((d, f), _const_map(2)),
            _resident((d, f), _const_map(2)),
            _resident((f, d), _const_map(2)),
        ],
        out_specs=rows.row_spec(d),
        compiler_params=_cparams("arbitrary"),
        name="ffn_swiglu",
    )(x, mod, g, wg, wu, wd)


def _gelu_tanh(x):
    c = math.sqrt(2.0 / math.pi)
    return x * (0.5 * (1.0 + jnp.tanh(c * (x + 0.044715 * (x * x * x)))))


def _odd_kernel(x_ref, mod_ref, g_ref, win_ref, vg_ref, ws_ref, bst_ref, wout_ref, o_ref):
    x = x_ref[...]
    tm = x.shape[0]
    shift, scale, gate = mod_ref[3:4, :], mod_ref[4:5, :], mod_ref[5:6, :]
    h = _modulate(x, g_ref[...], shift, scale).astype(BF16)
    uv = _gelu_tanh(_dot(h, win_ref[...]))
    width = uv.shape[1] // 2
    u = uv[:, :width]
    v = _rms(uv[:, width:], vg_ref[...]).astype(BF16)
    gw = width // CMLP_GROUPS
    chunks = []
    for c in range(tm // CMLP_CHUNK):
        r0 = c * CMLP_CHUNK
        groups = []
        for g in range(CMLP_GROUPS):
            vg = v[r0:r0 + CMLP_CHUNK, g * gw:(g + 1) * gw]
            mixed = _dot(ws_ref[g], vg) + bst_ref[:, g:g + 1]
            groups.append(u[r0:r0 + CMLP_CHUNK, g * gw:(g + 1) * gw] * mixed)
        chunks.append(jnp.concatenate(groups, axis=1))
    z = jnp.concatenate(chunks, axis=0).astype(BF16)
    o_ref[...] = x + gate * _dot(z, wout_ref[...])


def _odd_mixer(x, mod, layer, g, w_in, v_g, w_s, b_s_t, w_out, rows):
    m, d = x.shape
    width = w_out.shape[0]
    return pl.pallas_call(
        _odd_kernel,
        out_shape=jax.ShapeDtypeStruct((m, d), F32),
        grid=(rows.nblocks,),
        in_specs=[
            rows.row_spec(d),
            rows.mod_spec(layer, d),
            _resident((1, d), _const_map(2)),
            _resident((d, 2 * width), _const_map(2)),
            _resident((1, width), _const_map(2)),
            _resident((CMLP_GROUPS, CMLP_CHUNK, CMLP_CHUNK), _const_map(3)),
            _resident((CMLP_CHUNK, CMLP_GROUPS), _const_map(2)),
            _resident((width, d), _const_map(2)),
        ],
        out_specs=rows.row_spec(d),
        compiler_params=_cparams("arbitrary"),
        name="odd_gmlp",
    )(x, mod, g, w_in, v_g, w_s, b_s_t, w_out)


_C_Q = 0
_C_K = _C_Q + GLA_QK
_C_V = _C_K + GLA_QK
_C_OG = _C_V + GLA_VW
_C_CQ = _C_OG + GLA_VW
_C_CKV = _C_CQ + MLA_Q_RANK
_C_MISC = _C_CKV + MLA_KV_RANK
_C_END = _C_MISC + HEAD_PAD


def _rope(x, cos, sin):
    lane = lax.broadcasted_iota(jnp.int32, x.shape, 1)
    nxt = pltpu.roll(x, HEAD_PAD - 1, 1)
    prv = pltpu.roll(x, 1, 1)
    swapped = jnp.where(lane % 2 == 0, -nxt, prv)
    return x * cos + swapped * sin


def _head_rms(x, g):
    ms = jnp.sum(x * x, axis=-1, keepdims=True) * (1.0 / MLA_QK_DIM)
    return x * lax.rsqrt(ms + EPS) * g


def _mla_keys(ckv_bf, krope_blk, kvbk_ref, kvbv_ref, kn_g, cos, sin, k_ref, v_ref):
    kn = _dot(ckv_bf, kvbk_ref[...])
    for h in range(MLA_HEADS):
        sl = slice(h * HEAD_PAD, (h + 1) * HEAD_PAD)
        kh = _head_rms(kn[:, sl] + krope_blk, kn_g)
        if cos is not None:
            kh = _rope(kh, cos, sin)
        k_ref[:, sl] = kh.astype(BF16)
    v_ref[...] = _dot(ckv_bf, kvbv_ref[...]).astype(BF16)


def _even_in_kernel(x_ref, mod_ref, g_ref, win_ref, w2_ref, gb_ref, qag_ref, qb_ref, qng_ref,
                    kvag_ref, kvbk_ref, kvbv_ref, kng_ref, cos_ref, sin_ref,
                    gq_ref, gk_ref, gv_ref, og_ref, la_ref, qm_ref, km_ref, vm_ref, ckv_ref, kr_ref):
    x = x_ref[...]
    shift, scale = mod_ref[3:4, :], mod_ref[4:5, :]
    h = _modulate(x, g_ref[...], shift, scale).astype(BF16)
    proj = _dot(h, win_ref[...])
    cos, sin = cos_ref[...], sin_ref[...]

    gq_ref[...] = proj[:, _C_Q:_C_K] * (GLA_DK ** -0.5)
    gk_ref[...] = proj[:, _C_K:_C_V]
    gv_ref[...] = proj[:, _C_V:_C_OG].astype(BF16)
    og_ref[...] = proj[:, _C_OG:_C_CQ]

    misc = proj[:, _C_MISC:_C_END]
    z = _dot(misc.astype(BF16), w2_ref[...]) + gb_ref[...]
    la_ref[...] = -(jnp.maximum(-z, 0.0) + jnp.log1p(jnp.exp(-jnp.abs(z)))) * (1.0 / GLA_GATE_NORM)

    lane = lax.broadcasted_iota(jnp.int32, misc.shape, 1)
    krope_blk = jnp.where((lane >= MLA_NOPE) & (lane < MLA_QK_DIM), misc, 0.0)
    kr_ref[...] = krope_blk

    cq = _rms(proj[:, _C_CQ:_C_CKV], qag_ref[...]).astype(BF16)
    qn = _dot(cq, qb_ref[...])
    qng = qng_ref[...]
    for hd in range(MLA_HEADS):
        sl = slice(hd * HEAD_PAD, (hd + 1) * HEAD_PAD)
        qm_ref[:, sl] = _rope(_head_rms(qn[:, sl], qng), cos, sin).astype(BF16)

    ckv = _rms(proj[:, _C_CKV:_C_MISC], kvag_ref[...])
    ckv_ref[...] = ckv
    _mla_keys(ckv.astype(BF16), krope_blk, kvbk_ref, kvbv_ref, kng_ref[...], cos, sin, km_ref, vm_ref)


def _even_in(x, mod, layer, g, wts, cos_t, sin_t, rows):
    m, d = x.shape
    tm = rows.tm
    c2 = _const_map(2)
    rope_spec = pl.BlockSpec((tm, HEAD_PAD), lambda i: (rows.rope_block(i), 0))
    widths = [GLA_QK, GLA_QK, GLA_VW, GLA_VW, 2 * GLA_QK, MLA_HEADS * HEAD_PAD, MLA_HEADS * HEAD_PAD,
              MLA_HEADS * MLA_V, MLA_KV_RANK, HEAD_PAD]
    dtypes = [F32, F32, BF16, F32, F32, BF16, BF16, BF16, F32, F32]
    return pl.pallas_call(
        _even_in_kernel,
        out_shape=[jax.ShapeDtypeStruct((m, w), t) for w, t in zip(widths, dtypes)],
        grid=(rows.nblocks,),
        in_specs=[
            rows.row_spec(d),
            rows.mod_spec(layer, d),
            _resident((1, d), c2),
            _resident(wts["w_in"].shape, c2),
            _resident(wts["w2"].shape, c2),
            _resident(wts["gate_b"].shape, c2),
            _resident(wts["qa_g"].shape, c2),
            _resident(wts["qb"].shape, c2),
            _resident(wts["qn_g"].shape, c2),
            _resident(wts["kva_g"].shape, c2),
            _resident(wts["kvb_k"].shape, c2),
            _resident(wts["kvb_v"].shape, c2),
            _resident(wts["kn_g"].shape, c2),
            rope_spec,
            rope_spec,
        ],
        out_specs=[rows.row_spec(w) for w in widths],
        compiler_params=_cparams("arbitrary"),
        name="even_in_proj",
    )(x, mod, g, wts["w_in"], wts["w2"], wts["gate_b"], wts["qa_g"], wts["qb"], wts["qn_g"],
      wts["kva_g"], wts["kvb_k"], wts["kvb_v"], wts["kn_g"], cos_t, sin_t)


def _ctx_kv_kernel(ckv_ref, kr_ref, kvbk_ref, kvbv_ref, kng_ref, k_ref, v_ref):
    _mla_keys(ckv_ref[...].astype(BF16), kr_ref[...], kvbk_ref, kvbv_ref, kng_ref[...], None, None,
              k_ref, v_ref)


def _ctx_kv(cache_ckv, cache_krope_blk, kvb_k, kvb_v, kn_g):
    nb, ne, p, r = cache_ckv.shape
    return pl.pallas_call(
        _ctx_kv_kernel,
        out_shape=[jax.ShapeDtypeStruct((ne, nb, p, MLA_HEADS * HEAD_PAD), BF16),
                   jax.ShapeDtypeStruct((ne, nb, p, MLA_HEADS * MLA_V), BF16)],
        grid=(ne, nb),
        in_specs=[
            pl.BlockSpec((None, None, p, r), lambda j, b: (b, j, 0, 0)),
            pl.BlockSpec((None, None, p, HEAD_PAD), lambda j, b: (b, j, 0, 0)),
            pl.BlockSpec((None,) + kvb_k.shape[1:], lambda j, b: (j, 0, 0)),
            pl.BlockSpec((None,) + kvb_v.shape[1:], lambda j, b: (j, 0, 0)),
            pl.BlockSpec((None,) + kn_g.shape[1:], lambda j, b: (j, 0, 0)),
        ],
        out_specs=[pl.BlockSpec((None, None, p, MLA_HEADS * HEAD_PAD), lambda j, b: (j, b, 0, 0)),
                   pl.BlockSpec((None, None, p, MLA_HEADS * MLA_V), lambda j, b: (j, b, 0, 0))],
        compiler_params=_cparams("arbitrary", "arbitrary"),
        name="ctx_kv",
    )(cache_ckv, cache_krope_blk, kvb_k, kvb_v, kn_g)


def _split3(x):
    hi = x.astype(BF16)
    r = x - hi.astype(F32)
    mid = r.astype(BF16)
    lo = (r - mid.astype(F32)).astype(BF16)
    return hi, mid, lo


def _block_row_bcast(a, blk, off):
    c, n = a.shape
    if blk >= 8:
        pieces = [jnp.broadcast_to(a[b * blk + off:b * blk + off + 1, :], (blk, n))
                  for b in range(c // blk)]
        return pieces[0] if len(pieces) == 1 else jnp.concatenate(pieces, axis=0)
    a3 = a.reshape(c // 8, 8, n)
    sub = lax.broadcasted_iota(jnp.int32, a3.shape, 1) // blk
    out = jnp.broadcast_to(a3[:, off:off + 1, :], a3.shape)
    for s in range(1, 8 // blk):
        cand = jnp.broadcast_to(a3[:, s * blk + off:s * blk + off + 1, :], a3.shape)
        out = jnp.where(sub == s, cand, out)
    return out.reshape(c, n)


def _head_stack(a, head_w):
    lane_head = lax.broadcasted_iota(jnp.int32, a.shape, 1) // head_w
    zero = jnp.zeros_like(a)
    return jnp.concatenate([jnp.where(lane_head == h, a, zero) for h in range(GLA_HEADS)], axis=0)


def _gla_chunk(c, reverse, q_ref, k_ref, v_ref, la_ref, tri_ref, oacc_ref, st_ref):
    C = GLA_CHUNK
    d = 1 if reverse else 0
    r0 = pl.multiple_of(c * C, C)
    q = q_ref[pl.ds(r0, C), :]
    k = k_ref[pl.ds(r0, C), :]
    v = v_ref[pl.ds(r0, C), :]
    la = la_ref[pl.ds(r0, C), d * GLA_QK:(d + 1) * GLA_QK]

    tri = tri_ref[d]
    hi, mid, lo = _split3(la)
    cum = _dot(tri, hi) + _dot(tri, mid) + _dot(tri, lo)

    row = lax.broadcasted_iota(jnp.int32, (C, 1), 0)
    t_idx = lax.broadcasted_iota(jnp.int32, (C, GLA_HEADS * C), 0)
    s_idx = lax.broadcasted_iota(jnp.int32, (C, GLA_HEADS * C), 1) % C

    q_bf = q.astype(BF16)
    att = jnp.where(t_idx == s_idx, _dot_nt(q_bf, _head_stack(k.astype(BF16), GLA_DK)), 0.0)
    m = C // 2
    while m >= 1:
        blk = 2 * m
        upper = (row % blk) >= m
        q_side = ~upper if reverse else upper
        if m == 1:
            eq = jnp.where(q_side, la, NEG_BIG)
            ek = jnp.where(q_side, NEG_BIG, 0.0)
        else:
            ref_rows = _block_row_bcast(cum, blk, m if reverse else m - 1)
            eq = jnp.where(q_side, cum - ref_rows, NEG_BIG)
            ek = jnp.where(q_side, NEG_BIG, ref_rows - cum)
        qm = (q * jnp.exp(eq)).astype(BF16)
        km = (k * jnp.exp(ek)).astype(BF16)
        a_m = _dot_nt(qm, _head_stack(km, GLA_DK))
        if blk < C:
            a_m = jnp.where((t_idx // blk) == (s_idx // blk), a_m, 0.0)
        att = att + a_m
        m //= 2

    s_bd = st_ref[d]
    o = _dot(att.astype(BF16), _head_stack(v, GLA_DV))
    o = o + _dot((q * jnp.exp(cum)).astype(BF16), s_bd.astype(BF16))
    oacc_ref[pl.ds(r0, C), :] += o

    last = cum[0:1, :] if reverse else cum[C - 1:C, :]
    kct = (k * jnp.exp(last - cum)).T.astype(BF16)
    ds = _dot(kct, v)
    decay_col = jnp.exp(jnp.broadcast_to(last, (8, GLA_QK))).T[:, 0:1]
    hd = lax.broadcasted_iota(jnp.int32, ds.shape, 0) // GLA_DK
    he = lax.broadcasted_iota(jnp.int32, ds.shape, 1) // GLA_DV
    st_ref[d] = s_bd * decay_col + jnp.where(hd == he, ds, 0.0)


def _gla_kernel(*refs, has_state, emit_state):
    it = iter(refs)
    q_ref, k_ref, v_ref, la_ref, og_ref = next(it), next(it), next(it), next(it), next(it)
    s0_ref = next(it) if has_state else None
    tri_ref, gg_ref = next(it), next(it)
    _ = next(it)
    o_ref = next(it)
    sout_ref = next(it) if emit_state else None
    oacc_ref, st_ref = next(it), next(it)

    t = q_ref.shape[0]
    nc = t // GLA_CHUNK
    oacc_ref[...] = jnp.zeros_like(oacc_ref)
    st_ref[...] = jnp.zeros_like(st_ref)
    if has_state:
        for d in range(2):
            for h in range(GLA_HEADS):
                st_ref[d, h * GLA_DK:(h + 1) * GLA_DK, h * GLA_DV:(h + 1) * GLA_DV] = s0_ref[d, h]

    def body(c, carry):
        _gla_chunk(c, False, q_ref, k_ref, v_ref, la_ref, tri_ref, oacc_ref, st_ref)
        _gla_chunk(nc - 1 - c, True, q_ref, k_ref, v_ref, la_ref, tri_ref, oacc_ref, st_ref)
        return carry

    lax.fori_loop(0, nc, body, 0)

    gg = gg_ref[...]
    for h in range(GLA_HEADS):
        sl = slice(h * GLA_DV, (h + 1) * GLA_DV)
        o_ref[:, sl] = (_rms(oacc_ref[:, sl], gg) * _silu(og_ref[:, sl])).astype(BF16)
    if emit_state:
        for d in range(2):
            for h in range(GLA_HEADS):
                sout_ref[d, h] = st_ref[d, h * GLA_DK:(h + 1) * GLA_DK, h * GLA_DV:(h + 1) * GLA_DV]


def _gla(gq, gk, gv, la, og, state, layer_j, tri, gg, out_buf, nbatch, t, row_off, emit_state):
    b0 = row_off // t
    seq = lambda w: pl.BlockSpec((t, w), lambda b: (b0 + b, 0))
    in_specs = [seq(GLA_QK), seq(GLA_QK), seq(GLA_VW), seq(2 * GLA_QK), seq(GLA_VW)]
    args = [gq, gk, gv, la, og]
    if state is not None:
        in_specs.append(pl.BlockSpec((None, None, 2, GLA_HEADS, GLA_DK, GLA_DV),
                                     lambda b: (b, layer_j, 0, 0, 0, 0)))
        args.append(state)
    in_specs += [_resident(tri.shape, _const_map(3)), _resident(gg.shape, _const_map(2)),
                 pl.BlockSpec(memory_space=pl.ANY)]
    args += [tri, gg, out_buf]
    out_shape = [jax.ShapeDtypeStruct(out_buf.shape, out_buf.dtype)]
    out_specs = [seq(GLA_VW)]
    if emit_state:
        out_shape.append(jax.ShapeDtypeStruct((nbatch, 2, GLA_HEADS, GLA_DK, GLA_DV), F32))
        out_specs.append(pl.BlockSpec((None, 2, GLA_HEADS, GLA_DK, GLA_DV), lambda b: (b, 0, 0, 0, 0)))
    res = pl.pallas_call(
        functools.partial(_gla_kernel, has_state=state is not None, emit_state=emit_state),
        out_shape=out_shape,
        grid=(nbatch,),
        in_specs=in_specs,
        out_specs=out_specs,
        scratch_shapes=[pltpu.VMEM((t, GLA_VW), F32), pltpu.VMEM((2, GLA_QK, GLA_VW), F32)],
        input_output_aliases={len(args) - 1: 0},
        compiler_params=_cparams("arbitrary"),
        name="gla_scan",
    )(*args)
    return res if emit_state else (res[0], None)


def _mla_kernel(*refs, has_ctx):
    it = iter(refs)
    q_ref, k_ref, v_ref = next(it), next(it), next(it)
    kc_ref, vc_ref = (next(it), next(it)) if has_ctx else (None, None)
    _ = next(it)
    o_ref = next(it)
    scale = MLA_QK_DIM ** -0.5
    pair_w = 2 * MLA_V
    for hp in range(MLA_HEADS // 2):
        vp = v_ref[:, hp * pair_w:(hp + 1) * pair_w]
        vcp = vc_ref[:, hp * pair_w:(hp + 1) * pair_w] if has_ctx else None
        outs = []
        for hh in range(2):
            sl = slice((2 * hp + hh) * HEAD_PAD, (2 * hp + hh + 1) * HEAD_PAD)
            q = q_ref[:, sl]
            s = _dot_nt(q, k_ref[:, sl]) * scale
            mx = jnp.max(s, axis=-1, keepdims=True)
            if has_ctx:
                sc = _dot_nt(q, kc_ref[:, sl]) * scale
                mx = jnp.maximum(mx, jnp.max(sc, axis=-1, keepdims=True))
            p = jnp.exp(s - mx)
            den = jnp.sum(p, axis=-1, keepdims=True)
            o = _dot(p.astype(BF16), vp)
            if has_ctx:
                pc = jnp.exp(sc - mx)
                den = den + jnp.sum(pc, axis=-1, keepdims=True)
                o = o + _dot(pc.astype(BF16), vcp)
            outs.append(o / den)
        lane = lax.broadcasted_iota(jnp.int32, outs[0].shape, 1)
        o_ref[:, hp * pair_w:(hp + 1) * pair_w] = jnp.where(lane < MLA_V, outs[0], outs[1]).astype(BF16)


def _mla(qm, km, vm, kc, vc, layer_j, out_buf, nbatch, t, row_off, tq):
    nq = t // tq
    q0, b0 = row_off // tq, row_off // t
    qspec = lambda w: pl.BlockSpec((tq, w), lambda b, i: (q0 + b * nq + i, 0))
    kspec = lambda w: pl.BlockSpec((t, w), lambda b, i: (b0 + b, 0))
    in_specs = [qspec(MLA_HEADS * HEAD_PAD), kspec(MLA_HEADS * HEAD_PAD), kspec(MLA_HEADS * MLA_V)]
    args = [qm, km, vm]
    if kc is not None:
        p = kc.shape[2]
        in_specs += [pl.BlockSpec((None, None, p, kc.shape[3]), lambda b, i: (layer_j, b, 0, 0)),
                     pl.BlockSpec((None, None, p, vc.shape[3]), lambda b, i: (layer_j, b, 0, 0))]
        args += [kc, vc]
    in_specs.append(pl.BlockSpec(memory_space=pl.ANY))
    args.append(out_buf)
    return pl.pallas_call(
        functools.partial(_mla_kernel, has_ctx=kc is not None),
        out_shape=jax.ShapeDtypeStruct(out_buf.shape, out_buf.dtype),
        grid=(nbatch, nq),
        in_specs=in_specs,
        out_specs=qspec(MLA_HEADS * MLA_V),
        input_output_aliases={len(args) - 1: 0},
        compiler_params=_cparams("arbitrary", "arbitrary"),
        name="mla_attention",
    )(*args)


def _even_out_kernel(x_ref, mod_ref, a_ref, b_ref, wa_ref, wb_ref, o_ref):
    mix = _dot(a_ref[...], wa_ref[...]) + _dot(b_ref[...], wb_ref[...])
    o_ref[...] = x_ref[...] + mod_ref[5:6, :] * mix


def _even_out(x, mod, layer, gla_out, mla_out, w_a, w_b, rows):
    m, d = x.shape
    return pl.pallas_call(
        _even_out_kernel,
        out_shape=jax.ShapeDtypeStruct((m, d), F32),
        grid=(rows.nblocks,),
        in_specs=[
            rows.row_spec(d),
            rows.mod_spec(layer, d),
            rows.row_spec(gla_out.shape[1]),
            rows.row_spec(mla_out.shape[1]),
            _resident(w_a.shape, _const_map(2)),
            _resident(w_b.shape, _const_map(2)),
        ],
        out_specs=rows.row_spec(d),
        compiler_params=_cparams("arbitrary"),
        name="even_out_proj",
    )(x, mod, gla_out, mla_out, w_a, w_b)


def _pack_even_weights(j, even_w_in, gla_gate_w2, gla_gate_b, mla_qa_g, mla_qb_w, mla_qn_g,
                       mla_kva_g, mla_kvb_w, mla_kn_g):
    d = even_w_in.shape[1]
    w = even_w_in[j]
    o = 0
    parts = {}
    for name, width in (("q", GLA_QK), ("k", GLA_QK), ("v", GLA_VW), ("og", GLA_VW),
                        ("gl", 2 * GLA_GATE_RANK), ("cq", MLA_Q_RANK), ("ckv", MLA_KV_RANK),
                        ("kr", MLA_ROPE)):
        parts[name] = w[:, o:o + width]
        o += width
    zeros = lambda n: jnp.zeros((d, n), w.dtype)
    misc = jnp.concatenate([parts["gl"], zeros(MLA_NOPE - 2 * GLA_GATE_RANK), parts["kr"],
                            zeros(HEAD_PAD - MLA_QK_DIM)], axis=1)
    w_in = jnp.concatenate([parts["q"], parts["k"], parts["v"], parts["og"], parts["cq"],
                            parts["ckv"], misc], axis=1).astype(BF16)

    w2 = jnp.zeros((HEAD_PAD, 2 * GLA_QK), F32)
    w2 = w2.at[:GLA_GATE_RANK, :GLA_QK].set(gla_gate_w2[j, 0])
    w2 = w2.at[GLA_GATE_RANK:2 * GLA_GATE_RANK, GLA_QK:].set(gla_gate_w2[j, 1])

    pad_head = lambda a: jnp.pad(a, [(0, 0)] * (a.ndim - 1) + [(0, HEAD_PAD - a.shape[-1])])
    qb = pad_head(mla_qb_w[j]).reshape(MLA_Q_RANK, MLA_HEADS * HEAD_PAD)
    kvb = mla_kvb_w[j]
    kvb_k = pad_head(kvb[..., :MLA_NOPE]).reshape(MLA_KV_RANK, MLA_HEADS * HEAD_PAD)
    kvb_v = kvb[..., MLA_NOPE:].reshape(MLA_KV_RANK, MLA_HEADS * MLA_V)
    return {
        "w_in": w_in,
        "w2": w2.astype(BF16),
        "gate_b": gla_gate_b[j].reshape(1, 2 * GLA_QK),
        "qa_g": mla_qa_g[j][None, :],
        "qb": qb.astype(BF16),
        "qn_g": pad_head(mla_qn_g[j])[None, :],
        "kva_g": mla_kva_g[j][None, :],
        "kvb_k": kvb_k.astype(BF16),
        "kvb_v": kvb_v.astype(BF16),
        "kn_g": pad_head(mla_kn_g[j])[None, :],
    }


def _rope_tables(n, tm):
    pairs = MLA_ROPE // 4
    pos = np.arange(n)
    inv = ROPE_BASE ** (-jnp.arange(pairs, dtype=F32) / pairs)
    row = jnp.asarray(pos // GRID_W, F32)
    col = jnp.asarray(pos % GRID_W, F32)
    ang = jnp.concatenate([row[:, None] * inv, col[:, None] * inv], axis=-1)
    cos = jnp.repeat(jnp.cos(ang), 2, axis=-1)
    sin = jnp.repeat(jnp.sin(ang), 2, axis=-1)
    lpad, rpad = MLA_NOPE, HEAD_PAD - MLA_QK_DIM
    cos = jnp.pad(cos, ((tm, 0), (0, 0)), constant_values=1.0)
    cos = jnp.pad(cos, ((0, 0), (lpad, rpad)), constant_values=1.0)
    sin = jnp.pad(sin, ((tm, 0), (lpad, rpad)))
    return cos, sin


def _tri_masks():
    r = np.arange(GLA_CHUNK)
    lower = (r[None, :] <= r[:, None]).astype(np.float32)
    return jnp.asarray(np.stack([lower, lower.T]), BF16)


def _pick_tile(pref, *sizes):
    tm = pref
    while any(s % tm for s in sizes):
        tm //= 2
    return tm


def kernel(x_prompt, x_sample, c, c_ctx, cache_ckv, cache_krope, state_gla, ada_w, ada_b, norm_g,
           ffn1_wg, ffn1_wu, ffn1_wd, ffn2_wg, ffn2_wu, ffn2_wd, even_w_in, even_w_out,
           gla_gate_w2, gla_gate_b, gla_norm_g, mla_qa_g, mla_qb_w, mla_kva_g, mla_kvb_w,
           mla_qn_g, mla_kn_g, odd_w_in, odd_v_g, odd_ws, odd_bs, odd_w_out):
    batch, seq, d = x_prompt.shape
    dec_batch, dec_seq, _ = x_sample.shape
    depth = ada_w.shape[0]
    n_even = even_w_in.shape[0]
    m_ctx, m_dec = batch * seq, dec_batch * dec_seq
    assert 1 + dec_batch <= COND_PAD and seq % CMLP_CHUNK == 0 and dec_seq % CMLP_CHUNK == 0

    rows_ffn = _Rows(m_ctx, dec_batch, dec_seq, _pick_tile(512, m_ctx, dec_seq))
    rows_mix = _Rows(m_ctx, dec_batch, dec_seq, _pick_tile(256, m_ctx, dec_seq))

    cond = jnp.concatenate([c_ctx[None, :], c, jnp.zeros((COND_PAD - 1 - dec_batch, d), F32)], axis=0)
    mod = _modulation_all(cond, ada_w, ada_b).reshape(depth, COND_PAD, N_MOD, d)

    cos_t, sin_t = _rope_tables(dec_seq, rows_mix.tm)
    tri = _tri_masks()
    krope_blk = jnp.pad(cache_krope, ((0, 0), (0, 0), (0, 0), (MLA_NOPE, HEAD_PAD - MLA_QK_DIM)))
    even_wts = [_pack_even_weights(j, even_w_in, gla_gate_w2, gla_gate_b, mla_qa_g, mla_qb_w, mla_qn_g,
                                   mla_kva_g, mla_kvb_w, mla_kn_g) for j in range(n_even)]
    kc, vc = _ctx_kv(cache_ckv, krope_blk,
                     jnp.stack([w["kvb_k"] for w in even_wts]),
                     jnp.stack([w["kvb_v"] for w in even_wts]),
                     jnp.stack([w["kn_g"] for w in even_wts]))

    x = jnp.concatenate([x_prompt.reshape(m_ctx, d), x_sample.reshape(m_dec, d)], axis=0)
    m = m_ctx + m_dec
    new_ckv, new_krope, new_gla = [], [], []
    for i in range(depth):
        j = i // 2
        x = _ffn(x, mod, i, 0, norm_g[i, 0][None, :], ffn1_wg[i].astype(BF16), ffn1_wu[i].astype(BF16),
                 ffn1_wd[i].astype(BF16), rows_ffn)
        g_mix = norm_g[i, 1][None, :]
        if i % 2 == 0:
            gq, gk, gv, og, la, qm, km, vm, ckv, kr = _even_in(x, mod, i, g_mix, even_wts[j], cos_t, sin_t,
                                                               rows_mix)
            gg = gla_norm_g[j][None, :]
            gla_out = jnp.zeros((m, GLA_VW), BF16)
            gla_out, st = _gla(gq, gk, gv, la, og, None, j, tri, gg, gla_out, batch, seq, 0, True)
            gla_out, _ = _gla(gq, gk, gv, la, og, state_gla, j, tri, gg, gla_out, dec_batch, dec_seq,
                              m_ctx, False)
            mla_out = jnp.zeros((m, MLA_HEADS * MLA_V), BF16)
            mla_out = _mla(qm, km, vm, None, None, j, mla_out, batch, seq, 0, _pick_tile(256, seq))
            mla_out = _mla(qm, km, vm, kc, vc, j, mla_out, dec_batch, dec_seq, m_ctx,
                           _pick_tile(256, dec_seq))
            w_out = even_w_out[j].astype(BF16)
            x = _even_out(x, mod, i, gla_out, mla_out, w_out[:GLA_VW], w_out[GLA_VW:], rows_mix)
            new_ckv.append(ckv[:m_ctx].reshape(batch, seq, MLA_KV_RANK))
            new_krope.append(kr[:m_ctx, MLA_NOPE:MLA_QK_DIM].reshape(batch, seq, MLA_ROPE))
            new_gla.append(st)
        else:
            x = _odd_mixer(x, mod, i, g_mix, odd_w_in[j].astype(BF16), odd_v_g[j][None, :],
                           odd_ws[j].astype(BF16), odd_bs[j].T, odd_w_out[j].astype(BF16), rows_mix)
        x = _ffn(x, mod, i, 6, norm_g[i, 2][None, :], ffn2_wg[i].astype(BF16), ffn2_wu[i].astype(BF16),
                 ffn2_wd[i].astype(BF16), rows_ffn)

    y_prompt = x[:m_ctx].reshape(batch, seq, d)
    y_sample = x[m_ctx:].reshape(dec_batch, dec_seq, d)
    return (y_prompt, y_sample, jnp.stack(new_ckv, axis=1), jnp.stack(new_krope, axis=1),
            jnp.stack(new_gla, axis=1))
```

```python
import functools
import math

import numpy as np
import jax
import jax.numpy as jnp
from jax import lax
from jax.experimental import pallas as pl
from jax.experimental.pallas import tpu as pltpu

F32 = jnp.float32
BF16 = jnp.bfloat16

EPS = 1e-6
N_MOD = 9
GRID_W = 64
ROPE_BASE = 10000.0
GLA_HEADS = 4
GLA_DK = 64
GLA_DV = 128
GLA_QK = GLA_HEADS * GLA_DK
GLA_VW = GLA_HEADS * GLA_DV
GLA_GATE_RANK = 16
GLA_GATE_NORM = 16.0
GLA_CHUNK = 64
MLA_HEADS = 8
MLA_NOPE = 64
MLA_ROPE = 32
MLA_V = 64
MLA_QK_DIM = MLA_NOPE + MLA_ROPE
MLA_Q_RANK = 384
MLA_KV_RANK = 256
HEAD_PAD = 128
CMLP_CHUNK = 128
CMLP_GROUPS = 4
COND_PAD = 16
NEG_BIG = -1e30

VMEM_LIMIT = 56 * 1024 * 1024


def _cparams(*sem):
    return pltpu.CompilerParams(dimension_semantics=sem, vmem_limit_bytes=VMEM_LIMIT)


def _resident(shape, index_map):
    return pl.BlockSpec(shape, index_map, pipeline_mode=pl.Buffered(1))


def _const_map(nd):
    return lambda *_: (0,) * nd


def _silu(x):
    return x * jax.nn.sigmoid(x)


def _rms(x, g):
    return x * lax.rsqrt(jnp.mean(x * x, axis=-1, keepdims=True) + EPS) * g


def _modulate(x, g, shift, scale):
    return _rms(x, g) * (1.0 + scale) + shift


def _dot(a, b):
    return jnp.dot(a, b, preferred_element_type=F32)


def _dot_nt(a, b):
    return lax.dot_general(a, b, (((1,), (1,)), ((), ())), preferred_element_type=F32)


def _mod_kernel(c_ref, w_ref, b_ref, o_ref):
    s = _silu(c_ref[...]).astype(BF16)
    o_ref[...] = _dot(s, w_ref[...].astype(BF16)) + b_ref[...]


def _modulation_all(cond, ada_w, ada_b):
    depth, d, n = ada_w.shape
    tn = n // 4
    return pl.pallas_call(
        _mod_kernel,
        out_shape=jax.ShapeDtypeStruct((depth, COND_PAD, n), F32),
        grid=(depth, n // tn),
        in_specs=[
            pl.BlockSpec((COND_PAD, d), lambda l, j: (0, 0)),
            pl.BlockSpec((None, d, tn), lambda l, j: (l, 0, j)),
            pl.BlockSpec((None, 1, tn), lambda l, j: (l, 0, j)),
        ],
        out_specs=pl.BlockSpec((None, COND_PAD, tn), lambda l, j: (l, 0, j)),
        compiler_params=_cparams("arbitrary", "arbitrary"),
        name="adaln_modulation",
    )(cond, ada_w, ada_b.reshape(depth, 1, n))


class _Rows:
    def __init__(self, m_ctx, dec_batch, dec_seq, tm):
        assert m_ctx % tm == 0 and dec_seq % tm == 0
        self.tm = tm
        self.m = m_ctx + dec_batch * dec_seq
        self.nb_ctx = m_ctx // tm
        self.nb_seq = dec_seq // tm
        self.nblocks = self.m // tm

    def cond(self, i):
        return jnp.where(i < self.nb_ctx, 0, 1 + (i - self.nb_ctx) // self.nb_seq)

    def rope_block(self, i):
        return jnp.where(i < self.nb_ctx, 0, 1 + (i - self.nb_ctx) % self.nb_seq)

    def mod_spec(self, layer, d):
        return pl.BlockSpec((None, None, N_MOD, d), lambda i: (layer, self.cond(i), 0, 0))

    def row_spec(self, width):
        return pl.BlockSpec((self.tm, width), lambda i: (i, 0))

    def split_specs(self, width):
        ctx = pl.BlockSpec((self.tm, width), lambda i: (jnp.minimum(i, self.nb_ctx - 1), 0))
        dec = pl.BlockSpec((self.tm, width), lambda i: (jnp.maximum(i - self.nb_ctx, 0), 0))
        return [ctx, dec]


def _ffn_kernel(*refs, row0, nb_ctx, split_in, split_out):
    it = iter(refs)
    x_refs = [next(it) for _ in range(2 if split_in else 1)]
    mod_ref, g_ref, wg_ref, wu_ref, wd_ref = (next(it) for _ in range(5))
    o_refs = [next(it) for _ in range(2 if split_out else 1)]
    is_ctx = pl.program_id(0) < nb_ctx
    x = jnp.where(is_ctx, x_refs[0][...], x_refs[1][...]) if split_in else x_refs[0][...]
    shift = mod_ref[row0:row0 + 1, :]
    scale = mod_ref[row0 + 1:row0 + 2, :]
    gate = mod_ref[row0 + 2:row0 + 3, :]
    h = _modulate(x, g_ref[...], shift, scale).astype(BF16)
    a = _dot(h, wg_ref[...])
    u = _dot(h, wu_ref[...])
    act = (_silu(a) * u).astype(BF16)
    y = _dot(act, wd_ref[...])
    out = x + (0.5 * gate) * y
    if split_out:
        @pl.when(is_ctx)
        def _():
            o_refs[0][...] = out

        @pl.when(jnp.logical_not(is_ctx))
        def _():
            o_refs[1][...] = out
    else:
        o_refs[0][...] = out


def _ffn(xs, mod, layer, row0, g, wg, wu, wd, rows, split_out=False):
    split_in = isinstance(xs, (tuple, list))
    xs = list(xs) if split_in else [xs]
    d = xs[0].shape[1]
    f = wg.shape[1]
    if split_out:
        m_ctx = rows.nb_ctx * rows.tm
        out_shape = [jax.ShapeDtypeStruct((m_ctx, d), F32), jax.ShapeDtypeStruct((rows.m - m_ctx, d), F32)]
        out_specs = rows.split_specs(d)
    else:
        out_shape = jax.ShapeDtypeStruct((rows.m, d), F32)
        out_specs = rows.row_spec(d)
    return pl.pallas_call(
        functools.partial(_ffn_kernel, row0=row0, nb_ctx=rows.nb_ctx, split_in=split_in,
                          split_out=split_out),
        out_shape=out_shape,
        grid=(rows.nblocks,),
        in_specs=(rows.split_specs(d) if split_in else [rows.row_spec(d)]) + [
            rows.mod_spec(layer, d),
            _resident((1, d), _const_map(2)),
            _resident((d, f), _const_map(2)),
            _resident((d, f), _const_map(2)),
            _resident((f, d), _const_map(2)),
        ],
        out_specs=out_specs,
        compiler_params=_cparams("arbitrary"),
        name="ffn_swiglu",
    )(*xs, mod, g, wg, wu, wd)


def _gelu_tanh(x):
    c = math.sqrt(2.0 / math.pi)
    return x * (0.5 * (1.0 + jnp.tanh(c * (x + 0.044715 * (x * x * x)))))


def _odd_kernel(x_ref, mod_ref, g_ref, win_ref, vg_ref, ws_ref, bst_ref, wout_ref, o_ref):
    x = x_ref[...]
    tm = x.shape[0]
    shift, scale, gate = mod_ref[3:4, :], mod_ref[4:5, :], mod_ref[5:6, :]
    h = _modulate(x, g_ref[...], shift, scale).astype(BF16)
    uv = _gelu_tanh(_dot(h, win_ref[...]))
    width = uv.shape[1] // 2
    u = uv[:, :width]
    v = _rms(uv[:, width:], vg_ref[...]).astype(BF16)
    gw = width // CMLP_GROUPS
    chunks = []
    for c in range(tm // CMLP_CHUNK):
        r0 = c * CMLP_CHUNK
        groups = []
        for g in range(CMLP_GROUPS):
            vg = v[r0:r0 + CMLP_CHUNK, g * gw:(g + 1) * gw]
            mixed = _dot(ws_ref[g], vg) + bst_ref[:, g:g + 1]
            groups.append(u[r0:r0 + CMLP_CHUNK, g * gw:(g + 1) * gw] * mixed)
        chunks.append(jnp.concatenate(groups, axis=1))
    z = jnp.concatenate(chunks, axis=0).astype(BF16)
    o_ref[...] = x + gate * _dot(z, wout_ref[...])


def _odd_mixer(x, mod, layer, g, w_in, v_g, w_s, b_s_t, w_out, rows):
    m, d = x.shape
    width = w_out.shape[0]
    return pl.pallas_call(
        _odd_kernel,
        out_shape=jax.ShapeDtypeStruct((m, d), F32),
        grid=(rows.nblocks,),
        in_specs=[
            rows.row_spec(d),
            rows.mod_spec(layer, d),
            _resident((1, d), _const_map(2)),
            _resident((d, 2 * width), _const_map(2)),
            _resident((1, width), _const_map(2)),
            _resident((CMLP_GROUPS, CMLP_CHUNK, CMLP_CHUNK), _const_map(3)),
            _resident((CMLP_CHUNK, CMLP_GROUPS), _const_map(2)),
            _resident((width, d), _const_map(2)),
        ],
        out_specs=rows.row_spec(d),
        compiler_params=_cparams("arbitrary"),
        name="odd_gmlp",
    )(x, mod, g, w_in, v_g, w_s, b_s_t, w_out)


_C_Q = 0
_C_K = _C_Q + GLA_QK
_C_V = _C_K + GLA_QK
_C_OG = _C_V + GLA_VW
_C_CQ = _C_OG + GLA_VW
_C_CKV = _C_CQ + MLA_Q_RANK
_C_KA = _C_CKV + MLA_KV_RANK
_C_KB = _C_KA + HEAD_PAD
_C_END = _C_KB + HEAD_PAD


def _head_scale(x, live):
    ss = jnp.sum(jnp.where(live, x * x, 0.0), axis=-1, keepdims=True)
    return lax.rsqrt(ss * (1.0 / MLA_QK_DIM) + EPS)


def _even_in_kernel(x_ref, mod_ref, g_ref, win_ref, w2_ref, gb_ref, qag_ref, qb_ref, kvag_ref,
                    kvbk_ref, kvbv_ref, kng_ref, tq_ref, ta_ref, tb_ref, ckv_buf, kr_buf,
                    gq_ref, gk_ref, gv_ref, og_ref, la_ref, qm_ref, km_ref, vm_ref, ckv_ref, kr_ref,
                    *, nb_ctx):
    del ckv_buf, kr_buf
    is_ctx = pl.program_id(0) < nb_ctx
    x = x_ref[...]
    shift, scale = mod_ref[3:4, :], mod_ref[4:5, :]
    h = _modulate(x, g_ref[...], shift, scale).astype(BF16)
    proj = _dot(h, win_ref[...])

    gq_ref[...] = proj[:, _C_Q:_C_K] * (GLA_DK ** -0.5)
    gk_ref[...] = proj[:, _C_K:_C_V]
    gv_ref[...] = proj[:, _C_V:_C_OG].astype(BF16)
    og_ref[...] = proj[:, _C_OG:_C_CQ]

    ka = proj[:, _C_KA:_C_KB]
    kb = proj[:, _C_KB:_C_END]
    ckv = _rms(proj[:, _C_CKV:_C_KA], kvag_ref[...])

    @pl.when(is_ctx)
    def _():
        ckv_ref[...] = ckv
        kr_ref[...] = ka[:, MLA_NOPE:MLA_QK_DIM]

    z =_dot(ka.astype(BF16), w2_ref[...]) + gb_ref[...]
    la_ref[...] = -(jnp.maximum(-z, 0.0) + jnp.log1p(jnp.exp(-jnp.abs(z)))) * (1.0 / GLA_GATE_NORM)

    lane = lax.broadcasted_iota(jnp.int32, ka.shape, 1)
    live = lane < MLA_QK_DIM

    cq = _rms(proj[:, _C_CQ:_C_CKV], qag_ref[...]).astype(BF16)
    qn = _dot(cq, qb_ref[...])
    tq = tq_ref[...]
    for hd in range(MLA_HEADS):
        sl = slice(hd * HEAD_PAD, (hd + 1) * HEAD_PAD)
        qh = qn[:, sl]
        qm_ref[:, sl] = (qh * tq * _head_scale(qh, live)).astype(BF16)

    ckv_bf = ckv.astype(BF16)
    vm_ref[...] = _dot(ckv_bf, kvbv_ref[...]).astype(BF16)
    kn = _dot(ckv_bf, kvbk_ref[...])
    is_rope = live & (lane >= MLA_NOPE)
    ss_rope = jnp.sum(jnp.where(is_rope, ka * ka, 0.0), axis=-1, keepdims=True)
    rot = ka * ta_ref[...] + kb * tb_ref[...]
    kng = kng_ref[...]
    for hd in range(MLA_HEADS):
        sl = slice(hd * HEAD_PAD, (hd + 1) * HEAD_PAD)
        kh = kn[:, sl]
        ss = jnp.sum(kh * kh, axis=-1, keepdims=True) + ss_rope
        r = lax.rsqrt(ss * (1.0 / MLA_QK_DIM) + EPS)
        km_ref[:, sl] = ((kh * kng + rot) * r).astype(BF16)


def _even_in(x, mod, layer, g, wts, rows, layer_j, ckv_buf, kr_buf):
    m, d = x.shape
    tm = rows.tm
    c2 = _const_map(2)
    rope_spec = pl.BlockSpec((tm, HEAD_PAD), lambda i: (rows.rope_block(i), 0))
    widths = [GLA_QK, GLA_QK, GLA_VW, GLA_VW, 2 * GLA_QK, MLA_HEADS * HEAD_PAD, MLA_HEADS * HEAD_PAD,
              MLA_HEADS * MLA_V]
    dtypes = [F32, F32, BF16, F32, F32, BF16, BF16, BF16]
    names = ["w_in", "w2", "gate_b", "qa_g", "qb", "kva_g", "kvb_k", "kvb_v", "kn_g"]
    seq = ckv_buf.shape[2]
    assert seq % tm == 0
    nps = seq // tm

    def cache_spec(width):
        def index(i):
            ic = jnp.minimum(i, rows.nb_ctx - 1)
            return (ic // nps, layer_j, ic % nps, 0)
        return pl.BlockSpec((None, None, tm, width), index)

    any_spec = pl.BlockSpec(memory_space=pl.ANY)
    n_in = 3 + len(names) + 3
    return pl.pallas_call(
        functools.partial(_even_in_kernel, nb_ctx=rows.nb_ctx),
        out_shape=[jax.ShapeDtypeStruct((m, w), t) for w, t in zip(widths, dtypes)]
        + [jax.ShapeDtypeStruct(ckv_buf.shape, F32), jax.ShapeDtypeStruct(kr_buf.shape, F32)],
        grid=(rows.nblocks,),
        in_specs=[rows.row_spec(d), rows.mod_spec(layer, d), _resident((1, d), c2)]
        + [_resident(wts[n].shape, c2) for n in names] + [rope_spec] * 3 + [any_spec, any_spec],
        out_specs=[rows.row_spec(w) for w in widths] + [cache_spec(MLA_KV_RANK), cache_spec(MLA_ROPE)],
        input_output_aliases={n_in: len(widths), n_in + 1: len(widths) + 1},
        compiler_params=_cparams("arbitrary"),
        name="even_in_proj",
    )(x, mod, g, *[wts[n] for n in names], wts["t_q"], wts["t_a"], wts["t_b"], ckv_buf, kr_buf)


def _ctx_kv_kernel(ckv_ref, kr_ref, kvbk_ref, kvbv_ref, kng_ref, k_ref, v_ref):
    ckv_bf = ckv_ref[...].astype(BF16)
    v_ref[...] = _dot(ckv_bf, kvbv_ref[...]).astype(BF16)
    kn = _dot(ckv_bf, kvbk_ref[...])
    kr = kr_ref[...]
    lane = lax.broadcasted_iota(jnp.int32, kr.shape, 1)
    live = lane < MLA_QK_DIM
    kng = kng_ref[...]
    for hd in range(MLA_HEADS):
        sl = slice(hd * HEAD_PAD, (hd + 1) * HEAD_PAD)
        kh = kn[:, sl] + kr
        k_ref[:, sl] = (kh * kng * _head_scale(kh, live)).astype(BF16)


def _ctx_kv(cache_ckv, cache_krope_blk, kvb_k, kvb_v, kn_g):
    nb, ne, p, r = cache_ckv.shape
    return pl.pallas_call(
        _ctx_kv_kernel,
        out_shape=[jax.ShapeDtypeStruct((ne, nb, p, MLA_HEADS * HEAD_PAD), BF16),
                   jax.ShapeDtypeStruct((ne, nb, p, MLA_HEADS * MLA_V), BF16)],
        grid=(ne, nb),
        in_specs=[
            pl.BlockSpec((None, None, p, r), lambda j, b: (b, j, 0, 0)),
            pl.BlockSpec((None, None, p, HEAD_PAD), lambda j, b: (b, j, 0, 0)),
            pl.BlockSpec((None,) + kvb_k.shape[1:], lambda j, b: (j, 0, 0)),
            pl.BlockSpec((None,) + kvb_v.shape[1:], lambda j, b: (j, 0, 0)),
            pl.BlockSpec((None,) + kn_g.shape[1:], lambda j, b: (j, 0, 0)),
        ],
        out_specs=[pl.BlockSpec((None, None, p, MLA_HEADS * HEAD_PAD), lambda j, b: (j, b, 0, 0)),
                   pl.BlockSpec((None, None, p, MLA_HEADS * MLA_V), lambda j, b: (j, b, 0, 0))],
        compiler_params=_cparams("arbitrary", "arbitrary"),
        name="ctx_kv",
    )(cache_ckv, cache_krope_blk, kvb_k, kvb_v, kn_g)


def _split3(x):
    hi = x.astype(BF16)
    r = x - hi.astype(F32)
    mid = r.astype(BF16)
    lo = (r - mid.astype(F32)).astype(BF16)
    return hi, mid, lo


def _block_row_bcast(a, blk, off):
    c, n = a.shape
    if blk >= 8:
        pieces = [jnp.broadcast_to(a[b * blk + off:b * blk + off + 1, :], (blk, n))
                  for b in range(c // blk)]
        return pieces[0] if len(pieces) == 1 else jnp.concatenate(pieces, axis=0)
    a3 = a.reshape(c // 8, 8, n)
    sub = lax.broadcasted_iota(jnp.int32, a3.shape, 1) // blk
    out = jnp.broadcast_to(a3[:, off:off + 1, :], a3.shape)
    for s in range(1, 8 // blk):
        cand = jnp.broadcast_to(a3[:, s * blk + off:s * blk + off + 1, :], a3.shape)
        out = jnp.where(sub == s, cand, out)
    return out.reshape(c, n)


def _head_stack(a, head_w):
    lane_head = lax.broadcasted_iota(jnp.int32, a.shape, 1) // head_w
    zero = jnp.zeros_like(a)
    return jnp.concatenate([jnp.where(lane_head == h, a, zero) for h in range(GLA_HEADS)], axis=0)


def _gla_chunk(c, reverse, q_ref, k_ref, v_ref, la_ref, tri_ref, oacc_ref, st_ref):
    C = GLA_CHUNK
    d = 1 if reverse else 0
    r0 = pl.multiple_of(c * C, C)
    q = q_ref[pl.ds(r0, C), :]
    k = k_ref[pl.ds(r0, C), :]
    v = v_ref[pl.ds(r0, C), :]
    la = la_ref[pl.ds(r0, C), d * GLA_QK:(d + 1) * GLA_QK]

    tri = tri_ref[d]
    hi, mid, lo = _split3(la)
    cum = _dot(tri, hi) + _dot(tri, mid) + _dot(tri, lo)

    row = lax.broadcasted_iota(jnp.int32, (C, 1), 0)
    t_idx = lax.broadcasted_iota(jnp.int32, (C, GLA_HEADS * C), 0)
    s_idx = lax.broadcasted_iota(jnp.int32, (C, GLA_HEADS * C), 1) % C

    q_bf = q.astype(BF16)
    att = jnp.where(t_idx == s_idx, _dot_nt(q_bf, _head_stack(k.astype(BF16), GLA_DK)), 0.0)
    m = C // 2
    while m >= 1:
        blk = 2 * m
        upper = (row % blk) >= m
        q_side = ~upper if reverse else upper
        if m == 1:
            eq = jnp.where(q_side, la, NEG_BIG)
            ek = jnp.where(q_side, NEG_BIG, 0.0)
        else:
            ref_rows = _block_row_bcast(cum, blk, m if reverse else m - 1)
            eq = jnp.where(q_side, cum - ref_rows, NEG_BIG)
            ek = jnp.where(q_side, NEG_BIG, ref_rows - cum)
        qm = (q * jnp.exp(eq)).astype(BF16)
        km = (k * jnp.exp(ek)).astype(BF16)
        a_m = _dot_nt(qm, _head_stack(km, GLA_DK))
        if blk < C:
            a_m = jnp.where((t_idx // blk) == (s_idx // blk), a_m, 0.0)
        att = att + a_m
        m //= 2

    s_bd = st_ref[d]
    o = _dot(att.astype(BF16), _head_stack(v, GLA_DV))
    o = o + _dot((q * jnp.exp(cum)).astype(BF16), s_bd.astype(BF16))
    oacc_ref[pl.ds(r0, C), :] += o

    last = cum[0:1, :] if reverse else cum[C - 1:C, :]
    kct = (k * jnp.exp(last - cum)).T.astype(BF16)
    ds = _dot(kct, v)
    decay_col = jnp.exp(jnp.broadcast_to(last, (8, GLA_QK))).T[:, 0:1]
    hd = lax.broadcasted_iota(jnp.int32, ds.shape, 0) // GLA_DK
    he = lax.broadcasted_iota(jnp.int32, ds.shape, 1) // GLA_DV
    st_ref[d] = s_bd * decay_col + jnp.where(hd == he, ds, 0.0)


def _gla_kernel(*refs, has_state, emit_state):
    it = iter(refs)
    q_ref, k_ref, v_ref, la_ref, og_ref = next(it), next(it), next(it), next(it), next(it)
    s0_ref = next(it) if has_state else None
    tri_ref, gg_ref = next(it), next(it)
    for _ in range(2 if emit_state else 1):
        next(it)
    o_ref = next(it)
    sout_ref = next(it) if emit_state else None
    oacc_ref, st_ref = next(it), next(it)

    t = q_ref.shape[0]
    nc = t // GLA_CHUNK
    oacc_ref[...] = jnp.zeros_like(oacc_ref)
    st_ref[...] = jnp.zeros_like(st_ref)
    if has_state:
        for d in range(2):
            for h in range(GLA_HEADS):
                st_ref[d, h * GLA_DK:(h + 1) * GLA_DK, h * GLA_DV:(h + 1) * GLA_DV] = s0_ref[d, h]

    def body(c, carry):
        _gla_chunk(c, False, q_ref, k_ref, v_ref, la_ref, tri_ref, oacc_ref, st_ref)
        _gla_chunk(nc - 1 - c, True, q_ref, k_ref, v_ref, la_ref, tri_ref, oacc_ref, st_ref)
        return carry

    lax.fori_loop(0, nc, body, 0)

    gg = gg_ref[...]
    for h in range(GLA_HEADS):
        sl = slice(h * GLA_DV, (h + 1) * GLA_DV)
        o_ref[:, sl] = (_rms(oacc_ref[:, sl], gg) * _silu(og_ref[:, sl])).astype(BF16)
    if emit_state:
        for d in range(2):
            for h in range(GLA_HEADS):
                sout_ref[d, h] = st_ref[d, h * GLA_DK:(h + 1) * GLA_DK, h * GLA_DV:(h + 1) * GLA_DV]


def _gla(gq, gk, gv, la, og, state, layer_j, tri, gg, out_buf, nbatch, t, row_off, state_buf=None):
    emit_state = state_buf is not None
    b0 = row_off // t
    seq = lambda w: pl.BlockSpec((t, w), lambda b: (b0 + b, 0))
    in_specs = [seq(GLA_QK), seq(GLA_QK), seq(GLA_VW), seq(2 * GLA_QK), seq(GLA_VW)]
    args = [gq, gk, gv, la, og]
    if state is not None:
        in_specs.append(pl.BlockSpec((None, None, 2, GLA_HEADS, GLA_DK, GLA_DV),
                                     lambda b: (b, layer_j, 0, 0, 0, 0)))
        args.append(state)
    in_specs += [_resident(tri.shape, _const_map(3)), _resident(gg.shape, _const_map(2)),
                 pl.BlockSpec(memory_space=pl.ANY)]
    args += [tri, gg, out_buf]
    out_shape = [jax.ShapeDtypeStruct(out_buf.shape, out_buf.dtype)]
    out_specs = [seq(GLA_VW)]
    aliases = {len(args) - 1: 0}
    if emit_state:
        in_specs.append(pl.BlockSpec(memory_space=pl.ANY))
        args.append(state_buf)
        aliases[len(args) - 1] = 1
        out_shape.append(jax.ShapeDtypeStruct(state_buf.shape, F32))
        out_specs.append(pl.BlockSpec((None, None, 2, GLA_HEADS, GLA_DK, GLA_DV),
                                      lambda b: (b, layer_j, 0, 0, 0, 0)))
    res = pl.pallas_call(
        functools.partial(_gla_kernel, has_state=state is not None, emit_state=emit_state),
        out_shape=out_shape,
        grid=(nbatch,),
        in_specs=in_specs,
        out_specs=out_specs,
        scratch_shapes=[pltpu.VMEM((t, GLA_VW), F32), pltpu.VMEM((2, GLA_QK, GLA_VW), F32)],
        input_output_aliases=aliases,
        compiler_params=_cparams("arbitrary"),
        name="gla_scan",
    )(*args)
    return res if emit_state else (res[0], None)


def _mla_kernel(*refs, has_ctx):
    it = iter(refs)
    q_ref, k_ref, v_ref = next(it), next(it), next(it)
    kc_ref, vc_ref = (next(it), next(it)) if has_ctx else (None, None)
    _ = next(it)
    o_ref = next(it)
    scale = MLA_QK_DIM ** -0.5
    pair_w = 2 * MLA_V
    for hp in range(MLA_HEADS // 2):
        vp = v_ref[:, hp * pair_w:(hp + 1) * pair_w]
        vcp = vc_ref[:, hp * pair_w:(hp + 1) * pair_w] if has_ctx else None
        outs = []
        for hh in range(2):
            sl = slice((2 * hp + hh) * HEAD_PAD, (2 * hp + hh + 1) * HEAD_PAD)
            q = q_ref[:, sl]
            s = _dot_nt(q, k_ref[:, sl]) * scale
            mx = jnp.max(s, axis=-1, keepdims=True)
            if has_ctx:
                sc = _dot_nt(q, kc_ref[:, sl]) * scale
                mx = jnp.maximum(mx, jnp.max(sc, axis=-1, keepdims=True))
            p = jnp.exp(s - mx)
            den = jnp.sum(p, axis=-1, keepdims=True)
            o = _dot(p.astype(BF16), vp)
            if has_ctx:
                pc = jnp.exp(sc - mx)
                den = den + jnp.sum(pc, axis=-1, keepdims=True)
                o = o + _dot(pc.astype(BF16), vcp)
            outs.append(o / den)
        lane = lax.broadcasted_iota(jnp.int32, outs[0].shape, 1)
        o_ref[:, hp * pair_w:(hp + 1) * pair_w] = jnp.where(lane < MLA_V, outs[0], outs[1]).astype(BF16)


def _mla(qm, km, vm, kc, vc, layer_j, out_buf, nbatch, t, row_off, tq):
    nq = t // tq
    q0, b0 = row_off // tq, row_off // t
    qspec = lambda w: pl.BlockSpec((tq, w), lambda b, i: (q0 + b * nq + i, 0))
    kspec = lambda w: pl.BlockSpec((t, w), lambda b, i: (b0 + b, 0))
    in_specs = [qspec(MLA_HEADS * HEAD_PAD), kspec(MLA_HEADS * HEAD_PAD), kspec(MLA_HEADS * MLA_V)]
    args = [qm, km, vm]
    if kc is not None:
        p = kc.shape[2]
        in_specs += [pl.BlockSpec((None, None, p, kc.shape[3]), lambda b, i: (layer_j, b, 0, 0)),
                     pl.BlockSpec((None, None, p, vc.shape[3]), lambda b, i: (layer_j, b, 0, 0))]
        args += [kc, vc]
    in_specs.append(pl.BlockSpec(memory_space=pl.ANY))
    args.append(out_buf)
    return pl.pallas_call(
        functools.partial(_mla_kernel, has_ctx=kc is not None),
        out_shape=jax.ShapeDtypeStruct(out_buf.shape, out_buf.dtype),
        grid=(nbatch, nq),
        in_specs=in_specs,
        out_specs=qspec(MLA_HEADS * MLA_V),
        input_output_aliases={len(args) - 1: 0},
        compiler_params=_cparams("arbitrary", "arbitrary"),
        name="mla_attention",
    )(*args)


def _even_out_kernel(x_ref, mod_ref, a_ref, b_ref, wa_ref, wb_ref, o_ref):
    mix = _dot(a_ref[...], wa_ref[...]) + _dot(b_ref[...], wb_ref[...])
    o_ref[...] = x_ref[...] + mod_ref[5:6, :] * mix


def _even_out(x, mod, layer, gla_out, mla_out, w_a, w_b, rows):
    m, d = x.shape
    return pl.pallas_call(
        _even_out_kernel,
        out_shape=jax.ShapeDtypeStruct((m, d), F32),
        grid=(rows.nblocks,),
        in_specs=[
            rows.row_spec(d),
            rows.mod_spec(layer, d),
            rows.row_spec(gla_out.shape[1]),
            rows.row_spec(mla_out.shape[1]),
            _resident(w_a.shape, _const_map(2)),
            _resident(w_b.shape, _const_map(2)),
        ],
        out_specs=rows.row_spec(d),
        compiler_params=_cparams("arbitrary"),
        name="even_out_proj",
    )(x, mod, gla_out, mla_out, w_a, w_b)


def _swap_signed(a):
    pairs = a.reshape(a.shape[:-1] + (a.shape[-1] // 2, 2))
    return jnp.stack([-pairs[..., 1], pairs[..., 0]], axis=-1).reshape(a.shape)


def _swap_pairs(a):
    pairs = a.reshape(a.shape[:-1] + (a.shape[-1] // 2, 2))
    return pairs[..., ::-1].reshape(a.shape)


def _pack_even_weights(j, cos, sin, even_w_in, gla_gate_w2, gla_gate_b, mla_qa_g, mla_qb_w, mla_qn_g,
                       mla_kva_g, mla_kvb_w, mla_kn_g):
    d = even_w_in.shape[1]
    w = even_w_in[j]
    o = 0
    parts = {}
    for name, width in (("q", GLA_QK), ("k", GLA_QK), ("v", GLA_VW), ("og", GLA_VW),
                        ("gl", 2 * GLA_GATE_RANK), ("cq", MLA_Q_RANK), ("ckv", MLA_KV_RANK),
                        ("kr", MLA_ROPE)):
        parts[name] = w[:, o:o + width]
        o += width
    zeros = lambda n: jnp.zeros((d, n), w.dtype)
    kr, kr_sw = parts["kr"], _swap_signed(parts["kr"])
    blk_a = jnp.concatenate([parts["gl"], zeros(MLA_NOPE - 2 * GLA_GATE_RANK), kr, kr_sw], axis=1)
    blk_b = jnp.concatenate([zeros(MLA_NOPE), kr_sw, kr], axis=1)
    w_in = jnp.concatenate([parts["q"], parts["k"], parts["v"], parts["og"], parts["cq"],
                            parts["ckv"], blk_a, blk_b], axis=1).astype(BF16)

    w2 = jnp.zeros((HEAD_PAD, 2 * GLA_QK), F32)
    w2 = w2.at[:GLA_GATE_RANK, :GLA_QK].set(gla_gate_w2[j, 0])
    w2 = w2.at[GLA_GATE_RANK:2 * GLA_GATE_RANK, GLA_QK:].set(gla_gate_w2[j, 1])

    pad_head = lambda a: jnp.pad(a, [(0, 0)] * (a.ndim - 1) + [(0, HEAD_PAD - a.shape[-1])])
    qb = mla_qb_w[j]
    qb = jnp.concatenate([qb, _swap_signed(qb[..., MLA_NOPE:])], axis=-1)
    qb = qb.reshape(MLA_Q_RANK, MLA_HEADS * HEAD_PAD)
    kvb = mla_kvb_w[j]
    kvb_k = pad_head(kvb[..., :MLA_NOPE]).reshape(MLA_KV_RANK, MLA_HEADS * HEAD_PAD)
    kvb_v = kvb[..., MLA_NOPE:].reshape(MLA_KV_RANK, MLA_HEADS * MLA_V)

    def rot_tables(g):
        n = cos.shape[0]
        g_n = jnp.broadcast_to(g[:MLA_NOPE], (n, MLA_NOPE))
        g_c = g[MLA_NOPE:] * cos
        g_s = _swap_pairs(g[MLA_NOPE:]) * sin
        return g_n, g_c, g_s

    qn_n, qn_c, qn_s = rot_tables(mla_qn_g[j])
    kn_n, kn_c, kn_s = rot_tables(mla_kn_g[j])
    kzero = jnp.zeros_like(kn_n)
    return {
        "w_in": w_in,
        "w2": w2.astype(BF16),
        "gate_b": gla_gate_b[j].reshape(1, 2 * GLA_QK),
        "qa_g": mla_qa_g[j][None, :],
        "qb": qb.astype(BF16),
        "kva_g": mla_kva_g[j][None, :],
        "kvb_k": kvb_k.astype(BF16),
        "kvb_v": kvb_v.astype(BF16),
        "kn_g": pad_head(mla_kn_g[j][:MLA_NOPE])[None, :],
        "kn_g_ctx": jnp.concatenate([mla_kn_g[j], mla_kn_g[j][MLA_NOPE:]])[None, :],
        "t_q": jnp.concatenate([qn_n, qn_c, qn_s], axis=1),
        "t_a": jnp.concatenate([kzero, kn_c, kn_s], axis=1),
        "t_b": jnp.concatenate([kzero, kn_s, kn_c], axis=1),
    }


def _rope_tables(n, tm):
    pairs = MLA_ROPE // 4
    pos = np.arange(n)
    inv = ROPE_BASE ** (-jnp.arange(pairs, dtype=F32) / pairs)
    row = jnp.asarray(pos // GRID_W, F32)
    col = jnp.asarray(pos % GRID_W, F32)
    ang = jnp.concatenate([row[:, None] * inv, col[:, None] * inv], axis=-1)
    cos = jnp.repeat(jnp.cos(ang), 2, axis=-1)
    sin = jnp.repeat(jnp.sin(ang), 2, axis=-1)
    cos = jnp.pad(cos, ((tm, 0), (0, 0)), constant_values=1.0)
    sin = jnp.pad(sin, ((tm, 0), (0, 0)))
    return cos, sin


def _tri_masks():
    r = np.arange(GLA_CHUNK)
    lower = (r[None, :] <= r[:, None]).astype(np.float32)
    return jnp.asarray(np.stack([lower, lower.T]), BF16)


def _pick_tile(pref, *sizes):
    tm = pref
    while any(s % tm for s in sizes):
        tm //= 2
    return tm


def kernel(x_prompt, x_sample, c, c_ctx, cache_ckv, cache_krope, state_gla, ada_w, ada_b, norm_g,
           ffn1_wg, ffn1_wu, ffn1_wd, ffn2_wg, ffn2_wu, ffn2_wd, even_w_in, even_w_out,
           gla_gate_w2, gla_gate_b, gla_norm_g, mla_qa_g, mla_qb_w, mla_kva_g, mla_kvb_w,
           mla_qn_g, mla_kn_g, odd_w_in, odd_v_g, odd_ws, odd_bs, odd_w_out):
    batch, seq, d = x_prompt.shape
    dec_batch, dec_seq, _ = x_sample.shape
    depth = ada_w.shape[0]
    n_even = even_w_in.shape[0]
    m_ctx, m_dec = batch * seq, dec_batch * dec_seq
    assert 1 + dec_batch <= COND_PAD and seq % CMLP_CHUNK == 0 and dec_seq % CMLP_CHUNK == 0

    rows_ffn = _Rows(m_ctx, dec_batch, dec_seq, _pick_tile(512, m_ctx, dec_seq))
    rows_mix = _Rows(m_ctx, dec_batch, dec_seq, _pick_tile(256, seq, dec_seq))

    cond = jnp.concatenate([c_ctx[None, :], c, jnp.zeros((COND_PAD - 1 - dec_batch, d), F32)], axis=0)
    mod = _modulation_all(cond, ada_w, ada_b).reshape(depth, COND_PAD, N_MOD, d)

    cos_t, sin_t = _rope_tables(dec_seq, rows_mix.tm)
    tri = _tri_masks()
    krope_blk = jnp.concatenate([jnp.zeros(cache_krope.shape[:-1] + (MLA_NOPE,), F32), cache_krope,
                                 cache_krope], axis=-1)
    even_wts = [_pack_even_weights(j, cos_t, sin_t, even_w_in, gla_gate_w2, gla_gate_b, mla_qa_g, mla_qb_w,
                                   mla_qn_g, mla_kva_g, mla_kvb_w, mla_kn_g) for j in range(n_even)]
    kc, vc = _ctx_kv(cache_ckv, krope_blk,
                     jnp.stack([w["kvb_k"] for w in even_wts]),
                     jnp.stack([w["kvb_v"] for w in even_wts]),
                     jnp.stack([w["kn_g_ctx"] for w in even_wts]))

    x = (x_prompt.reshape(m_ctx, d), x_sample.reshape(m_dec, d))
    m = m_ctx + m_dec
    new_ckv = jnp.zeros((batch, n_even, seq, MLA_KV_RANK), F32)
    new_krope = jnp.zeros((batch, n_even, seq, MLA_ROPE), F32)
    new_gla = jnp.zeros((batch, n_even, 2, GLA_HEADS, GLA_DK, GLA_DV), F32)
    for i in range(depth):
        j = i // 2
        x = _ffn(x, mod, i, 0, norm_g[i, 0][None, :], ffn1_wg[i].astype(BF16), ffn1_wu[i].astype(BF16),
                 ffn1_wd[i].astype(BF16), rows_ffn)
        g_mix = norm_g[i, 1][None, :]
        if i % 2 == 0:
            gq, gk, gv, og, la, qm, km, vm, new_ckv, new_krope = _even_in(
                x, mod, i, g_mix, even_wts[j], rows_mix, j, new_ckv, new_krope)
            gg = gla_norm_g[j][None, :]
            gla_out = jnp.zeros((m, GLA_VW), BF16)
            gla_out, new_gla = _gla(gq, gk, gv, la, og, None, j, tri, gg, gla_out, batch, seq, 0, new_gla)
            gla_out, _ = _gla(gq, gk, gv, la, og, state_gla, j, tri, gg, gla_out, dec_batch, dec_seq, m_ctx)
            mla_out = jnp.zeros((m, MLA_HEADS * MLA_V), BF16)
            mla_out = _mla(qm, km, vm, None, None, j, mla_out, batch, seq, 0, _pick_tile(256, seq))
            mla_out = _mla(qm, km, vm, kc, vc, j, mla_out, dec_batch, dec_seq, m_ctx,
                           _pick_tile(256, dec_seq))
            w_out = even_w_out[j].astype(BF16)
            x = _even_out(x, mod, i, gla_out, mla_out, w_out[:GLA_VW], w_out[GLA_VW:], rows_mix)
        else:
            x = _odd_mixer(x, mod, i, g_mix, odd_w_in[j].astype(BF16), odd_v_g[j][None, :],
                           odd_ws[j].astype(BF16), odd_bs[j].T, odd_w_out[j].astype(BF16), rows_mix)
        x = _ffn(x, mod, i, 6, norm_g[i, 2][None, :], ffn2_wg[i].astype(BF16), ffn2_wu[i].astype(BF16),
                 ffn2_wd[i].astype(BF16), rows_ffn, split_out=(i == depth - 1))

    y_prompt, y_sample = x
    return (y_prompt.reshape(batch, seq, d), y_sample.reshape(dec_batch, dec_seq, d), new_ckv, new_krope,
            new_gla)
```

```python
import functools
import math

import numpy as np
import jax
import jax.numpy as jnp
from jax import lax
from jax.experimental import pallas as pl
from jax.experimental.pallas import tpu as pltpu

F32 = jnp.float32
BF16 = jnp.bfloat16

EPS = 1e-6
N_MOD = 9
GRID_W = 64
ROPE_BASE = 10000.0
GLA_HEADS = 4
GLA_DK = 64
GLA_DV = 128
GLA_QK = GLA_HEADS * GLA_DK
GLA_VW = GLA_HEADS * GLA_DV
GLA_GATE_RANK = 16
GLA_GATE_NORM = 16.0
GLA_CHUNK = 64
MLA_HEADS = 8
MLA_NOPE = 64
MLA_ROPE = 32
MLA_V = 64
MLA_QK_DIM = MLA_NOPE + MLA_ROPE
MLA_Q_RANK = 384
MLA_KV_RANK = 256
HEAD_PAD = 128
CMLP_CHUNK = 128
CMLP_GROUPS = 4
COND_PAD = 16

VMEM_LIMIT = 56 * 1024 * 1024


def _cparams(*sem):
    return pltpu.CompilerParams(dimension_semantics=sem, vmem_limit_bytes=VMEM_LIMIT)


def _resident(shape, index_map):
    return pl.BlockSpec(shape, index_map, pipeline_mode=pl.Buffered(1))


def _const_map(nd):
    return lambda *_: (0,) * nd


def _layer_spec(stack, layer):
    nd = stack.ndim - 1
    return _resident((None,) + stack.shape[1:], lambda *_: (layer,) + (0,) * nd)


def _silu(x):
    return x * jax.nn.sigmoid(x)


def _rms(x, g):
    return x * lax.rsqrt(jnp.mean(x * x, axis=-1, keepdims=True) + EPS) * g


def _modulate(x, g, shift, scale):
    return _rms(x, g) * (1.0 + scale) + shift


def _dot(a, b):
    return jnp.dot(a, b, preferred_element_type=F32)


def _dot_nt(a, b):
    return lax.dot_general(a, b, (((1,), (1,)), ((), ())), preferred_element_type=F32)


def _mod_kernel(c_ref, w_ref, b_ref, o_ref):
    s = _silu(c_ref[...]).astype(BF16)
    o_ref[...] = _dot(s, w_ref[...].astype(BF16)) + b_ref[...]


def _modulation_all(cond, ada_w, ada_b):
    depth, d, n = ada_w.shape
    tn = n // 4
    return pl.pallas_call(
        _mod_kernel,
        out_shape=jax.ShapeDtypeStruct((depth, COND_PAD, n), F32),
        grid=(depth, n // tn),
        in_specs=[
            pl.BlockSpec((COND_PAD, d), lambda l, j: (0, 0)),
            pl.BlockSpec((None, d, tn), lambda l, j: (l, 0, j)),
            pl.BlockSpec((None, 1, tn), lambda l, j: (l, 0, j)),
        ],
        out_specs=pl.BlockSpec((None, COND_PAD, tn), lambda l, j: (l, 0, j)),
        compiler_params=_cparams("arbitrary", "arbitrary"),
        name="adaln_modulation",
    )(cond, ada_w, ada_b.reshape(depth, 1, n))


class _Rows:
    def __init__(self, m_ctx, dec_batch, dec_seq, tm):
        assert m_ctx % tm == 0 and dec_seq % tm == 0
        self.tm = tm
        self.m_ctx = m_ctx
        self.m = m_ctx + dec_batch * dec_seq
        self.nb_ctx = m_ctx // tm
        self.nb_seq = dec_seq // tm
        self.nblocks = self.m // tm

    def cond(self, i):
        return jnp.where(i < self.nb_ctx, 0, 1 + (i - self.nb_ctx) // self.nb_seq)

    def rope_block(self, i):
        return jnp.where(i < self.nb_ctx, 0, 1 + (i - self.nb_ctx) % self.nb_seq)

    def mod_spec(self, layer, d):
        return pl.BlockSpec((None, None, N_MOD, d), lambda i: (layer, self.cond(i), 0, 0))

    def row_spec(self, width):
        return pl.BlockSpec((self.tm, width), lambda i: (i, 0))

    def ctx_spec(self, width):
        return pl.BlockSpec((self.tm, width), lambda i: (jnp.minimum(i, self.nb_ctx - 1), 0))

    def dec_spec(self, width):
        return pl.BlockSpec((self.tm, width), lambda i: (jnp.maximum(i - self.nb_ctx, 0), 0))

    def split_specs(self, width):
        return [self.ctx_spec(width), self.dec_spec(width)]

    def split_shapes(self, width, dtype):
        return [jax.ShapeDtypeStruct((self.m_ctx, width), dtype),
                jax.ShapeDtypeStruct((self.m - self.m_ctx, width), dtype)]


def _load_split(refs, is_ctx):
    if len(refs) == 1:
        return refs[0][...]
    return jnp.where(is_ctx, refs[0][...], refs[1][...])


def _ffn_kernel(*refs, row0, nb_ctx, split_in, split_out, premix):
    it = iter(refs)
    x_refs = [next(it) for _ in range(2 if split_in else 1)]
    mod_ref, g_ref, wg_ref, wu_ref, wd_ref = (next(it) for _ in range(5))
    if premix:
        a_refs, b_refs = [next(it), next(it)], [next(it), next(it)]
        wo_ref = next(it)
    o_refs = [next(it) for _ in range(2 if split_out else 1)]
    is_ctx = pl.program_id(0) < nb_ctx
    x = _load_split(x_refs, is_ctx)
    if premix:
        half = wo_ref.shape[0] // 2
        mix = _dot(_load_split(a_refs, is_ctx), wo_ref[:half, :])
        mix = mix + _dot(_load_split(b_refs, is_ctx), wo_ref[half:, :])
        x = x + mod_ref[5:6, :] * mix
    shift = mod_ref[row0:row0 + 1, :]
    scale = mod_ref[row0 + 1:row0 + 2, :]
    gate = mod_ref[row0 + 2:row0 + 3, :]
    h = _modulate(x, g_ref[...], shift, scale).astype(BF16)
    a = _dot(h, wg_ref[...])
    u = _dot(h, wu_ref[...])
    act = (_silu(a) * u).astype(BF16)
    y = _dot(act, wd_ref[...])
    out = x + (0.5 * gate) * y
    if split_out:
        @pl.when(is_ctx)
        def _():
            o_refs[0][...] = out

        @pl.when(jnp.logical_not(is_ctx))
        def _():
            o_refs[1][...] = out
    else:
        o_refs[0][...] = out


def _ffn(xs, mod, layer, row0, g, wg, wu, wd, rows, split_out=False, premix=None):
    split_in = isinstance(xs, (tuple, list))
    xs = list(xs) if split_in else [xs]
    d = xs[0].shape[1]
    in_specs = (rows.split_specs(d) if split_in else [rows.row_spec(d)]) + [
        rows.mod_spec(layer, d),
        _resident((1, d), _const_map(2)),
        _layer_spec(wg, layer), _layer_spec(wu, layer), _layer_spec(wd, layer),
    ]
    args = xs + [mod, g, wg, wu, wd]
    if premix is not None:
        gla, mla, w_out, j = premix
        in_specs += rows.split_specs(gla[0].shape[1]) + rows.split_specs(mla[0].shape[1])
        in_specs.append(_layer_spec(w_out, j))
        args += list(gla) + list(mla) + [w_out]
    return pl.pallas_call(
        functools.partial(_ffn_kernel, row0=row0, nb_ctx=rows.nb_ctx, split_in=split_in,
                          split_out=split_out, premix=premix is not None),
        out_shape=rows.split_shapes(d, F32) if split_out else jax.ShapeDtypeStruct((rows.m, d), F32),
        grid=(rows.nblocks,),
        in_specs=in_specs,
        out_specs=rows.split_specs(d) if split_out else rows.row_spec(d),
        compiler_params=_cparams("arbitrary"),
        name="ffn_swiglu",
    )(*args)


def _gelu_tanh(x):
    c = math.sqrt(2.0 / math.pi)
    return x * (0.5 * (1.0 + jnp.tanh(c * (x + 0.044715 * (x * x * x)))))


def _odd_kernel(x_ref, mod_ref, g_ref, win_ref, vg_ref, ws_ref, bst_ref, wout_ref, o_ref):
    x = x_ref[...]
    tm = x.shape[0]
    shift, scale, gate = mod_ref[3:4, :], mod_ref[4:5, :], mod_ref[5:6, :]
    h = _modulate(x, g_ref[...], shift, scale).astype(BF16)
    uv = _gelu_tanh(_dot(h, win_ref[...]))
    width = uv.shape[1] // 2
    u = uv[:, :width]
    v = _rms(uv[:, width:], vg_ref[...]).astype(BF16)
    gw = width // CMLP_GROUPS
    chunks = []
    for c in range(tm // CMLP_CHUNK):
        r0 = c * CMLP_CHUNK
        groups = []
        for g in range(CMLP_GROUPS):
            vg = v[r0:r0 + CMLP_CHUNK, g * gw:(g + 1) * gw]
            mixed = _dot(ws_ref[g], vg) + bst_ref[:, g:g + 1]
            groups.append(u[r0:r0 + CMLP_CHUNK, g * gw:(g + 1) * gw] * mixed)
        chunks.append(jnp.concatenate(groups, axis=1))
    z = jnp.concatenate(chunks, axis=0).astype(BF16)
    o_ref[...] = x + gate * _dot(z, wout_ref[...])


def _odd_mixer(x, mod, layer, layer_j, g, w_in, v_g, w_s, b_s_t, w_out, rows):
    m, d = x.shape
    return pl.pallas_call(
        _odd_kernel,
        out_shape=jax.ShapeDtypeStruct((m, d), F32),
        grid=(rows.nblocks,),
        in_specs=[
            rows.row_spec(d),
            rows.mod_spec(layer, d),
            _resident((1, d), _const_map(2)),
            _layer_spec(w_in, layer_j), _layer_spec(v_g, layer_j), _layer_spec(w_s, layer_j),
            _layer_spec(b_s_t, layer_j), _layer_spec(w_out, layer_j),
        ],
        out_specs=rows.row_spec(d),
        compiler_params=_cparams("arbitrary"),
        name="odd_gmlp",
    )(x, mod, g, w_in, v_g, w_s, b_s_t, w_out)


_C_Q = 0
_C_K = _C_Q + GLA_QK
_C_V = _C_K + GLA_QK
_C_OG = _C_V + GLA_VW
_C_CQ = _C_OG + GLA_VW
_C_CKV = _C_CQ + MLA_Q_RANK
_C_KA = _C_CKV + MLA_KV_RANK
_C_KB = _C_KA + HEAD_PAD
_C_END = _C_KB + HEAD_PAD


def _head_scale(x, live):
    ss = jnp.sum(jnp.where(live, x * x, 0.0), axis=-1, keepdims=True)
    return lax.rsqrt(ss * (1.0 / MLA_QK_DIM) + EPS)


def _even_in_kernel(x_ref, mod_ref, g_ref, win_ref, w2_ref, gb_ref, qag_ref, qb_ref, kvag_ref,
                    kvbk_ref, kvbv_ref, kng_ref, tq_ref, ta_ref, tb_ref,
                    gq_ref, gk_ref, gv_ref, og_ref, la_ref, qm_ref, km_ref, vm_ref, ckv_ref, kr_ref,
                    *, nb_ctx):
    is_ctx = pl.program_id(0) < nb_ctx
    x = x_ref[...]
    shift, scale = mod_ref[3:4, :], mod_ref[4:5, :]
    h = _modulate(x, g_ref[...], shift, scale).astype(BF16)
    proj = _dot(h, win_ref[...])

    gq_ref[...] = proj[:, _C_Q:_C_K] * (GLA_DK ** -0.5)
    gk_ref[...] = proj[:, _C_K:_C_V]
    gv_ref[...] = proj[:, _C_V:_C_OG].astype(BF16)
    og_ref[...] = proj[:, _C_OG:_C_CQ]

    ka = proj[:, _C_KA:_C_KB]
    kb = proj[:, _C_KB:_C_END]
    ckv = _rms(proj[:, _C_CKV:_C_KA], kvag_ref[...])

    @pl.when(is_ctx)
    def _():
        ckv_ref[...] = ckv
        kr_ref[...] = ka[:, MLA_NOPE:MLA_QK_DIM]

    z = _dot(ka.astype(BF16), w2_ref[...]) + gb_ref[...]
    la_ref[...] = -(jnp.maximum(-z, 0.0) + jnp.log1p(jnp.exp(-jnp.abs(z)))) * (1.0 / GLA_GATE_NORM)

    lane = lax.broadcasted_iota(jnp.int32, ka.shape, 1)
    live = lane < MLA_QK_DIM

    cq = _rms(proj[:, _C_CQ:_C_CKV], qag_ref[...]).astype(BF16)
    qn = _dot(cq, qb_ref[...])
    tq = tq_ref[...]
    for hd in range(MLA_HEADS):
        sl = slice(hd * HEAD_PAD, (hd + 1) * HEAD_PAD)
        qh = qn[:, sl]
        qm_ref[:, sl] = (qh * tq * _head_scale(qh, live)).astype(BF16)

    ckv_bf = ckv.astype(BF16)
    vm_ref[...] = _dot(ckv_bf, kvbv_ref[...]).astype(BF16)
    kn = _dot(ckv_bf, kvbk_ref[...])
    is_rope = live & (lane >= MLA_NOPE)
    ss_rope = jnp.sum(jnp.where(is_rope, ka * ka, 0.0), axis=-1, keepdims=True)
    rot = ka * ta_ref[...] + kb * tb_ref[...]
    kng = kng_ref[...]
    for hd in range(MLA_HEADS):
        sl = slice(hd * HEAD_PAD, (hd + 1) * HEAD_PAD)
        kh = kn[:, sl]
        ss = jnp.sum(kh * kh, axis=-1, keepdims=True) + ss_rope
        r = lax.rsqrt(ss * (1.0 / MLA_QK_DIM) + EPS)
        km_ref[:, sl] = ((kh * kng + rot) * r).astype(BF16)


def _even_in(x, mod, layer, g, wts, rows):
    m, d = x.shape
    tm = rows.tm
    c2 = _const_map(2)
    rope_spec = pl.BlockSpec((tm, HEAD_PAD), lambda i: (rows.rope_block(i), 0))
    widths = [GLA_QK, GLA_QK, GLA_VW, GLA_VW, 2 * GLA_QK, MLA_HEADS * HEAD_PAD, MLA_HEADS * HEAD_PAD,
              MLA_HEADS * MLA_V]
    dtypes = [F32, F32, BF16, F32, F32, BF16, BF16, BF16]
    names = ["w_in", "w2", "gate_b", "qa_g", "qb", "kva_g", "kvb_k", "kvb_v", "kn_g"]
    return pl.pallas_call(
        functools.partial(_even_in_kernel, nb_ctx=rows.nb_ctx),
        out_shape=[jax.ShapeDtypeStruct((m, w), t) for w, t in zip(widths, dtypes)]
        + [jax.ShapeDtypeStruct((rows.m_ctx, MLA_KV_RANK), F32),
           jax.ShapeDtypeStruct((rows.m_ctx, MLA_ROPE), F32)],
        grid=(rows.nblocks,),
        in_specs=[rows.row_spec(d), rows.mod_spec(layer, d), _resident((1, d), c2)]
        + [_resident(wts[n].shape, c2) for n in names] + [rope_spec] * 3,
        out_specs=[rows.row_spec(w) for w in widths] + [rows.ctx_spec(MLA_KV_RANK), rows.ctx_spec(MLA_ROPE)],
        compiler_params=_cparams("arbitrary"),
        name="even_in_proj",
    )(x, mod, g, *[wts[n] for n in names], wts["t_q"], wts["t_a"], wts["t_b"])


def _ctx_kv_kernel(ckv_ref, kr_ref, kvbk_ref, kvbv_ref, kng_ref, k_ref, v_ref):
    ckv_bf = ckv_ref[...].astype(BF16)
    v_ref[...] = _dot(ckv_bf, kvbv_ref[...]).astype(BF16)
    kn = _dot(ckv_bf, kvbk_ref[...])
    kr = kr_ref[...]
    lane = lax.broadcasted_iota(jnp.int32, kr.shape, 1)
    live = lane < MLA_QK_DIM
    kng = kng_ref[...]
    for hd in range(MLA_HEADS):
        sl = slice(hd * HEAD_PAD, (hd + 1) * HEAD_PAD)
        kh = kn[:, sl] + kr
        k_ref[:, sl] = (kh * kng * _head_scale(kh, live)).astype(BF16)


def _ctx_kv(cache_ckv, cache_krope_blk, kvb_k, kvb_v, kn_g):
    nb, ne, p, r = cache_ckv.shape
    return pl.pallas_call(
        _ctx_kv_kernel,
        out_shape=[jax.ShapeDtypeStruct((ne, nb, p, MLA_HEADS * HEAD_PAD), BF16),
                   jax.ShapeDtypeStruct((ne, nb, p, MLA_HEADS * MLA_V), BF16)],
        grid=(ne, nb),
        in_specs=[
            pl.BlockSpec((None, None, p, r), lambda j, b: (b, j, 0, 0)),
            pl.BlockSpec((None, None, p, HEAD_PAD), lambda j, b: (b, j, 0, 0)),
            pl.BlockSpec((None,) + kvb_k.shape[1:], lambda j, b: (j, 0, 0)),
            pl.BlockSpec((None,) + kvb_v.shape[1:], lambda j, b: (j, 0, 0)),
            pl.BlockSpec((None,) + kn_g.shape[1:], lambda j, b: (j, 0, 0)),
        ],
        out_specs=[pl.BlockSpec((None, None, p, MLA_HEADS * HEAD_PAD), lambda j, b: (j, b, 0, 0)),
                   pl.BlockSpec((None, None, p, MLA_HEADS * MLA_V), lambda j, b: (j, b, 0, 0))],
        compiler_params=_cparams("arbitrary", "arbitrary"),
        name="ctx_kv",
    )(cache_ckv, cache_krope_blk, kvb_k, kvb_v, kn_g)


def _split3(x):
    hi = x.astype(BF16)
    r = x - hi.astype(F32)
    mid = r.astype(BF16)
    lo = (r - mid.astype(F32)).astype(BF16)
    return hi, mid, lo


def _block_row_bcast(a, blk, off):
    c, n = a.shape
    if blk >= 8:
        pieces = [jnp.broadcast_to(a[b * blk + off:b * blk + off + 1, :], (blk, n))
                  for b in range(c // blk)]
        return pieces[0] if len(pieces) == 1 else jnp.concatenate(pieces, axis=0)
    a3 = a.reshape(c // 8, 8, n)
    sub = lax.broadcasted_iota(jnp.int32, a3.shape, 1) // blk
    out = jnp.broadcast_to(a3[:, off:off + 1, :], a3.shape)
    for s in range(1, 8 // blk):
        cand = jnp.broadcast_to(a3[:, s * blk + off:s * blk + off + 1, :], a3.shape)
        out = jnp.where(sub == s, cand, out)
    return out.reshape(c, n)


def _head_stack(a, head_w):
    lane_head = lax.broadcasted_iota(jnp.int32, a.shape, 1) // head_w
    zero = jnp.zeros_like(a)
    return jnp.concatenate([jnp.where(lane_head == h, a, zero) for h in range(GLA_HEADS)], axis=0)


def _gla_pair_weights(q, k, la_f, la_b, cum_f, cum_b):
    C = GLA_CHUNK
    row = lax.broadcasted_iota(jnp.int32, (C, 1), 0)
    t_idx = lax.broadcasted_iota(jnp.int32, (C, GLA_HEADS * C), 0)
    s_idx = lax.broadcasted_iota(jnp.int32, (C, GLA_HEADS * C), 1) % C
    k_stack = _head_stack(k.astype(BF16), GLA_DK)
    att = jnp.where(t_idx == s_idx, 2.0 * _dot_nt(q.astype(BF16), k_stack), 0.0)
    m = C // 2
    while m >= 1:
        blk = 2 * m
        upper = (row % blk) >= m
        if m == 1:
            qm = (q * jnp.exp(jnp.where(upper, la_f, la_b))).astype(BF16)
            km_stack = k_stack
        else:
            ref_f = _block_row_bcast(cum_f, blk, m - 1)
            ref_b = _block_row_bcast(cum_b, blk, m)
            eq = jnp.where(upper, cum_f - ref_f, cum_b - ref_b)
            ek = jnp.where(upper, ref_b - cum_b, ref_f - cum_f)
            qm = (q * jnp.exp(eq)).astype(BF16)
            km_stack = _head_stack((k * jnp.exp(ek)).astype(BF16), GLA_DK)
        pair = ((t_idx // blk) == (s_idx // blk)) & ((t_idx // m) != (s_idx // m))
        att = jnp.where(pair, _dot_nt(qm, km_stack), att)
        m //= 2
    return att


def _gla_phase_a(c, q_ref, k_ref, v_ref, la_ref, tri_ref, oacc_ref, qst_ref, ds_ref, dc_ref):
    C = GLA_CHUNK
    r0 = pl.multiple_of(c * C, C)
    q = q_ref[pl.ds(r0, C), :]
    k = k_ref[pl.ds(r0, C), :]
    v = v_ref[pl.ds(r0, C), :]
    la = la_ref[pl.ds(r0, C), :]
    cums = []
    for d in range(2):
        hi, mid, lo = _split3(la[:, d * GLA_QK:(d + 1) * GLA_QK])
        tri = tri_ref[d]
        cums.append(_dot(tri, hi) + _dot(tri, mid) + _dot(tri, lo))
    att = _gla_pair_weights(q, k, la[:, :GLA_QK], la[:, GLA_QK:], cums[0], cums[1])
    oacc_ref[pl.ds(r0, C), :] = _dot(att.astype(BF16), _head_stack(v, GLA_DV))
    for d, cum in enumerate(cums):
        last = cum[C - 1:C, :] if d == 0 else cum[0:1, :]
        qst_ref[d, c] = _head_stack((q * jnp.exp(cum)).astype(BF16), GLA_DK)
        kct = (k * jnp.exp(last - cum)).T.astype(BF16)
        ds_ref[d, c] = jnp.concatenate(
            [_dot(kct[h * GLA_DK:(h + 1) * GLA_DK, :], v[:, h * GLA_DV:(h + 1) * GLA_DV])
             for h in range(GLA_HEADS)], axis=0)
        decay_col = jnp.exp(jnp.broadcast_to(last, (8, GLA_QK))).T[:, 0:1]
        dc_ref[d, c] = jnp.broadcast_to(decay_col, (GLA_QK, GLA_DV))


def _gla_phase_c(c, og_ref, gg, oacc_ref, qst_ref, sb_ref, o_ref):
    C = GLA_CHUNK
    r0 = pl.multiple_of(c * C, C)
    q_cat = jnp.concatenate([qst_ref[0, c], qst_ref[1, c]], axis=1)
    s_cat = jnp.concatenate([sb_ref[0, c], sb_ref[1, c]], axis=0)
    inter = _dot(q_cat, s_cat)
    for h in range(GLA_HEADS):
        sl = slice(h * GLA_DV, (h + 1) * GLA_DV)
        o = oacc_ref[pl.ds(r0, C), sl] + inter[h * C:(h + 1) * C, :]
        o_ref[pl.ds(r0, C), sl] = (_rms(o, gg) * _silu(og_ref[pl.ds(r0, C), sl])).astype(BF16)


def _gla_kernel(*refs, has_state):
    it = iter(refs)
    q_ref, k_ref, v_ref, la_ref, og_ref = (next(it) for _ in range(5))
    s0_ref = next(it) if has_state else None
    tri_ref, gg_ref, o_ref, sout_ref = (next(it) for _ in range(4))
    oacc_ref, qst_ref, ds_ref, dc_ref, sb_ref, st_ref = (next(it) for _ in range(6))

    nc = q_ref.shape[0] // GLA_CHUNK
    per_step = 4

    def phase_a(i, carry):
        for u in range(per_step):
            _gla_phase_a(per_step * i + u, q_ref, k_ref, v_ref, la_ref, tri_ref, oacc_ref, qst_ref,
                         ds_ref, dc_ref)
        return carry

    lax.fori_loop(0, nc // per_step, phase_a, 0)

    for d in range(2):
        for h in range(GLA_HEADS):
            rows = slice(h * GLA_DK, (h + 1) * GLA_DK)
            st_ref[d, rows, :] = s0_ref[d, h] if has_state else jnp.zeros((GLA_DK, GLA_DV), F32)

    def phase_b(c, carry):
        for d in range(2):
            cc = c if d == 0 else nc - 1 - c
            s = st_ref[d]
            sb_ref[d, cc] = s.astype(BF16)
            st_ref[d] = s * dc_ref[d, cc] + ds_ref[d, cc]
        return carry

    lax.fori_loop(0, nc, phase_b, 0)
    for d in range(2):
        for h in range(GLA_HEADS):
            sout_ref[d, h] = st_ref[d, h * GLA_DK:(h + 1) * GLA_DK, :]

    gg = gg_ref[...]

    def phase_c(i, carry):
        for u in range(per_step):
            _gla_phase_c(per_step * i + u, og_ref, gg, oacc_ref, qst_ref, sb_ref, o_ref)
        return carry

    lax.fori_loop(0, nc // per_step, phase_c, 0)


def _gla(gq, gk, gv, la, og, state, layer_j, tri, gg, nbatch, t, row_off):
    assert t % (2 * GLA_CHUNK) == 0
    b0 = row_off // t
    nc = t // GLA_CHUNK
    seq = lambda w: pl.BlockSpec((t, w), lambda b: (b0 + b, 0))
    in_specs = [seq(GLA_QK), seq(GLA_QK), seq(GLA_VW), seq(2 * GLA_QK), seq(GLA_VW)]
    args = [gq, gk, gv, la, og]
    if state is not None:
        in_specs.append(pl.BlockSpec((None, None, 2, GLA_HEADS, GLA_DK, GLA_DV),
                                     lambda b: (b, layer_j, 0, 0, 0, 0)))
        args.append(state)
    in_specs += [_resident(tri.shape, _const_map(3)), _resident(gg.shape, _const_map(2))]
    args += [tri, gg]
    return pl.pallas_call(
        functools.partial(_gla_kernel, has_state=state is not None),
        out_shape=[jax.ShapeDtypeStruct((nbatch * t, GLA_VW), BF16),
                   jax.ShapeDtypeStruct((nbatch, 2, GLA_HEADS, GLA_DK, GLA_DV), F32)],
        grid=(nbatch,),
        in_specs=in_specs,
        out_specs=[pl.BlockSpec((t, GLA_VW), lambda b: (b, 0)),
                   pl.BlockSpec((None, 2, GLA_HEADS, GLA_DK, GLA_DV), lambda b: (b, 0, 0, 0, 0))],
        scratch_shapes=[
            pltpu.VMEM((t, GLA_VW), F32),
            pltpu.VMEM((2, nc, GLA_HEADS * GLA_CHUNK, GLA_QK), BF16),
            pltpu.VMEM((2, nc, GLA_QK, GLA_DV), F32),
            pltpu.VMEM((2, nc, GLA_QK, GLA_DV), F32),
            pltpu.VMEM((2, nc, GLA_QK, GLA_DV), BF16),
            pltpu.VMEM((2, GLA_QK, GLA_DV), F32),
        ],
        compiler_params=_cparams("arbitrary"),
        name="gla_scan",
    )(*args)


def _mla_kernel(*refs, has_ctx):
    it = iter(refs)
    q_ref, k_ref, v_ref = next(it), next(it), next(it)
    kc_ref, vc_ref = (next(it), next(it)) if has_ctx else (None, None)
    o_ref = next(it)
    scale = MLA_QK_DIM ** -0.5
    pair_w = 2 * MLA_V
    for hp in range(MLA_HEADS // 2):
        vp = v_ref[:, hp * pair_w:(hp + 1) * pair_w]
        vcp = vc_ref[:, hp * pair_w:(hp + 1) * pair_w] if has_ctx else None
        outs = []
        for hh in range(2):
            sl = slice((2 * hp + hh) * HEAD_PAD, (2 * hp + hh + 1) * HEAD_PAD)
            q = q_ref[:, sl]
            s = _dot_nt(q, k_ref[:, sl]) * scale
            mx = jnp.max(s, axis=-1, keepdims=True)
            if has_ctx:
                sc = _dot_nt(q, kc_ref[:, sl]) * scale
                mx = jnp.maximum(mx, jnp.max(sc, axis=-1, keepdims=True))
            p = jnp.exp(s - mx)
            den = jnp.sum(p, axis=-1, keepdims=True)
            o = _dot(p.astype(BF16), vp)
            if has_ctx:
                pc = jnp.exp(sc - mx)
                den = den + jnp.sum(pc, axis=-1, keepdims=True)
                o = o + _dot(pc.astype(BF16), vcp)
            outs.append(o / den)
        lane = lax.broadcasted_iota(jnp.int32, outs[0].shape, 1)
        o_ref[:, hp * pair_w:(hp + 1) * pair_w] = jnp.where(lane < MLA_V, outs[0], outs[1]).astype(BF16)


def _mla(qm, km, vm, kc, vc, layer_j, nbatch, t, row_off, tq):
    nq = t // tq
    q0, b0 = row_off // tq, row_off // t
    qspec = lambda w: pl.BlockSpec((tq, w), lambda b, i: (q0 + b * nq + i, 0))
    kspec = lambda w: pl.BlockSpec((t, w), lambda b, i: (b0 + b, 0))
    in_specs = [qspec(MLA_HEADS * HEAD_PAD), kspec(MLA_HEADS * HEAD_PAD), kspec(MLA_HEADS * MLA_V)]
    args = [qm, km, vm]
    if kc is not None:
        p = kc.shape[2]
        in_specs += [pl.BlockSpec((None, None, p, kc.shape[3]), lambda b, i: (layer_j, b, 0, 0)),
                     pl.BlockSpec((None, None, p, vc.shape[3]), lambda b, i: (layer_j, b, 0, 0))]
        args += [kc, vc]
    return pl.pallas_call(
        functools.partial(_mla_kernel, has_ctx=kc is not None),
        out_shape=jax.ShapeDtypeStruct((nbatch * t, MLA_HEADS * MLA_V), BF16),
        grid=(nbatch, nq),
        in_specs=in_specs,
        out_specs=pl.BlockSpec((tq, MLA_HEADS * MLA_V), lambda b, i: (b * nq + i, 0)),
        compiler_params=_cparams("arbitrary", "arbitrary"),
        name="mla_attention",
    )(*args)


def _swap_signed(a):
    pairs = a.reshape(a.shape[:-1] + (a.shape[-1] // 2, 2))
    return jnp.stack([-pairs[..., 1], pairs[..., 0]], axis=-1).reshape(a.shape)


def _swap_pairs(a):
    pairs = a.reshape(a.shape[:-1] + (a.shape[-1] // 2, 2))
    return pairs[..., ::-1].reshape(a.shape)


def _pack_even_weights(j, cos, sin, even_w_in, gla_gate_w2, gla_gate_b, mla_qa_g, mla_qb_w, mla_qn_g,
                       mla_kva_g, mla_kvb_w, mla_kn_g):
    d = even_w_in.shape[1]
    w = even_w_in[j]
    o = 0
    parts = {}
    for name, width in (("q", GLA_QK), ("k", GLA_QK), ("v", GLA_VW), ("og", GLA_VW),
                        ("gl", 2 * GLA_GATE_RANK), ("cq", MLA_Q_RANK), ("ckv", MLA_KV_RANK),
                        ("kr", MLA_ROPE)):
        parts[name] = w[:, o:o + width]
        o += width
    zeros = lambda n: jnp.zeros((d, n), w.dtype)
    kr, kr_sw = parts["kr"], _swap_signed(parts["kr"])
    blk_a = jnp.concatenate([parts["gl"], zeros(MLA_NOPE - 2 * GLA_GATE_RANK), kr, kr_sw], axis=1)
    blk_b = jnp.concatenate([zeros(MLA_NOPE), kr_sw, kr], axis=1)
    w_in = jnp.concatenate([parts["q"], parts["k"], parts["v"], parts["og"], parts["cq"],
                            parts["ckv"], blk_a, blk_b], axis=1).astype(BF16)

    w2 = jnp.zeros((HEAD_PAD, 2 * GLA_QK), F32)
    w2 = w2.at[:GLA_GATE_RANK, :GLA_QK].set(gla_gate_w2[j, 0])
    w2 = w2.at[GLA_GATE_RANK:2 * GLA_GATE_RANK, GLA_QK:].set(gla_gate_w2[j, 1])

    pad_head = lambda a: jnp.pad(a, [(0, 0)] * (a.ndim - 1) + [(0, HEAD_PAD - a.shape[-1])])
    qb = mla_qb_w[j]
    qb = jnp.concatenate([qb, _swap_signed(qb[..., MLA_NOPE:])], axis=-1)
    qb = qb.reshape(MLA_Q_RANK, MLA_HEADS * HEAD_PAD)
    kvb = mla_kvb_w[j]
    kvb_k = pad_head(kvb[..., :MLA_NOPE]).reshape(MLA_KV_RANK, MLA_HEADS * HEAD_PAD)
    kvb_v = kvb[..., MLA_NOPE:].reshape(MLA_KV_RANK, MLA_HEADS * MLA_V)

    def rot_tables(g):
        n = cos.shape[0]
        g_n = jnp.broadcast_to(g[:MLA_NOPE], (n, MLA_NOPE))
        g_c = g[MLA_NOPE:] * cos
        g_s = _swap_pairs(g[MLA_NOPE:]) * sin
        return g_n, g_c, g_s

    qn_n, qn_c, qn_s = rot_tables(mla_qn_g[j])
    kn_n, kn_c, kn_s = rot_tables(mla_kn_g[j])
    kzero = jnp.zeros_like(kn_n)
    return {
        "w_in": w_in,
        "w2": w2.astype(BF16),
        "gate_b": gla_gate_b[j].reshape(1, 2 * GLA_QK),
        "qa_g": mla_qa_g[j][None, :],
        "qb": qb.astype(BF16),
        "kva_g": mla_kva_g[j][None, :],
        "kvb_k": kvb_k.astype(BF16),
        "kvb_v": kvb_v.astype(BF16),
        "kn_g": pad_head(mla_kn_g[j][:MLA_NOPE])[None, :],
        "kn_g_ctx": jnp.concatenate([mla_kn_g[j], mla_kn_g[j][MLA_NOPE:]])[None, :],
        "t_q": jnp.concatenate([qn_n, qn_c, qn_s], axis=1),
        "t_a": jnp.concatenate([kzero, kn_c, kn_s], axis=1),
        "t_b": jnp.concatenate([kzero, kn_s, kn_c], axis=1),
    }


def _rope_tables(n, tm):
    pairs = MLA_ROPE // 4
    pos = np.arange(n)
    inv = ROPE_BASE ** (-jnp.arange(pairs, dtype=F32) / pairs)
    row = jnp.asarray(pos // GRID_W, F32)
    col = jnp.asarray(pos % GRID_W, F32)
    ang = jnp.concatenate([row[:, None] * inv, col[:, None] * inv], axis=-1)
    cos = jnp.repeat(jnp.cos(ang), 2, axis=-1)
    sin = jnp.repeat(jnp.sin(ang), 2, axis=-1)
    cos = jnp.pad(cos, ((tm, 0), (0, 0)), constant_values=1.0)
    sin = jnp.pad(sin, ((tm, 0), (0, 0)))
    return cos, sin


def _tri_masks():
    r = np.arange(GLA_CHUNK)
    lower = (r[None, :] <= r[:, None]).astype(np.float32)
    return jnp.asarray(np.stack([lower, lower.T]), BF16)


def _pick_tile(pref, *sizes):
    tm = pref
    while any(s % tm for s in sizes):
        tm //= 2
    return tm


def kernel(x_prompt, x_sample, c, c_ctx, cache_ckv, cache_krope, state_gla, ada_w, ada_b, norm_g,
           ffn1_wg, ffn1_wu, ffn1_wd, ffn2_wg, ffn2_wu, ffn2_wd, even_w_in, even_w_out,
           gla_gate_w2, gla_gate_b, gla_norm_g, mla_qa_g, mla_qb_w, mla_kva_g, mla_kvb_w,
           mla_qn_g, mla_kn_g, odd_w_in, odd_v_g, odd_ws, odd_bs, odd_w_out):
    batch, seq, d = x_prompt.shape
    dec_batch, dec_seq, _ = x_sample.shape
    depth = ada_w.shape[0]
    n_even = even_w_in.shape[0]
    m_ctx, m_dec = batch * seq, dec_batch * dec_seq
    assert 1 + dec_batch <= COND_PAD and seq % CMLP_CHUNK == 0 and dec_seq % CMLP_CHUNK == 0

    rows_ffn = _Rows(m_ctx, dec_batch, dec_seq, _pick_tile(512, m_ctx, dec_seq))
    rows_mix = _Rows(m_ctx, dec_batch, dec_seq, _pick_tile(256, m_ctx, dec_seq))

    cond = jnp.concatenate([c_ctx[None, :], c, jnp.zeros((COND_PAD - 1 - dec_batch, d), F32)], axis=0)
    mod = _modulation_all(cond, ada_w, ada_b).reshape(depth, COND_PAD, N_MOD, d)

    cos_t, sin_t = _rope_tables(dec_seq, rows_mix.tm)
    tri = _tri_masks()
    krope_blk = jnp.concatenate([jnp.zeros(cache_krope.shape[:-1] + (MLA_NOPE,), F32), cache_krope,
                                 cache_krope], axis=-1)
    even_wts = [_pack_even_weights(j, cos_t, sin_t, even_w_in, gla_gate_w2, gla_gate_b, mla_qa_g, mla_qb_w,
                                   mla_qn_g, mla_kva_g, mla_kvb_w, mla_kn_g) for j in range(n_even)]
    kc, vc = _ctx_kv(cache_ckv, krope_blk,
                     jnp.stack([w["kvb_k"] for w in even_wts]),
                     jnp.stack([w["kvb_v"] for w in even_wts]),
                     jnp.stack([w["kn_g_ctx"] for w in even_wts]))

    ffn1 = [w.astype(BF16) for w in (ffn1_wg, ffn1_wu, ffn1_wd)]
    ffn2 = [w.astype(BF16) for w in (ffn2_wg, ffn2_wu, ffn2_wd)]
    w_out_bf = even_w_out.astype(BF16)
    odd_wts = (odd_w_in.astype(BF16), odd_v_g[:, None, :], odd_ws.astype(BF16),
               jnp.swapaxes(odd_bs, 1, 2), odd_w_out.astype(BF16))

    x = (x_prompt.reshape(m_ctx, d), x_sample.reshape(m_dec, d))
    new_ckv, new_krope, new_gla = [], [], []
    for i in range(depth):
        j = i // 2
        x = _ffn(x, mod, i, 0, norm_g[i, 0][None, :], *ffn1, rows_ffn)
        g_mix = norm_g[i, 1][None, :]
        premix = None
        if i % 2 == 0:
            gq, gk, gv, og, la, qm, km, vm, ckv, kr = _even_in(x, mod, i, g_mix, even_wts[j], rows_mix)
            gg = gla_norm_g[j][None, :]
            gla_ctx, st = _gla(gq, gk, gv, la, og, None, j, tri, gg, batch, seq, 0)
            gla_dec, _ = _gla(gq, gk, gv, la, og, state_gla, j, tri, gg, dec_batch, dec_seq, m_ctx)
            mla_ctx = _mla(qm, km, vm, None, None, j, batch, seq, 0, _pick_tile(256, seq))
            mla_dec = _mla(qm, km, vm, kc, vc, j, dec_batch, dec_seq, m_ctx, _pick_tile(256, dec_seq))
            premix = ((gla_ctx, gla_dec), (mla_ctx, mla_dec), w_out_bf, j)
            new_ckv.append(ckv.reshape(batch, seq, MLA_KV_RANK))
            new_krope.append(kr.reshape(batch, seq, MLA_ROPE))
            new_gla.append(st)
        else:
            x = _odd_mixer(x, mod, i, j, g_mix, *odd_wts, rows_mix)
        x = _ffn(x, mod, i, 6, norm_g[i, 2][None, :], *ffn2, rows_ffn, split_out=(i == depth - 1),
                 premix=premix)

    y_prompt, y_sample = x
    return (y_prompt.reshape(batch, seq, d), y_sample.reshape(dec_batch, dec_seq, d),
            jnp.stack(new_ckv, axis=1), jnp.stack(new_krope, axis=1), jnp.stack(new_gla, axis=1))
```

```python
import functools
import math

import numpy as np
import jax
import jax.numpy as jnp
from jax import lax
from jax.experimental import pallas as pl
from jax.experimental.pallas import tpu as pltpu

F32 = jnp.float32
BF16 = jnp.bfloat16

EPS = 1e-6
N_MOD = 9
GRID_W = 64
ROPE_BASE = 10000.0
GLA_HEADS = 4
GLA_DK = 64
GLA_DV = 128
GLA_QK = GLA_HEADS * GLA_DK
GLA_VW = GLA_HEADS * GLA_DV
GLA_GATE_RANK = 16
GLA_GATE_NORM = 16.0
GLA_CHUNK = 64
MLA_HEADS = 8
MLA_NOPE = 64
MLA_ROPE = 32
MLA_V = 64
MLA_QK_DIM = MLA_NOPE + MLA_ROPE
MLA_Q_RANK = 384
MLA_KV_RANK = 256
HEAD_PAD = 128
CMLP_CHUNK = 128
CMLP_GROUPS = 4
COND_PAD = 16

VMEM_LIMIT = 56 * 1024 * 1024


def _cparams(*sem):
    return pltpu.CompilerParams(dimension_semantics=sem, vmem_limit_bytes=VMEM_LIMIT)


def _resident(shape, index_map):
    return pl.BlockSpec(shape, index_map, pipeline_mode=pl.Buffered(1))


def _const_map(nd):
    return lambda *_: (0,) * nd


def _layer_spec(stack, layer):
    nd = stack.ndim - 1
    return _resident((None,) + stack.shape[1:], lambda *_: (layer,) + (0,) * nd)


def _silu(x):
    return x * jax.nn.sigmoid(x)


def _rms(x, g):
    return x * lax.rsqrt(jnp.mean(x * x, axis=-1, keepdims=True) + EPS) * g


def _modulate(x, g, shift, scale):
    return _rms(x, g) * (1.0 + scale) + shift


def _dot(a, b):
    return jnp.dot(a, b, preferred_element_type=F32)


def _dot_nt(a, b):
    return lax.dot_general(a, b, (((1,), (1,)), ((), ())), preferred_element_type=F32)


def _mod_kernel(c_ref, w_ref, b_ref, o_ref):
    s = _silu(c_ref[...]).astype(BF16)
    o_ref[...] = _dot(s, w_ref[...].astype(BF16)) + b_ref[...]


def _modulation_all(cond, ada_w, ada_b):
    depth, d, n = ada_w.shape
    tn = n // 4
    return pl.pallas_call(
        _mod_kernel,
        out_shape=jax.ShapeDtypeStruct((depth, COND_PAD, n), F32),
        grid=(depth, n // tn),
        in_specs=[
            pl.BlockSpec((COND_PAD, d), lambda l, j: (0, 0)),
            pl.BlockSpec((None, d, tn), lambda l, j: (l, 0, j)),
            pl.BlockSpec((None, 1, tn), lambda l, j: (l, 0, j)),
        ],
        out_specs=pl.BlockSpec((None, COND_PAD, tn), lambda l, j: (l, 0, j)),
        compiler_params=_cparams("arbitrary", "arbitrary"),
        name="adaln_modulation",
    )(cond, ada_w, ada_b.reshape(depth, 1, n))


class _Rows:
    def __init__(self, m_ctx, dec_batch, dec_seq, tm):
        assert m_ctx % tm == 0 and dec_seq % tm == 0
        self.tm = tm
        self.m_ctx = m_ctx
        self.m = m_ctx + dec_batch * dec_seq
        self.nb_ctx = m_ctx // tm
        self.nb_seq = dec_seq // tm
        self.nblocks = self.m // tm

    def cond(self, i):
        return jnp.where(i < self.nb_ctx, 0, 1 + (i - self.nb_ctx) // self.nb_seq)

    def rope_block(self, i):
        return jnp.where(i < self.nb_ctx, 0, 1 + (i - self.nb_ctx) % self.nb_seq)

    def mod_spec(self, layer, d):
        return pl.BlockSpec((None, None, N_MOD, d), lambda i: (layer, self.cond(i), 0, 0))

    def row_spec(self, width):
        return pl.BlockSpec((self.tm, width), lambda i: (i, 0))

    def ctx_spec(self, width):
        return pl.BlockSpec((self.tm, width), lambda i: (jnp.minimum(i, self.nb_ctx - 1), 0))

    def dec_spec(self, width):
        return pl.BlockSpec((self.tm, width), lambda i: (jnp.maximum(i - self.nb_ctx, 0), 0))

    def split_specs(self, width):
        return [self.ctx_spec(width), self.dec_spec(width)]

    def split_shapes(self, width, dtype):
        return [jax.ShapeDtypeStruct((self.m_ctx, width), dtype),
                jax.ShapeDtypeStruct((self.m - self.m_ctx, width), dtype)]


def _load_split(refs, is_ctx):
    if len(refs) == 1:
        return refs[0][...]
    return jnp.where(is_ctx, refs[0][...], refs[1][...])


def _ffn_kernel(*refs, row0, nb_ctx, split_in, split_out, premix):
    it = iter(refs)
    x_refs = [next(it) for _ in range(2 if split_in else 1)]
    mod_ref, g_ref, wg_ref, wu_ref, wd_ref = (next(it) for _ in range(5))
    if premix:
        a_refs, b_refs = [next(it), next(it)], [next(it), next(it)]
        wo_ref = next(it)
    o_refs = [next(it) for _ in range(2 if split_out else 1)]
    is_ctx = pl.program_id(0) < nb_ctx
    x = _load_split(x_refs, is_ctx)
    if premix:
        half = wo_ref.shape[0] // 2
        mix = _dot(_load_split(a_refs, is_ctx), wo_ref[:half, :])
        mix = mix + _dot(_load_split(b_refs, is_ctx), wo_ref[half:, :])
        x = x + mod_ref[5:6, :] * mix
    shift = mod_ref[row0:row0 + 1, :]
    scale = mod_ref[row0 + 1:row0 + 2, :]
    gate = mod_ref[row0 + 2:row0 + 3, :]
    h = _modulate(x, g_ref[...], shift, scale).astype(BF16)
    a = _dot(h, wg_ref[...])
    u = _dot(h, wu_ref[...])
    act = (_silu(a) * u).astype(BF16)
    y = _dot(act, wd_ref[...])
    out = x + (0.5 * gate) * y
    if split_out:
        @pl.when(is_ctx)
        def _():
            o_refs[0][...] = out

        @pl.when(jnp.logical_not(is_ctx))
        def _():
            o_refs[1][...] = out
    else:
        o_refs[0][...] = out


def _ffn(xs, mod, layer, row0, g, wg, wu, wd, rows, split_out=False, premix=None):
    split_in = isinstance(xs, (tuple, list))
    xs = list(xs) if split_in else [xs]
    d = xs[0].shape[1]
    in_specs = (rows.split_specs(d) if split_in else [rows.row_spec(d)]) + [
        rows.mod_spec(layer, d),
        _resident((1, d), _const_map(2)),
        _layer_spec(wg, layer), _layer_spec(wu, layer), _layer_spec(wd, layer),
    ]
    args = xs + [mod, g, wg, wu, wd]
    if premix is not None:
        gla, mla, w_out, j = premix
        in_specs += rows.split_specs(gla[0].shape[1]) + rows.split_specs(mla[0].shape[1])
        in_specs.append(_layer_spec(w_out, j))
        args += list(gla) + list(mla) + [w_out]
    return pl.pallas_call(
        functools.partial(_ffn_kernel, row0=row0, nb_ctx=rows.nb_ctx, split_in=split_in,
                          split_out=split_out, premix=premix is not None),
        out_shape=rows.split_shapes(d, F32) if split_out else jax.ShapeDtypeStruct((rows.m, d), F32),
        grid=(rows.nblocks,),
        in_specs=in_specs,
        out_specs=rows.split_specs(d) if split_out else rows.row_spec(d),
        compiler_params=_cparams("arbitrary"),
        name="ffn_swiglu",
    )(*args)


def _gelu_tanh(x):
    c = math.sqrt(2.0 / math.pi)
    return x * (0.5 * (1.0 + jnp.tanh(c * (x + 0.044715 * (x * x * x)))))


def _odd_kernel(x_ref, mod_ref, g_ref, win_ref, vg_ref, ws_ref, bst_ref, wout_ref, o_ref):
    x = x_ref[...]
    tm = x.shape[0]
    shift, scale, gate = mod_ref[3:4, :], mod_ref[4:5, :], mod_ref[5:6, :]
    h = _modulate(x, g_ref[...], shift, scale).astype(BF16)
    uv = _gelu_tanh(_dot(h, win_ref[...]))
    width = uv.shape[1] // 2
    u = uv[:, :width]
    v = _rms(uv[:, width:], vg_ref[...]).astype(BF16)
    gw = width // CMLP_GROUPS
    chunks = []
    for c in range(tm // CMLP_CHUNK):
        r0 = c * CMLP_CHUNK
        groups = []
        for g in range(CMLP_GROUPS):
            vg = v[r0:r0 + CMLP_CHUNK, g * gw:(g + 1) * gw]
            mixed = _dot(ws_ref[g], vg) + bst_ref[:, g:g + 1]
            groups.append(u[r0:r0 + CMLP_CHUNK, g * gw:(g + 1) * gw] * mixed)
        chunks.append(jnp.concatenate(groups, axis=1))
    z = jnp.concatenate(chunks, axis=0).astype(BF16)
    o_ref[...] = x + gate * _dot(z, wout_ref[...])


def _odd_mixer(x, mod, layer, layer_j, g, w_in, v_g, w_s, b_s_t, w_out, rows):
    m, d = x.shape
    return pl.pallas_call(
        _odd_kernel,
        out_shape=jax.ShapeDtypeStruct((m, d), F32),
        grid=(rows.nblocks,),
        in_specs=[
            rows.row_spec(d),
            rows.mod_spec(layer, d),
            _resident((1, d), _const_map(2)),
            _layer_spec(w_in, layer_j), _layer_spec(v_g, layer_j), _layer_spec(w_s, layer_j),
            _layer_spec(b_s_t, layer_j), _layer_spec(w_out, layer_j),
        ],
        out_specs=rows.row_spec(d),
        compiler_params=_cparams("arbitrary"),
        name="odd_gmlp",
    )(x, mod, g, w_in, v_g, w_s, b_s_t, w_out)


_C_Q = 0
_C_K = _C_Q + GLA_QK
_C_V = _C_K + GLA_QK
_C_OG = _C_V + GLA_VW
_C_CQ = _C_OG + GLA_VW
_C_CKV = _C_CQ + MLA_Q_RANK
_C_KA = _C_CKV + MLA_KV_RANK
_C_KB = _C_KA + HEAD_PAD
_C_END = _C_KB + HEAD_PAD


def _head_scale(x, live):
    ss = jnp.sum(jnp.where(live, x * x, 0.0), axis=-1, keepdims=True)
    return lax.rsqrt(ss * (1.0 / MLA_QK_DIM) + EPS)


def _even_in_kernel(x_ref, mod_ref, g_ref, win_ref, w2_ref, gb_ref, qag_ref, qb_ref, kvag_ref,
                    kvbk_ref, kvbv_ref, kng_ref, tq_ref, ta_ref, tb_ref,
                    gq_ref, gk_ref, gv_ref, og_ref, la_ref, qm_ref, km_ref, vm_ref, ckv_ref, kr_ref,
                    *, nb_ctx):
    is_ctx = pl.program_id(0) < nb_ctx
    x = x_ref[...]
    shift, scale = mod_ref[3:4, :], mod_ref[4:5, :]
    h = _modulate(x, g_ref[...], shift, scale).astype(BF16)
    proj = _dot(h, win_ref[...])

    gq_ref[...] = proj[:, _C_Q:_C_K] * (GLA_DK ** -0.5)
    gk_ref[...] = proj[:, _C_K:_C_V]
    gv_ref[...] = proj[:, _C_V:_C_OG].astype(BF16)
    og_ref[...] = proj[:, _C_OG:_C_CQ]

    ka = proj[:, _C_KA:_C_KB]
    kb = proj[:, _C_KB:_C_END]
    ckv = _rms(proj[:, _C_CKV:_C_KA], kvag_ref[...])

    @pl.when(is_ctx)
    def _():
        ckv_ref[...] = ckv
        kr_ref[...] = ka[:, MLA_NOPE:MLA_QK_DIM]

    z = _dot(ka.astype(BF16), w2_ref[...]) + gb_ref[...]
    la_ref[...] = -(jnp.maximum(-z, 0.0) + jnp.log1p(jnp.exp(-jnp.abs(z)))) * (1.0 / GLA_GATE_NORM)

    lane = lax.broadcasted_iota(jnp.int32, ka.shape, 1)
    live = lane < MLA_QK_DIM

    cq = _rms(proj[:, _C_CQ:_C_CKV], qag_ref[...]).astype(BF16)
    qn = _dot(cq, qb_ref[...])
    tq = tq_ref[...]
    for hd in range(MLA_HEADS):
        sl = slice(hd * HEAD_PAD, (hd + 1) * HEAD_PAD)
        qh = qn[:, sl]
        qm_ref[:, sl] = (qh * tq * _head_scale(qh, live)).astype(BF16)

    ckv_bf = ckv.astype(BF16)
    vm_ref[...] = _dot(ckv_bf, kvbv_ref[...]).astype(BF16)
    kn = _dot(ckv_bf, kvbk_ref[...])
    is_rope = live & (lane >= MLA_NOPE)
    ss_rope = jnp.sum(jnp.where(is_rope, ka * ka, 0.0), axis=-1, keepdims=True)
    rot = ka * ta_ref[...] + kb * tb_ref[...]
    kng = kng_ref[...]
    for hd in range(MLA_HEADS):
        sl = slice(hd * HEAD_PAD, (hd + 1) * HEAD_PAD)
        kh = kn[:, sl]
        ss = jnp.sum(kh * kh, axis=-1, keepdims=True) + ss_rope
        r = lax.rsqrt(ss * (1.0 / MLA_QK_DIM) + EPS)
        km_ref[:, sl] = ((kh * kng + rot) * r).astype(BF16)


def _even_in(x, mod, layer, g, wts, rows):
    m, d = x.shape
    tm = rows.tm
    c2 = _const_map(2)
    rope_spec = pl.BlockSpec((tm, HEAD_PAD), lambda i: (rows.rope_block(i), 0))
    widths = [GLA_QK, GLA_QK, GLA_VW, GLA_VW, 2 * GLA_QK, MLA_HEADS * HEAD_PAD, MLA_HEADS * HEAD_PAD,
              MLA_HEADS * MLA_V]
    dtypes = [F32, F32, BF16, F32, F32, BF16, BF16, BF16]
    names = ["w_in", "w2", "gate_b", "qa_g", "qb", "kva_g", "kvb_k", "kvb_v", "kn_g"]
    return pl.pallas_call(
        functools.partial(_even_in_kernel, nb_ctx=rows.nb_ctx),
        out_shape=[jax.ShapeDtypeStruct((m, w), t) for w, t in zip(widths, dtypes)]
        + [jax.ShapeDtypeStruct((rows.m_ctx, MLA_KV_RANK), F32),
           jax.ShapeDtypeStruct((rows.m_ctx, MLA_ROPE), F32)],
        grid=(rows.nblocks,),
        in_specs=[rows.row_spec(d), rows.mod_spec(layer, d), _resident((1, d), c2)]
        + [_resident(wts[n].shape, c2) for n in names] + [rope_spec] * 3,
        out_specs=[rows.row_spec(w) for w in widths] + [rows.ctx_spec(MLA_KV_RANK), rows.ctx_spec(MLA_ROPE)],
        compiler_params=_cparams("arbitrary"),
        name="even_in_proj",
    )(x, mod, g, *[wts[n] for n in names], wts["t_q"], wts["t_a"], wts["t_b"])


def _ctx_kv_kernel(ckv_ref, kr_ref, kvbk_ref, kvbv_ref, kng_ref, k_ref, v_ref):
    ckv_bf = ckv_ref[...].astype(BF16)
    v_ref[...] = _dot(ckv_bf, kvbv_ref[...]).astype(BF16)
    kn = _dot(ckv_bf, kvbk_ref[...])
    kr = kr_ref[...]
    lane = lax.broadcasted_iota(jnp.int32, kr.shape, 1)
    live = lane < MLA_QK_DIM
    kng = kng_ref[...]
    for hd in range(MLA_HEADS):
        sl = slice(hd * HEAD_PAD, (hd + 1) * HEAD_PAD)
        kh = kn[:, sl] + kr
        k_ref[:, sl] = (kh * kng * _head_scale(kh, live)).astype(BF16)


def _ctx_kv(cache_ckv, cache_krope_blk, kvb_k, kvb_v, kn_g):
    nb, ne, p, r = cache_ckv.shape
    return pl.pallas_call(
        _ctx_kv_kernel,
        out_shape=[jax.ShapeDtypeStruct((ne, nb, p, MLA_HEADS * HEAD_PAD), BF16),
                   jax.ShapeDtypeStruct((ne, nb, p, MLA_HEADS * MLA_V), BF16)],
        grid=(ne, nb),
        in_specs=[
            pl.BlockSpec((None, None, p, r), lambda j, b: (b, j, 0, 0)),
            pl.BlockSpec((None, None, p, HEAD_PAD), lambda j, b: (b, j, 0, 0)),
            pl.BlockSpec((None,) + kvb_k.shape[1:], lambda j, b: (j, 0, 0)),
            pl.BlockSpec((None,) + kvb_v.shape[1:], lambda j, b: (j, 0, 0)),
            pl.BlockSpec((None,) + kn_g.shape[1:], lambda j, b: (j, 0, 0)),
        ],
        out_specs=[pl.BlockSpec((None, None, p, MLA_HEADS * HEAD_PAD), lambda j, b: (j, b, 0, 0)),
                   pl.BlockSpec((None, None, p, MLA_HEADS * MLA_V), lambda j, b: (j, b, 0, 0))],
        compiler_params=_cparams("arbitrary", "arbitrary"),
        name="ctx_kv",
    )(cache_ckv, cache_krope_blk, kvb_k, kvb_v, kn_g)


def _split3(x):
    hi = x.astype(BF16)
    r = x - hi.astype(F32)
    mid = r.astype(BF16)
    lo = (r - mid.astype(F32)).astype(BF16)
    return hi, mid, lo


def _block_row_bcast(a, blk, off):
    c, n = a.shape
    if blk >= 8:
        pieces = [jnp.broadcast_to(a[b * blk + off:b * blk + off + 1, :], (blk, n))
                  for b in range(c // blk)]
        return pieces[0] if len(pieces) == 1 else jnp.concatenate(pieces, axis=0)
    a3 = a.reshape(c // 8, 8, n)
    sub = lax.broadcasted_iota(jnp.int32, a3.shape, 1) // blk
    out = jnp.broadcast_to(a3[:, off:off + 1, :], a3.shape)
    for s in range(1, 8 // blk):
        cand = jnp.broadcast_to(a3[:, s * blk + off:s * blk + off + 1, :], a3.shape)
        out = jnp.where(sub == s, cand, out)
    return out.reshape(c, n)


def _head_stack(a, head_w):
    lane_head = lax.broadcasted_iota(jnp.int32, a.shape, 1) // head_w
    zero = jnp.zeros_like(a)
    return jnp.concatenate([jnp.where(lane_head == h, a, zero) for h in range(GLA_HEADS)], axis=0)


def _gla_pair_weights(q, k, la_f, la_b, cum_f, cum_b):
    C = GLA_CHUNK
    row = lax.broadcasted_iota(jnp.int32, (C, 1), 0)
    t_idx = lax.broadcasted_iota(jnp.int32, (C, GLA_HEADS * C), 0)
    s_idx = lax.broadcasted_iota(jnp.int32, (C, GLA_HEADS * C), 1) % C
    k_stack = _head_stack(k.astype(BF16), GLA_DK)
    att = jnp.where(t_idx == s_idx, 2.0 * _dot_nt(q.astype(BF16), k_stack), 0.0)
    m = C // 2
    while m >= 1:
        blk = 2 * m
        upper = (row % blk) >= m
        if m == 1:
            qm = (q * jnp.exp(jnp.where(upper, la_f, la_b))).astype(BF16)
            km_stack = k_stack
        else:
            ref_f = _block_row_bcast(cum_f, blk, m - 1)
            ref_b = _block_row_bcast(cum_b, blk, m)
            eq = jnp.where(upper, cum_f - ref_f, cum_b - ref_b)
            ek = jnp.where(upper, ref_b - cum_b, ref_f - cum_f)
            qm = (q * jnp.exp(eq)).astype(BF16)
            km_stack = _head_stack((k * jnp.exp(ek)).astype(BF16), GLA_DK)
        pair = ((t_idx // blk) == (s_idx // blk)) & ((t_idx // m) != (s_idx // m))
        att = jnp.where(pair, _dot_nt(qm, km_stack), att)
        m //= 2
    return att


def _gla_phase_a(c, q_ref, k_ref, v_ref, la_ref, tri_ref, oacc_ref, qst_ref, ds_ref, dc_ref):
    C = GLA_CHUNK
    r0 = pl.multiple_of(c * C, C)
    q = q_ref[pl.ds(r0, C), :]
    k = k_ref[pl.ds(r0, C), :]
    v = v_ref[pl.ds(r0, C), :]
    la = la_ref[pl.ds(r0, C), :]
    cums = []
    for d in range(2):
        hi, mid, lo = _split3(la[:, d * GLA_QK:(d + 1) * GLA_QK])
        tri = tri_ref[d]
        cums.append(_dot(tri, hi) + _dot(tri, mid) + _dot(tri, lo))
    att = _gla_pair_weights(q, k, la[:, :GLA_QK], la[:, GLA_QK:], cums[0], cums[1])
    oacc_ref[pl.ds(r0, C), :] = _dot(att.astype(BF16), _head_stack(v, GLA_DV))
    for d, cum in enumerate(cums):
        last = cum[C - 1:C, :] if d == 0 else cum[0:1, :]
        qst_ref[d, c] = _head_stack((q * jnp.exp(cum)).astype(BF16), GLA_DK)
        kct = (k * jnp.exp(last - cum)).T.astype(BF16)
        ds_ref[d, c] = jnp.concatenate(
            [_dot(kct[h * GLA_DK:(h + 1) * GLA_DK, :], v[:, h * GLA_DV:(h + 1) * GLA_DV])
             for h in range(GLA_HEADS)], axis=0)
        decay_col = jnp.exp(jnp.broadcast_to(last, (8, GLA_QK))).T[:, 0:1]
        dc_ref[d, c] = jnp.broadcast_to(decay_col, (GLA_QK, GLA_DV))


def _gla_phase_c(c, og_ref, gg, oacc_ref, qst_ref, sb_ref, o_ref):
    C = GLA_CHUNK
    r0 = pl.multiple_of(c * C, C)
    q_cat = jnp.concatenate([qst_ref[0, c], qst_ref[1, c]], axis=1)
    s_cat = jnp.concatenate([sb_ref[0, c], sb_ref[1, c]], axis=0)
    inter = _dot(q_cat, s_cat)
    for h in range(GLA_HEADS):
        sl = slice(h * GLA_DV, (h + 1) * GLA_DV)
        o = oacc_ref[pl.ds(r0, C), sl] + inter[h * C:(h + 1) * C, :]
        o_ref[pl.ds(r0, C), sl] = (_rms(o, gg) * _silu(og_ref[pl.ds(r0, C), sl])).astype(BF16)


def _gla_kernel(*refs, has_state):
    it = iter(refs)
    q_ref, k_ref, v_ref, la_ref, og_ref = (next(it) for _ in range(5))
    s0_ref = next(it) if has_state else None
    tri_ref, gg_ref, o_ref, sout_ref = (next(it) for _ in range(4))
    oacc_ref, qst_ref, ds_ref, dc_ref, sb_ref, st_ref = (next(it) for _ in range(6))

    nc = q_ref.shape[0] // GLA_CHUNK
    per_step = 4

    def phase_a(i, carry):
        for u in range(per_step):
            _gla_phase_a(per_step * i + u, q_ref, k_ref, v_ref, la_ref, tri_ref, oacc_ref, qst_ref,
                         ds_ref, dc_ref)
        return carry

    lax.fori_loop(0, nc // per_step, phase_a, 0)

    for d in range(2):
        for h in range(GLA_HEADS):
            rows = slice(h * GLA_DK, (h + 1) * GLA_DK)
            st_ref[d, rows, :] = s0_ref[d, h] if has_state else jnp.zeros((GLA_DK, GLA_DV), F32)

    def phase_b(c, carry):
        for d in range(2):
            cc = c if d == 0 else nc - 1 - c
            s = st_ref[d]
            sb_ref[d, cc] = s.astype(BF16)
            st_ref[d] = s * dc_ref[d, cc] + ds_ref[d, cc]
        return carry

    lax.fori_loop(0, nc, phase_b, 0)
    for d in range(2):
        for h in range(GLA_HEADS):
            sout_ref[d, h] = st_ref[d, h * GLA_DK:(h + 1) * GLA_DK, :]

    gg = gg_ref[...]

    def phase_c(i, carry):
        for u in range(per_step):
            _gla_phase_c(per_step * i + u, og_ref, gg, oacc_ref, qst_ref, sb_ref, o_ref)
        return carry

    lax.fori_loop(0, nc // per_step, phase_c, 0)


def _gla(gq, gk, gv, la, og, state, layer_j, tri, gg, nbatch, t, row_off):
    assert t % (2 * GLA_CHUNK) == 0
    b0 = row_off // t
    nc = t // GLA_CHUNK
    seq = lambda w: pl.BlockSpec((t, w), lambda b: (b0 + b, 0))
    in_specs = [seq(GLA_QK), seq(GLA_QK), seq(GLA_VW), seq(2 * GLA_QK), seq(GLA_VW)]
    args = [gq, gk, gv, la, og]
    if state is not None:
        in_specs.append(pl.BlockSpec((None, None, 2, GLA_HEADS, GLA_DK, GLA_DV),
                                     lambda b: (b, layer_j, 0, 0, 0, 0)))
        args.append(state)
    in_specs += [_resident(tri.shape, _const_map(3)), _resident(gg.shape, _const_map(2))]
    args += [tri, gg]
    return pl.pallas_call(
        functools.partial(_gla_kernel, has_state=state is not None),
        out_shape=[jax.ShapeDtypeStruct((nbatch * t, GLA_VW), BF16),
                   jax.ShapeDtypeStruct((nbatch, 2, GLA_HEADS, GLA_DK, GLA_DV), F32)],
        grid=(nbatch,),
        in_specs=in_specs,
        out_specs=[pl.BlockSpec((t, GLA_VW), lambda b: (b, 0)),
                   pl.BlockSpec((None, 2, GLA_HEADS, GLA_DK, GLA_DV), lambda b: (b, 0, 0, 0, 0))],
        scratch_shapes=[
            pltpu.VMEM((t, GLA_VW), F32),
            pltpu.VMEM((2, nc, GLA_HEADS * GLA_CHUNK, GLA_QK), BF16),
            pltpu.VMEM((2, nc, GLA_QK, GLA_DV), F32),
            pltpu.VMEM((2, nc, GLA_QK, GLA_DV), F32),
            pltpu.VMEM((2, nc, GLA_QK, GLA_DV), BF16),
            pltpu.VMEM((2, GLA_QK, GLA_DV), F32),
        ],
        compiler_params=_cparams("arbitrary"),
        name="gla_scan",
    )(*args)


def _mla_kernel(*refs, has_ctx):
    it = iter(refs)
    q_ref, k_ref, v_ref = next(it), next(it), next(it)
    kc_ref, vc_ref = (next(it), next(it)) if has_ctx else (None, None)
    o_ref = next(it)
    pair_w = 2 * MLA_V
    for hp in range(MLA_HEADS // 2):
        vp = v_ref[:, hp * pair_w:(hp + 1) * pair_w]
        vcp = vc_ref[:, hp * pair_w:(hp + 1) * pair_w] if has_ctx else None
        outs = []
        for hh in range(2):
            sl = slice((2 * hp + hh) * HEAD_PAD, (2 * hp + hh + 1) * HEAD_PAD)
            q = q_ref[:, sl]
            s = _dot_nt(q, k_ref[:, sl])
            mx = jnp.max(s, axis=-1, keepdims=True)
            if has_ctx:
                sc = _dot_nt(q, kc_ref[:, sl])
                mx = jnp.maximum(mx, jnp.max(sc, axis=-1, keepdims=True))
            p = jnp.exp2(s - mx)
            den = jnp.sum(p, axis=-1, keepdims=True)
            o = _dot(p.astype(BF16), vp)
            if has_ctx:
                pc = jnp.exp2(sc - mx)
                den = den + jnp.sum(pc, axis=-1, keepdims=True)
                o = o + _dot(pc.astype(BF16), vcp)
            outs.append(o / den)
        lane = lax.broadcasted_iota(jnp.int32, outs[0].shape, 1)
        o_ref[:, hp * pair_w:(hp + 1) * pair_w] = jnp.where(lane < MLA_V, outs[0], outs[1]).astype(BF16)


def _mla(qm, km, vm, kc, vc, layer_j, nbatch, t, row_off, tq):
    nq = t // tq
    q0, b0 = row_off // tq, row_off // t
    qspec = lambda w: pl.BlockSpec((tq, w), lambda b, i: (q0 + b * nq + i, 0))
    kspec = lambda w: pl.BlockSpec((t, w), lambda b, i: (b0 + b, 0))
    in_specs = [qspec(MLA_HEADS * HEAD_PAD), kspec(MLA_HEADS * HEAD_PAD), kspec(MLA_HEADS * MLA_V)]
    args = [qm, km, vm]
    if kc is not None:
        p = kc.shape[2]
        in_specs += [pl.BlockSpec((None, None, p, kc.shape[3]), lambda b, i: (layer_j, b, 0, 0)),
                     pl.BlockSpec((None, None, p, vc.shape[3]), lambda b, i: (layer_j, b, 0, 0))]
        args += [kc, vc]
    return pl.pallas_call(
        functools.partial(_mla_kernel, has_ctx=kc is not None),
        out_shape=jax.ShapeDtypeStruct((nbatch * t, MLA_HEADS * MLA_V), BF16),
        grid=(nbatch, nq),
        in_specs=in_specs,
        out_specs=pl.BlockSpec((tq, MLA_HEADS * MLA_V), lambda b, i: (b * nq + i, 0)),
        compiler_params=_cparams("arbitrary", "arbitrary"),
        name="mla_attention",
    )(*args)


def _swap_signed(a):
    pairs = a.reshape(a.shape[:-1] + (a.shape[-1] // 2, 2))
    return jnp.stack([-pairs[..., 1], pairs[..., 0]], axis=-1).reshape(a.shape)


def _swap_pairs(a):
    pairs = a.reshape(a.shape[:-1] + (a.shape[-1] // 2, 2))
    return pairs[..., ::-1].reshape(a.shape)


def _pack_even_weights(j, cos, sin, even_w_in, gla_gate_w2, gla_gate_b, mla_qa_g, mla_qb_w, mla_qn_g,
                       mla_kva_g, mla_kvb_w, mla_kn_g):
    d = even_w_in.shape[1]
    w = even_w_in[j]
    o = 0
    parts = {}
    for name, width in (("q", GLA_QK), ("k", GLA_QK), ("v", GLA_VW), ("og", GLA_VW),
                        ("gl", 2 * GLA_GATE_RANK), ("cq", MLA_Q_RANK), ("ckv", MLA_KV_RANK),
                        ("kr", MLA_ROPE)):
        parts[name] = w[:, o:o + width]
        o += width
    zeros = lambda n: jnp.zeros((d, n), w.dtype)
    kr, kr_sw = parts["kr"], _swap_signed(parts["kr"])
    blk_a = jnp.concatenate([parts["gl"], zeros(MLA_NOPE - 2 * GLA_GATE_RANK), kr, kr_sw], axis=1)
    blk_b = jnp.concatenate([zeros(MLA_NOPE), kr_sw, kr], axis=1)
    w_in = jnp.concatenate([parts["q"], parts["k"], parts["v"], parts["og"], parts["cq"],
                            parts["ckv"], blk_a, blk_b], axis=1).astype(BF16)

    w2 = jnp.zeros((HEAD_PAD, 2 * GLA_QK), F32)
    w2 = w2.at[:GLA_GATE_RANK, :GLA_QK].set(gla_gate_w2[j, 0])
    w2 = w2.at[GLA_GATE_RANK:2 * GLA_GATE_RANK, GLA_QK:].set(gla_gate_w2[j, 1])

    pad_head = lambda a: jnp.pad(a, [(0, 0)] * (a.ndim - 1) + [(0, HEAD_PAD - a.shape[-1])])
    qb = mla_qb_w[j]
    qb = jnp.concatenate([qb, _swap_signed(qb[..., MLA_NOPE:])], axis=-1)
    qb = qb.reshape(MLA_Q_RANK, MLA_HEADS * HEAD_PAD)
    kvb = mla_kvb_w[j]
    kvb_k = pad_head(kvb[..., :MLA_NOPE]).reshape(MLA_KV_RANK, MLA_HEADS * HEAD_PAD)
    kvb_v = kvb[..., MLA_NOPE:].reshape(MLA_KV_RANK, MLA_HEADS * MLA_V)

    def rot_tables(g):
        n = cos.shape[0]
        g_n = jnp.broadcast_to(g[:MLA_NOPE], (n, MLA_NOPE))
        g_c = g[MLA_NOPE:] * cos
        g_s = _swap_pairs(g[MLA_NOPE:]) * sin
        return g_n, g_c, g_s

    qn_n, qn_c, qn_s = rot_tables(mla_qn_g[j])
    kn_n, kn_c, kn_s = rot_tables(mla_kn_g[j])
    kzero = jnp.zeros_like(kn_n)
    return {
        "w_in": w_in,
        "w2": w2.astype(BF16),
        "gate_b": gla_gate_b[j].reshape(1, 2 * GLA_QK),
        "qa_g": mla_qa_g[j][None, :],
        "qb": qb.astype(BF16),
        "kva_g": mla_kva_g[j][None, :],
        "kvb_k": kvb_k.astype(BF16),
        "kvb_v": kvb_v.astype(BF16),
        "kn_g": pad_head(mla_kn_g[j][:MLA_NOPE])[None, :],
        "kn_g_ctx": jnp.concatenate([mla_kn_g[j], mla_kn_g[j][MLA_NOPE:]])[None, :],
        "t_q": jnp.concatenate([qn_n, qn_c, qn_s], axis=1) * (MLA_QK_DIM ** -0.5 * math.log2(math.e)),
        "t_a": jnp.concatenate([kzero, kn_c, kn_s], axis=1),
        "t_b": jnp.concatenate([kzero, kn_s, kn_c], axis=1),
    }


def _rope_tables(n, tm):
    pairs = MLA_ROPE // 4
    pos = np.arange(n)
    inv = ROPE_BASE ** (-jnp.arange(pairs, dtype=F32) / pairs)
    row = jnp.asarray(pos // GRID_W, F32)
    col = jnp.asarray(pos % GRID_W, F32)
    ang = jnp.concatenate([row[:, None] * inv, col[:, None] * inv], axis=-1)
    cos = jnp.repeat(jnp.cos(ang), 2, axis=-1)
    sin = jnp.repeat(jnp.sin(ang), 2, axis=-1)
    cos = jnp.pad(cos, ((tm, 0), (0, 0)), constant_values=1.0)
    sin = jnp.pad(sin, ((tm, 0), (0, 0)))
    return cos, sin


def _tri_masks():
    r = np.arange(GLA_CHUNK)
    lower = (r[None, :] <= r[:, None]).astype(np.float32)
    return jnp.asarray(np.stack([lower, lower.T]), BF16)


def _pick_tile(pref, *sizes):
    tm = pref
    while any(s % tm for s in sizes):
        tm //= 2
    return tm


def kernel(x_prompt, x_sample, c, c_ctx, cache_ckv, cache_krope, state_gla, ada_w, ada_b, norm_g,
           ffn1_wg, ffn1_wu, ffn1_wd, ffn2_wg, ffn2_wu, ffn2_wd, even_w_in, even_w_out,
           gla_gate_w2, gla_gate_b, gla_norm_g, mla_qa_g, mla_qb_w, mla_kva_g, mla_kvb_w,
           mla_qn_g, mla_kn_g, odd_w_in, odd_v_g, odd_ws, odd_bs, odd_w_out):
    batch, seq, d = x_prompt.shape
    dec_batch, dec_seq, _ = x_sample.shape
    depth = ada_w.shape[0]
    n_even = even_w_in.shape[0]
    m_ctx, m_dec = batch * seq, dec_batch * dec_seq
    assert 1 + dec_batch <= COND_PAD and seq % CMLP_CHUNK == 0 and dec_seq % CMLP_CHUNK == 0

    rows_ffn = _Rows(m_ctx, dec_batch, dec_seq, _pick_tile(512, m_ctx, dec_seq))
    rows_mix = _Rows(m_ctx, dec_batch, dec_seq, _pick_tile(512, m_ctx, dec_seq))

    cond = jnp.concatenate([c_ctx[None, :], c, jnp.zeros((COND_PAD - 1 - dec_batch, d), F32)], axis=0)
    mod = _modulation_all(cond, ada_w, ada_b).reshape(depth, COND_PAD, N_MOD, d)

    cos_t, sin_t = _rope_tables(dec_seq, rows_mix.tm)
    tri = _tri_masks()
    krope_blk = jnp.concatenate([jnp.zeros(cache_krope.shape[:-1] + (MLA_NOPE,), F32), cache_krope,
                                 cache_krope], axis=-1)
    even_wts = [_pack_even_weights(j, cos_t, sin_t, even_w_in, gla_gate_w2, gla_gate_b, mla_qa_g, mla_qb_w,
                                   mla_qn_g, mla_kva_g, mla_kvb_w, mla_kn_g) for j in range(n_even)]
    kc, vc = _ctx_kv(cache_ckv, krope_blk,
                     jnp.stack([w["kvb_k"] for w in even_wts]),
                     jnp.stack([w["kvb_v"] for w in even_wts]),
                     jnp.stack([w["kn_g_ctx"] for w in even_wts]))

    ffn1 = [w.astype(BF16) for w in (ffn1_wg, ffn1_wu, ffn1_wd)]
    ffn2 = [w.astype(BF16) for w in (ffn2_wg, ffn2_wu, ffn2_wd)]
    w_out_bf = even_w_out.astype(BF16)
    odd_wts = (odd_w_in.astype(BF16), odd_v_g[:, None, :], odd_ws.astype(BF16),
               jnp.swapaxes(odd_bs, 1, 2), odd_w_out.astype(BF16))

    x = (x_prompt.reshape(m_ctx, d), x_sample.reshape(m_dec, d))
    new_ckv, new_krope, new_gla = [], [], []
    for i in range(depth):
        j = i // 2
        x = _ffn(x, mod, i, 0, norm_g[i, 0][None, :], *ffn1, rows_ffn)
        g_mix = norm_g[i, 1][None, :]
        premix = None
        if i % 2 == 0:
            gq, gk, gv, og, la, qm, km, vm, ckv, kr = _even_in(x, mod, i, g_mix, even_wts[j], rows_mix)
            gg = gla_norm_g[j][None, :]
            gla_ctx, st = _gla(gq, gk, gv, la, og, None, j, tri, gg, batch, seq, 0)
            gla_dec, _ = _gla(gq, gk, gv, la, og, state_gla, j, tri, gg, dec_batch, dec_seq, m_ctx)
            mla_ctx = _mla(qm, km, vm, None, None, j, batch, seq, 0, _pick_tile(256, seq))
            mla_dec = _mla(qm, km, vm, kc, vc, j, dec_batch, dec_seq, m_ctx, _pick_tile(256, dec_seq))
            premix = ((gla_ctx, gla_dec), (mla_ctx, mla_dec), w_out_bf, j)
            new_ckv.append(ckv.reshape(batch, seq, MLA_KV_RANK))
            new_krope.append(kr.reshape(batch, seq, MLA_ROPE))
            new_gla.append(st)
        else:
            x = _odd_mixer(x, mod, i, j, g_mix, *odd_wts, rows_mix)
        x = _ffn(x, mod, i, 6, norm_g[i, 2][None, :], *ffn2, rows_ffn, split_out=(i == depth - 1),
                 premix=premix)

    y_prompt, y_sample = x
    return (y_prompt.reshape(batch, seq, d), y_sample.reshape(dec_batch, dec_seq, d),
            jnp.stack(new_ckv, axis=1), jnp.stack(new_krope, axis=1), jnp.stack(new_gla, axis=1))
```

```python
import functools
import math

import numpy as np
import jax
import jax.numpy as jnp
from jax import lax
from jax.experimental import pallas as pl
from jax.experimental.pallas import tpu as pltpu

F32 = jnp.float32
BF16 = jnp.bfloat16

EPS = 1e-6
N_MOD = 9
GRID_W = 64
ROPE_BASE = 10000.0
GLA_HEADS = 4
GLA_DK = 64
GLA_DV = 128
GLA_QK = GLA_HEADS * GLA_DK
GLA_VW = GLA_HEADS * GLA_DV
GLA_GATE_RANK = 16
GLA_GATE_NORM = 16.0
GLA_CHUNK = 64
MLA_HEADS = 8
MLA_NOPE = 64
MLA_ROPE = 32
MLA_V = 64
MLA_QK_DIM = MLA_NOPE + MLA_ROPE
MLA_Q_RANK = 384
MLA_KV_RANK = 256
HEAD_PAD = 128
CMLP_CHUNK = 128
CMLP_GROUPS = 4
COND_PAD = 16

VMEM_LIMIT = 56 * 1024 * 1024


def _cparams(*sem):
    return pltpu.CompilerParams(dimension_semantics=sem, vmem_limit_bytes=VMEM_LIMIT)


def _resident(shape, index_map):
    return pl.BlockSpec(shape, index_map, pipeline_mode=pl.Buffered(1))


def _const_map(nd):
    return lambda *_: (0,) * nd


def _layer_spec(stack, layer):
    nd = stack.ndim - 1
    return _resident((None,) + stack.shape[1:], lambda *_: (layer,) + (0,) * nd)


def _silu(x):
    return x * jax.nn.sigmoid(x)


def _rms(x, g):
    return x * lax.rsqrt(jnp.mean(x * x, axis=-1, keepdims=True) + EPS) * g


def _modulate(x, g, shift, scale):
    return _rms(x, g) * (1.0 + scale) + shift


def _dot(a, b):
    return jnp.dot(a, b, preferred_element_type=F32)


def _dot_nt(a, b):
    return lax.dot_general(a, b, (((1,), (1,)), ((), ())), preferred_element_type=F32)


def _mod_kernel(c_ref, w_ref, b_ref, o_ref):
    s = _silu(c_ref[...]).astype(BF16)
    o_ref[...] = _dot(s, w_ref[...].astype(BF16)) + b_ref[...]


def _modulation_all(cond, ada_w, ada_b):
    depth, d, n = ada_w.shape
    tn = n // 4
    return pl.pallas_call(
        _mod_kernel,
        out_shape=jax.ShapeDtypeStruct((depth, COND_PAD, n), F32),
        grid=(depth, n // tn),
        in_specs=[
            pl.BlockSpec((COND_PAD, d), lambda l, j: (0, 0)),
            pl.BlockSpec((None, d, tn), lambda l, j: (l, 0, j)),
            pl.BlockSpec((None, 1, tn), lambda l, j: (l, 0, j)),
        ],
        out_specs=pl.BlockSpec((None, COND_PAD, tn), lambda l, j: (l, 0, j)),
        compiler_params=_cparams("arbitrary", "arbitrary"),
        name="adaln_modulation",
    )(cond, ada_w, ada_b.reshape(depth, 1, n))


class _Rows:
    def __init__(self, m_ctx, dec_batch, dec_seq, tm):
        assert m_ctx % tm == 0 and dec_seq % tm == 0
        self.tm = tm
        self.m_ctx = m_ctx
        self.m = m_ctx + dec_batch * dec_seq
        self.nb_ctx = m_ctx // tm
        self.nb_seq = dec_seq // tm
        self.nblocks = self.m // tm

    def cond(self, i):
        return jnp.where(i < self.nb_ctx, 0, 1 + (i - self.nb_ctx) // self.nb_seq)

    def rope_block(self, i):
        return jnp.where(i < self.nb_ctx, 0, 1 + (i - self.nb_ctx) % self.nb_seq)

    def mod_spec(self, layer, d):
        return pl.BlockSpec((None, None, N_MOD, d), lambda i: (layer, self.cond(i), 0, 0))

    def row_spec(self, width):
        return pl.BlockSpec((self.tm, width), lambda i: (i, 0))

    def ctx_spec(self, width):
        return pl.BlockSpec((self.tm, width), lambda i: (jnp.minimum(i, self.nb_ctx - 1), 0))

    def dec_spec(self, width):
        return pl.BlockSpec((self.tm, width), lambda i: (jnp.maximum(i - self.nb_ctx, 0), 0))

    def split_specs(self, width):
        return [self.ctx_spec(width), self.dec_spec(width)]

    def split_shapes(self, width, dtype):
        return [jax.ShapeDtypeStruct((self.m_ctx, width), dtype),
                jax.ShapeDtypeStruct((self.m - self.m_ctx, width), dtype)]


def _load_split(refs, is_ctx):
    if len(refs) == 1:
        return refs[0][...]
    return jnp.where(is_ctx, refs[0][...], refs[1][...])


def _ffn_kernel(*refs, row0, nb_ctx, split_in, split_out, premix):
    it = iter(refs)
    x_refs = [next(it) for _ in range(2 if split_in else 1)]
    mod_ref, g_ref, wg_ref, wu_ref, wd_ref = (next(it) for _ in range(5))
    if premix:
        a_refs, b_refs = [next(it), next(it)], [next(it), next(it)]
        wo_ref = next(it)
    o_refs = [next(it) for _ in range(2 if split_out else 1)]
    is_ctx = pl.program_id(0) < nb_ctx
    x = _load_split(x_refs, is_ctx)
    if premix:
        half = wo_ref.shape[0] // 2
        mix = _dot(_load_split(a_refs, is_ctx), wo_ref[:half, :])
        mix = mix + _dot(_load_split(b_refs, is_ctx), wo_ref[half:, :])
        x = x + mod_ref[5:6, :] * mix
    shift = mod_ref[row0:row0 + 1, :]
    scale = mod_ref[row0 + 1:row0 + 2, :]
    gate = mod_ref[row0 + 2:row0 + 3, :]
    h = _modulate(x, g_ref[...], shift, scale).astype(BF16)
    a = _dot(h, wg_ref[...])
    u = _dot(h, wu_ref[...])
    act = (_silu(a) * u).astype(BF16)
    y = _dot(act, wd_ref[...])
    out = x + (0.5 * gate) * y
    if split_out:
        @pl.when(is_ctx)
        def _():
            o_refs[0][...] = out

        @pl.when(jnp.logical_not(is_ctx))
        def _():
            o_refs[1][...] = out
    else:
        o_refs[0][...] = out


def _ffn(xs, mod, layer, row0, g, wg, wu, wd, rows, split_out=False, premix=None):
    split_in = isinstance(xs, (tuple, list))
    xs = list(xs) if split_in else [xs]
    d = xs[0].shape[1]
    in_specs = (rows.split_specs(d) if split_in else [rows.row_spec(d)]) + [
        rows.mod_spec(layer, d),
        _resident((1, d), _const_map(2)),
        _layer_spec(wg, layer), _layer_spec(wu, layer), _layer_spec(wd, layer),
    ]
    args = xs + [mod, g, wg, wu, wd]
    if premix is not None:
        gla, mla, w_out, j = premix
        in_specs += rows.split_specs(gla[0].shape[1]) + rows.split_specs(mla[0].shape[1])
        in_specs.append(_layer_spec(w_out, j))
        args += list(gla) + list(mla) + [w_out]
    return pl.pallas_call(
        functools.partial(_ffn_kernel, row0=row0, nb_ctx=rows.nb_ctx, split_in=split_in,
                          split_out=split_out, premix=premix is not None),
        out_shape=rows.split_shapes(d, F32) if split_out else jax.ShapeDtypeStruct((rows.m, d), F32),
        grid=(rows.nblocks,),
        in_specs=in_specs,
        out_specs=rows.split_specs(d) if split_out else rows.row_spec(d),
        compiler_params=_cparams("arbitrary"),
        name="ffn_swiglu",
    )(*args)


def _gelu_tanh(x):
    c = math.sqrt(2.0 / math.pi)
    return x * (0.5 * (1.0 + jnp.tanh(c * (x + 0.044715 * (x * x * x)))))


def _odd_kernel(x_ref, mod_ref, g_ref, win_ref, vg_ref, ws_ref, bst_ref, wout_ref, o_ref):
    x = x_ref[...]
    tm = x.shape[0]
    shift, scale, gate = mod_ref[3:4, :], mod_ref[4:5, :], mod_ref[5:6, :]
    h = _modulate(x, g_ref[...], shift, scale).astype(BF16)
    uv = _gelu_tanh(_dot(h, win_ref[...]))
    width = uv.shape[1] // 2
    u = uv[:, :width]
    v = _rms(uv[:, width:], vg_ref[...]).astype(BF16)
    gw = width // CMLP_GROUPS
    chunks = []
    for c in range(tm // CMLP_CHUNK):
        r0 = c * CMLP_CHUNK
        groups = []
        for g in range(CMLP_GROUPS):
            vg = v[r0:r0 + CMLP_CHUNK, g * gw:(g + 1) * gw]
            mixed = _dot(ws_ref[g], vg) + bst_ref[:, g:g + 1]
            groups.append(u[r0:r0 + CMLP_CHUNK, g * gw:(g + 1) * gw] * mixed)
        chunks.append(jnp.concatenate(groups, axis=1))
    z = jnp.concatenate(chunks, axis=0).astype(BF16)
    o_ref[...] = x + gate * _dot(z, wout_ref[...])


def _odd_mixer(x, mod, layer, layer_j, g, w_in, v_g, w_s, b_s_t, w_out, rows):
    m, d = x.shape
    return pl.pallas_call(
        _odd_kernel,
        out_shape=jax.ShapeDtypeStruct((m, d), F32),
        grid=(rows.nblocks,),
        in_specs=[
            rows.row_spec(d),
            rows.mod_spec(layer, d),
            _resident((1, d), _const_map(2)),
            _layer_spec(w_in, layer_j), _layer_spec(v_g, layer_j), _layer_spec(w_s, layer_j),
            _layer_spec(b_s_t, layer_j), _layer_spec(w_out, layer_j),
        ],
        out_specs=rows.row_spec(d),
        compiler_params=_cparams("arbitrary"),
        name="odd_gmlp",
    )(x, mod, g, w_in, v_g, w_s, b_s_t, w_out)


_C_CQ = 0
_C_CKV = _C_CQ + MLA_Q_RANK
_C_KA = _C_CKV + MLA_KV_RANK
_C_KB = _C_KA + HEAD_PAD
_C_Q = _C_KB + HEAD_PAD
_C_K = _C_Q + GLA_QK
_C_V = _C_K + GLA_QK
_C_OG = _C_V + GLA_VW
_C_END = _C_OG + GLA_VW


def _head_scale(x, live):
    ss = jnp.sum(jnp.where(live, x * x, 0.0), axis=-1, keepdims=True)
    return lax.rsqrt(ss * (1.0 / MLA_QK_DIM) + EPS)


def _even_in_kernel(x_ref, mod_ref, g_ref, win_ref, w2_ref, gb_ref, qag_ref, qb_ref, kvag_ref,
                    kvbk_ref, kvbv_ref, kng_ref, tq_ref, ta_ref, tb_ref,
                    gq_ref, gk_ref, gv_ref, og_ref, la_ref, qm_ref, km_ref, vm_ref, ckv_ref, kr_ref,
                    *, nb_ctx):
    is_ctx = pl.program_id(0) < nb_ctx
    x = x_ref[...]
    shift, scale = mod_ref[3:4, :], mod_ref[4:5, :]
    h = _modulate(x, g_ref[...], shift, scale).astype(BF16)

    proj_q = _dot(h, win_ref[:, _C_CQ:_C_CKV])
    proj_kv = _dot(h, win_ref[:, _C_CKV:_C_Q])
    ckv = _rms(proj_kv[:, :MLA_KV_RANK], kvag_ref[...])
    ka = proj_kv[:, _C_KA - _C_CKV:_C_KB - _C_CKV]
    kb = proj_kv[:, _C_KB - _C_CKV:]

    @pl.when(is_ctx)
    def _():
        ckv_ref[...] = ckv
        kr_ref[...] = ka[:, MLA_NOPE:MLA_QK_DIM]

    cq = _rms(proj_q, qag_ref[...]).astype(BF16)
    qn = _dot(cq, qb_ref[...])
    ckv_bf = ckv.astype(BF16)
    kn = _dot(ckv_bf, kvbk_ref[...])
    vm_t = _dot(ckv_bf, kvbv_ref[...])
    z = _dot(ka.astype(BF16), w2_ref[...]) + gb_ref[...]
    proj_g = _dot(h, win_ref[:, _C_Q:])

    lane = lax.broadcasted_iota(jnp.int32, ka.shape, 1)
    live = lane < MLA_QK_DIM
    tq = tq_ref[...]
    for hd in range(MLA_HEADS):
        sl = slice(hd * HEAD_PAD, (hd + 1) * HEAD_PAD)
        qh = qn[:, sl]
        qm_ref[:, sl] = (qh * tq * _head_scale(qh, live)).astype(BF16)

    vm_ref[...] = vm_t.T.astype(BF16)
    is_rope = live & (lane >= MLA_NOPE)
    ss_rope = jnp.sum(jnp.where(is_rope, ka * ka, 0.0), axis=-1, keepdims=True)
    rot = ka * ta_ref[...] + kb * tb_ref[...]
    kng = kng_ref[...]
    for hd in range(MLA_HEADS):
        sl = slice(hd * HEAD_PAD, (hd + 1) * HEAD_PAD)
        kh = kn[:, sl]
        ss = jnp.sum(kh * kh, axis=-1, keepdims=True) + ss_rope
        r = lax.rsqrt(ss * (1.0 / MLA_QK_DIM) + EPS)
        km_ref[:, sl] = ((kh * kng + rot) * r).astype(BF16)

    la_ref[...] = -(jnp.maximum(-z, 0.0) + jnp.log1p(jnp.exp(-jnp.abs(z)))) * (1.0 / GLA_GATE_NORM)
    gq_ref[...] = proj_g[:, :_C_K - _C_Q] * (GLA_DK ** -0.5)
    gk_ref[...] = proj_g[:, _C_K - _C_Q:_C_V - _C_Q]
    gv_ref[...] = proj_g[:, _C_V - _C_Q:_C_OG - _C_Q].astype(BF16)
    og_ref[...] = proj_g[:, _C_OG - _C_Q:]


def _even_in(x, mod, layer, g, wts, rows):
    m, d = x.shape
    tm = rows.tm
    c2 = _const_map(2)
    rope_spec = pl.BlockSpec((tm, HEAD_PAD), lambda i: (rows.rope_block(i), 0))
    widths = [GLA_QK, GLA_QK, GLA_VW, GLA_VW, 2 * GLA_QK, MLA_HEADS * HEAD_PAD, MLA_HEADS * HEAD_PAD]
    dtypes = [F32, F32, BF16, F32, F32, BF16, BF16]
    names = ["w_in", "w2", "gate_b", "qa_g", "qb", "kva_g", "kvb_k", "kvb_v", "kn_g"]
    return pl.pallas_call(
        functools.partial(_even_in_kernel, nb_ctx=rows.nb_ctx),
        out_shape=[jax.ShapeDtypeStruct((m, w), t) for w, t in zip(widths, dtypes)]
        + [jax.ShapeDtypeStruct((MLA_HEADS * MLA_V, m), BF16),
           jax.ShapeDtypeStruct((rows.m_ctx, MLA_KV_RANK), F32),
           jax.ShapeDtypeStruct((rows.m_ctx, MLA_ROPE), F32)],
        grid=(rows.nblocks,),
        in_specs=[rows.row_spec(d), rows.mod_spec(layer, d), _resident((1, d), c2)]
        + [_resident(wts[n].shape, c2) for n in names] + [rope_spec] * 3,
        out_specs=[rows.row_spec(w) for w in widths]
        + [pl.BlockSpec((MLA_HEADS * MLA_V, tm), lambda i: (0, i)),
           rows.ctx_spec(MLA_KV_RANK), rows.ctx_spec(MLA_ROPE)],
        compiler_params=_cparams("arbitrary"),
        name="even_in_proj",
    )(x, mod, g, *[wts[n] for n in names], wts["t_q"], wts["t_a"], wts["t_b"])


def _ctx_kv_kernel(ckv_ref, kr_ref, kvbk_ref, kvbv_ref, kng_ref, k_ref, v_ref):
    ckv_bf = ckv_ref[...].astype(BF16)
    v_ref[...] = _dot(ckv_bf, kvbv_ref[...]).T.astype(BF16)
    kn = _dot(ckv_bf, kvbk_ref[...])
    kr = kr_ref[...]
    lane = lax.broadcasted_iota(jnp.int32, kr.shape, 1)
    live = lane < MLA_QK_DIM
    kng = kng_ref[...]
    for hd in range(MLA_HEADS):
        sl = slice(hd * HEAD_PAD, (hd + 1) * HEAD_PAD)
        kh = kn[:, sl] + kr
        k_ref[:, sl] = (kh * kng * _head_scale(kh, live)).astype(BF16)


def _ctx_kv(cache_ckv, cache_krope_blk, kvb_k, kvb_v, kn_g):
    nb, ne, p, r = cache_ckv.shape
    return pl.pallas_call(
        _ctx_kv_kernel,
        out_shape=[jax.ShapeDtypeStruct((ne, nb, p, MLA_HEADS * HEAD_PAD), BF16),
                   jax.ShapeDtypeStruct((ne, nb, MLA_HEADS * MLA_V, p), BF16)],
        grid=(ne, nb),
        in_specs=[
            pl.BlockSpec((None, None, p, r), lambda j, b: (b, j, 0, 0)),
            pl.BlockSpec((None, None, p, HEAD_PAD), lambda j, b: (b, j, 0, 0)),
            pl.BlockSpec((None,) + kvb_k.shape[1:], lambda j, b: (j, 0, 0)),
            pl.BlockSpec((None,) + kvb_v.shape[1:], lambda j, b: (j, 0, 0)),
            pl.BlockSpec((None,) + kn_g.shape[1:], lambda j, b: (j, 0, 0)),
        ],
        out_specs=[pl.BlockSpec((None, None, p, MLA_HEADS * HEAD_PAD), lambda j, b: (j, b, 0, 0)),
                   pl.BlockSpec((None, None, MLA_HEADS * MLA_V, p), lambda j, b: (j, b, 0, 0))],
        compiler_params=_cparams("arbitrary", "arbitrary"),
        name="ctx_kv",
    )(cache_ckv, cache_krope_blk, kvb_k, kvb_v, kn_g)


def _split3(x):
    hi = x.astype(BF16)
    r = x - hi.astype(F32)
    mid = r.astype(BF16)
    lo = (r - mid.astype(F32)).astype(BF16)
    return hi, mid, lo


def _block_row_bcast(a, blk, off):
    c, n = a.shape
    if blk >= 8:
        pieces = [jnp.broadcast_to(a[b * blk + off:b * blk + off + 1, :], (blk, n))
                  for b in range(c // blk)]
        return pieces[0] if len(pieces) == 1 else jnp.concatenate(pieces, axis=0)
    a3 = a.reshape(c // 8, 8, n)
    sub = lax.broadcasted_iota(jnp.int32, a3.shape, 1) // blk
    out = jnp.broadcast_to(a3[:, off:off + 1, :], a3.shape)
    for s in range(1, 8 // blk):
        cand = jnp.broadcast_to(a3[:, s * blk + off:s * blk + off + 1, :], a3.shape)
        out = jnp.where(sub == s, cand, out)
    return out.reshape(c, n)


def _head_stack(a, head_w):
    lane_head = lax.broadcasted_iota(jnp.int32, a.shape, 1) // head_w
    zero = jnp.zeros_like(a)
    return jnp.concatenate([jnp.where(lane_head == h, a, zero) for h in range(GLA_HEADS)], axis=0)


def _run_skewed(stage_gens, skew=1):
    done = [False] * len(stage_gens)
    tick = 0
    while not all(done):
        for u, gen in enumerate(stage_gens):
            if tick >= u * skew and not done[u]:
                try:
                    next(gen)
                except StopIteration:
                    done[u] = True
        tick += 1


def _gla_phase_a(c, q_ref, k_ref, v_ref, la_ref, tri_ref, oacc_ref, qst_ref, ds_ref, dc_ref):
    C = GLA_CHUNK
    r0 = pl.multiple_of(c * C, C)
    q = q_ref[pl.ds(r0, C), :]
    k = k_ref[pl.ds(r0, C), :]
    la = la_ref[pl.ds(r0, C), :]
    la_f, la_b = la[:, :GLA_QK], la[:, GLA_QK:]
    cums = []
    for d, la_d in enumerate((la_f, la_b)):
        hi, mid, lo = _split3(la_d)
        tri = tri_ref[d]
        cums.append(_dot(tri, hi) + _dot(tri, mid) + _dot(tri, lo))
    cum_f, cum_b = cums
    k_stack = _head_stack(k.astype(BF16), GLA_DK)
    yield

    row = lax.broadcasted_iota(jnp.int32, (C, 1), 0)
    t_idx = lax.broadcasted_iota(jnp.int32, (C, GLA_HEADS * C), 0)
    s_idx = lax.broadcasted_iota(jnp.int32, (C, GLA_HEADS * C), 1) % C
    att = jnp.where(t_idx == s_idx, 2.0 * _dot_nt(q.astype(BF16), k_stack), 0.0)
    m = C // 2
    while m >= 1:
        blk = 2 * m
        upper = (row % blk) >= m
        if m == 1:
            qm = (q * jnp.exp(jnp.where(upper, la_f, la_b))).astype(BF16)
            km_stack = k_stack
        else:
            ref_f = _block_row_bcast(cum_f, blk, m - 1)
            ref_b = _block_row_bcast(cum_b, blk, m)
            eq = jnp.where(upper, cum_f - ref_f, cum_b - ref_b)
            ek = jnp.where(upper, ref_b - cum_b, ref_f - cum_f)
            qm = (q * jnp.exp(eq)).astype(BF16)
            km_stack = _head_stack((k * jnp.exp(ek)).astype(BF16), GLA_DK)
        yield
        pair = ((t_idx // blk) == (s_idx // blk)) & ((t_idx // m) != (s_idx // m))
        att = jnp.where(pair, _dot_nt(qm, km_stack), att)
        m //= 2

    v = v_ref[pl.ds(r0, C), :]
    v_stack = _head_stack(v, GLA_DV)
    qcs = [(q * jnp.exp(cum)).astype(BF16) for cum in cums]
    yield
    oacc_ref[pl.ds(r0, C), :] = _dot(att.astype(BF16), v_stack)
    for d, cum in enumerate(cums):
        last = cum[C - 1:C, :] if d == 0 else cum[0:1, :]
        qst_ref[d, c] = _head_stack(qcs[d], GLA_DK)
        kct = (k * jnp.exp(last - cum)).T.astype(BF16)
        decay_col = jnp.exp(jnp.broadcast_to(last, (8, GLA_QK))).T[:, 0:1]
        dc_ref[d, c] = jnp.broadcast_to(decay_col, (GLA_QK, GLA_DV))
        yield
        ds_ref[d, c] = jnp.concatenate(
            [_dot(kct[h * GLA_DK:(h + 1) * GLA_DK, :], v[:, h * GLA_DV:(h + 1) * GLA_DV])
             for h in range(GLA_HEADS)], axis=0)


def _gla_phase_c(c, og_ref, gg, oacc_ref, qst_ref, sb_ref, o_ref):
    C = GLA_CHUNK
    r0 = pl.multiple_of(c * C, C)
    q_cat = jnp.concatenate([qst_ref[0, c], qst_ref[1, c]], axis=1)
    s_cat = jnp.concatenate([sb_ref[0, c], sb_ref[1, c]], axis=0)
    inter = _dot(q_cat, s_cat)
    yield
    for h in range(GLA_HEADS):
        sl = slice(h * GLA_DV, (h + 1) * GLA_DV)
        o = oacc_ref[pl.ds(r0, C), sl] + inter[h * C:(h + 1) * C, :]
        o_ref[pl.ds(r0, C), sl] = (_rms(o, gg) * _silu(og_ref[pl.ds(r0, C), sl])).astype(BF16)


def _gla_kernel(*refs, has_state):
    it = iter(refs)
    q_ref, k_ref, v_ref, la_ref, og_ref = (next(it) for _ in range(5))
    s0_ref = next(it) if has_state else None
    tri_ref, gg_ref, o_ref, sout_ref = (next(it) for _ in range(4))
    oacc_ref, qst_ref, ds_ref, dc_ref, sb_ref, st_ref = (next(it) for _ in range(6))

    nc = q_ref.shape[0] // GLA_CHUNK
    per_step = 4

    def phase_a(i, carry):
        _run_skewed([_gla_phase_a(per_step * i + u, q_ref, k_ref, v_ref, la_ref, tri_ref, oacc_ref,
                                  qst_ref, ds_ref, dc_ref) for u in range(per_step)])
        return carry

    lax.fori_loop(0, nc // per_step, phase_a, 0)

    for d in range(2):
        for h in range(GLA_HEADS):
            rows = slice(h * GLA_DK, (h + 1) * GLA_DK)
            st_ref[d, rows, :] = s0_ref[d, h] if has_state else jnp.zeros((GLA_DK, GLA_DV), F32)

    def phase_b(c, carry):
        for d in range(2):
            cc = c if d == 0 else nc - 1 - c
            s = st_ref[d]
            sb_ref[d, cc] = s.astype(BF16)
            st_ref[d] = s * dc_ref[d, cc] + ds_ref[d, cc]
        return carry

    lax.fori_loop(0, nc, phase_b, 0)
    for d in range(2):
        for h in range(GLA_HEADS):
            sout_ref[d, h] = st_ref[d, h * GLA_DK:(h + 1) * GLA_DK, :]

    gg = gg_ref[...]

    def phase_c(i, carry):
        _run_skewed([_gla_phase_c(per_step * i + u, og_ref, gg, oacc_ref, qst_ref, sb_ref, o_ref)
                     for u in range(per_step)])
        return carry

    lax.fori_loop(0, nc // per_step, phase_c, 0)


def _gla(gq, gk, gv, la, og, state, layer_j, tri, gg, nbatch, t, row_off):
    assert t % (2 * GLA_CHUNK) == 0
    b0 = row_off // t
    nc = t // GLA_CHUNK
    seq = lambda w: pl.BlockSpec((t, w), lambda b: (b0 + b, 0))
    in_specs = [seq(GLA_QK), seq(GLA_QK), seq(GLA_VW), seq(2 * GLA_QK), seq(GLA_VW)]
    args = [gq, gk, gv, la, og]
    if state is not None:
        in_specs.append(pl.BlockSpec((None, None, 2, GLA_HEADS, GLA_DK, GLA_DV),
                                     lambda b: (b, layer_j, 0, 0, 0, 0)))
        args.append(state)
    in_specs += [_resident(tri.shape, _const_map(3)), _resident(gg.shape, _const_map(2))]
    args += [tri, gg]
    return pl.pallas_call(
        functools.partial(_gla_kernel, has_state=state is not None),
        out_shape=[jax.ShapeDtypeStruct((nbatch * t, GLA_VW), BF16),
                   jax.ShapeDtypeStruct((nbatch, 2, GLA_HEADS, GLA_DK, GLA_DV), F32)],
        grid=(nbatch,),
        in_specs=in_specs,
        out_specs=[pl.BlockSpec((t, GLA_VW), lambda b: (b, 0)),
                   pl.BlockSpec((None, 2, GLA_HEADS, GLA_DK, GLA_DV), lambda b: (b, 0, 0, 0, 0))],
        scratch_shapes=[
            pltpu.VMEM((t, GLA_VW), F32),
            pltpu.VMEM((2, nc, GLA_HEADS * GLA_CHUNK, GLA_QK), BF16),
            pltpu.VMEM((2, nc, GLA_QK, GLA_DV), F32),
            pltpu.VMEM((2, nc, GLA_QK, GLA_DV), F32),
            pltpu.VMEM((2, nc, GLA_QK, GLA_DV), BF16),
            pltpu.VMEM((2, GLA_QK, GLA_DV), F32),
        ],
        compiler_params=_cparams("arbitrary"),
        name="gla_scan",
    )(*args)


def _mla_kernel(*refs, has_ctx):
    it = iter(refs)
    q_ref, k_ref, v_ref = next(it), next(it), next(it)
    kc_ref, vc_ref = (next(it), next(it)) if has_ctx else (None, None)
    o_ref = next(it)
    def scores(hd):
        sl = slice(hd * HEAD_PAD, (hd + 1) * HEAD_PAD)
        q = q_ref[:, sl]
        return [_dot_nt(r[:, sl], q) for r in ((k_ref, kc_ref) if has_ctx else (k_ref,))]

    def softmax(ss):
        mx = functools.reduce(jnp.maximum, [jnp.max(s, axis=0, keepdims=True) for s in ss])
        ps = [jnp.exp2(s - mx) for s in ss]
        den = sum(jnp.sum(p, axis=0, keepdims=True) for p in ps)
        return [p.astype(BF16) for p in ps], den

    def values(hd, ps, den):
        vrows = slice(hd * MLA_V, (hd + 1) * MLA_V)
        o = sum(_dot(r[vrows, :], p) for r, p in zip((v_ref, vc_ref), ps))
        return o / den

    s_q, p_q, outs = {}, {}, []
    for step in range(MLA_HEADS + 2):
        if step < MLA_HEADS:
            s_q[step] = scores(step)
        if 0 <= step - 1 < MLA_HEADS:
            p_q[step - 1] = softmax(s_q.pop(step - 1))
        if 0 <= step - 2 < MLA_HEADS:
            outs.append(values(step - 2, *p_q.pop(step - 2)))
    o_ref[...] = jnp.concatenate(outs, axis=0).T.astype(BF16)


def _mla(qm, km, vm, kc, vc, layer_j, nbatch, t, row_off, tq):
    nq = t // tq
    q0, b0 = row_off // tq, row_off // t
    qspec = lambda w: pl.BlockSpec((tq, w), lambda b, i: (q0 + b * nq + i, 0))
    kspec = lambda w: pl.BlockSpec((t, w), lambda b, i: (b0 + b, 0))
    vspec = pl.BlockSpec((MLA_HEADS * MLA_V, t), lambda b, i: (0, b0 + b))
    in_specs = [qspec(MLA_HEADS * HEAD_PAD), kspec(MLA_HEADS * HEAD_PAD), vspec]
    args = [qm, km, vm]
    if kc is not None:
        in_specs += [pl.BlockSpec((None, None) + kc.shape[2:], lambda b, i: (layer_j, b, 0, 0)),
                     pl.BlockSpec((None, None) + vc.shape[2:], lambda b, i: (layer_j, b, 0, 0))]
        args += [kc, vc]
    return pl.pallas_call(
        functools.partial(_mla_kernel, has_ctx=kc is not None),
        out_shape=jax.ShapeDtypeStruct((nbatch * t, MLA_HEADS * MLA_V), BF16),
        grid=(nbatch, nq),
        in_specs=in_specs,
        out_specs=pl.BlockSpec((tq, MLA_HEADS * MLA_V), lambda b, i: (b * nq + i, 0)),
        compiler_params=_cparams("arbitrary", "arbitrary"),
        name="mla_attention",
    )(*args)


def _swap_signed(a):
    pairs = a.reshape(a.shape[:-1] + (a.shape[-1] // 2, 2))
    return jnp.stack([-pairs[..., 1], pairs[..., 0]], axis=-1).reshape(a.shape)


def _swap_pairs(a):
    pairs = a.reshape(a.shape[:-1] + (a.shape[-1] // 2, 2))
    return pairs[..., ::-1].reshape(a.shape)


def _pack_even_weights(j, cos, sin, even_w_in, gla_gate_w2, gla_gate_b, mla_qa_g, mla_qb_w, mla_qn_g,
                       mla_kva_g, mla_kvb_w, mla_kn_g):
    d = even_w_in.shape[1]
    w = even_w_in[j]
    o = 0
    parts = {}
    for name, width in (("q", GLA_QK), ("k", GLA_QK), ("v", GLA_VW), ("og", GLA_VW),
                        ("gl", 2 * GLA_GATE_RANK), ("cq", MLA_Q_RANK), ("ckv", MLA_KV_RANK),
                        ("kr", MLA_ROPE)):
        parts[name] = w[:, o:o + width]
        o += width
    zeros = lambda n: jnp.zeros((d, n), w.dtype)
    kr, kr_sw = parts["kr"], _swap_signed(parts["kr"])
    blk_a = jnp.concatenate([parts["gl"], zeros(MLA_NOPE - 2 * GLA_GATE_RANK), kr, kr_sw], axis=1)
    blk_b = jnp.concatenate([zeros(MLA_NOPE), kr_sw, kr], axis=1)
    w_in = jnp.concatenate([parts["cq"], parts["ckv"], blk_a, blk_b, parts["q"], parts["k"], parts["v"],
                            parts["og"]], axis=1).astype(BF16)
    assert w_in.shape[1] == _C_END

    w2 = jnp.zeros((HEAD_PAD, 2 * GLA_QK), F32)
    w2 = w2.at[:GLA_GATE_RANK, :GLA_QK].set(gla_gate_w2[j, 0])
    w2 = w2.at[GLA_GATE_RANK:2 * GLA_GATE_RANK, GLA_QK:].set(gla_gate_w2[j, 1])

    pad_head = lambda a: jnp.pad(a, [(0, 0)] * (a.ndim - 1) + [(0, HEAD_PAD - a.shape[-1])])
    qb = mla_qb_w[j]
    qb = jnp.concatenate([qb, _swap_signed(qb[..., MLA_NOPE:])], axis=-1)
    qb = qb.reshape(MLA_Q_RANK, MLA_HEADS * HEAD_PAD)
    kvb = mla_kvb_w[j]
    kvb_k = pad_head(kvb[..., :MLA_NOPE]).reshape(MLA_KV_RANK, MLA_HEADS * HEAD_PAD)
    kvb_v = kvb[..., MLA_NOPE:].reshape(MLA_KV_RANK, MLA_HEADS * MLA_V)

    def rot_tables(g):
        n = cos.shape[0]
        g_n = jnp.broadcast_to(g[:MLA_NOPE], (n, MLA_NOPE))
        g_c = g[MLA_NOPE:] * cos
        g_s = _swap_pairs(g[MLA_NOPE:]) * sin
        return g_n, g_c, g_s

    qn_n, qn_c, qn_s = rot_tables(mla_qn_g[j])
    kn_n, kn_c, kn_s = rot_tables(mla_kn_g[j])
    kzero = jnp.zeros_like(kn_n)
    return {
        "w_in": w_in,
        "w2": w2.astype(BF16),
        "gate_b": gla_gate_b[j].reshape(1, 2 * GLA_QK),
        "qa_g": mla_qa_g[j][None, :],
        "qb": qb.astype(BF16),
        "kva_g": mla_kva_g[j][None, :],
        "kvb_k": kvb_k.astype(BF16),
        "kvb_v": kvb_v.astype(BF16),
        "kn_g": pad_head(mla_kn_g[j][:MLA_NOPE])[None, :],
        "kn_g_ctx": jnp.concatenate([mla_kn_g[j], mla_kn_g[j][MLA_NOPE:]])[None, :],
        "t_q": jnp.concatenate([qn_n, qn_c, qn_s], axis=1) * (MLA_QK_DIM ** -0.5 * math.log2(math.e)),
        "t_a": jnp.concatenate([kzero, kn_c, kn_s], axis=1),
        "t_b": jnp.concatenate([kzero, kn_s, kn_c], axis=1),
    }


def _rope_tables(n, tm):
    pairs = MLA_ROPE // 4
    pos = np.arange(n)
    inv = ROPE_BASE ** (-jnp.arange(pairs, dtype=F32) / pairs)
    row = jnp.asarray(pos // GRID_W, F32)
    col = jnp.asarray(pos % GRID_W, F32)
    ang = jnp.concatenate([row[:, None] * inv, col[:, None] * inv], axis=-1)
    cos = jnp.repeat(jnp.cos(ang), 2, axis=-1)
    sin = jnp.repeat(jnp.sin(ang), 2, axis=-1)
    cos = jnp.pad(cos, ((tm, 0), (0, 0)), constant_values=1.0)
    sin = jnp.pad(sin, ((tm, 0), (0, 0)))
    return cos, sin


def _tri_masks():
    r = np.arange(GLA_CHUNK)
    lower = (r[None, :] <= r[:, None]).astype(np.float32)
    return jnp.asarray(np.stack([lower, lower.T]), BF16)


def _pick_tile(pref, *sizes):
    tm = pref
    while any(s % tm for s in sizes):
        tm //= 2
    return tm


def kernel(x_prompt, x_sample, c, c_ctx, cache_ckv, cache_krope, state_gla, ada_w, ada_b, norm_g,
           ffn1_wg, ffn1_wu, ffn1_wd, ffn2_wg, ffn2_wu, ffn2_wd, even_w_in, even_w_out,
           gla_gate_w2, gla_gate_b, gla_norm_g, mla_qa_g, mla_qb_w, mla_kva_g, mla_kvb_w,
           mla_qn_g, mla_kn_g, odd_w_in, odd_v_g, odd_ws, odd_bs, odd_w_out):
    batch, seq, d = x_prompt.shape
    dec_batch, dec_seq, _ = x_sample.shape
    depth = ada_w.shape[0]
    n_even = even_w_in.shape[0]
    m_ctx, m_dec = batch * seq, dec_batch * dec_seq
    assert 1 + dec_batch <= COND_PAD and seq % CMLP_CHUNK == 0 and dec_seq % CMLP_CHUNK == 0

    rows_ffn = _Rows(m_ctx, dec_batch, dec_seq, _pick_tile(512, m_ctx, dec_seq))
    rows_mix = _Rows(m_ctx, dec_batch, dec_seq, _pick_tile(512, m_ctx, dec_seq))

    cond = jnp.concatenate([c_ctx[None, :], c, jnp.zeros((COND_PAD - 1 - dec_batch, d), F32)], axis=0)
    mod = _modulation_all(cond, ada_w, ada_b).reshape(depth, COND_PAD, N_MOD, d)

    cos_t, sin_t = _rope_tables(dec_seq, rows_mix.tm)
    tri = _tri_masks()
    krope_blk = jnp.concatenate([jnp.zeros(cache_krope.shape[:-1] + (MLA_NOPE,), F32), cache_krope,
                                 cache_krope], axis=-1)
    even_wts = [_pack_even_weights(j, cos_t, sin_t, even_w_in, gla_gate_w2, gla_gate_b, mla_qa_g, mla_qb_w,
                                   mla_qn_g, mla_kva_g, mla_kvb_w, mla_kn_g) for j in range(n_even)]
    kc, vc = _ctx_kv(cache_ckv, krope_blk,
                     jnp.stack([w["kvb_k"] for w in even_wts]),
                     jnp.stack([w["kvb_v"] for w in even_wts]),
                     jnp.stack([w["kn_g_ctx"] for w in even_wts]))

    ffn1 = [w.astype(BF16) for w in (ffn1_wg, ffn1_wu, ffn1_wd)]
    ffn2 = [w.astype(BF16) for w in (ffn2_wg, ffn2_wu, ffn2_wd)]
    w_out_bf = even_w_out.astype(BF16)
    odd_wts = (odd_w_in.astype(BF16), odd_v_g[:, None, :], odd_ws.astype(BF16),
               jnp.swapaxes(odd_bs, 1, 2), odd_w_out.astype(BF16))

    x = (x_prompt.reshape(m_ctx, d), x_sample.reshape(m_dec, d))
    new_ckv, new_krope, new_gla = [], [], []
    for i in range(depth):
        j = i // 2
        x = _ffn(x, mod, i, 0, norm_g[i, 0][None, :], *ffn1, rows_ffn)
        g_mix = norm_g[i, 1][None, :]
        premix = None
        if i % 2 == 0:
            gq, gk, gv, og, la, qm, km, vm, ckv, kr = _even_in(x, mod, i, g_mix, even_wts[j], rows_mix)
            gg = gla_norm_g[j][None, :]
            gla_ctx, st = _gla(gq, gk, gv, la, og, None, j, tri, gg, batch, seq, 0)
            gla_dec, _ = _gla(gq, gk, gv, la, og, state_gla, j, tri, gg, dec_batch, dec_seq, m_ctx)
            mla_ctx = _mla(qm, km, vm, None, None, j, batch, seq, 0, _pick_tile(256, seq))
            mla_dec = _mla(qm, km, vm, kc, vc, j, dec_batch, dec_seq, m_ctx, _pick_tile(256, dec_seq))
            premix = ((gla_ctx, gla_dec), (mla_ctx, mla_dec), w_out_bf, j)
            new_ckv.append(ckv.reshape(batch, seq, MLA_KV_RANK))
            new_krope.append(kr.reshape(batch, seq, MLA_ROPE))
            new_gla.append(st)
        else:
            x = _odd_mixer(x, mod, i, j, g_mix, *odd_wts, rows_mix)
        x = _ffn(x, mod, i, 6, norm_g[i, 2][None, :], *ffn2, rows_ffn, split_out=(i == depth - 1),
                 premix=premix)

    y_prompt, y_sample = x
    return (y_prompt.reshape(batch, seq, d), y_sample.reshape(dec_batch, dec_seq, d),
            jnp.stack(new_ckv, axis=1), jnp.stack(new_krope, axis=1), jnp.stack(new_gla, axis=1))
```

```python
import functools
import math

import numpy as np
import jax
import jax.numpy as jnp
from jax import lax
from jax.experimental import pallas as pl
from jax.experimental.pallas import tpu as pltpu

F32 = jnp.float32
BF16 = jnp.bfloat16

EPS = 1e-6
N_MOD = 9
GRID_W = 64
ROPE_BASE = 10000.0
GLA_HEADS = 4
GLA_DK = 64
GLA_DV = 128
GLA_QK = GLA_HEADS * GLA_DK
GLA_VW = GLA_HEADS * GLA_DV
GLA_GATE_RANK = 16
GLA_GATE_NORM = 16.0
GLA_CHUNK = 64
MLA_HEADS = 8
MLA_NOPE = 64
MLA_ROPE = 32
MLA_V = 64
MLA_QK_DIM = MLA_NOPE + MLA_ROPE
MLA_Q_RANK = 384
MLA_KV_RANK = 256
HEAD_PAD = 128
CMLP_CHUNK = 128
CMLP_GROUPS = 4
COND_PAD = 16

VMEM_LIMIT = 56 * 1024 * 1024


def _cparams(*sem):
    return pltpu.CompilerParams(dimension_semantics=sem, vmem_limit_bytes=VMEM_LIMIT)


def _resident(shape, index_map):
    return pl.BlockSpec(shape, index_map, pipeline_mode=pl.Buffered(1))


def _const_map(nd):
    return lambda *_: (0,) * nd


def _layer_spec(stack, layer):
    nd = stack.ndim - 1
    return _resident((None,) + stack.shape[1:], lambda *_: (layer,) + (0,) * nd)


def _silu(x):
    return x * jax.nn.sigmoid(x)


def _rms(x, g):
    return x * lax.rsqrt(jnp.mean(x * x, axis=-1, keepdims=True) + EPS) * g


def _modulate(x, g, shift, scale):
    return _rms(x, g) * (1.0 + scale) + shift


def _dot(a, b):
    return jnp.dot(a, b, preferred_element_type=F32)


def _dot_nt(a, b):
    return lax.dot_general(a, b, (((1,), (1,)), ((), ())), preferred_element_type=F32)


def _mod_kernel(c_ref, w_ref, b_ref, o_ref):
    s = _silu(c_ref[...]).astype(BF16)
    o_ref[...] = _dot(s, w_ref[...].astype(BF16)) + b_ref[...]


def _modulation_all(cond, ada_w, ada_b):
    depth, d, n = ada_w.shape
    tn = n // 4
    return pl.pallas_call(
        _mod_kernel,
        out_shape=jax.ShapeDtypeStruct((depth, COND_PAD, n), F32),
        grid=(depth, n // tn),
        in_specs=[
            pl.BlockSpec((COND_PAD, d), lambda l, j: (0, 0)),
            pl.BlockSpec((None, d, tn), lambda l, j: (l, 0, j)),
            pl.BlockSpec((None, 1, tn), lambda l, j: (l, 0, j)),
        ],
        out_specs=pl.BlockSpec((None, COND_PAD, tn), lambda l, j: (l, 0, j)),
        compiler_params=_cparams("arbitrary", "arbitrary"),
        name="adaln_modulation",
    )(cond, ada_w, ada_b.reshape(depth, 1, n))


class _Rows:
    def __init__(self, m_ctx, dec_batch, dec_seq, tm):
        assert m_ctx % tm == 0 and dec_seq % tm == 0
        self.tm = tm
        self.m_ctx = m_ctx
        self.m = m_ctx + dec_batch * dec_seq
        self.nb_ctx = m_ctx // tm
        self.nb_seq = dec_seq // tm
        self.nblocks = self.m // tm

    def cond(self, i):
        return jnp.where(i < self.nb_ctx, 0, 1 + (i - self.nb_ctx) // self.nb_seq)

    def rope_block(self, i):
        return jnp.where(i < self.nb_ctx, 0, 1 + (i - self.nb_ctx) % self.nb_seq)

    def mod_spec(self, layer, d):
        return pl.BlockSpec((None, None, N_MOD, d), lambda i: (layer, self.cond(i), 0, 0))

    def row_spec(self, width):
        return pl.BlockSpec((self.tm, width), lambda i: (i, 0))

    def ctx_spec(self, width):
        return pl.BlockSpec((self.tm, width), lambda i: (jnp.minimum(i, self.nb_ctx - 1), 0))

    def dec_spec(self, width):
        return pl.BlockSpec((self.tm, width), lambda i: (jnp.maximum(i - self.nb_ctx, 0), 0))

    def split_specs(self, width):
        return [self.ctx_spec(width), self.dec_spec(width)]

    def split_shapes(self, width, dtype):
        return [jax.ShapeDtypeStruct((self.m_ctx, width), dtype),
                jax.ShapeDtypeStruct((self.m - self.m_ctx, width), dtype)]


def _load_split(refs, is_ctx):
    if len(refs) == 1:
        return refs[0][...]
    return jnp.where(is_ctx, refs[0][...], refs[1][...])


FFN_STAGE_CHUNKS = 8


def _stage_weights_bf16(layer, hbm_refs, vmem_refs, stage_refs, sem):
    chunks = []
    for src, dst, stage in zip(hbm_refs, vmem_refs, stage_refs):
        rows = stage.shape[1]
        assert dst.shape[0] % rows == 0
        chunks += [(src, dst, stage, r0) for r0 in range(0, dst.shape[0], rows)]

    def copy(i):
        src, _, stage, r0 = chunks[i]
        return pltpu.make_async_copy(src.at[layer, pl.ds(r0, stage.shape[1]), :], stage.at[i % 2],
                                     sem.at[i % 2])

    copy(0).start()
    for i, (_, dst, stage, r0) in enumerate(chunks):
        if i + 1 < len(chunks):
            copy(i + 1).start()
        copy(i).wait()
        dst[pl.ds(r0, stage.shape[1]), :] = stage[i % 2].astype(BF16)


def _ffn_kernel(*refs, layer, row0, nb_ctx, split_in, split_out, premix):
    it = iter(refs)
    x_refs = [next(it) for _ in range(2 if split_in else 1)]
    mod_ref, g_ref = next(it), next(it)
    w_hbm = [next(it) for _ in range(3)]
    if premix:
        a_refs, b_refs = [next(it), next(it)], [next(it), next(it)]
        wo_ref = next(it)
    o_refs = [next(it) for _ in range(2 if split_out else 1)]
    wg_ref, wu_ref, wd_ref = w_vmem = [next(it) for _ in range(3)]
    stage_in, stage_down, sem = next(it), next(it), next(it)

    @pl.when(pl.program_id(0) == 0)
    def _():
        _stage_weights_bf16(layer, w_hbm, w_vmem, [stage_in, stage_in, stage_down], sem)

    is_ctx = pl.program_id(0) < nb_ctx
    x = _load_split(x_refs, is_ctx)
    if premix:
        half = wo_ref.shape[0] // 2
        mix = _dot(_load_split(a_refs, is_ctx), wo_ref[:half, :])
        mix = mix + _dot(_load_split(b_refs, is_ctx), wo_ref[half:, :])
        x = x + mod_ref[5:6, :] * mix
    shift = mod_ref[row0:row0 + 1, :]
    scale = mod_ref[row0 + 1:row0 + 2, :]
    gate = mod_ref[row0 + 2:row0 + 3, :]
    h = _modulate(x, g_ref[...], shift, scale).astype(BF16)
    a = _dot(h, wg_ref[...])
    u = _dot(h, wu_ref[...])
    act = (_silu(a) * u).astype(BF16)
    y = _dot(act, wd_ref[...])
    out = x + (0.5 * gate) * y
    if split_out:
        @pl.when(is_ctx)
        def _():
            o_refs[0][...] = out

        @pl.when(jnp.logical_not(is_ctx))
        def _():
            o_refs[1][...] = out
    else:
        o_refs[0][...] = out


def _ffn(xs, mod, layer, row0, g, wg, wu, wd, rows, split_out=False, premix=None):
    split_in = isinstance(xs, (tuple, list))
    xs = list(xs) if split_in else [xs]
    d = xs[0].shape[1]
    f = wg.shape[2]
    hbm = pl.BlockSpec(memory_space=pl.ANY)
    in_specs = (rows.split_specs(d) if split_in else [rows.row_spec(d)]) + [
        rows.mod_spec(layer, d),
        _resident((1, d), _const_map(2)),
        hbm, hbm, hbm,
    ]
    args = xs + [mod, g, wg, wu, wd]
    if premix is not None:
        gla, mla, w_out, j = premix
        in_specs += rows.split_specs(gla[0].shape[1]) + rows.split_specs(mla[0].shape[1])
        in_specs.append(_layer_spec(w_out, j))
        args += list(gla) + list(mla) + [w_out]
    return pl.pallas_call(
        functools.partial(_ffn_kernel, layer=layer, row0=row0, nb_ctx=rows.nb_ctx, split_in=split_in,
                          split_out=split_out, premix=premix is not None),
        out_shape=rows.split_shapes(d, F32) if split_out else jax.ShapeDtypeStruct((rows.m, d), F32),
        grid=(rows.nblocks,),
        in_specs=in_specs,
        out_specs=rows.split_specs(d) if split_out else rows.row_spec(d),
        scratch_shapes=[
            pltpu.VMEM((d, f), BF16), pltpu.VMEM((d, f), BF16), pltpu.VMEM((f, d), BF16),
            pltpu.VMEM((2, d // FFN_STAGE_CHUNKS, f), F32),
            pltpu.VMEM((2, f // FFN_STAGE_CHUNKS, d), F32),
            pltpu.SemaphoreType.DMA((2,)),
        ],
        compiler_params=_cparams("arbitrary"),
        name="ffn_swiglu",
    )(*args)


def _gelu_tanh(x):
    c = math.sqrt(2.0 / math.pi)
    return x * (0.5 * (1.0 + jnp.tanh(c * (x + 0.044715 * (x * x * x)))))


def _odd_kernel(x_ref, mod_ref, g_ref, win_ref, vg_ref, ws_ref, bst_ref, wout_ref, o_ref):
    x = x_ref[...]
    tm = x.shape[0]
    shift, scale, gate = mod_ref[3:4, :], mod_ref[4:5, :], mod_ref[5:6, :]
    h = _modulate(x, g_ref[...], shift, scale).astype(BF16)
    uv = _gelu_tanh(_dot(h, win_ref[...]))
    width = uv.shape[1] // 2
    u = uv[:, :width]
    v = _rms(uv[:, width:], vg_ref[...]).astype(BF16)
    gw = width // CMLP_GROUPS
    chunks = []
    for c in range(tm // CMLP_CHUNK):
        r0 = c * CMLP_CHUNK
        groups = []
        for g in range(CMLP_GROUPS):
            vg = v[r0:r0 + CMLP_CHUNK, g * gw:(g + 1) * gw]
            mixed = _dot(ws_ref[g], vg) + bst_ref[:, g:g + 1]
            groups.append(u[r0:r0 + CMLP_CHUNK, g * gw:(g + 1) * gw] * mixed)
        chunks.append(jnp.concatenate(groups, axis=1))
    z = jnp.concatenate(chunks, axis=0).astype(BF16)
    o_ref[...] = x + gate * _dot(z, wout_ref[...])


def _odd_mixer(x, mod, layer, layer_j, g, w_in, v_g, w_s, b_s_t, w_out, rows):
    m, d = x.shape
    return pl.pallas_call(
        _odd_kernel,
        out_shape=jax.ShapeDtypeStruct((m, d), F32),
        grid=(rows.nblocks,),
        in_specs=[
            rows.row_spec(d),
            rows.mod_spec(layer, d),
            _resident((1, d), _const_map(2)),
            _layer_spec(w_in, layer_j), _layer_spec(v_g, layer_j), _layer_spec(w_s, layer_j),
            _layer_spec(b_s_t, layer_j), _layer_spec(w_out, layer_j),
        ],
        out_specs=rows.row_spec(d),
        compiler_params=_cparams("arbitrary"),
        name="odd_gmlp",
    )(x, mod, g, w_in, v_g, w_s, b_s_t, w_out)


_C_CQ = 0
_C_CKV = _C_CQ + MLA_Q_RANK
_C_KA = _C_CKV + MLA_KV_RANK
_C_KB = _C_KA + HEAD_PAD
_C_Q = _C_KB + HEAD_PAD
_C_K = _C_Q + GLA_QK
_C_V = _C_K + GLA_QK
_C_OG = _C_V + GLA_VW
_C_END = _C_OG + GLA_VW


def _head_scale(x, live):
    ss = jnp.sum(jnp.where(live, x * x, 0.0), axis=-1, keepdims=True)
    return lax.rsqrt(ss * (1.0 / MLA_QK_DIM) + EPS)


def _even_in_kernel(x_ref, mod_ref, g_ref, win_ref, w2_ref, gb_ref, qag_ref, qb_ref, kvag_ref,
                    kvbk_ref, kvbv_ref, kng_ref, tq_ref, ta_ref, tb_ref,
                    gq_ref, gk_ref, gv_ref, og_ref, la_ref, qm_ref, km_ref, vm_ref, ckv_ref, kr_ref,
                    *, nb_ctx):
    is_ctx = pl.program_id(0) < nb_ctx
    x = x_ref[...]
    shift, scale = mod_ref[3:4, :], mod_ref[4:5, :]
    h = _modulate(x, g_ref[...], shift, scale).astype(BF16)

    proj_q = _dot(h, win_ref[:, _C_CQ:_C_CKV])
    proj_kv = _dot(h, win_ref[:, _C_CKV:_C_Q])
    ckv = _rms(proj_kv[:, :MLA_KV_RANK], kvag_ref[...])
    ka = proj_kv[:, _C_KA - _C_CKV:_C_KB - _C_CKV]
    kb = proj_kv[:, _C_KB - _C_CKV:]

    @pl.when(is_ctx)
    def _():
        ckv_ref[...] = ckv
        kr_ref[...] = ka[:, MLA_NOPE:MLA_QK_DIM]

    cq = _rms(proj_q, qag_ref[...]).astype(BF16)
    qn = _dot(cq, qb_ref[...])
    ckv_bf = ckv.astype(BF16)
    kn = _dot(ckv_bf, kvbk_ref[...])
    vm_t = _dot(ckv_bf, kvbv_ref[...])
    z = _dot(ka.astype(BF16), w2_ref[...]) + gb_ref[...]

    lane = lax.broadcasted_iota(jnp.int32, ka.shape, 1)
    live = lane < MLA_QK_DIM
    tq = tq_ref[...]

    def q_heads(heads):
        for hd in heads:
            sl = slice(hd * HEAD_PAD, (hd + 1) * HEAD_PAD)
            qh = qn[:, sl]
            qm_ref[:, sl] = (qh * tq * _head_scale(qh, live)).astype(BF16)

    is_rope = live & (lane >= MLA_NOPE)
    ss_rope = jnp.sum(jnp.where(is_rope, ka * ka, 0.0), axis=-1, keepdims=True)
    rot = ka * ta_ref[...] + kb * tb_ref[...]
    kng = kng_ref[...]

    def k_heads(heads):
        for hd in heads:
            sl = slice(hd * HEAD_PAD, (hd + 1) * HEAD_PAD)
            kh = kn[:, sl]
            ss = jnp.sum(kh * kh, axis=-1, keepdims=True) + ss_rope
            r = lax.rsqrt(ss * (1.0 / MLA_QK_DIM) + EPS)
            km_ref[:, sl] = ((kh * kng + rot) * r).astype(BF16)

    half = MLA_HEADS // 2
    gqk = _dot(h, win_ref[:, _C_Q:_C_V])
    q_heads(range(half))
    gq_ref[...] = gqk[:, :GLA_QK] * (GLA_DK ** -0.5)
    gk_ref[...] = gqk[:, GLA_QK:]
    gv = _dot(h, win_ref[:, _C_V:_C_OG])
    q_heads(range(half, MLA_HEADS))
    gv_ref[...] = gv.astype(BF16)
    og = _dot(h, win_ref[:, _C_OG:])
    k_heads(range(half))
    og_ref[...] = og
    vm_ref[...] = vm_t.T.astype(BF16)
    k_heads(range(half, MLA_HEADS))
    la_ref[...] = -(jnp.maximum(-z, 0.0) + jnp.log1p(jnp.exp(-jnp.abs(z)))) * (1.0 / GLA_GATE_NORM)


def _even_in(x, mod, layer, g, wts, rows):
    m, d = x.shape
    tm = rows.tm
    c2 = _const_map(2)
    rope_spec = pl.BlockSpec((tm, HEAD_PAD), lambda i: (rows.rope_block(i), 0))
    widths = [GLA_QK, GLA_QK, GLA_VW, GLA_VW, 2 * GLA_QK, MLA_HEADS * HEAD_PAD, MLA_HEADS * HEAD_PAD]
    dtypes = [F32, F32, BF16, F32, F32, BF16, BF16]
    names = ["w_in", "w2", "gate_b", "qa_g", "qb", "kva_g", "kvb_k", "kvb_v", "kn_g"]
    return pl.pallas_call(
        functools.partial(_even_in_kernel, nb_ctx=rows.nb_ctx),
        out_shape=[jax.ShapeDtypeStruct((m, w), t) for w, t in zip(widths, dtypes)]
        + [jax.ShapeDtypeStruct((MLA_HEADS * MLA_V, m), BF16),
           jax.ShapeDtypeStruct((rows.m_ctx, MLA_KV_RANK), F32),
           jax.ShapeDtypeStruct((rows.m_ctx, MLA_ROPE), F32)],
        grid=(rows.nblocks,),
        in_specs=[rows.row_spec(d), rows.mod_spec(layer, d), _resident((1, d), c2)]
        + [_resident(wts[n].shape, c2) for n in names] + [rope_spec] * 3,
        out_specs=[rows.row_spec(w) for w in widths]
        + [pl.BlockSpec((MLA_HEADS * MLA_V, tm), lambda i: (0, i)),
           rows.ctx_spec(MLA_KV_RANK), rows.ctx_spec(MLA_ROPE)],
        compiler_params=_cparams("arbitrary"),
        name="even_in_proj",
    )(x, mod, g, *[wts[n] for n in names], wts["t_q"], wts["t_a"], wts["t_b"])


def _ctx_kv_kernel(ckv_ref, kr_ref, kvbk_ref, kvbv_ref, kng_ref, k_ref, v_ref):
    ckv_bf = ckv_ref[...].astype(BF16)
    v_ref[...] = _dot(ckv_bf, kvbv_ref[...]).T.astype(BF16)
    kn = _dot(ckv_bf, kvbk_ref[...])
    kr = kr_ref[...]
    lane = lax.broadcasted_iota(jnp.int32, kr.shape, 1)
    live = lane < MLA_QK_DIM
    kng = kng_ref[...]
    for hd in range(MLA_HEADS):
        sl = slice(hd * HEAD_PAD, (hd + 1) * HEAD_PAD)
        kh = kn[:, sl] + kr
        k_ref[:, sl] = (kh * kng * _head_scale(kh, live)).astype(BF16)


def _ctx_kv(cache_ckv, cache_krope_blk, kvb_k, kvb_v, kn_g):
    nb, ne, p, r = cache_ckv.shape
    return pl.pallas_call(
        _ctx_kv_kernel,
        out_shape=[jax.ShapeDtypeStruct((ne, nb, p, MLA_HEADS * HEAD_PAD), BF16),
                   jax.ShapeDtypeStruct((ne, nb, MLA_HEADS * MLA_V, p), BF16)],
        grid=(ne, nb),
        in_specs=[
            pl.BlockSpec((None, None, p, r), lambda j, b: (b, j, 0, 0)),
            pl.BlockSpec((None, None, p, HEAD_PAD), lambda j, b: (b, j, 0, 0)),
            pl.BlockSpec((None,) + kvb_k.shape[1:], lambda j, b: (j, 0, 0)),
            pl.BlockSpec((None,) + kvb_v.shape[1:], lambda j, b: (j, 0, 0)),
            pl.BlockSpec((None,) + kn_g.shape[1:], lambda j, b: (j, 0, 0)),
        ],
        out_specs=[pl.BlockSpec((None, None, p, MLA_HEADS * HEAD_PAD), lambda j, b: (j, b, 0, 0)),
                   pl.BlockSpec((None, None, MLA_HEADS * MLA_V, p), lambda j, b: (j, b, 0, 0))],
        compiler_params=_cparams("arbitrary", "arbitrary"),
        name="ctx_kv",
    )(cache_ckv, cache_krope_blk, kvb_k, kvb_v, kn_g)


def _split3(x):
    hi = x.astype(BF16)
    r = x - hi.astype(F32)
    mid = r.astype(BF16)
    lo = (r - mid.astype(F32)).astype(BF16)
    return hi, mid, lo


def _block_row_bcast(a, blk, off):
    c, n = a.shape
    if blk >= 8:
        pieces = [jnp.broadcast_to(a[b * blk + off:b * blk + off + 1, :], (blk, n))
                  for b in range(c // blk)]
        return pieces[0] if len(pieces) == 1 else jnp.concatenate(pieces, axis=0)
    a3 = a.reshape(c // 8, 8, n)
    sub = lax.broadcasted_iota(jnp.int32, a3.shape, 1) // blk
    out = jnp.broadcast_to(a3[:, off:off + 1, :], a3.shape)
    for s in range(1, 8 // blk):
        cand = jnp.broadcast_to(a3[:, s * blk + off:s * blk + off + 1, :], a3.shape)
        out = jnp.where(sub == s, cand, out)
    return out.reshape(c, n)


def _head_stack(a, head_w):
    lane_head = lax.broadcasted_iota(jnp.int32, a.shape, 1) // head_w
    zero = jnp.zeros_like(a)
    return jnp.concatenate([jnp.where(lane_head == h, a, zero) for h in range(GLA_HEADS)], axis=0)


def _run_skewed(stage_gens, skew=1):
    done = [False] * len(stage_gens)
    tick = 0
    while not all(done):
        for u, gen in enumerate(stage_gens):
            if tick >= u * skew and not done[u]:
                try:
                    next(gen)
                except StopIteration:
                    done[u] = True
        tick += 1


def _gla_phase_a(c, q_ref, k_ref, v_ref, la_ref, tri_ref, pair_ref, oacc_ref, qst_ref, ds_ref, dc_ref):
    C = GLA_CHUNK
    r0 = pl.multiple_of(c * C, C)
    q = q_ref[pl.ds(r0, C), :]
    k = k_ref[pl.ds(r0, C), :]
    la = la_ref[pl.ds(r0, C), :]
    la_f, la_b = la[:, :GLA_QK], la[:, GLA_QK:]
    cums = []
    for d, la_d in enumerate((la_f, la_b)):
        hi, mid, lo = _split3(la_d)
        tri = tri_ref[d]
        cums.append(_dot(tri, hi) + _dot(tri, mid) + _dot(tri, lo))
    cum_f, cum_b = cums
    k_stack = _head_stack(k.astype(BF16), GLA_DK)
    yield

    row = lax.broadcasted_iota(jnp.int32, (C, 1), 0)
    pair_m = pair_ref[...]
    att = jnp.where(pair_m == 0, 2.0 * _dot_nt(q.astype(BF16), k_stack), 0.0)
    m = C // 2
    while m >= 1:
        blk = 2 * m
        upper = (row % blk) >= m
        if m == 1:
            qm = (q * jnp.exp(jnp.where(upper, la_f, la_b))).astype(BF16)
            km_stack = k_stack
        else:
            ref_f = _block_row_bcast(cum_f, blk, m - 1)
            ref_b = _block_row_bcast(cum_b, blk, m)
            eq = jnp.where(upper, cum_f - ref_f, cum_b - ref_b)
            ek = jnp.where(upper, ref_b - cum_b, ref_f - cum_f)
            qm = (q * jnp.exp(eq)).astype(BF16)
            km_stack = _head_stack((k * jnp.exp(ek)).astype(BF16), GLA_DK)
        yield
        att = jnp.where(pair_m == m, _dot_nt(qm, km_stack), att)
        m //= 2

    v = v_ref[pl.ds(r0, C), :]
    v_stack = _head_stack(v, GLA_DV)
    qcs = [(q * jnp.exp(cum)).astype(BF16) for cum in cums]
    yield
    oacc_ref[pl.ds(r0, C), :] = _dot(att.astype(BF16), v_stack)
    for d, cum in enumerate(cums):
        last = cum[C - 1:C, :] if d == 0 else cum[0:1, :]
        qst_ref[d, c] = _head_stack(qcs[d], GLA_DK)
        kct = (k * jnp.exp(last - cum)).T.astype(BF16)
        decay_col = jnp.exp(jnp.broadcast_to(last, (8, GLA_QK))).T[:, 0:1]
        dc_ref[d, c] = jnp.broadcast_to(decay_col, (GLA_QK, GLA_DV))
        yield
        ds_ref[d, c] = jnp.concatenate(
            [_dot(kct[h * GLA_DK:(h + 1) * GLA_DK, :], v[:, h * GLA_DV:(h + 1) * GLA_DV])
             for h in range(GLA_HEADS)], axis=0)


def _gla_phase_c(c, og_ref, gg, oacc_ref, qst_ref, sb_ref, o_ref):
    C = GLA_CHUNK
    r0 = pl.multiple_of(c * C, C)
    q_cat = jnp.concatenate([qst_ref[0, c], qst_ref[1, c]], axis=1)
    s_cat = jnp.concatenate([sb_ref[0, c], sb_ref[1, c]], axis=0)
    inter = _dot(q_cat, s_cat)
    yield
    for h in range(GLA_HEADS):
        sl = slice(h * GLA_DV, (h + 1) * GLA_DV)
        o = oacc_ref[pl.ds(r0, C), sl] + inter[h * C:(h + 1) * C, :]
        o_ref[pl.ds(r0, C), sl] = (_rms(o, gg) * _silu(og_ref[pl.ds(r0, C), sl])).astype(BF16)


def _gla_kernel(*refs, has_state):
    it = iter(refs)
    q_ref, k_ref, v_ref, la_ref, og_ref = (next(it) for _ in range(5))
    s0_ref = next(it) if has_state else None
    tri_ref, pair_ref, gg_ref, o_ref, sout_ref = (next(it) for _ in range(5))
    oacc_ref, qst_ref, ds_ref, dc_ref, sb_ref, st_ref = (next(it) for _ in range(6))

    nc = q_ref.shape[0] // GLA_CHUNK
    per_step = 4

    def phase_a(i, carry):
        _run_skewed([_gla_phase_a(per_step * i + u, q_ref, k_ref, v_ref, la_ref, tri_ref, pair_ref, oacc_ref,
                                  qst_ref, ds_ref, dc_ref) for u in range(per_step)])
        return carry

    lax.fori_loop(0, nc // per_step, phase_a, 0)

    for d in range(2):
        for h in range(GLA_HEADS):
            rows = slice(h * GLA_DK, (h + 1) * GLA_DK)
            st_ref[d, rows, :] = s0_ref[d, h] if has_state else jnp.zeros((GLA_DK, GLA_DV), F32)

    def phase_b(c, carry):
        for d in range(2):
            cc = c if d == 0 else nc - 1 - c
            s = st_ref[d]
            sb_ref[d, cc] = s.astype(BF16)
            st_ref[d] = s * dc_ref[d, cc] + ds_ref[d, cc]
        return carry

    lax.fori_loop(0, nc, phase_b, 0)
    for d in range(2):
        for h in range(GLA_HEADS):
            sout_ref[d, h] = st_ref[d, h * GLA_DK:(h + 1) * GLA_DK, :]

    gg = gg_ref[...]

    def phase_c(i, carry):
        _run_skewed([_gla_phase_c(per_step * i + u, og_ref, gg, oacc_ref, qst_ref, sb_ref, o_ref)
                     for u in range(per_step)])
        return carry

    lax.fori_loop(0, nc // per_step, phase_c, 0)


def _gla(gq, gk, gv, la, og, state, layer_j, tri, gg, nbatch, t, row_off):
    assert t % (2 * GLA_CHUNK) == 0
    b0 = row_off // t
    nc = t // GLA_CHUNK
    seq = lambda w: pl.BlockSpec((t, w), lambda b: (b0 + b, 0))
    in_specs = [seq(GLA_QK), seq(GLA_QK), seq(GLA_VW), seq(2 * GLA_QK), seq(GLA_VW)]
    args = [gq, gk, gv, la, og]
    if state is not None:
        in_specs.append(pl.BlockSpec((None, None, 2, GLA_HEADS, GLA_DK, GLA_DV),
                                     lambda b: (b, layer_j, 0, 0, 0, 0)))
        args.append(state)
    tri, pair_m = tri
    in_specs += [_resident(tri.shape, _const_map(3)), _resident(pair_m.shape, _const_map(2)),
                 _resident(gg.shape, _const_map(2))]
    args += [tri, pair_m, gg]
    return pl.pallas_call(
        functools.partial(_gla_kernel, has_state=state is not None),
        out_shape=[jax.ShapeDtypeStruct((nbatch * t, GLA_VW), BF16),
                   jax.ShapeDtypeStruct((nbatch, 2, GLA_HEADS, GLA_DK, GLA_DV), F32)],
        grid=(nbatch,),
        in_specs=in_specs,
        out_specs=[pl.BlockSpec((t, GLA_VW), lambda b: (b, 0)),
                   pl.BlockSpec((None, 2, GLA_HEADS, GLA_DK, GLA_DV), lambda b: (b, 0, 0, 0, 0))],
        scratch_shapes=[
            pltpu.VMEM((t, GLA_VW), F32),
            pltpu.VMEM((2, nc, GLA_HEADS * GLA_CHUNK, GLA_QK), BF16),
            pltpu.VMEM((2, nc, GLA_QK, GLA_DV), F32),
            pltpu.VMEM((2, nc, GLA_QK, GLA_DV), F32),
            pltpu.VMEM((2, nc, GLA_QK, GLA_DV), BF16),
            pltpu.VMEM((2, GLA_QK, GLA_DV), F32),
        ],
        compiler_params=_cparams("arbitrary"),
        name="gla_scan",
    )(*args)


def _mla_kernel(*refs, has_ctx):
    it = iter(refs)
    q_ref, k_ref, v_ref = next(it), next(it), next(it)
    kc_ref, vc_ref = (next(it), next(it)) if has_ctx else (None, None)
    o_ref = next(it)
    def scores(hd):
        sl = slice(hd * HEAD_PAD, (hd + 1) * HEAD_PAD)
        q = q_ref[:, sl]
        return [_dot_nt(r[:, sl], q) for r in ((k_ref, kc_ref) if has_ctx else (k_ref,))]

    def softmax(ss):
        mx = functools.reduce(jnp.maximum, [jnp.max(s, axis=0, keepdims=True) for s in ss])
        ps = [jnp.exp2(s - mx) for s in ss]
        den = sum(jnp.sum(p, axis=0, keepdims=True) for p in ps)
        return [p.astype(BF16) for p in ps], den

    def values(hd, ps, den):
        vrows = slice(hd * MLA_V, (hd + 1) * MLA_V)
        o = sum(_dot(r[vrows, :], p) for r, p in zip((v_ref, vc_ref), ps))
        return o / den

    s_q, p_q, outs = {}, {}, []
    for step in range(MLA_HEADS + 2):
        if step < MLA_HEADS:
            s_q[step] = scores(step)
        if 0 <= step - 1 < MLA_HEADS:
            p_q[step - 1] = softmax(s_q.pop(step - 1))
        if 0 <= step - 2 < MLA_HEADS:
            outs.append(values(step - 2, *p_q.pop(step - 2)))
    o_ref[...] = jnp.concatenate(outs, axis=0).T.astype(BF16)


def _mla(qm, km, vm, kc, vc, layer_j, nbatch, t, row_off, tq):
    nq = t // tq
    q0, b0 = row_off // tq, row_off // t
    qspec = lambda w: pl.BlockSpec((tq, w), lambda b, i: (q0 + b * nq + i, 0))
    kspec = lambda w: pl.BlockSpec((t, w), lambda b, i: (b0 + b, 0))
    vspec = pl.BlockSpec((MLA_HEADS * MLA_V, t), lambda b, i: (0, b0 + b))
    in_specs = [qspec(MLA_HEADS * HEAD_PAD), kspec(MLA_HEADS * HEAD_PAD), vspec]
    args = [qm, km, vm]
    if kc is not None:
        in_specs += [pl.BlockSpec((None, None) + kc.shape[2:], lambda b, i: (layer_j, b, 0, 0)),
                     pl.BlockSpec((None, None) + vc.shape[2:], lambda b, i: (layer_j, b, 0, 0))]
        args += [kc, vc]
    return pl.pallas_call(
        functools.partial(_mla_kernel, has_ctx=kc is not None),
        out_shape=jax.ShapeDtypeStruct((nbatch * t, MLA_HEADS * MLA_V), BF16),
        grid=(nbatch, nq),
        in_specs=in_specs,
        out_specs=pl.BlockSpec((tq, MLA_HEADS * MLA_V), lambda b, i: (b * nq + i, 0)),
        compiler_params=_cparams("arbitrary", "arbitrary"),
        name="mla_attention",
    )(*args)


def _swap_signed(a):
    pairs = a.reshape(a.shape[:-1] + (a.shape[-1] // 2, 2))
    return jnp.stack([-pairs[..., 1], pairs[..., 0]], axis=-1).reshape(a.shape)


def _swap_pairs(a):
    pairs = a.reshape(a.shape[:-1] + (a.shape[-1] // 2, 2))
    return pairs[..., ::-1].reshape(a.shape)


def _pack_even_weights(j, cos, sin, even_w_in, gla_gate_w2, gla_gate_b, mla_qa_g, mla_qb_w, mla_qn_g,
                       mla_kva_g, mla_kvb_w, mla_kn_g):
    d = even_w_in.shape[1]
    w = even_w_in[j]
    o = 0
    parts = {}
    for name, width in (("q", GLA_QK), ("k", GLA_QK), ("v", GLA_VW), ("og", GLA_VW),
                        ("gl", 2 * GLA_GATE_RANK), ("cq", MLA_Q_RANK), ("ckv", MLA_KV_RANK),
                        ("kr", MLA_ROPE)):
        parts[name] = w[:, o:o + width]
        o += width
    zeros = lambda n: jnp.zeros((d, n), w.dtype)
    kr, kr_sw = parts["kr"], _swap_signed(parts["kr"])
    blk_a = jnp.concatenate([parts["gl"], zeros(MLA_NOPE - 2 * GLA_GATE_RANK), kr, kr_sw], axis=1)
    blk_b = jnp.concatenate([zeros(MLA_NOPE), kr_sw, kr], axis=1)
    w_in = jnp.concatenate([parts["cq"], parts["ckv"], blk_a, blk_b, parts["q"], parts["k"], parts["v"],
                            parts["og"]], axis=1).astype(BF16)
    assert w_in.shape[1] == _C_END

    w2 = jnp.zeros((HEAD_PAD, 2 * GLA_QK), F32)
    w2 = w2.at[:GLA_GATE_RANK, :GLA_QK].set(gla_gate_w2[j, 0])
    w2 = w2.at[GLA_GATE_RANK:2 * GLA_GATE_RANK, GLA_QK:].set(gla_gate_w2[j, 1])

    pad_head = lambda a: jnp.pad(a, [(0, 0)] * (a.ndim - 1) + [(0, HEAD_PAD - a.shape[-1])])
    qb = mla_qb_w[j]
    qb = jnp.concatenate([qb, _swap_signed(qb[..., MLA_NOPE:])], axis=-1)
    qb = qb.reshape(MLA_Q_RANK, MLA_HEADS * HEAD_PAD)
    kvb = mla_kvb_w[j]
    kvb_k = pad_head(kvb[..., :MLA_NOPE]).reshape(MLA_KV_RANK, MLA_HEADS * HEAD_PAD)
    kvb_v = kvb[..., MLA_NOPE:].reshape(MLA_KV_RANK, MLA_HEADS * MLA_V)

    def rot_tables(g):
        n = cos.shape[0]
        g_n = jnp.broadcast_to(g[:MLA_NOPE], (n, MLA_NOPE))
        g_c = g[MLA_NOPE:] * cos
        g_s = _swap_pairs(g[MLA_NOPE:]) * sin
        return g_n, g_c, g_s

    qn_n, qn_c, qn_s = rot_tables(mla_qn_g[j])
    kn_n, kn_c, kn_s = rot_tables(mla_kn_g[j])
    kzero = jnp.zeros_like(kn_n)
    return {
        "w_in": w_in,
        "w2": w2.astype(BF16),
        "gate_b": gla_gate_b[j].reshape(1, 2 * GLA_QK),
        "qa_g": mla_qa_g[j][None, :],
        "qb": qb.astype(BF16),
        "kva_g": mla_kva_g[j][None, :],
        "kvb_k": kvb_k.astype(BF16),
        "kvb_v": kvb_v.astype(BF16),
        "kn_g": pad_head(mla_kn_g[j][:MLA_NOPE])[None, :],
        "kn_g_ctx": jnp.concatenate([mla_kn_g[j], mla_kn_g[j][MLA_NOPE:]])[None, :],
        "t_q": jnp.concatenate([qn_n, qn_c, qn_s], axis=1) * (MLA_QK_DIM ** -0.5 * math.log2(math.e)),
        "t_a": jnp.concatenate([kzero, kn_c, kn_s], axis=1),
        "t_b": jnp.concatenate([kzero, kn_s, kn_c], axis=1),
    }


def _rope_tables(n, tm):
    pairs = MLA_ROPE // 4
    pos = np.arange(n)
    inv = ROPE_BASE ** (-jnp.arange(pairs, dtype=F32) / pairs)
    row = jnp.asarray(pos // GRID_W, F32)
    col = jnp.asarray(pos % GRID_W, F32)
    ang = jnp.concatenate([row[:, None] * inv, col[:, None] * inv], axis=-1)
    cos = jnp.repeat(jnp.cos(ang), 2, axis=-1)
    sin = jnp.repeat(jnp.sin(ang), 2, axis=-1)
    cos = jnp.pad(cos, ((tm, 0), (0, 0)), constant_values=1.0)
    sin = jnp.pad(sin, ((tm, 0), (0, 0)))
    return cos, sin


def _tri_masks():
    r = np.arange(GLA_CHUNK)
    lower = (r[None, :] <= r[:, None]).astype(np.float32)
    diff = r[:, None] ^ r[None, :]
    level = np.where(diff > 0, 1 << np.floor(np.log2(np.maximum(diff, 1))).astype(np.int64), 0)
    pair_m = np.tile(level, (1, GLA_HEADS)).astype(np.int32)
    return jnp.asarray(np.stack([lower, lower.T]), BF16), jnp.asarray(pair_m)


def _pick_tile(pref, *sizes):
    tm = pref
    while any(s % tm for s in sizes):
        tm //= 2
    return tm


def kernel(x_prompt, x_sample, c, c_ctx, cache_ckv, cache_krope, state_gla, ada_w, ada_b, norm_g,
           ffn1_wg, ffn1_wu, ffn1_wd, ffn2_wg, ffn2_wu, ffn2_wd, even_w_in, even_w_out,
           gla_gate_w2, gla_gate_b, gla_norm_g, mla_qa_g, mla_qb_w, mla_kva_g, mla_kvb_w,
           mla_qn_g, mla_kn_g, odd_w_in, odd_v_g, odd_ws, odd_bs, odd_w_out):
    batch, seq, d = x_prompt.shape
    dec_batch, dec_seq, _ = x_sample.shape
    depth = ada_w.shape[0]
    n_even = even_w_in.shape[0]
    m_ctx, m_dec = batch * seq, dec_batch * dec_seq
    assert 1 + dec_batch <= COND_PAD and seq % CMLP_CHUNK == 0 and dec_seq % CMLP_CHUNK == 0

    rows_ffn = _Rows(m_ctx, dec_batch, dec_seq, _pick_tile(512, m_ctx, dec_seq))
    rows_mix = _Rows(m_ctx, dec_batch, dec_seq, _pick_tile(512, m_ctx, dec_seq))

    cond = jnp.concatenate([c_ctx[None, :], c, jnp.zeros((COND_PAD - 1 - dec_batch, d), F32)], axis=0)
    mod = _modulation_all(cond, ada_w, ada_b).reshape(depth, COND_PAD, N_MOD, d)

    cos_t, sin_t = _rope_tables(dec_seq, rows_mix.tm)
    tri = _tri_masks()
    krope_blk = jnp.concatenate([jnp.zeros(cache_krope.shape[:-1] + (MLA_NOPE,), F32), cache_krope,
                                 cache_krope], axis=-1)
    even_wts = [_pack_even_weights(j, cos_t, sin_t, even_w_in, gla_gate_w2, gla_gate_b, mla_qa_g, mla_qb_w,
                                   mla_qn_g, mla_kva_g, mla_kvb_w, mla_kn_g) for j in range(n_even)]
    kc, vc = _ctx_kv(cache_ckv, krope_blk,
                     jnp.stack([w["kvb_k"] for w in even_wts]),
                     jnp.stack([w["kvb_v"] for w in even_wts]),
                     jnp.stack([w["kn_g_ctx"] for w in even_wts]))

    ffn1 = (ffn1_wg, ffn1_wu, ffn1_wd)
    ffn2 = (ffn2_wg, ffn2_wu, ffn2_wd)
    w_out_bf = even_w_out.astype(BF16)
    odd_wts = (odd_w_in.astype(BF16), odd_v_g[:, None, :], odd_ws.astype(BF16),
               jnp.swapaxes(odd_bs, 1, 2), odd_w_out.astype(BF16))

    x = (x_prompt.reshape(m_ctx, d), x_sample.reshape(m_dec, d))
    new_ckv, new_krope, new_gla = [], [], []
    for i in range(depth):
        j = i // 2
        x = _ffn(x, mod, i, 0, norm_g[i, 0][None, :], *ffn1, rows_ffn)
        g_mix = norm_g[i, 1][None, :]
        premix = None
        if i % 2 == 0:
            gq, gk, gv, og, la, qm, km, vm, ckv, kr = _even_in(x, mod, i, g_mix, even_wts[j], rows_mix)
            gg = gla_norm_g[j][None, :]
            gla_ctx, st = _gla(gq, gk, gv, la, og, None, j, tri, gg, batch, seq, 0)
            gla_dec, _ = _gla(gq, gk, gv, la, og, state_gla, j, tri, gg, dec_batch, dec_seq, m_ctx)
            mla_ctx = _mla(qm, km, vm, None, None, j, batch, seq, 0, _pick_tile(256, seq))
            mla_dec = _mla(qm, km, vm, kc, vc, j, dec_batch, dec_seq, m_ctx, _pick_tile(512, dec_seq))
            premix = ((gla_ctx, gla_dec), (mla_ctx, mla_dec), w_out_bf, j)
            new_ckv.append(ckv.reshape(batch, seq, MLA_KV_RANK))
            new_krope.append(kr.reshape(batch, seq, MLA_ROPE))
            new_gla.append(st)
        else:
            x = _odd_mixer(x, mod, i, j, g_mix, *odd_wts, rows_mix)
        x = _ffn(x, mod, i, 6, norm_g[i, 2][None, :], *ffn2, rows_ffn, split_out=(i == depth - 1),
                 premix=premix)

    y_prompt, y_sample = x
    return (y_prompt.reshape(batch, seq, d), y_sample.reshape(dec_batch, dec_seq, d),
            jnp.stack(new_ckv, axis=1), jnp.stack(new_krope, axis=1), jnp.stack(new_gla, axis=1))
```

```python
import functools
import math

import numpy as np
import jax
import jax.numpy as jnp
from jax import lax
from jax.experimental import pallas as pl
from jax.experimental.pallas import tpu as pltpu

F32 = jnp.float32
BF16 = jnp.bfloat16

EPS = 1e-6
N_MOD = 9
GRID_W = 64
ROPE_BASE = 10000.0
GLA_HEADS = 4
GLA_DK = 64
GLA_DV = 128
GLA_QK = GLA_HEADS * GLA_DK
GLA_VW = GLA_HEADS * GLA_DV
GLA_GATE_RANK = 16
GLA_GATE_NORM = 16.0
GLA_CHUNK = 64
MLA_HEADS = 8
MLA_NOPE = 64
MLA_ROPE = 32
MLA_V = 64
MLA_QK_DIM = MLA_NOPE + MLA_ROPE
MLA_Q_RANK = 384
MLA_KV_RANK = 256
HEAD_PAD = 128
CMLP_CHUNK = 128
CMLP_GROUPS = 4
COND_PAD = 16

VMEM_LIMIT = 56 * 1024 * 1024


def _cparams(*sem):
    return pltpu.CompilerParams(dimension_semantics=sem, vmem_limit_bytes=VMEM_LIMIT)


def _resident(shape, index_map):
    return pl.BlockSpec(shape, index_map, pipeline_mode=pl.Buffered(1))


def _const_map(nd):
    return lambda *_: (0,) * nd


def _layer_spec(stack, layer):
    nd = stack.ndim - 1
    return _resident((None,) + stack.shape[1:], lambda *_: (layer,) + (0,) * nd)


def _silu(x):
    return x * jax.nn.sigmoid(x)


def _rms(x, g):
    return x * lax.rsqrt(jnp.mean(x * x, axis=-1, keepdims=True) + EPS) * g


def _modulate(x, g, shift, scale):
    return _rms(x, g) * (1.0 + scale) + shift


def _dot(a, b):
    return jnp.dot(a, b, preferred_element_type=F32)


def _dot_nt(a, b):
    return lax.dot_general(a, b, (((1,), (1,)), ((), ())), preferred_element_type=F32)


def _mod_kernel(c_ref, w_ref, b_ref, o_ref):
    s = _silu(c_ref[...]).astype(BF16)
    o_ref[...] = _dot(s, w_ref[...].astype(BF16)) + b_ref[...]


def _modulation_all(cond, ada_w, ada_b):
    depth, d, n = ada_w.shape
    tn = n // 4
    return pl.pallas_call(
        _mod_kernel,
        out_shape=jax.ShapeDtypeStruct((depth, COND_PAD, n), F32),
        grid=(depth, n // tn),
        in_specs=[
            pl.BlockSpec((COND_PAD, d), lambda l, j: (0, 0)),
            pl.BlockSpec((None, d, tn), lambda l, j: (l, 0, j)),
            pl.BlockSpec((None, 1, tn), lambda l, j: (l, 0, j)),
        ],
        out_specs=pl.BlockSpec((None, COND_PAD, tn), lambda l, j: (l, 0, j)),
        compiler_params=_cparams("arbitrary", "arbitrary"),
        name="adaln_modulation",
    )(cond, ada_w, ada_b.reshape(depth, 1, n))


class _Rows:
    def __init__(self, m_ctx, dec_batch, dec_seq, tm):
        assert m_ctx % tm == 0 and dec_seq % tm == 0
        self.tm = tm
        self.m_ctx = m_ctx
        self.m = m_ctx + dec_batch * dec_seq
        self.nb_ctx = m_ctx // tm
        self.nb_seq = dec_seq // tm
        self.nblocks = self.m // tm

    def cond(self, i):
        return jnp.where(i < self.nb_ctx, 0, 1 + (i - self.nb_ctx) // self.nb_seq)

    def rope_block(self, i):
        return jnp.where(i < self.nb_ctx, 0, 1 + (i - self.nb_ctx) % self.nb_seq)

    def mod_spec(self, layer, d):
        return pl.BlockSpec((None, None, N_MOD, d), lambda i: (layer, self.cond(i), 0, 0))

    def row_spec(self, width):
        return pl.BlockSpec((self.tm, width), lambda i: (i, 0))

    def ctx_spec(self, width):
        return pl.BlockSpec((self.tm, width), lambda i: (jnp.minimum(i, self.nb_ctx - 1), 0))

    def dec_spec(self, width):
        return pl.BlockSpec((self.tm, width), lambda i: (jnp.maximum(i - self.nb_ctx, 0), 0))

    def split_specs(self, width):
        return [self.ctx_spec(width), self.dec_spec(width)]

    def split_shapes(self, width, dtype):
        return [jax.ShapeDtypeStruct((self.m_ctx, width), dtype),
                jax.ShapeDtypeStruct((self.m - self.m_ctx, width), dtype)]


def _load_split(refs, is_ctx):
    if len(refs) == 1:
        return refs[0][...]
    return jnp.where(is_ctx, refs[0][...], refs[1][...])


FFN_STAGE_CHUNKS = 16
FFN_STAGE_SLOTS = 4


def _stage_weights_bf16(layer, hbm_refs, vmem_refs, stage_refs, sem):
    chunks = []
    for src, dst, stage in zip(hbm_refs, vmem_refs, stage_refs):
        rows = stage.shape[1]
        assert dst.shape[0] % rows == 0
        chunks += [(src, dst, stage, r0) for r0 in range(0, dst.shape[0], rows)]
    slots = sem.shape[0]
    ahead = slots - 1

    def copy(i):
        src, _, stage, r0 = chunks[i]
        return pltpu.make_async_copy(src.at[layer, pl.ds(r0, stage.shape[1]), :], stage.at[i % slots],
                                     sem.at[i % slots])

    for i in range(min(ahead, len(chunks))):
        copy(i).start()
    for i, (_, dst, stage, r0) in enumerate(chunks):
        if i + ahead < len(chunks):
            copy(i + ahead).start()
        copy(i).wait()
        dst[pl.ds(r0, stage.shape[1]), :] = stage[i % slots].astype(BF16)


def _ffn_kernel(*refs, layer, row0, nb_ctx, split_in, split_out, premix):
    it = iter(refs)
    x_refs = [next(it) for _ in range(2 if split_in else 1)]
    mod_ref, g_ref = next(it), next(it)
    w_hbm = [next(it) for _ in range(3)]
    if premix:
        a_refs, b_refs = [next(it), next(it)], [next(it), next(it)]
        wo_ref = next(it)
    o_refs = [next(it) for _ in range(2 if split_out else 1)]
    wg_ref, wu_ref, wd_ref = w_vmem = [next(it) for _ in range(3)]
    stage_in, stage_down, sem = next(it), next(it), next(it)

    @pl.when(pl.program_id(0) == 0)
    def _():
        _stage_weights_bf16(layer, w_hbm, w_vmem, [stage_in, stage_in, stage_down], sem)

    is_ctx = pl.program_id(0) < nb_ctx
    x = _load_split(x_refs, is_ctx)
    if premix:
        half = wo_ref.shape[0] // 2
        mix = _dot(_load_split(a_refs, is_ctx), wo_ref[:half, :])
        mix = mix + _dot(_load_split(b_refs, is_ctx), wo_ref[half:, :])
        x = x + mod_ref[5:6, :] * mix
    shift = mod_ref[row0:row0 + 1, :]
    scale = mod_ref[row0 + 1:row0 + 2, :]
    gate = mod_ref[row0 + 2:row0 + 3, :]
    h = _modulate(x, g_ref[...], shift, scale).astype(BF16)
    a = _dot(h, wg_ref[...])
    u = _dot(h, wu_ref[...])
    act = (_silu(a) * u).astype(BF16)
    y = _dot(act, wd_ref[...])
    out = x + (0.5 * gate) * y
    if split_out:
        @pl.when(is_ctx)
        def _():
            o_refs[0][...] = out

        @pl.when(jnp.logical_not(is_ctx))
        def _():
            o_refs[1][...] = out
    else:
        o_refs[0][...] = out


def _ffn(xs, mod, layer, row0, g, wg, wu, wd, rows, split_out=False, premix=None):
    split_in = isinstance(xs, (tuple, list))
    xs = list(xs) if split_in else [xs]
    d = xs[0].shape[1]
    f = wg.shape[2]
    hbm = pl.BlockSpec(memory_space=pl.ANY)
    in_specs = (rows.split_specs(d) if split_in else [rows.row_spec(d)]) + [
        rows.mod_spec(layer, d),
        _resident((1, d), _const_map(2)),
        hbm, hbm, hbm,
    ]
    args = xs + [mod, g, wg, wu, wd]
    if premix is not None:
        gla, mla, w_out, j = premix
        in_specs += rows.split_specs(gla[0].shape[1]) + rows.split_specs(mla[0].shape[1])
        in_specs.append(_layer_spec(w_out, j))
        args += list(gla) + list(mla) + [w_out]
    return pl.pallas_call(
        functools.partial(_ffn_kernel, layer=layer, row0=row0, nb_ctx=rows.nb_ctx, split_in=split_in,
                          split_out=split_out, premix=premix is not None),
        out_shape=rows.split_shapes(d, F32) if split_out else jax.ShapeDtypeStruct((rows.m, d), F32),
        grid=(rows.nblocks,),
        in_specs=in_specs,
        out_specs=rows.split_specs(d) if split_out else rows.row_spec(d),
        scratch_shapes=[
            pltpu.VMEM((d, f), BF16), pltpu.VMEM((d, f), BF16), pltpu.VMEM((f, d), BF16),
            pltpu.VMEM((FFN_STAGE_SLOTS, d // FFN_STAGE_CHUNKS, f), F32),
            pltpu.VMEM((FFN_STAGE_SLOTS, f // FFN_STAGE_CHUNKS, d), F32),
            pltpu.SemaphoreType.DMA((FFN_STAGE_SLOTS,)),
        ],
        compiler_params=_cparams("arbitrary"),
        name="ffn_swiglu",
    )(*args)


def _gelu_tanh(x):
    c = math.sqrt(2.0 / math.pi)
    return x * (0.5 * (1.0 + jnp.tanh(c * (x + 0.044715 * (x * x * x)))))


def _odd_kernel(x_ref, mod_ref, g_ref, win_hbm, vg_ref, ws_ref, bst_ref, wout_hbm, o_ref,
                win_ref, wout_ref, stage_in, stage_out, sem, *, layer_j):
    @pl.when(pl.program_id(0) == 0)
    def _():
        _stage_weights_bf16(layer_j, [win_hbm, wout_hbm], [win_ref, wout_ref], [stage_in, stage_out], sem)

    x = x_ref[...]
    tm = x.shape[0]
    shift, scale, gate = mod_ref[3:4, :], mod_ref[4:5, :], mod_ref[5:6, :]
    h = _modulate(x, g_ref[...], shift, scale).astype(BF16)
    uv = _gelu_tanh(_dot(h, win_ref[...]))
    width = uv.shape[1] // 2
    u = uv[:, :width]
    v = _rms(uv[:, width:], vg_ref[...]).astype(BF16)
    gw = width // CMLP_GROUPS
    chunks = []
    for c in range(tm // CMLP_CHUNK):
        r0 = c * CMLP_CHUNK
        groups = []
        for g in range(CMLP_GROUPS):
            vg = v[r0:r0 + CMLP_CHUNK, g * gw:(g + 1) * gw]
            mixed = _dot(ws_ref[g], vg) + bst_ref[:, g:g + 1]
            groups.append(u[r0:r0 + CMLP_CHUNK, g * gw:(g + 1) * gw] * mixed)
        chunks.append(jnp.concatenate(groups, axis=1))
    z = jnp.concatenate(chunks, axis=0).astype(BF16)
    o_ref[...] = x + gate * _dot(z, wout_ref[...])


def _odd_mixer(x, mod, layer, layer_j, g, w_in, v_g, w_s, b_s_t, w_out, rows):
    m, d = x.shape
    n_in, width = w_in.shape[2], w_out.shape[1]
    hbm = pl.BlockSpec(memory_space=pl.ANY)
    return pl.pallas_call(
        functools.partial(_odd_kernel, layer_j=layer_j),
        out_shape=jax.ShapeDtypeStruct((m, d), F32),
        grid=(rows.nblocks,),
        in_specs=[
            rows.row_spec(d),
            rows.mod_spec(layer, d),
            _resident((1, d), _const_map(2)),
            hbm, _layer_spec(v_g, layer_j), _layer_spec(w_s, layer_j), _layer_spec(b_s_t, layer_j), hbm,
        ],
        out_specs=rows.row_spec(d),
        scratch_shapes=[
            pltpu.VMEM((d, n_in), BF16), pltpu.VMEM((width, d), BF16),
            pltpu.VMEM((FFN_STAGE_SLOTS, d // FFN_STAGE_CHUNKS, n_in), F32),
            pltpu.VMEM((FFN_STAGE_SLOTS, width // FFN_STAGE_CHUNKS, d), F32),
            pltpu.SemaphoreType.DMA((FFN_STAGE_SLOTS,)),
        ],
        compiler_params=_cparams("arbitrary"),
        name="odd_gmlp",
    )(x, mod, g, w_in, v_g, w_s, b_s_t, w_out)


_C_CQ = 0
_C_CKV = _C_CQ + MLA_Q_RANK
_C_KA = _C_CKV + MLA_KV_RANK
_C_KB = _C_KA + HEAD_PAD
_C_Q = _C_KB + HEAD_PAD
_C_K = _C_Q + GLA_QK
_C_V = _C_K + GLA_QK
_C_OG = _C_V + GLA_VW
_C_END = _C_OG + GLA_VW


def _head_scale(x, live):
    ss = jnp.sum(jnp.where(live, x * x, 0.0), axis=-1, keepdims=True)
    return lax.rsqrt(ss * (1.0 / MLA_QK_DIM) + EPS)


def _even_in_kernel(x_ref, mod_ref, g_ref, win_ref, w2_ref, gb_ref, qag_ref, qb_ref, kvag_ref,
                    kvbk_ref, kvbv_ref, kng_ref, tq_ref, ta_ref, tb_ref,
                    gq_ref, gk_ref, gv_ref, og_ref, la_ref, qm_ref, km_ref, vm_ref, ckv_ref, kr_ref,
                    *, nb_ctx):
    is_ctx = pl.program_id(0) < nb_ctx
    x = x_ref[...]
    shift, scale = mod_ref[3:4, :], mod_ref[4:5, :]
    h = _modulate(x, g_ref[...], shift, scale).astype(BF16)

    proj_q = _dot(h, win_ref[:, _C_CQ:_C_CKV])
    proj_kv = _dot(h, win_ref[:, _C_CKV:_C_Q])
    ckv = _rms(proj_kv[:, :MLA_KV_RANK], kvag_ref[...])
    ka = proj_kv[:, _C_KA - _C_CKV:_C_KB - _C_CKV]
    kb = proj_kv[:, _C_KB - _C_CKV:]

    @pl.when(is_ctx)
    def _():
        ckv_ref[...] = ckv
        kr_ref[...] = ka[:, MLA_NOPE:MLA_QK_DIM]

    cq = _rms(proj_q, qag_ref[...]).astype(BF16)
    qn = _dot(cq, qb_ref[...])
    ckv_bf = ckv.astype(BF16)
    kn = _dot(ckv_bf, kvbk_ref[...])
    vm_t = _dot(ckv_bf, kvbv_ref[...])
    z = _dot(ka.astype(BF16), w2_ref[...]) + gb_ref[...]

    lane = lax.broadcasted_iota(jnp.int32, ka.shape, 1)
    live = lane < MLA_QK_DIM
    tq = tq_ref[...]

    def q_heads(heads):
        for hd in heads:
            sl = slice(hd * HEAD_PAD, (hd + 1) * HEAD_PAD)
            qh = qn[:, sl]
            qm_ref[:, sl] = (qh * tq * _head_scale(qh, live)).astype(BF16)

    is_rope = live & (lane >= MLA_NOPE)
    ss_rope = jnp.sum(jnp.where(is_rope, ka * ka, 0.0), axis=-1, keepdims=True)
    rot = ka * ta_ref[...] + kb * tb_ref[...]
    kng = kng_ref[...]

    def k_heads(heads):
        for hd in heads:
            sl = slice(hd * HEAD_PAD, (hd + 1) * HEAD_PAD)
            kh = kn[:, sl]
            ss = jnp.sum(kh * kh, axis=-1, keepdims=True) + ss_rope
            r = lax.rsqrt(ss * (1.0 / MLA_QK_DIM) + EPS)
            km_ref[:, sl] = ((kh * kng + rot) * r).astype(BF16)

    half = MLA_HEADS // 2
    gqk = _dot(h, win_ref[:, _C_Q:_C_V])
    q_heads(range(half))
    gq_ref[...] = gqk[:, :GLA_QK] * (GLA_DK ** -0.5)
    gk_ref[...] = gqk[:, GLA_QK:]
    gv = _dot(h, win_ref[:, _C_V:_C_OG])
    q_heads(range(half, MLA_HEADS))
    gv_ref[...] = gv.astype(BF16)
    og = _dot(h, win_ref[:, _C_OG:])
    k_heads(range(half))
    og_ref[...] = og
    vm_ref[...] = vm_t.T.astype(BF16)
    k_heads(range(half, MLA_HEADS))
    la_ref[...] = (jnp.minimum(z, 0.0) - jnp.log(1.0 + jnp.exp(-jnp.abs(z)))) * (1.0 / GLA_GATE_NORM)


def _even_in(x, mod, layer, g, wts, rows):
    m, d = x.shape
    tm = rows.tm
    c2 = _const_map(2)
    rope_spec = pl.BlockSpec((tm, HEAD_PAD), lambda i: (rows.rope_block(i), 0))
    widths = [GLA_QK, GLA_QK, GLA_VW, GLA_VW, 2 * GLA_QK, MLA_HEADS * HEAD_PAD, MLA_HEADS * HEAD_PAD]
    dtypes = [F32, F32, BF16, F32, F32, BF16, BF16]
    names = ["w_in", "w2", "gate_b", "qa_g", "qb", "kva_g", "kvb_k", "kvb_v", "kn_g"]
    return pl.pallas_call(
        functools.partial(_even_in_kernel, nb_ctx=rows.nb_ctx),
        out_shape=[jax.ShapeDtypeStruct((m, w), t) for w, t in zip(widths, dtypes)]
        + [jax.ShapeDtypeStruct((MLA_HEADS * MLA_V, m), BF16),
           jax.ShapeDtypeStruct((rows.m_ctx, MLA_KV_RANK), F32),
           jax.ShapeDtypeStruct((rows.m_ctx, MLA_ROPE), F32)],
        grid=(rows.nblocks,),
        in_specs=[rows.row_spec(d), rows.mod_spec(layer, d), _resident((1, d), c2)]
        + [_resident(wts[n].shape, c2) for n in names] + [rope_spec] * 3,
        out_specs=[rows.row_spec(w) for w in widths]
        + [pl.BlockSpec((MLA_HEADS * MLA_V, tm), lambda i: (0, i)),
           rows.ctx_spec(MLA_KV_RANK), rows.ctx_spec(MLA_ROPE)],
        compiler_params=_cparams("arbitrary"),
        name="even_in_proj",
    )(x, mod, g, *[wts[n] for n in names], wts["t_q"], wts["t_a"], wts["t_b"])


def _ctx_kv_kernel(ckv_ref, kr_ref, kvbk_ref, kvbv_ref, kng_ref, k_ref, v_ref):
    ckv_bf = ckv_ref[...].astype(BF16)
    v_ref[...] = _dot(ckv_bf, kvbv_ref[...]).T.astype(BF16)
    kn = _dot(ckv_bf, kvbk_ref[...])
    kr = kr_ref[...]
    lane = lax.broadcasted_iota(jnp.int32, kr.shape, 1)
    live = lane < MLA_QK_DIM
    kng = kng_ref[...]
    for hd in range(MLA_HEADS):
        sl = slice(hd * HEAD_PAD, (hd + 1) * HEAD_PAD)
        kh = kn[:, sl] + kr
        k_ref[:, sl] = (kh * kng * _head_scale(kh, live)).astype(BF16)


def _ctx_kv(cache_ckv, cache_krope_blk, kvb_k, kvb_v, kn_g):
    nb, ne, p, r = cache_ckv.shape
    return pl.pallas_call(
        _ctx_kv_kernel,
        out_shape=[jax.ShapeDtypeStruct((ne, nb, p, MLA_HEADS * HEAD_PAD), BF16),
                   jax.ShapeDtypeStruct((ne, nb, MLA_HEADS * MLA_V, p), BF16)],
        grid=(ne, nb),
        in_specs=[
            pl.BlockSpec((None, None, p, r), lambda j, b: (b, j, 0, 0)),
            pl.BlockSpec((None, None, p, HEAD_PAD), lambda j, b: (b, j, 0, 0)),
            pl.BlockSpec((None,) + kvb_k.shape[1:], lambda j, b: (j, 0, 0)),
            pl.BlockSpec((None,) + kvb_v.shape[1:], lambda j, b: (j, 0, 0)),
            pl.BlockSpec((None,) + kn_g.shape[1:], lambda j, b: (j, 0, 0)),
        ],
        out_specs=[pl.BlockSpec((None, None, p, MLA_HEADS * HEAD_PAD), lambda j, b: (j, b, 0, 0)),
                   pl.BlockSpec((None, None, MLA_HEADS * MLA_V, p), lambda j, b: (j, b, 0, 0))],
        compiler_params=_cparams("arbitrary", "arbitrary"),
        name="ctx_kv",
    )(cache_ckv, cache_krope_blk, kvb_k, kvb_v, kn_g)


def _split3(x):
    hi = x.astype(BF16)
    r = x - hi.astype(F32)
    mid = r.astype(BF16)
    lo = (r - mid.astype(F32)).astype(BF16)
    return hi, mid, lo


def _block_row_bcast(a, blk, off):
    c, n = a.shape
    if blk >= 8:
        pieces = [jnp.broadcast_to(a[b * blk + off:b * blk + off + 1, :], (blk, n))
                  for b in range(c // blk)]
        return pieces[0] if len(pieces) == 1 else jnp.concatenate(pieces, axis=0)
    a3 = a.reshape(c // 8, 8, n)
    sub = lax.broadcasted_iota(jnp.int32, a3.shape, 1) // blk
    out = jnp.broadcast_to(a3[:, off:off + 1, :], a3.shape)
    for s in range(1, 8 // blk):
        cand = jnp.broadcast_to(a3[:, s * blk + off:s * blk + off + 1, :], a3.shape)
        out = jnp.where(sub == s, cand, out)
    return out.reshape(c, n)


def _head_stack(a, head_w):
    lane_head = lax.broadcasted_iota(jnp.int32, a.shape, 1) // head_w
    zero = jnp.zeros_like(a)
    return jnp.concatenate([jnp.where(lane_head == h, a, zero) for h in range(GLA_HEADS)], axis=0)


def _run_skewed(stage_gens, skew=1):
    done = [False] * len(stage_gens)
    tick = 0
    while not all(done):
        for u, gen in enumerate(stage_gens):
            if tick >= u * skew and not done[u]:
                try:
                    next(gen)
                except StopIteration:
                    done[u] = True
        tick += 1


def _gla_phase_a(c, q_ref, k_ref, v_ref, la_ref, tri_ref, pair_ref, oacc_ref, qst_ref, ds_ref, dc_ref):
    C = GLA_CHUNK
    r0 = pl.multiple_of(c * C, C)
    q = q_ref[pl.ds(r0, C), :]
    k = k_ref[pl.ds(r0, C), :]
    la = la_ref[pl.ds(r0, C), :]
    la_f, la_b = la[:, :GLA_QK], la[:, GLA_QK:]
    cums = []
    for d, la_d in enumerate((la_f, la_b)):
        hi, mid, lo = _split3(la_d)
        tri = tri_ref[d]
        cums.append(_dot(tri, hi) + _dot(tri, mid) + _dot(tri, lo))
    cum_f, cum_b = cums
    k_stack = _head_stack(k.astype(BF16), GLA_DK)
    yield

    row = lax.broadcasted_iota(jnp.int32, (C, 1), 0)
    pair_m = pair_ref[...]
    att = jnp.where(pair_m == 0, 2.0 * _dot_nt(q.astype(BF16), k_stack), 0.0)
    m = C // 2
    while m >= 1:
        blk = 2 * m
        upper = (row % blk) >= m
        if m == 1:
            qm = (q * jnp.exp(jnp.where(upper, la_f, la_b))).astype(BF16)
            km_stack = k_stack
        else:
            ref_f = _block_row_bcast(cum_f, blk, m - 1)
            ref_b = _block_row_bcast(cum_b, blk, m)
            eq = jnp.where(upper, cum_f - ref_f, cum_b - ref_b)
            ek = jnp.where(upper, ref_b - cum_b, ref_f - cum_f)
            qm = (q * jnp.exp(eq)).astype(BF16)
            km_stack = _head_stack((k * jnp.exp(ek)).astype(BF16), GLA_DK)
        yield
        att = jnp.where(pair_m == m, _dot_nt(qm, km_stack), att)
        m //= 2

    v = v_ref[pl.ds(r0, C), :]
    v_stack = _head_stack(v, GLA_DV)
    qcs = [(q * jnp.exp(cum)).astype(BF16) for cum in cums]
    yield
    oacc_ref[pl.ds(r0, C), :] = _dot(att.astype(BF16), v_stack)
    for d, cum in enumerate(cums):
        last = cum[C - 1:C, :] if d == 0 else cum[0:1, :]
        qst_ref[d, c] = _head_stack(qcs[d], GLA_DK)
        kct = (k * jnp.exp(last - cum)).T.astype(BF16)
        decay_col = jnp.exp(jnp.broadcast_to(last, (8, GLA_QK))).T[:, 0:1]
        dc_ref[d, c] = jnp.broadcast_to(decay_col, (GLA_QK, GLA_DV))
        yield
        ds_ref[d, c] = jnp.concatenate(
            [_dot(kct[h * GLA_DK:(h + 1) * GLA_DK, :], v[:, h * GLA_DV:(h + 1) * GLA_DV])
             for h in range(GLA_HEADS)], axis=0)


def _gla_phase_c(c, og_ref, gg, oacc_ref, qst_ref, sb_ref, o_ref):
    C = GLA_CHUNK
    r0 = pl.multiple_of(c * C, C)
    q_cat = jnp.concatenate([qst_ref[0, c], qst_ref[1, c]], axis=1)
    s_cat = jnp.concatenate([sb_ref[0, c], sb_ref[1, c]], axis=0)
    inter = _dot(q_cat, s_cat)
    yield
    for h in range(GLA_HEADS):
        sl = slice(h * GLA_DV, (h + 1) * GLA_DV)
        o = oacc_ref[pl.ds(r0, C), sl] + inter[h * C:(h + 1) * C, :]
        o_ref[pl.ds(r0, C), sl] = (_rms(o, gg) * _silu(og_ref[pl.ds(r0, C), sl])).astype(BF16)


def _gla_kernel(*refs, has_state):
    it = iter(refs)
    q_ref, k_ref, v_ref, la_ref, og_ref = (next(it) for _ in range(5))
    s0_ref = next(it) if has_state else None
    tri_ref, pair_ref, gg_ref, o_ref, sout_ref = (next(it) for _ in range(5))
    oacc_ref, qst_ref, ds_ref, dc_ref, sb_ref, st_ref = (next(it) for _ in range(6))

    nc = q_ref.shape[0] // GLA_CHUNK
    per_step = 4

    def phase_a(i, carry):
        _run_skewed([_gla_phase_a(per_step * i + u, q_ref, k_ref, v_ref, la_ref, tri_ref, pair_ref, oacc_ref,
                                  qst_ref, ds_ref, dc_ref) for u in range(per_step)])
        return carry

    lax.fori_loop(0, nc // per_step, phase_a, 0)

    for d in range(2):
        for h in range(GLA_HEADS):
            rows = slice(h * GLA_DK, (h + 1) * GLA_DK)
            st_ref[d, rows, :] = s0_ref[d, h] if has_state else jnp.zeros((GLA_DK, GLA_DV), F32)

    def phase_b(c, carry):
        for d in range(2):
            cc = c if d == 0 else nc - 1 - c
            s = st_ref[d]
            sb_ref[d, cc] = s.astype(BF16)
            st_ref[d] = s * dc_ref[d, cc] + ds_ref[d, cc]
        return carry

    lax.fori_loop(0, nc, phase_b, 0)
    for d in range(2):
        for h in range(GLA_HEADS):
            sout_ref[d, h] = st_ref[d, h * GLA_DK:(h + 1) * GLA_DK, :]

    gg = gg_ref[...]

    def phase_c(i, carry):
        _run_skewed([_gla_phase_c(per_step * i + u, og_ref, gg, oacc_ref, qst_ref, sb_ref, o_ref)
                     for u in range(per_step)])
        return carry

    lax.fori_loop(0, nc // per_step, phase_c, 0)


def _gla(gq, gk, gv, la, og, state, layer_j, tri, gg, nbatch, t, row_off):
    assert t % (2 * GLA_CHUNK) == 0
    b0 = row_off // t
    nc = t // GLA_CHUNK
    seq = lambda w: pl.BlockSpec((t, w), lambda b: (b0 + b, 0))
    in_specs = [seq(GLA_QK), seq(GLA_QK), seq(GLA_VW), seq(2 * GLA_QK), seq(GLA_VW)]
    args = [gq, gk, gv, la, og]
    if state is not None:
        in_specs.append(pl.BlockSpec((None, None, 2, GLA_HEADS, GLA_DK, GLA_DV),
                                     lambda b: (b, layer_j, 0, 0, 0, 0)))
        args.append(state)
    tri, pair_m = tri
    in_specs += [_resident(tri.shape, _const_map(3)), _resident(pair_m.shape, _const_map(2)),
                 _resident(gg.shape, _const_map(2))]
    args += [tri, pair_m, gg]
    return pl.pallas_call(
        functools.partial(_gla_kernel, has_state=state is not None),
        out_shape=[jax.ShapeDtypeStruct((nbatch * t, GLA_VW), BF16),
                   jax.ShapeDtypeStruct((nbatch, 2, GLA_HEADS, GLA_DK, GLA_DV), F32)],
        grid=(nbatch,),
        in_specs=in_specs,
        out_specs=[pl.BlockSpec((t, GLA_VW), lambda b: (b, 0)),
                   pl.BlockSpec((None, 2, GLA_HEADS, GLA_DK, GLA_DV), lambda b: (b, 0, 0, 0, 0))],
        scratch_shapes=[
            pltpu.VMEM((t, GLA_VW), F32),
            pltpu.VMEM((2, nc, GLA_HEADS * GLA_CHUNK, GLA_QK), BF16),
            pltpu.VMEM((2, nc, GLA_QK, GLA_DV), F32),
            pltpu.VMEM((2, nc, GLA_QK, GLA_DV), F32),
            pltpu.VMEM((2, nc, GLA_QK, GLA_DV), BF16),
            pltpu.VMEM((2, GLA_QK, GLA_DV), F32),
        ],
        compiler_params=_cparams("arbitrary"),
        name="gla_scan",
    )(*args)


def _mla_kernel(*refs, has_ctx):
    it = iter(refs)
    q_ref, k_ref, v_ref = next(it), next(it), next(it)
    kc_ref, vc_ref = (next(it), next(it)) if has_ctx else (None, None)
    o_ref = next(it)
    def scores(hd):
        sl = slice(hd * HEAD_PAD, (hd + 1) * HEAD_PAD)
        q = q_ref[:, sl]
        return [_dot_nt(r[:, sl], q) for r in ((k_ref, kc_ref) if has_ctx else (k_ref,))]

    def softmax(ss):
        mx = functools.reduce(jnp.maximum, [jnp.max(s, axis=0, keepdims=True) for s in ss])
        ps = [jnp.exp2(s - mx) for s in ss]
        den = sum(jnp.sum(p, axis=0, keepdims=True) for p in ps)
        return [p.astype(BF16) for p in ps], den

    def values(hd, ps, den):
        vrows = slice(hd * MLA_V, (hd + 1) * MLA_V)
        o = sum(_dot(r[vrows, :], p) for r, p in zip((v_ref, vc_ref), ps))
        return o / den

    s_q, p_q, outs = {}, {}, []
    for step in range(MLA_HEADS + 2):
        if step < MLA_HEADS:
            s_q[step] = scores(step)
        if 0 <= step - 1 < MLA_HEADS:
            p_q[step - 1] = softmax(s_q.pop(step - 1))
        if 0 <= step - 2 < MLA_HEADS:
            outs.append(values(step - 2, *p_q.pop(step - 2)))
    o_ref[...] = jnp.concatenate(outs, axis=0).T.astype(BF16)


def _mla(qm, km, vm, kc, vc, layer_j, nbatch, t, row_off, tq):
    nq = t // tq
    q0, b0 = row_off // tq, row_off // t
    qspec = lambda w: pl.BlockSpec((tq, w), lambda b, i: (q0 + b * nq + i, 0))
    kspec = lambda w: pl.BlockSpec((t, w), lambda b, i: (b0 + b, 0))
    vspec = pl.BlockSpec((MLA_HEADS * MLA_V, t), lambda b, i: (0, b0 + b))
    in_specs = [qspec(MLA_HEADS * HEAD_PAD), kspec(MLA_HEADS * HEAD_PAD), vspec]
    args = [qm, km, vm]
    if kc is not None:
        in_specs += [pl.BlockSpec((None, None) + kc.shape[2:], lambda b, i: (layer_j, b, 0, 0)),
                     pl.BlockSpec((None, None) + vc.shape[2:], lambda b, i: (layer_j, b, 0, 0))]
        args += [kc, vc]
    return pl.pallas_call(
        functools.partial(_mla_kernel, has_ctx=kc is not None),
        out_shape=jax.ShapeDtypeStruct((nbatch * t, MLA_HEADS * MLA_V), BF16),
        grid=(nbatch, nq),
        in_specs=in_specs,
        out_specs=pl.BlockSpec((tq, MLA_HEADS * MLA_V), lambda b, i: (b * nq + i, 0)),
        compiler_params=_cparams("arbitrary", "arbitrary"),
        name="mla_attention",
    )(*args)


def _swap_signed(a):
    pairs = a.reshape(a.shape[:-1] + (a.shape[-1] // 2, 2))
    return jnp.stack([-pairs[..., 1], pairs[..., 0]], axis=-1).reshape(a.shape)


def _swap_pairs(a):
    pairs = a.reshape(a.shape[:-1] + (a.shape[-1] // 2, 2))
    return pairs[..., ::-1].reshape(a.shape)


def _pack_even_weights(j, cos, sin, even_w_in, gla_gate_w2, gla_gate_b, mla_qa_g, mla_qb_w, mla_qn_g,
                       mla_kva_g, mla_kvb_w, mla_kn_g):
    d = even_w_in.shape[1]
    w = even_w_in[j]
    o = 0
    parts = {}
    for name, width in (("q", GLA_QK), ("k", GLA_QK), ("v", GLA_VW), ("og", GLA_VW),
                        ("gl", 2 * GLA_GATE_RANK), ("cq", MLA_Q_RANK), ("ckv", MLA_KV_RANK),
                        ("kr", MLA_ROPE)):
        parts[name] = w[:, o:o + width]
        o += width
    zeros = lambda n: jnp.zeros((d, n), w.dtype)
    kr, kr_sw = parts["kr"], _swap_signed(parts["kr"])
    blk_a = jnp.concatenate([parts["gl"], zeros(MLA_NOPE - 2 * GLA_GATE_RANK), kr, kr_sw], axis=1)
    blk_b = jnp.concatenate([zeros(MLA_NOPE), kr_sw, kr], axis=1)
    w_in = jnp.concatenate([parts["cq"], parts["ckv"], blk_a, blk_b, parts["q"], parts["k"], parts["v"],
                            parts["og"]], axis=1).astype(BF16)
    assert w_in.shape[1] == _C_END

    w2 = jnp.zeros((HEAD_PAD, 2 * GLA_QK), F32)
    w2 = w2.at[:GLA_GATE_RANK, :GLA_QK].set(gla_gate_w2[j, 0])
    w2 = w2.at[GLA_GATE_RANK:2 * GLA_GATE_RANK, GLA_QK:].set(gla_gate_w2[j, 1])

    pad_head = lambda a: jnp.pad(a, [(0, 0)] * (a.ndim - 1) + [(0, HEAD_PAD - a.shape[-1])])
    qb = mla_qb_w[j]
    qb = jnp.concatenate([qb, _swap_signed(qb[..., MLA_NOPE:])], axis=-1)
    qb = qb.reshape(MLA_Q_RANK, MLA_HEADS * HEAD_PAD)
    kvb = mla_kvb_w[j]
    kvb_k = pad_head(kvb[..., :MLA_NOPE]).reshape(MLA_KV_RANK, MLA_HEADS * HEAD_PAD)
    kvb_v = kvb[..., MLA_NOPE:].reshape(MLA_KV_RANK, MLA_HEADS * MLA_V)

    def rot_tables(g):
        n = cos.shape[0]
        g_n = jnp.broadcast_to(g[:MLA_NOPE], (n, MLA_NOPE))
        g_c = g[MLA_NOPE:] * cos
        g_s = _swap_pairs(g[MLA_NOPE:]) * sin
        return g_n, g_c, g_s

    qn_n, qn_c, qn_s = rot_tables(mla_qn_g[j])
    kn_n, kn_c, kn_s = rot_tables(mla_kn_g[j])
    kzero = jnp.zeros_like(kn_n)
    return {
        "w_in": w_in,
        "w2": w2.astype(BF16),
        "gate_b": gla_gate_b[j].reshape(1, 2 * GLA_QK),
        "qa_g": mla_qa_g[j][None, :],
        "qb": qb.astype(BF16),
        "kva_g": mla_kva_g[j][None, :],
        "kvb_k": kvb_k.astype(BF16),
        "kvb_v": kvb_v.astype(BF16),
        "kn_g": pad_head(mla_kn_g[j][:MLA_NOPE])[None, :],
        "kn_g_ctx": jnp.concatenate([mla_kn_g[j], mla_kn_g[j][MLA_NOPE:]])[None, :],
        "t_q": jnp.concatenate([qn_n, qn_c, qn_s], axis=1) * (MLA_QK_DIM ** -0.5 * math.log2(math.e)),
        "t_a": jnp.concatenate([kzero, kn_c, kn_s], axis=1),
        "t_b": jnp.concatenate([kzero, kn_s, kn_c], axis=1),
    }


def _rope_tables(n, tm):
    pairs = MLA_ROPE // 4
    pos = np.arange(n)
    inv = ROPE_BASE ** (-jnp.arange(pairs, dtype=F32) / pairs)
    row = jnp.asarray(pos // GRID_W, F32)
    col = jnp.asarray(pos % GRID_W, F32)
    ang = jnp.concatenate([row[:, None] * inv, col[:, None] * inv], axis=-1)
    cos = jnp.repeat(jnp.cos(ang), 2, axis=-1)
    sin = jnp.repeat(jnp.sin(ang), 2, axis=-1)
    cos = jnp.pad(cos, ((tm, 0), (0, 0)), constant_values=1.0)
    sin = jnp.pad(sin, ((tm, 0), (0, 0)))
    return cos, sin


def _tri_masks():
    r = np.arange(GLA_CHUNK)
    lower = (r[None, :] <= r[:, None]).astype(np.float32)
    diff = r[:, None] ^ r[None, :]
    level = np.where(diff > 0, 1 << np.floor(np.log2(np.maximum(diff, 1))).astype(np.int64), 0)
    pair_m = np.tile(level, (1, GLA_HEADS)).astype(np.int32)
    return jnp.asarray(np.stack([lower, lower.T]), BF16), jnp.asarray(pair_m)


def _pick_tile(pref, *sizes):
    tm = pref
    while any(s % tm for s in sizes):
        tm //= 2
    return tm


def kernel(x_prompt, x_sample, c, c_ctx, cache_ckv, cache_krope, state_gla, ada_w, ada_b, norm_g,
           ffn1_wg, ffn1_wu, ffn1_wd, ffn2_wg, ffn2_wu, ffn2_wd, even_w_in, even_w_out,
           gla_gate_w2, gla_gate_b, gla_norm_g, mla_qa_g, mla_qb_w, mla_kva_g, mla_kvb_w,
           mla_qn_g, mla_kn_g, odd_w_in, odd_v_g, odd_ws, odd_bs, odd_w_out):
    batch, seq, d = x_prompt.shape
    dec_batch, dec_seq, _ = x_sample.shape
    depth = ada_w.shape[0]
    n_even = even_w_in.shape[0]
    m_ctx, m_dec = batch * seq, dec_batch * dec_seq
    assert 1 + dec_batch <= COND_PAD and seq % CMLP_CHUNK == 0 and dec_seq % CMLP_CHUNK == 0

    rows_ffn = _Rows(m_ctx, dec_batch, dec_seq, _pick_tile(512, m_ctx, dec_seq))
    rows_mix = _Rows(m_ctx, dec_batch, dec_seq, _pick_tile(512, m_ctx, dec_seq))

    cond = jnp.concatenate([c_ctx[None, :], c, jnp.zeros((COND_PAD - 1 - dec_batch, d), F32)], axis=0)
    mod = _modulation_all(cond, ada_w, ada_b).reshape(depth, COND_PAD, N_MOD, d)

    cos_t, sin_t = _rope_tables(dec_seq, rows_mix.tm)
    tri = _tri_masks()
    krope_blk = jnp.concatenate([jnp.zeros(cache_krope.shape[:-1] + (MLA_NOPE,), F32), cache_krope,
                                 cache_krope], axis=-1)
    even_wts = [_pack_even_weights(j, cos_t, sin_t, even_w_in, gla_gate_w2, gla_gate_b, mla_qa_g, mla_qb_w,
                                   mla_qn_g, mla_kva_g, mla_kvb_w, mla_kn_g) for j in range(n_even)]
    kc, vc = _ctx_kv(cache_ckv, krope_blk,
                     jnp.stack([w["kvb_k"] for w in even_wts]),
                     jnp.stack([w["kvb_v"] for w in even_wts]),
                     jnp.stack([w["kn_g_ctx"] for w in even_wts]))

    ffn1 = (ffn1_wg, ffn1_wu, ffn1_wd)
    ffn2 = (ffn2_wg, ffn2_wu, ffn2_wd)
    w_out_bf = even_w_out.astype(BF16)
    odd_wts = (odd_w_in, odd_v_g[:, None, :], odd_ws.astype(BF16), jnp.swapaxes(odd_bs, 1, 2), odd_w_out)

    x = (x_prompt.reshape(m_ctx, d), x_sample.reshape(m_dec, d))
    new_ckv, new_krope, new_gla = [], [], []
    for i in range(depth):
        j = i // 2
        x = _ffn(x, mod, i, 0, norm_g[i, 0][None, :], *ffn1, rows_ffn)
        g_mix = norm_g[i, 1][None, :]
        premix = None
        if i % 2 == 0:
            gq, gk, gv, og, la, qm, km, vm, ckv, kr = _even_in(x, mod, i, g_mix, even_wts[j], rows_mix)
            gg = gla_norm_g[j][None, :]
            gla_ctx, st = _gla(gq, gk, gv, la, og, None, j, tri, gg, batch, seq, 0)
            gla_dec, _ = _gla(gq, gk, gv, la, og, state_gla, j, tri, gg, dec_batch, dec_seq, m_ctx)
            mla_ctx = _mla(qm, km, vm, None, None, j, batch, seq, 0, _pick_tile(256, seq))
            mla_dec = _mla(qm, km, vm, kc, vc, j, dec_batch, dec_seq, m_ctx, _pick_tile(512, dec_seq))
            premix = ((gla_ctx, gla_dec), (mla_ctx, mla_dec), w_out_bf, j)
            new_ckv.append(ckv.reshape(batch, seq, MLA_KV_RANK))
            new_krope.append(kr.reshape(batch, seq, MLA_ROPE))
            new_gla.append(st)
        else:
            x = _odd_mixer(x, mod, i, j, g_mix, *odd_wts, rows_mix)
        x = _ffn(x, mod, i, 6, norm_g[i, 2][None, :], *ffn2, rows_ffn, split_out=(i == depth - 1),
                 premix=premix)

    y_prompt, y_sample = x
    return (y_prompt.reshape(batch, seq, d), y_sample.reshape(dec_batch, dec_seq, d),
            jnp.stack(new_ckv, axis=1), jnp.stack(new_krope, axis=1), jnp.stack(new_gla, axis=1))
```

```python
import functools
import math

import numpy as np
import jax
import jax.numpy as jnp
from jax import lax
from jax.experimental import pallas as pl
from jax.experimental.pallas import tpu as pltpu

F32 = jnp.float32
BF16 = jnp.bfloat16

EPS = 1e-6
N_MOD = 9
GRID_W = 64
ROPE_BASE = 10000.0
GLA_HEADS = 4
GLA_DK = 64
GLA_DV = 128
GLA_QK = GLA_HEADS * GLA_DK
GLA_VW = GLA_HEADS * GLA_DV
GLA_GATE_RANK = 16
GLA_GATE_NORM = 16.0
GLA_CHUNK = 64
MLA_HEADS = 8
MLA_NOPE = 64
MLA_ROPE = 32
MLA_V = 64
MLA_QK_DIM = MLA_NOPE + MLA_ROPE
MLA_Q_RANK = 384
MLA_KV_RANK = 256
HEAD_PAD = 128
CMLP_CHUNK = 128
CMLP_GROUPS = 4
COND_PAD = 16

VMEM_LIMIT = 56 * 1024 * 1024


def _cparams(*sem):
    return pltpu.CompilerParams(dimension_semantics=sem, vmem_limit_bytes=VMEM_LIMIT)


def _resident(shape, index_map):
    return pl.BlockSpec(shape, index_map, pipeline_mode=pl.Buffered(1))


def _const_map(nd):
    return lambda *_: (0,) * nd


def _layer_spec(stack, layer):
    nd = stack.ndim - 1
    return _resident((None,) + stack.shape[1:], lambda *_: (layer,) + (0,) * nd)


def _silu(x):
    return x * jax.nn.sigmoid(x)


def _rms(x, g):
    return x * lax.rsqrt(jnp.mean(x * x, axis=-1, keepdims=True) + EPS) * g


def _modulate(x, g, shift, scale):
    return _rms(x, g) * (1.0 + scale) + shift


def _dot(a, b):
    return jnp.dot(a, b, preferred_element_type=F32)


def _dot_nt(a, b):
    return lax.dot_general(a, b, (((1,), (1,)), ((), ())), preferred_element_type=F32)


def _mod_kernel(c_ref, w_ref, b_ref, o_ref):
    s = _silu(c_ref[...]).astype(BF16)
    o_ref[...] = _dot(s, w_ref[...].astype(BF16)) + b_ref[...]


def _modulation_all(cond, ada_w, ada_b):
    depth, d, n = ada_w.shape
    tn = n // 4
    return pl.pallas_call(
        _mod_kernel,
        out_shape=jax.ShapeDtypeStruct((depth, COND_PAD, n), F32),
        grid=(depth, n // tn),
        in_specs=[
            pl.BlockSpec((COND_PAD, d), lambda l, j: (0, 0)),
            pl.BlockSpec((None, d, tn), lambda l, j: (l, 0, j)),
            pl.BlockSpec((None, 1, tn), lambda l, j: (l, 0, j)),
        ],
        out_specs=pl.BlockSpec((None, COND_PAD, tn), lambda l, j: (l, 0, j)),
        compiler_params=_cparams("arbitrary", "arbitrary"),
        name="adaln_modulation",
    )(cond, ada_w, ada_b.reshape(depth, 1, n))


class _Rows:
    def __init__(self, m_ctx, dec_batch, dec_seq, tm):
        assert m_ctx % tm == 0 and dec_seq % tm == 0
        self.tm = tm
        self.m_ctx = m_ctx
        self.m = m_ctx + dec_batch * dec_seq
        self.nb_ctx = m_ctx // tm
        self.nb_seq = dec_seq // tm
        self.nblocks = self.m // tm

    def cond(self, i):
        return jnp.where(i < self.nb_ctx, 0, 1 + (i - self.nb_ctx) // self.nb_seq)

    def rope_block(self, i):
        return jnp.where(i < self.nb_ctx, 0, 1 + (i - self.nb_ctx) % self.nb_seq)

    def mod_spec(self, layer, d):
        return pl.BlockSpec((None, None, N_MOD, d), lambda i: (layer, self.cond(i), 0, 0))

    def row_spec(self, width):
        return pl.BlockSpec((self.tm, width), lambda i: (i, 0))

    def ctx_spec(self, width):
        return pl.BlockSpec((self.tm, width), lambda i: (jnp.minimum(i, self.nb_ctx - 1), 0))

    def dec_spec(self, width):
        return pl.BlockSpec((self.tm, width), lambda i: (jnp.maximum(i - self.nb_ctx, 0), 0))

    def split_specs(self, width):
        return [self.ctx_spec(width), self.dec_spec(width)]

    def split_shapes(self, width, dtype):
        return [jax.ShapeDtypeStruct((self.m_ctx, width), dtype),
                jax.ShapeDtypeStruct((self.m - self.m_ctx, width), dtype)]


def _load_split(refs, is_ctx):
    if len(refs) == 1:
        return refs[0][...]
    return jnp.where(is_ctx, refs[0][...], refs[1][...])


FFN_STAGE_CHUNKS = 16
FFN_STAGE_SLOTS = 4


def _stage_weights_bf16(layer, hbm_refs, vmem_refs, stage_refs, sem):
    chunks = []
    for src, dst, stage in zip(hbm_refs, vmem_refs, stage_refs):
        rows = stage.shape[1]
        assert dst.shape[0] % rows == 0
        chunks += [(src, dst, stage, r0) for r0 in range(0, dst.shape[0], rows)]
    slots = sem.shape[0]
    ahead = slots - 1

    def copy(i):
        src, _, stage, r0 = chunks[i]
        return pltpu.make_async_copy(src.at[layer, pl.ds(r0, stage.shape[1]), :], stage.at[i % slots],
                                     sem.at[i % slots])

    for i in range(min(ahead, len(chunks))):
        copy(i).start()
    for i, (_, dst, stage, r0) in enumerate(chunks):
        if i + ahead < len(chunks):
            copy(i + ahead).start()
        copy(i).wait()
        dst[pl.ds(r0, stage.shape[1]), :] = stage[i % slots].astype(BF16)


def _ffn_kernel(*refs, layer, row0, nb_ctx, split_in, split_out, premix):
    it = iter(refs)
    x_refs = [next(it) for _ in range(2 if split_in else 1)]
    mod_ref, g_ref = next(it), next(it)
    w_hbm = [next(it) for _ in range(3)]
    if premix:
        a_refs, b_refs = [next(it), next(it)], [next(it), next(it)]
        wo_ref = next(it)
    o_refs = [next(it) for _ in range(2 if split_out else 1)]
    wg_ref, wu_ref, wd_ref = w_vmem = [next(it) for _ in range(3)]
    stage_in, stage_down, sem = next(it), next(it), next(it)

    @pl.when(pl.program_id(0) == 0)
    def _():
        _stage_weights_bf16(layer, w_hbm, w_vmem, [stage_in, stage_in, stage_down], sem)

    is_ctx = pl.program_id(0) < nb_ctx
    x = _load_split(x_refs, is_ctx)
    if premix:
        half = wo_ref.shape[0] // 2
        mix = _dot(_load_split(a_refs, is_ctx), wo_ref[:half, :])
        mix = mix + _dot(_load_split(b_refs, is_ctx), wo_ref[half:, :])
        x = x + mod_ref[5:6, :] * mix
    shift = mod_ref[row0:row0 + 1, :]
    scale = mod_ref[row0 + 1:row0 + 2, :]
    gate = mod_ref[row0 + 2:row0 + 3, :]
    h = _modulate(x, g_ref[...], shift, scale).astype(BF16)
    a = _dot(h, wg_ref[...])
    u = _dot(h, wu_ref[...])
    act = (_silu(a) * u).astype(BF16)
    y = _dot(act, wd_ref[...])
    out = x + (0.5 * gate) * y
    if split_out:
        @pl.when(is_ctx)
        def _():
            o_refs[0][...] = out

        @pl.when(jnp.logical_not(is_ctx))
        def _():
            o_refs[1][...] = out
    else:
        o_refs[0][...] = out


def _ffn(xs, mod, layer, row0, g, wg, wu, wd, rows, split_out=False, premix=None):
    split_in = isinstance(xs, (tuple, list))
    xs = list(xs) if split_in else [xs]
    d = xs[0].shape[1]
    f = wg.shape[2]
    hbm = pl.BlockSpec(memory_space=pl.ANY)
    in_specs = (rows.split_specs(d) if split_in else [rows.row_spec(d)]) + [
        rows.mod_spec(layer, d),
        _layer_spec(*g),
        hbm, hbm, hbm,
    ]
    args = xs + [mod, g[0], wg, wu, wd]
    if premix is not None:
        gla, mla, w_out, j = premix
        in_specs += rows.split_specs(gla[0].shape[1]) + rows.split_specs(mla[0].shape[1])
        in_specs.append(_layer_spec(w_out, j))
        args += list(gla) + list(mla) + [w_out]
    return pl.pallas_call(
        functools.partial(_ffn_kernel, layer=layer, row0=row0, nb_ctx=rows.nb_ctx, split_in=split_in,
                          split_out=split_out, premix=premix is not None),
        out_shape=rows.split_shapes(d, F32) if split_out else jax.ShapeDtypeStruct((rows.m, d), F32),
        grid=(rows.nblocks,),
        in_specs=in_specs,
        out_specs=rows.split_specs(d) if split_out else rows.row_spec(d),
        scratch_shapes=[
            pltpu.VMEM((d, f), BF16), pltpu.VMEM((d, f), BF16), pltpu.VMEM((f, d), BF16),
            pltpu.VMEM((FFN_STAGE_SLOTS, d // FFN_STAGE_CHUNKS, f), F32),
            pltpu.VMEM((FFN_STAGE_SLOTS, f // FFN_STAGE_CHUNKS, d), F32),
            pltpu.SemaphoreType.DMA((FFN_STAGE_SLOTS,)),
        ],
        compiler_params=_cparams("arbitrary"),
        name="ffn_swiglu",
    )(*args)


def _gelu_tanh(x):
    c = math.sqrt(2.0 / math.pi)
    return x * (0.5 * (1.0 + jnp.tanh(c * (x + 0.044715 * (x * x * x)))))


def _odd_kernel(x_ref, mod_ref, g_ref, win_hbm, vg_ref, ws_ref, bst_ref, wout_hbm, o_ref,
                win_ref, wout_ref, stage_in, stage_out, sem, *, layer_j):
    @pl.when(pl.program_id(0) == 0)
    def _():
        _stage_weights_bf16(layer_j, [win_hbm, wout_hbm], [win_ref, wout_ref], [stage_in, stage_out], sem)

    x = x_ref[...]
    tm = x.shape[0]
    shift, scale, gate = mod_ref[3:4, :], mod_ref[4:5, :], mod_ref[5:6, :]
    h = _modulate(x, g_ref[...], shift, scale).astype(BF16)
    uv = _gelu_tanh(_dot(h, win_ref[...]))
    width = uv.shape[1] // 2
    u = uv[:, :width]
    v = _rms(uv[:, width:], vg_ref[...]).astype(BF16)
    gw = width // CMLP_GROUPS
    chunks = []
    for c in range(tm // CMLP_CHUNK):
        r0 = c * CMLP_CHUNK
        groups = []
        for g in range(CMLP_GROUPS):
            vg = v[r0:r0 + CMLP_CHUNK, g * gw:(g + 1) * gw]
            mixed = _dot(ws_ref[g], vg) + bst_ref[:, g:g + 1]
            groups.append(u[r0:r0 + CMLP_CHUNK, g * gw:(g + 1) * gw] * mixed)
        chunks.append(jnp.concatenate(groups, axis=1))
    z = jnp.concatenate(chunks, axis=0).astype(BF16)
    o_ref[...] = x + gate * _dot(z, wout_ref[...])


def _odd_mixer(x, mod, layer, layer_j, g, w_in, v_g, w_s, b_s_t, w_out, rows):
    m, d = x.shape
    n_in, width = w_in.shape[2], w_out.shape[1]
    hbm = pl.BlockSpec(memory_space=pl.ANY)
    return pl.pallas_call(
        functools.partial(_odd_kernel, layer_j=layer_j),
        out_shape=jax.ShapeDtypeStruct((m, d), F32),
        grid=(rows.nblocks,),
        in_specs=[
            rows.row_spec(d),
            rows.mod_spec(layer, d),
            _layer_spec(*g),
            hbm, _layer_spec(v_g, layer_j), _layer_spec(w_s, layer_j), _layer_spec(b_s_t, layer_j), hbm,
        ],
        out_specs=rows.row_spec(d),
        scratch_shapes=[
            pltpu.VMEM((d, n_in), BF16), pltpu.VMEM((width, d), BF16),
            pltpu.VMEM((FFN_STAGE_SLOTS, d // FFN_STAGE_CHUNKS, n_in), F32),
            pltpu.VMEM((FFN_STAGE_SLOTS, width // FFN_STAGE_CHUNKS, d), F32),
            pltpu.SemaphoreType.DMA((FFN_STAGE_SLOTS,)),
        ],
        compiler_params=_cparams("arbitrary"),
        name="odd_gmlp",
    )(x, mod, g[0], w_in, v_g, w_s, b_s_t, w_out)


_C_CQ = 0
_C_CKV = _C_CQ + MLA_Q_RANK
_C_KA = _C_CKV + MLA_KV_RANK
_C_KB = _C_KA + HEAD_PAD
_C_Q = _C_KB + HEAD_PAD
_C_K = _C_Q + GLA_QK
_C_V = _C_K + GLA_QK
_C_OG = _C_V + GLA_VW
_C_END = _C_OG + GLA_VW


def _head_scale(x, live):
    ss = jnp.sum(jnp.where(live, x * x, 0.0), axis=-1, keepdims=True)
    return lax.rsqrt(ss * (1.0 / MLA_QK_DIM) + EPS)


def _even_in_kernel(x_ref, mod_ref, g_ref, win_ref, w2_ref, gb_ref, qag_ref, qb_ref, kvag_ref,
                    kvbk_ref, kvbv_ref, kng_ref, tq_ref, ta_ref, tb_ref,
                    gq_ref, gk_ref, gv_ref, og_ref, la_ref, qm_ref, km_ref, vm_ref, ckv_ref, kr_ref,
                    *, nb_ctx, n_sub):
    is_ctx = pl.program_id(0) < nb_ctx
    shift, scale = mod_ref[3:4, :], mod_ref[4:5, :]
    rb = x_ref.shape[0] // n_sub

    def stages(r0):
        rs = slice(r0, r0 + rb)
        h = _modulate(x_ref[rs, :], g_ref[...], shift, scale).astype(BF16)
        yield
        proj_q = _dot(h, win_ref[:, _C_CQ:_C_CKV])
        proj_kv = _dot(h, win_ref[:, _C_CKV:_C_Q])
        yield
        ckv = _rms(proj_kv[:, :MLA_KV_RANK], kvag_ref[...])
        ka = proj_kv[:, _C_KA - _C_CKV:_C_KB - _C_CKV]
        kb = proj_kv[:, _C_KB - _C_CKV:]

        @pl.when(is_ctx)
        def _():
            ckv_ref[rs, :] = ckv
            kr_ref[rs, :] = ka[:, MLA_NOPE:MLA_QK_DIM]

        cq = _rms(proj_q, qag_ref[...]).astype(BF16)
        ckv_bf = ckv.astype(BF16)
        yield
        qn = _dot(cq, qb_ref[...])
        kn = _dot(ckv_bf, kvbk_ref[...])
        vm_t = _dot(ckv_bf, kvbv_ref[...])
        z = _dot(ka.astype(BF16), w2_ref[...]) + gb_ref[...]
        yield

        lane = lax.broadcasted_iota(jnp.int32, ka.shape, 1)
        live = lane < MLA_QK_DIM
        tq = tq_ref[rs, :]

        def q_heads(heads):
            for hd in heads:
                sl = slice(hd * HEAD_PAD, (hd + 1) * HEAD_PAD)
                qh = qn[:, sl]
                qm_ref[rs, sl] = (qh * tq * _head_scale(qh, live)).astype(BF16)

        is_rope = live & (lane >= MLA_NOPE)
        ss_rope = jnp.sum(jnp.where(is_rope, ka * ka, 0.0), axis=-1, keepdims=True)
        rot = ka * ta_ref[rs, :] + kb * tb_ref[rs, :]
        kng = kng_ref[...]

        def k_heads(heads):
            for hd in heads:
                sl = slice(hd * HEAD_PAD, (hd + 1) * HEAD_PAD)
                kh = kn[:, sl]
                ss = jnp.sum(kh * kh, axis=-1, keepdims=True) + ss_rope
                r = lax.rsqrt(ss * (1.0 / MLA_QK_DIM) + EPS)
                km_ref[rs, sl] = ((kh * kng + rot) * r).astype(BF16)

        half = MLA_HEADS // 2
        gqk = _dot(h, win_ref[:, _C_Q:_C_V])
        q_heads(range(half))
        yield
        gq_ref[rs, :] = gqk[:, :GLA_QK] * (GLA_DK ** -0.5)
        gk_ref[rs, :] = gqk[:, GLA_QK:]
        gv = _dot(h, win_ref[:, _C_V:_C_OG])
        q_heads(range(half, MLA_HEADS))
        yield
        gv_ref[rs, :] = gv.astype(BF16)
        og = _dot(h, win_ref[:, _C_OG:])
        k_heads(range(half))
        yield
        og_ref[rs, :] = og
        vm_ref[:, rs] = vm_t.T.astype(BF16)
        k_heads(range(half, MLA_HEADS))
        la_ref[rs, :] = ((jnp.minimum(z, 0.0) - jnp.log(1.0 + jnp.exp(-jnp.abs(z))))
                         * (math.log2(math.e) / GLA_GATE_NORM))

    _run_skewed([stages(u * rb) for u in range(n_sub)], skew=2)


def _even_in(x, mod, layer, g, wts, layer_j, rows):
    m, d = x.shape
    tm = rows.tm
    rope_spec = pl.BlockSpec((None, tm, HEAD_PAD), lambda i: (layer_j, rows.rope_block(i), 0))
    widths = [GLA_QK, GLA_QK, GLA_VW, GLA_VW, 2 * GLA_QK, MLA_HEADS * HEAD_PAD, MLA_HEADS * HEAD_PAD]
    dtypes = [F32, F32, BF16, F32, F32, BF16, BF16]
    names = ["w_in", "w2", "gate_b", "qa_g", "qb", "kva_g", "kvb_k", "kvb_v", "kn_g"]
    return pl.pallas_call(
        functools.partial(_even_in_kernel, nb_ctx=rows.nb_ctx, n_sub=1),
        out_shape=[jax.ShapeDtypeStruct((m, w), t) for w, t in zip(widths, dtypes)]
        + [jax.ShapeDtypeStruct((MLA_HEADS * MLA_V, m), BF16),
           jax.ShapeDtypeStruct((rows.m_ctx, MLA_KV_RANK), F32),
           jax.ShapeDtypeStruct((rows.m_ctx, MLA_ROPE), F32)],
        grid=(rows.nblocks,),
        in_specs=[rows.row_spec(d), rows.mod_spec(layer, d), _layer_spec(*g)]
        + [_layer_spec(wts[n], layer_j) for n in names] + [rope_spec] * 3,
        out_specs=[rows.row_spec(w) for w in widths]
        + [pl.BlockSpec((MLA_HEADS * MLA_V, tm), lambda i: (0, i)),
           rows.ctx_spec(MLA_KV_RANK), rows.ctx_spec(MLA_ROPE)],
        compiler_params=_cparams("arbitrary"),
        name="even_in_proj",
    )(x, mod, g[0], *[wts[n] for n in names], wts["t_q"], wts["t_a"], wts["t_b"])


def _ctx_kv_kernel(ckv_ref, kr_ref, kvbk_ref, kvbv_ref, kng_ref, k_ref, v_ref):
    ckv_bf = ckv_ref[...].astype(BF16)
    v_ref[...] = _dot(ckv_bf, kvbv_ref[...]).T.astype(BF16)
    kn = _dot(ckv_bf, kvbk_ref[...])
    kr = kr_ref[...]
    lane = lax.broadcasted_iota(jnp.int32, kr.shape, 1)
    live = lane < MLA_QK_DIM
    kng = kng_ref[...]
    for hd in range(MLA_HEADS):
        sl = slice(hd * HEAD_PAD, (hd + 1) * HEAD_PAD)
        kh = kn[:, sl] + kr
        k_ref[:, sl] = (kh * kng * _head_scale(kh, live)).astype(BF16)


def _ctx_kv(cache_ckv, cache_krope_blk, kvb_k, kvb_v, kn_g):
    nb, ne, p, r = cache_ckv.shape
    return pl.pallas_call(
        _ctx_kv_kernel,
        out_shape=[jax.ShapeDtypeStruct((ne, nb, p, MLA_HEADS * HEAD_PAD), BF16),
                   jax.ShapeDtypeStruct((ne, nb, MLA_HEADS * MLA_V, p), BF16)],
        grid=(ne, nb),
        in_specs=[
            pl.BlockSpec((None, None, p, r), lambda j, b: (b, j, 0, 0)),
            pl.BlockSpec((None, None, p, HEAD_PAD), lambda j, b: (b, j, 0, 0)),
            pl.BlockSpec((None,) + kvb_k.shape[1:], lambda j, b: (j, 0, 0)),
            pl.BlockSpec((None,) + kvb_v.shape[1:], lambda j, b: (j, 0, 0)),
            pl.BlockSpec((None,) + kn_g.shape[1:], lambda j, b: (j, 0, 0)),
        ],
        out_specs=[pl.BlockSpec((None, None, p, MLA_HEADS * HEAD_PAD), lambda j, b: (j, b, 0, 0)),
                   pl.BlockSpec((None, None, MLA_HEADS * MLA_V, p), lambda j, b: (j, b, 0, 0))],
        compiler_params=_cparams("arbitrary", "arbitrary"),
        name="ctx_kv",
    )(cache_ckv, cache_krope_blk, kvb_k, kvb_v, kn_g)


def _split3(x):
    hi = x.astype(BF16)
    r = x - hi.astype(F32)
    mid = r.astype(BF16)
    lo = (r - mid.astype(F32)).astype(BF16)
    return hi, mid, lo


def _block_row_bcast(a, blk, off):
    c, n = a.shape
    if blk >= 8:
        pieces = [jnp.broadcast_to(a[b * blk + off:b * blk + off + 1, :], (blk, n))
                  for b in range(c // blk)]
        return pieces[0] if len(pieces) == 1 else jnp.concatenate(pieces, axis=0)
    a3 = a.reshape(c // 8, 8, n)
    sub = lax.broadcasted_iota(jnp.int32, a3.shape, 1) // blk
    out = jnp.broadcast_to(a3[:, off:off + 1, :], a3.shape)
    for s in range(1, 8 // blk):
        cand = jnp.broadcast_to(a3[:, s * blk + off:s * blk + off + 1, :], a3.shape)
        out = jnp.where(sub == s, cand, out)
    return out.reshape(c, n)


def _head_stack(a, head_w):
    lane_head = lax.broadcasted_iota(jnp.int32, a.shape, 1) // head_w
    zero = jnp.zeros_like(a)
    return jnp.concatenate([jnp.where(lane_head == h, a, zero) for h in range(GLA_HEADS)], axis=0)


def _run_skewed(stage_gens, skew=1):
    done = [False] * len(stage_gens)
    tick = 0
    while not all(done):
        for u, gen in enumerate(stage_gens):
            if tick >= u * skew and not done[u]:
                try:
                    next(gen)
                except StopIteration:
                    done[u] = True
        tick += 1


def _gla_phase_a(c, q_ref, k_ref, v_ref, la_ref, tri_ref, pair_ref, oacc_ref, qst_ref, ds_ref, dc_ref):
    C = GLA_CHUNK
    r0 = pl.multiple_of(c * C, C)
    q = q_ref[pl.ds(r0, C), :]
    k = k_ref[pl.ds(r0, C), :]
    la = la_ref[pl.ds(r0, C), :]
    la_f, la_b = la[:, :GLA_QK], la[:, GLA_QK:]
    cums = []
    for d, la_d in enumerate((la_f, la_b)):
        hi, mid, lo = _split3(la_d)
        tri = tri_ref[d]
        cums.append(_dot(tri, hi) + _dot(tri, mid) + _dot(tri, lo))
    cum_f, cum_b = cums
    k_stack = _head_stack(k.astype(BF16), GLA_DK)
    yield

    row = lax.broadcasted_iota(jnp.int32, (C, 1), 0)
    pair_m = pair_ref[...]
    att = jnp.where(pair_m == 0, 2.0 * _dot_nt(q.astype(BF16), k_stack), 0.0)
    m = C // 2
    while m >= 1:
        blk = 2 * m
        upper = (row % blk) >= m
        if m == 1:
            qm = (q * jnp.exp2(jnp.where(upper, la_f, la_b))).astype(BF16)
            km_stack = k_stack
        else:
            ref_f = _block_row_bcast(cum_f, blk, m - 1)
            ref_b = _block_row_bcast(cum_b, blk, m)
            eq = jnp.where(upper, cum_f - ref_f, cum_b - ref_b)
            ek = jnp.where(upper, ref_b - cum_b, ref_f - cum_f)
            qm = (q * jnp.exp2(eq)).astype(BF16)
            km_stack = _head_stack((k * jnp.exp2(ek)).astype(BF16), GLA_DK)
        yield
        att = jnp.where(pair_m == m, _dot_nt(qm, km_stack), att)
        m //= 2

    v = v_ref[pl.ds(r0, C), :]
    v_stack = _head_stack(v, GLA_DV)
    qcs = [(q * jnp.exp2(cum)).astype(BF16) for cum in cums]
    yield
    oacc_ref[pl.ds(r0, C), :] = _dot(att.astype(BF16), v_stack)
    for d, cum in enumerate(cums):
        last = cum[C - 1:C, :] if d == 0 else cum[0:1, :]
        qst_ref[d, c] = _head_stack(qcs[d], GLA_DK)
        kct = (k * jnp.exp2(last - cum)).T.astype(BF16)
        decay_col = jnp.exp2(jnp.broadcast_to(last, (8, GLA_QK))).T[:, 0:1]
        dc_ref[d, c] = jnp.broadcast_to(decay_col, (GLA_QK, GLA_DV))
        yield
        ds_ref[d, c] = jnp.concatenate(
            [_dot(kct[h * GLA_DK:(h + 1) * GLA_DK, :], v[:, h * GLA_DV:(h + 1) * GLA_DV])
             for h in range(GLA_HEADS)], axis=0)


def _gla_phase_c(c, og_ref, gg, oacc_ref, qst_ref, sb_ref, o_ref):
    C = GLA_CHUNK
    r0 = pl.multiple_of(c * C, C)
    q_cat = jnp.concatenate([qst_ref[0, c], qst_ref[1, c]], axis=1)
    s_cat = jnp.concatenate([sb_ref[0, c], sb_ref[1, c]], axis=0)
    inter = _dot(q_cat, s_cat)
    yield
    for h in range(GLA_HEADS):
        sl = slice(h * GLA_DV, (h + 1) * GLA_DV)
        o = oacc_ref[pl.ds(r0, C), sl] + inter[h * C:(h + 1) * C, :]
        o_ref[pl.ds(r0, C), sl] = (_rms(o, gg) * _silu(og_ref[pl.ds(r0, C), sl])).astype(BF16)


def _gla_kernel(*refs, has_state):
    it = iter(refs)
    q_ref, k_ref, v_ref, la_ref, og_ref = (next(it) for _ in range(5))
    s0_ref = next(it) if has_state else None
    tri_ref, pair_ref, gg_ref, o_ref, sout_ref = (next(it) for _ in range(5))
    oacc_ref, qst_ref, ds_ref, dc_ref, sb_ref, st_ref = (next(it) for _ in range(6))

    nc = q_ref.shape[0] // GLA_CHUNK
    per_step = 4

    def phase_a(i, carry):
        _run_skewed([_gla_phase_a(per_step * i + u, q_ref, k_ref, v_ref, la_ref, tri_ref, pair_ref, oacc_ref,
                                  qst_ref, ds_ref, dc_ref) for u in range(per_step)])
        return carry

    lax.fori_loop(0, nc // per_step, phase_a, 0)

    for d in range(2):
        for h in range(GLA_HEADS):
            rows = slice(h * GLA_DK, (h + 1) * GLA_DK)
            st_ref[d, rows, :] = s0_ref[d, h] if has_state else jnp.zeros((GLA_DK, GLA_DV), F32)

    def phase_b(c, carry):
        for d in range(2):
            cc = c if d == 0 else nc - 1 - c
            s = st_ref[d]
            sb_ref[d, cc] = s.astype(BF16)
            st_ref[d] = s * dc_ref[d, cc] + ds_ref[d, cc]
        return carry

    lax.fori_loop(0, nc, phase_b, 0)
    for d in range(2):
        for h in range(GLA_HEADS):
            sout_ref[d, h] = st_ref[d, h * GLA_DK:(h + 1) * GLA_DK, :]

    gg = gg_ref[...]

    def phase_c(i, carry):
        _run_skewed([_gla_phase_c(per_step * i + u, og_ref, gg, oacc_ref, qst_ref, sb_ref, o_ref)
                     for u in range(per_step)])
        return carry

    lax.fori_loop(0, nc // per_step, phase_c, 0)


def _gla(gq, gk, gv, la, og, state, layer_j, tri, gg, nbatch, t, row_off):
    assert t % (2 * GLA_CHUNK) == 0
    b0 = row_off // t
    nc = t // GLA_CHUNK
    seq = lambda w: pl.BlockSpec((t, w), lambda b: (b0 + b, 0))
    in_specs = [seq(GLA_QK), seq(GLA_QK), seq(GLA_VW), seq(2 * GLA_QK), seq(GLA_VW)]
    args = [gq, gk, gv, la, og]
    if state is not None:
        in_specs.append(pl.BlockSpec((None, None, 2, GLA_HEADS, GLA_DK, GLA_DV),
                                     lambda b: (b, layer_j, 0, 0, 0, 0)))
        args.append(state)
    tri, pair_m = tri
    in_specs += [_resident(tri.shape, _const_map(3)), _resident(pair_m.shape, _const_map(2)),
                 _layer_spec(gg, layer_j)]
    args += [tri, pair_m, gg]
    return pl.pallas_call(
        functools.partial(_gla_kernel, has_state=state is not None),
        out_shape=[jax.ShapeDtypeStruct((nbatch * t, GLA_VW), BF16),
                   jax.ShapeDtypeStruct((nbatch, 2, GLA_HEADS, GLA_DK, GLA_DV), F32)],
        grid=(nbatch,),
        in_specs=in_specs,
        out_specs=[pl.BlockSpec((t, GLA_VW), lambda b: (b, 0)),
                   pl.BlockSpec((None, 2, GLA_HEADS, GLA_DK, GLA_DV), lambda b: (b, 0, 0, 0, 0))],
        scratch_shapes=[
            pltpu.VMEM((t, GLA_VW), F32),
            pltpu.VMEM((2, nc, GLA_HEADS * GLA_CHUNK, GLA_QK), BF16),
            pltpu.VMEM((2, nc, GLA_QK, GLA_DV), F32),
            pltpu.VMEM((2, nc, GLA_QK, GLA_DV), F32),
            pltpu.VMEM((2, nc, GLA_QK, GLA_DV), BF16),
            pltpu.VMEM((2, GLA_QK, GLA_DV), F32),
        ],
        compiler_params=_cparams("arbitrary"),
        name="gla_scan",
    )(*args)


def _mla_kernel(*refs, has_ctx):
    it = iter(refs)
    q_ref, k_ref, v_ref = next(it), next(it), next(it)
    kc_ref, vc_ref = (next(it), next(it)) if has_ctx else (None, None)
    o_ref = next(it)
    def scores(hd):
        sl = slice(hd * HEAD_PAD, (hd + 1) * HEAD_PAD)
        q = q_ref[:, sl]
        return [_dot_nt(r[:, sl], q) for r in ((k_ref, kc_ref) if has_ctx else (k_ref,))]

    def softmax(ss):
        mx = functools.reduce(jnp.maximum, [jnp.max(s, axis=0, keepdims=True) for s in ss])
        ps = [jnp.exp2(s - mx) for s in ss]
        den = sum(jnp.sum(p, axis=0, keepdims=True) for p in ps)
        return [p.astype(BF16) for p in ps], den

    def values(hd, ps, den):
        vrows = slice(hd * MLA_V, (hd + 1) * MLA_V)
        o = sum(_dot(r[vrows, :], p) for r, p in zip((v_ref, vc_ref), ps))
        return o / den

    s_q, p_q, outs = {}, {}, []
    for step in range(MLA_HEADS + 2):
        if step < MLA_HEADS:
            s_q[step] = scores(step)
        if 0 <= step - 1 < MLA_HEADS:
            p_q[step - 1] = softmax(s_q.pop(step - 1))
        if 0 <= step - 2 < MLA_HEADS:
            outs.append(values(step - 2, *p_q.pop(step - 2)))
    o_ref[...] = jnp.concatenate(outs, axis=0).T.astype(BF16)


def _mla(qm, km, vm, kc, vc, layer_j, nbatch, t, row_off, tq):
    nq = t // tq
    q0, b0 = row_off // tq, row_off // t
    qspec = lambda w: pl.BlockSpec((tq, w), lambda b, i: (q0 + b * nq + i, 0))
    kspec = lambda w: pl.BlockSpec((t, w), lambda b, i: (b0 + b, 0))
    vspec = pl.BlockSpec((MLA_HEADS * MLA_V, t), lambda b, i: (0, b0 + b))
    in_specs = [qspec(MLA_HEADS * HEAD_PAD), kspec(MLA_HEADS * HEAD_PAD), vspec]
    args = [qm, km, vm]
    if kc is not None:
        in_specs += [pl.BlockSpec((None, None) + kc.shape[2:], lambda b, i: (layer_j, b, 0, 0)),
                     pl.BlockSpec((None, None) + vc.shape[2:], lambda b, i: (layer_j, b, 0, 0))]
        args += [kc, vc]
    return pl.pallas_call(
        functools.partial(_mla_kernel, has_ctx=kc is not None),
        out_shape=jax.ShapeDtypeStruct((nbatch * t, MLA_HEADS * MLA_V), BF16),
        grid=(nbatch, nq),
        in_specs=in_specs,
        out_specs=pl.BlockSpec((tq, MLA_HEADS * MLA_V), lambda b, i: (b * nq + i, 0)),
        compiler_params=_cparams("arbitrary", "arbitrary"),
        name="mla_attention",
    )(*args)


def _swap_signed(a):
    pairs = a.reshape(a.shape[:-1] + (a.shape[-1] // 2, 2))
    return jnp.stack([-pairs[..., 1], pairs[..., 0]], axis=-1).reshape(a.shape)


def _swap_pairs(a):
    pairs = a.reshape(a.shape[:-1] + (a.shape[-1] // 2, 2))
    return pairs[..., ::-1].reshape(a.shape)


def _pack_even_weights(cos, sin, even_w_in, gla_gate_w2, gla_gate_b, mla_qa_g, mla_qb_w, mla_qn_g,
                       mla_kva_g, mla_kvb_w, mla_kn_g):
    ne, d, _ = even_w_in.shape
    o = 0
    parts = {}
    for name, width in (("q", GLA_QK), ("k", GLA_QK), ("v", GLA_VW), ("og", GLA_VW),
                        ("gl", 2 * GLA_GATE_RANK), ("cq", MLA_Q_RANK), ("ckv", MLA_KV_RANK),
                        ("kr", MLA_ROPE)):
        parts[name] = even_w_in[:, :, o:o + width]
        o += width
    zeros = lambda n: jnp.zeros((ne, d, n), even_w_in.dtype)
    kr, kr_sw = parts["kr"], _swap_signed(parts["kr"])
    blk_a = jnp.concatenate([parts["gl"], zeros(MLA_NOPE - 2 * GLA_GATE_RANK), kr, kr_sw], axis=2)
    blk_b = jnp.concatenate([zeros(MLA_NOPE), kr_sw, kr], axis=2)
    w_in = jnp.concatenate([parts["cq"], parts["ckv"], blk_a, blk_b, parts["q"], parts["k"], parts["v"],
                            parts["og"]], axis=2).astype(BF16)
    assert w_in.shape[2] == _C_END

    w2_f = jnp.pad(gla_gate_w2[:, 0], ((0, 0), (0, 0), (0, GLA_QK)))
    w2_b = jnp.pad(gla_gate_w2[:, 1], ((0, 0), (0, 0), (GLA_QK, 0)))
    w2 = jnp.pad(jnp.concatenate([w2_f, w2_b], axis=1), ((0, 0), (0, HEAD_PAD - 2 * GLA_GATE_RANK), (0, 0)))

    pad_head = lambda a: jnp.pad(a, [(0, 0)] * (a.ndim - 1) + [(0, HEAD_PAD - a.shape[-1])])
    qb = jnp.concatenate([mla_qb_w, _swap_signed(mla_qb_w[..., MLA_NOPE:])], axis=-1)
    qb = qb.reshape(ne, MLA_Q_RANK, MLA_HEADS * HEAD_PAD)
    kvb_k = pad_head(mla_kvb_w[..., :MLA_NOPE]).reshape(ne, MLA_KV_RANK, MLA_HEADS * HEAD_PAD)
    kvb_v = mla_kvb_w[..., MLA_NOPE:].reshape(ne, MLA_KV_RANK, MLA_HEADS * MLA_V)

    def rot_tables(g):
        n = cos.shape[0]
        g_n = jnp.broadcast_to(g[:, None, :MLA_NOPE], (ne, n, MLA_NOPE))
        g_c = g[:, None, MLA_NOPE:] * cos
        g_s = _swap_pairs(g[:, MLA_NOPE:])[:, None, :] * sin
        return g_n, g_c, g_s

    qn_n, qn_c, qn_s = rot_tables(mla_qn_g)
    kn_n, kn_c, kn_s = rot_tables(mla_kn_g)
    kzero = jnp.zeros_like(kn_n)
    return {
        "w_in": w_in,
        "w2": w2.astype(BF16),
        "gate_b": gla_gate_b.reshape(ne, 1, 2 * GLA_QK),
        "qa_g": mla_qa_g[:, None, :],
        "qb": qb.astype(BF16),
        "kva_g": mla_kva_g[:, None, :],
        "kvb_k": kvb_k.astype(BF16),
        "kvb_v": kvb_v.astype(BF16),
        "kn_g": pad_head(mla_kn_g[:, :MLA_NOPE])[:, None, :],
        "kn_g_ctx": jnp.concatenate([mla_kn_g, mla_kn_g[:, MLA_NOPE:]], axis=1)[:, None, :],
        "t_q": jnp.concatenate([qn_n, qn_c, qn_s], axis=2) * (MLA_QK_DIM ** -0.5 * math.log2(math.e)),
        "t_a": jnp.concatenate([kzero, kn_c, kn_s], axis=2),
        "t_b": jnp.concatenate([kzero, kn_s, kn_c], axis=2),
    }


def _rope_tables(n, tm):
    pairs = MLA_ROPE // 4
    pos = np.arange(n)
    inv = ROPE_BASE ** (-jnp.arange(pairs, dtype=F32) / pairs)
    row = jnp.asarray(pos // GRID_W, F32)
    col = jnp.asarray(pos % GRID_W, F32)
    ang = jnp.concatenate([row[:, None] * inv, col[:, None] * inv], axis=-1)
    cos = jnp.repeat(jnp.cos(ang), 2, axis=-1)
    sin = jnp.repeat(jnp.sin(ang), 2, axis=-1)
    cos = jnp.pad(cos, ((tm, 0), (0, 0)), constant_values=1.0)
    sin = jnp.pad(sin, ((tm, 0), (0, 0)))
    return cos, sin


def _tri_masks():
    r = np.arange(GLA_CHUNK)
    lower = (r[None, :] <= r[:, None]).astype(np.float32)
    diff = r[:, None] ^ r[None, :]
    level = np.where(diff > 0, 1 << np.floor(np.log2(np.maximum(diff, 1))).astype(np.int64), 0)
    pair_m = np.tile(level, (1, GLA_HEADS)).astype(np.int32)
    return jnp.asarray(np.stack([lower, lower.T]), BF16), jnp.asarray(pair_m)


def _pick_tile(pref, *sizes):
    tm = pref
    while any(s % tm for s in sizes):
        tm //= 2
    return tm


def kernel(x_prompt, x_sample, c, c_ctx, cache_ckv, cache_krope, state_gla, ada_w, ada_b, norm_g,
           ffn1_wg, ffn1_wu, ffn1_wd, ffn2_wg, ffn2_wu, ffn2_wd, even_w_in, even_w_out,
           gla_gate_w2, gla_gate_b, gla_norm_g, mla_qa_g, mla_qb_w, mla_kva_g, mla_kvb_w,
           mla_qn_g, mla_kn_g, odd_w_in, odd_v_g, odd_ws, odd_bs, odd_w_out):
    batch, seq, d = x_prompt.shape
    dec_batch, dec_seq, _ = x_sample.shape
    depth = ada_w.shape[0]
    n_even = even_w_in.shape[0]
    m_ctx, m_dec = batch * seq, dec_batch * dec_seq
    assert 1 + dec_batch <= COND_PAD and seq % CMLP_CHUNK == 0 and dec_seq % CMLP_CHUNK == 0

    rows_ffn = _Rows(m_ctx, dec_batch, dec_seq, _pick_tile(512, m_ctx, dec_seq))
    rows_mix = _Rows(m_ctx, dec_batch, dec_seq, _pick_tile(512, m_ctx, dec_seq))

    cond = jnp.concatenate([c_ctx[None, :], c, jnp.zeros((COND_PAD - 1 - dec_batch, d), F32)], axis=0)
    mod = _modulation_all(cond, ada_w, ada_b).reshape(depth, COND_PAD, N_MOD, d)

    cos_t, sin_t = _rope_tables(dec_seq, rows_mix.tm)
    tri = _tri_masks()
    krope_blk = jnp.concatenate([jnp.zeros(cache_krope.shape[:-1] + (MLA_NOPE,), F32), cache_krope,
                                 cache_krope], axis=-1)
    even_wts = _pack_even_weights(cos_t, sin_t, even_w_in, gla_gate_w2, gla_gate_b, mla_qa_g, mla_qb_w,
                                  mla_qn_g, mla_kva_g, mla_kvb_w, mla_kn_g)
    kc, vc = _ctx_kv(cache_ckv, krope_blk, even_wts["kvb_k"], even_wts["kvb_v"], even_wts["kn_g_ctx"])

    ffn1 = (ffn1_wg, ffn1_wu, ffn1_wd)
    ffn2 = (ffn2_wg, ffn2_wu, ffn2_wd)
    w_out_bf = even_w_out.astype(BF16)
    odd_wts = (odd_w_in, odd_v_g[:, None, :], odd_ws.astype(BF16), jnp.swapaxes(odd_bs, 1, 2), odd_w_out)
    norms = norm_g.reshape(depth * 3, 1, d)
    gg = gla_norm_g[:, None, :]

    x = (x_prompt.reshape(m_ctx, d), x_sample.reshape(m_dec, d))
    new_ckv, new_krope, new_gla = [], [], []
    for i in range(depth):
        j = i // 2
        x = _ffn(x, mod, i, 0, (norms, 3 * i), *ffn1, rows_ffn)
        g_mix = (norms, 3 * i + 1)
        premix = None
        if i % 2 == 0:
            gq, gk, gv, og, la, qm, km, vm, ckv, kr = _even_in(x, mod, i, g_mix, even_wts, j, rows_mix)
            gla_ctx, st = _gla(gq, gk, gv, la, og, None, j, tri, gg, batch, seq, 0)
            gla_dec, _ = _gla(gq, gk, gv, la, og, state_gla, j, tri, gg, dec_batch, dec_seq, m_ctx)
            mla_ctx = _mla(qm, km, vm, None, None, j, batch, seq, 0, _pick_tile(256, seq))
            mla_dec = _mla(qm, km, vm, kc, vc, j, dec_batch, dec_seq, m_ctx, _pick_tile(512, dec_seq))
            premix = ((gla_ctx, gla_dec), (mla_ctx, mla_dec), w_out_bf, j)
            new_ckv.append(ckv.reshape(batch, seq, MLA_KV_RANK))
            new_krope.append(kr.reshape(batch, seq, MLA_ROPE))
            new_gla.append(st)
        else:
            x = _odd_mixer(x, mod, i, j, g_mix, *odd_wts, rows_mix)
        x = _ffn(x, mod, i, 6, (norms, 3 * i + 2), *ffn2, rows_ffn, split_out=(i == depth - 1),
                 premix=premix)

    y_prompt, y_sample = x
    return (y_prompt.reshape(batch, seq, d), y_sample.reshape(dec_batch, dec_seq, d),
            jnp.stack(new_ckv, axis=1), jnp.stack(new_krope, axis=1), jnp.stack(new_gla, axis=1))
```

```python
import functools
import math

import numpy as np
import jax
import jax.numpy as jnp
from jax import lax
from jax.experimental import pallas as pl
from jax.experimental.pallas import tpu as pltpu

F32 = jnp.float32
BF16 = jnp.bfloat16

EPS = 1e-6
N_MOD = 9
GRID_W = 64
ROPE_BASE = 10000.0
GLA_HEADS = 4
GLA_DK = 64
GLA_DV = 128
GLA_QK = GLA_HEADS * GLA_DK
GLA_VW = GLA_HEADS * GLA_DV
GLA_GATE_RANK = 16
GLA_GATE_NORM = 16.0
GLA_CHUNK = 64
MLA_HEADS = 8
MLA_NOPE = 64
MLA_ROPE = 32
MLA_V = 64
MLA_QK_DIM = MLA_NOPE + MLA_ROPE
MLA_Q_RANK = 384
MLA_KV_RANK = 256
HEAD_PAD = 128
CMLP_CHUNK = 128
CMLP_GROUPS = 4
COND_PAD = 16

VMEM_LIMIT = 56 * 1024 * 1024


def _cparams(*sem):
    return pltpu.CompilerParams(dimension_semantics=sem, vmem_limit_bytes=VMEM_LIMIT)


def _resident(shape, index_map):
    return pl.BlockSpec(shape, index_map, pipeline_mode=pl.Buffered(1))


def _const_map(nd):
    return lambda *_: (0,) * nd


def _layer_spec(stack, layer):
    nd = stack.ndim - 1
    return _resident((None,) + stack.shape[1:], lambda *_: (layer,) + (0,) * nd)


def _silu(x):
    return x * jax.nn.sigmoid(x)


def _rms(x, g):
    return x * lax.rsqrt(jnp.mean(x * x, axis=-1, keepdims=True) + EPS) * g


def _modulate(x, g, shift, scale):
    return _rms(x, g) * (1.0 + scale) + shift


def _dot(a, b):
    return jnp.dot(a, b, preferred_element_type=F32)


def _dot_nt(a, b):
    return lax.dot_general(a, b, (((1,), (1,)), ((), ())), preferred_element_type=F32)


def _mod_kernel(c_ref, w_ref, b_ref, o_ref):
    s = _silu(c_ref[...]).astype(BF16)
    o_ref[...] = _dot(s, w_ref[...].astype(BF16)) + b_ref[...]


def _modulation_all(cond, ada_w, ada_b):
    depth, d, n = ada_w.shape
    tn = n // 4
    return pl.pallas_call(
        _mod_kernel,
        out_shape=jax.ShapeDtypeStruct((depth, COND_PAD, n), F32),
        grid=(depth, n // tn),
        in_specs=[
            pl.BlockSpec((COND_PAD, d), lambda l, j: (0, 0)),
            pl.BlockSpec((None, d, tn), lambda l, j: (l, 0, j)),
            pl.BlockSpec((None, 1, tn), lambda l, j: (l, 0, j)),
        ],
        out_specs=pl.BlockSpec((None, COND_PAD, tn), lambda l, j: (l, 0, j)),
        compiler_params=_cparams("arbitrary", "arbitrary"),
        name="adaln_modulation",
    )(cond, ada_w, ada_b.reshape(depth, 1, n))


class _Rows:
    def __init__(self, m_ctx, dec_batch, dec_seq, tm):
        assert m_ctx % tm == 0 and dec_seq % tm == 0
        self.tm = tm
        self.m_ctx = m_ctx
        self.m = m_ctx + dec_batch * dec_seq
        self.nb_ctx = m_ctx // tm
        self.nb_seq = dec_seq // tm
        self.nblocks = self.m // tm

    def cond(self, i):
        return jnp.where(i < self.nb_ctx, 0, 1 + (i - self.nb_ctx) // self.nb_seq)

    def rope_block(self, i):
        return jnp.where(i < self.nb_ctx, 0, 1 + (i - self.nb_ctx) % self.nb_seq)

    def mod_spec(self, layer, d):
        return pl.BlockSpec((None, None, N_MOD, d), lambda i: (layer, self.cond(i), 0, 0))

    def row_spec(self, width):
        return pl.BlockSpec((self.tm, width), lambda i: (i, 0))

    def ctx_spec(self, width):
        return pl.BlockSpec((self.tm, width), lambda i: (jnp.minimum(i, self.nb_ctx - 1), 0))

    def dec_spec(self, width):
        return pl.BlockSpec((self.tm, width), lambda i: (jnp.maximum(i - self.nb_ctx, 0), 0))

    def split_specs(self, width):
        return [self.ctx_spec(width), self.dec_spec(width)]

    def split_shapes(self, width, dtype):
        return [jax.ShapeDtypeStruct((self.m_ctx, width), dtype),
                jax.ShapeDtypeStruct((self.m - self.m_ctx, width), dtype)]


def _load_split(refs, is_ctx):
    if len(refs) == 1:
        return refs[0][...]
    return jnp.where(is_ctx, refs[0][...], refs[1][...])


FFN_STAGE_CHUNKS = 16
FFN_STAGE_SLOTS = 8


def _stage_weights_bf16(layer, hbm_refs, vmem_refs, stage_refs, sem):
    chunks = []
    for src, dst, stage in zip(hbm_refs, vmem_refs, stage_refs):
        rows = stage.shape[1]
        assert dst.shape[0] % rows == 0
        chunks += [(src, dst, stage, r0) for r0 in range(0, dst.shape[0], rows)]
    slots = sem.shape[0]
    ahead = slots - 1

    def copy(i):
        src, _, stage, r0 = chunks[i]
        return pltpu.make_async_copy(src.at[layer, pl.ds(r0, stage.shape[1]), :], stage.at[i % slots],
                                     sem.at[i % slots])

    for i in range(min(ahead, len(chunks))):
        copy(i).start()
    for i, (_, dst, stage, r0) in enumerate(chunks):
        if i + ahead < len(chunks):
            copy(i + ahead).start()
        copy(i).wait()
        dst[pl.ds(r0, stage.shape[1]), :] = stage[i % slots].astype(BF16)


def _ffn_kernel(*refs, layer, row0, nb_ctx, split_in, split_out, premix):
    it = iter(refs)
    x_refs = [next(it) for _ in range(2 if split_in else 1)]
    mod_ref, g_ref = next(it), next(it)
    w_hbm = [next(it) for _ in range(3)]
    if premix:
        a_refs, b_refs = [next(it), next(it)], [next(it), next(it)]
        wo_ref = next(it)
    o_refs = [next(it) for _ in range(2 if split_out else 1)]
    wg_ref, wu_ref, wd_ref = w_vmem = [next(it) for _ in range(3)]
    stage_in, stage_down, sem = next(it), next(it), next(it)

    @pl.when(pl.program_id(0) == 0)
    def _():
        _stage_weights_bf16(layer, w_hbm, w_vmem, [stage_in, stage_in, stage_down], sem)

    is_ctx = pl.program_id(0) < nb_ctx
    x = _load_split(x_refs, is_ctx)
    if premix:
        half = wo_ref.shape[0] // 2
        mix = _dot(_load_split(a_refs, is_ctx), wo_ref[:half, :])
        mix = mix + _dot(_load_split(b_refs, is_ctx), wo_ref[half:, :])
        x = x + mod_ref[5:6, :] * mix
    shift = mod_ref[row0:row0 + 1, :]
    scale = mod_ref[row0 + 1:row0 + 2, :]
    gate = mod_ref[row0 + 2:row0 + 3, :]
    h = _modulate(x, g_ref[...], shift, scale).astype(BF16)
    a = _dot(h, wg_ref[...])
    u = _dot(h, wu_ref[...])
    act = (_silu(a) * u).astype(BF16)
    y = _dot(act, wd_ref[...])
    out = x + (0.5 * gate) * y
    if split_out:
        @pl.when(is_ctx)
        def _():
            o_refs[0][...] = out

        @pl.when(jnp.logical_not(is_ctx))
        def _():
            o_refs[1][...] = out
    else:
        o_refs[0][...] = out


def _ffn(xs, mod, layer, row0, g, wg, wu, wd, rows, split_out=False, premix=None):
    split_in = isinstance(xs, (tuple, list))
    xs = list(xs) if split_in else [xs]
    d = xs[0].shape[1]
    f = wg.shape[2]
    hbm = pl.BlockSpec(memory_space=pl.ANY)
    in_specs = (rows.split_specs(d) if split_in else [rows.row_spec(d)]) + [
        rows.mod_spec(layer, d),
        _layer_spec(*g),
        hbm, hbm, hbm,
    ]
    args = xs + [mod, g[0], wg, wu, wd]
    if premix is not None:
        gla, mla, w_out, j = premix
        in_specs += rows.split_specs(gla[0].shape[1]) + rows.split_specs(mla[0].shape[1])
        in_specs.append(_layer_spec(w_out, j))
        args += list(gla) + list(mla) + [w_out]
    return pl.pallas_call(
        functools.partial(_ffn_kernel, layer=layer, row0=row0, nb_ctx=rows.nb_ctx, split_in=split_in,
                          split_out=split_out, premix=premix is not None),
        out_shape=rows.split_shapes(d, F32) if split_out else jax.ShapeDtypeStruct((rows.m, d), F32),
        grid=(rows.nblocks,),
        in_specs=in_specs,
        out_specs=rows.split_specs(d) if split_out else rows.row_spec(d),
        scratch_shapes=[
            pltpu.VMEM((d, f), BF16), pltpu.VMEM((d, f), BF16), pltpu.VMEM((f, d), BF16),
            pltpu.VMEM((FFN_STAGE_SLOTS, d // FFN_STAGE_CHUNKS, f), F32),
            pltpu.VMEM((FFN_STAGE_SLOTS, f // FFN_STAGE_CHUNKS, d), F32),
            pltpu.SemaphoreType.DMA((FFN_STAGE_SLOTS,)),
        ],
        compiler_params=_cparams("arbitrary"),
        name="ffn_swiglu",
    )(*args)


def _gelu_tanh(x):
    c = math.sqrt(2.0 / math.pi)
    return x * (0.5 * (1.0 + jnp.tanh(c * (x + 0.044715 * (x * x * x)))))


def _odd_kernel(x_ref, mod_ref, g_ref, win_hbm, vg_ref, ws_ref, bst_ref, wout_hbm, o_ref,
                win_ref, wout_ref, stage_in, stage_out, sem, *, layer_j):
    @pl.when(pl.program_id(0) == 0)
    def _():
        _stage_weights_bf16(layer_j, [win_hbm, wout_hbm], [win_ref, wout_ref], [stage_in, stage_out], sem)

    x = x_ref[...]
    tm = x.shape[0]
    shift, scale, gate = mod_ref[3:4, :], mod_ref[4:5, :], mod_ref[5:6, :]
    h = _modulate(x, g_ref[...], shift, scale).astype(BF16)
    uv = _gelu_tanh(_dot(h, win_ref[...]))
    width = uv.shape[1] // 2
    u = uv[:, :width]
    v = _rms(uv[:, width:], vg_ref[...]).astype(BF16)
    gw = width // CMLP_GROUPS
    chunks = []
    for c in range(tm // CMLP_CHUNK):
        r0 = c * CMLP_CHUNK
        groups = []
        for g in range(CMLP_GROUPS):
            vg = v[r0:r0 + CMLP_CHUNK, g * gw:(g + 1) * gw]
            mixed = _dot(ws_ref[g], vg) + bst_ref[:, g:g + 1]
            groups.append(u[r0:r0 + CMLP_CHUNK, g * gw:(g + 1) * gw] * mixed)
        chunks.append(jnp.concatenate(groups, axis=1))
    z = jnp.concatenate(chunks, axis=0).astype(BF16)
    o_ref[...] = x + gate * _dot(z, wout_ref[...])


def _odd_mixer(x, mod, layer, layer_j, g, w_in, v_g, w_s, b_s_t, w_out, rows):
    m, d = x.shape
    n_in, width = w_in.shape[2], w_out.shape[1]
    hbm = pl.BlockSpec(memory_space=pl.ANY)
    return pl.pallas_call(
        functools.partial(_odd_kernel, layer_j=layer_j),
        out_shape=jax.ShapeDtypeStruct((m, d), F32),
        grid=(rows.nblocks,),
        in_specs=[
            rows.row_spec(d),
            rows.mod_spec(layer, d),
            _layer_spec(*g),
            hbm, _layer_spec(v_g, layer_j), _layer_spec(w_s, layer_j), _layer_spec(b_s_t, layer_j), hbm,
        ],
        out_specs=rows.row_spec(d),
        scratch_shapes=[
            pltpu.VMEM((d, n_in), BF16), pltpu.VMEM((width, d), BF16),
            pltpu.VMEM((FFN_STAGE_SLOTS, d // FFN_STAGE_CHUNKS, n_in), F32),
            pltpu.VMEM((FFN_STAGE_SLOTS, width // FFN_STAGE_CHUNKS, d), F32),
            pltpu.SemaphoreType.DMA((FFN_STAGE_SLOTS,)),
        ],
        compiler_params=_cparams("arbitrary"),
        name="odd_gmlp",
    )(x, mod, g[0], w_in, v_g, w_s, b_s_t, w_out)


_C_CQ = 0
_C_CKV = _C_CQ + MLA_Q_RANK
_C_KA = _C_CKV + MLA_KV_RANK
_C_KB = _C_KA + HEAD_PAD
_C_Q = _C_KB + HEAD_PAD
_C_K = _C_Q + GLA_QK
_C_V = _C_K + GLA_QK
_C_OG = _C_V + GLA_VW
_C_END = _C_OG + GLA_VW


def _head_scale(x, live):
    ss = jnp.sum(jnp.where(live, x * x, 0.0), axis=-1, keepdims=True)
    return lax.rsqrt(ss * (1.0 / MLA_QK_DIM) + EPS)


def _even_in_kernel(x_ref, mod_ref, g_ref, win_ref, w2_ref, gb_ref, qag_ref, qb_ref, kvag_ref,
                    kvbk_ref, kvbv_ref, kng_ref, tq_ref, ta_ref, tb_ref,
                    gq_ref, gk_ref, gv_ref, og_ref, la_ref, qm_ref, km_ref, vm_ref, ckv_ref, kr_ref,
                    *, nb_ctx, n_sub):
    is_ctx = pl.program_id(0) < nb_ctx
    shift, scale = mod_ref[3:4, :], mod_ref[4:5, :]
    rb = x_ref.shape[0] // n_sub

    def stages(r0):
        rs = slice(r0, r0 + rb)
        h = _modulate(x_ref[rs, :], g_ref[...], shift, scale).astype(BF16)
        yield
        proj_q = _dot(h, win_ref[:, _C_CQ:_C_CKV])
        proj_kv = _dot(h, win_ref[:, _C_CKV:_C_Q])
        yield
        ckv = _rms(proj_kv[:, :MLA_KV_RANK], kvag_ref[...])
        ka = proj_kv[:, _C_KA - _C_CKV:_C_KB - _C_CKV]
        kb = proj_kv[:, _C_KB - _C_CKV:]

        @pl.when(is_ctx)
        def _():
            ckv_ref[rs, :] = ckv
            kr_ref[rs, :] = ka[:, MLA_NOPE:MLA_QK_DIM]

        cq = _rms(proj_q, qag_ref[...]).astype(BF16)
        ckv_bf = ckv.astype(BF16)
        yield
        qn = _dot(cq, qb_ref[...])
        kn = _dot(ckv_bf, kvbk_ref[...])
        vm_t = _dot(ckv_bf, kvbv_ref[...])
        z = _dot(ka.astype(BF16), w2_ref[...]) + gb_ref[...]
        yield

        lane = lax.broadcasted_iota(jnp.int32, ka.shape, 1)
        live = lane < MLA_QK_DIM
        tq = tq_ref[rs, :]

        def q_heads(heads):
            for hd in heads:
                sl = slice(hd * HEAD_PAD, (hd + 1) * HEAD_PAD)
                qh = qn[:, sl]
                qm_ref[rs, sl] = (qh * tq * _head_scale(qh, live)).astype(BF16)

        is_rope = live & (lane >= MLA_NOPE)
        ss_rope = jnp.sum(jnp.where(is_rope, ka * ka, 0.0), axis=-1, keepdims=True)
        rot = ka * ta_ref[rs, :] + kb * tb_ref[rs, :]
        kng = kng_ref[...]

        def k_heads(heads):
            for hd in heads:
                sl = slice(hd * HEAD_PAD, (hd + 1) * HEAD_PAD)
                kh = kn[:, sl]
                ss = jnp.sum(kh * kh, axis=-1, keepdims=True) + ss_rope
                r = lax.rsqrt(ss * (1.0 / MLA_QK_DIM) + EPS)
                km_ref[rs, sl] = ((kh * kng + rot) * r).astype(BF16)

        half = MLA_HEADS // 2
        gqk = _dot(h, win_ref[:, _C_Q:_C_V])
        q_heads(range(half))
        yield
        gq_ref[rs, :] = gqk[:, :GLA_QK] * (GLA_DK ** -0.5)
        gk_ref[rs, :] = gqk[:, GLA_QK:]
        gv = _dot(h, win_ref[:, _C_V:_C_OG])
        q_heads(range(half, MLA_HEADS))
        yield
        gv_ref[rs, :] = gv.astype(BF16)
        og = _dot(h, win_ref[:, _C_OG:])
        k_heads(range(half))
        yield
        og_ref[rs, :] = og
        vm_ref[:, rs] = vm_t.T.astype(BF16)
        k_heads(range(half, MLA_HEADS))
        la_ref[rs, :] = ((jnp.minimum(z, 0.0) - jnp.log(1.0 + jnp.exp(-jnp.abs(z))))
                         * (math.log2(math.e) / GLA_GATE_NORM))

    _run_skewed([stages(u * rb) for u in range(n_sub)], skew=2)


def _even_in(x, mod, layer, g, wts, layer_j, rows):
    m, d = x.shape
    tm = rows.tm
    rope_spec = pl.BlockSpec((None, tm, HEAD_PAD), lambda i: (layer_j, rows.rope_block(i), 0))
    widths = [GLA_QK, GLA_QK, GLA_VW, GLA_VW, 2 * GLA_QK, MLA_HEADS * HEAD_PAD, MLA_HEADS * HEAD_PAD]
    dtypes = [F32, F32, BF16, F32, F32, BF16, BF16]
    names = ["w_in", "w2", "gate_b", "qa_g", "qb", "kva_g", "kvb_k", "kvb_v", "kn_g"]
    return pl.pallas_call(
        functools.partial(_even_in_kernel, nb_ctx=rows.nb_ctx, n_sub=1),
        out_shape=[jax.ShapeDtypeStruct((m, w), t) for w, t in zip(widths, dtypes)]
        + [jax.ShapeDtypeStruct((MLA_HEADS * MLA_V, m), BF16),
           jax.ShapeDtypeStruct((rows.m_ctx, MLA_KV_RANK), F32),
           jax.ShapeDtypeStruct((rows.m_ctx, MLA_ROPE), F32)],
        grid=(rows.nblocks,),
        in_specs=[rows.row_spec(d), rows.mod_spec(layer, d), _layer_spec(*g)]
        + [_layer_spec(wts[n], layer_j) for n in names] + [rope_spec] * 3,
        out_specs=[rows.row_spec(w) for w in widths]
        + [pl.BlockSpec((MLA_HEADS * MLA_V, tm), lambda i: (0, i)),
           rows.ctx_spec(MLA_KV_RANK), rows.ctx_spec(MLA_ROPE)],
        compiler_params=_cparams("arbitrary"),
        name="even_in_proj",
    )(x, mod, g[0], *[wts[n] for n in names], wts["t_q"], wts["t_a"], wts["t_b"])


def _ctx_kv_kernel(ckv_ref, kr_ref, kvbk_ref, kvbv_ref, kng_ref, k_ref, v_ref):
    ckv_bf = ckv_ref[...].astype(BF16)
    v_ref[...] = _dot(ckv_bf, kvbv_ref[...]).T.astype(BF16)
    kn = _dot(ckv_bf, kvbk_ref[...])
    kr = kr_ref[...]
    lane = lax.broadcasted_iota(jnp.int32, kr.shape, 1)
    live = lane < MLA_QK_DIM
    kng = kng_ref[...]
    for hd in range(MLA_HEADS):
        sl = slice(hd * HEAD_PAD, (hd + 1) * HEAD_PAD)
        kh = kn[:, sl] + kr
        k_ref[:, sl] = (kh * kng * _head_scale(kh, live)).astype(BF16)


def _ctx_kv(cache_ckv, cache_krope_blk, kvb_k, kvb_v, kn_g):
    nb, ne, p, r = cache_ckv.shape
    return pl.pallas_call(
        _ctx_kv_kernel,
        out_shape=[jax.ShapeDtypeStruct((ne, nb, p, MLA_HEADS * HEAD_PAD), BF16),
                   jax.ShapeDtypeStruct((ne, nb, MLA_HEADS * MLA_V, p), BF16)],
        grid=(ne, nb),
        in_specs=[
            pl.BlockSpec((None, None, p, r), lambda j, b: (b, j, 0, 0)),
            pl.BlockSpec((None, None, p, HEAD_PAD), lambda j, b: (b, j, 0, 0)),
            pl.BlockSpec((None,) + kvb_k.shape[1:], lambda j, b: (j, 0, 0)),
            pl.BlockSpec((None,) + kvb_v.shape[1:], lambda j, b: (j, 0, 0)),
            pl.BlockSpec((None,) + kn_g.shape[1:], lambda j, b: (j, 0, 0)),
        ],
        out_specs=[pl.BlockSpec((None, None, p, MLA_HEADS * HEAD_PAD), lambda j, b: (j, b, 0, 0)),
                   pl.BlockSpec((None, None, MLA_HEADS * MLA_V, p), lambda j, b: (j, b, 0, 0))],
        compiler_params=_cparams("arbitrary", "arbitrary"),
        name="ctx_kv",
    )(cache_ckv, cache_krope_blk, kvb_k, kvb_v, kn_g)


def _split3(x):
    hi = x.astype(BF16)
    r = x - hi.astype(F32)
    mid = r.astype(BF16)
    lo = (r - mid.astype(F32)).astype(BF16)
    return hi, mid, lo


def _block_row_bcast(a, blk, off):
    c, n = a.shape
    if blk >= 8:
        pieces = [jnp.broadcast_to(a[b * blk + off:b * blk + off + 1, :], (blk, n))
                  for b in range(c // blk)]
        return pieces[0] if len(pieces) == 1 else jnp.concatenate(pieces, axis=0)
    a3 = a.reshape(c // 8, 8, n)
    sub = lax.broadcasted_iota(jnp.int32, a3.shape, 1) // blk
    out = jnp.broadcast_to(a3[:, off:off + 1, :], a3.shape)
    for s in range(1, 8 // blk):
        cand = jnp.broadcast_to(a3[:, s * blk + off:s * blk + off + 1, :], a3.shape)
        out = jnp.where(sub == s, cand, out)
    return out.reshape(c, n)


def _head_stack(a, head_w):
    lane_head = lax.broadcasted_iota(jnp.int32, a.shape, 1) // head_w
    zero = jnp.zeros_like(a)
    return jnp.concatenate([jnp.where(lane_head == h, a, zero) for h in range(GLA_HEADS)], axis=0)


def _run_skewed(stage_gens, skew=1):
    done = [False] * len(stage_gens)
    tick = 0
    while not all(done):
        for u, gen in enumerate(stage_gens):
            if tick >= u * skew and not done[u]:
                try:
                    next(gen)
                except StopIteration:
                    done[u] = True
        tick += 1


def _gla_phase_a(c, q_ref, k_ref, v_ref, la_ref, tri_ref, pair_ref, oacc_ref, qst_ref, ds_ref, dc_ref):
    C = GLA_CHUNK
    r0 = pl.multiple_of(c * C, C)
    q = q_ref[pl.ds(r0, C), :]
    k = k_ref[pl.ds(r0, C), :]
    la = la_ref[pl.ds(r0, C), :]
    la_f, la_b = la[:, :GLA_QK], la[:, GLA_QK:]
    cums = []
    for d, la_d in enumerate((la_f, la_b)):
        hi, mid, lo = _split3(la_d)
        tri = tri_ref[d]
        cums.append(_dot(tri, hi) + _dot(tri, mid) + _dot(tri, lo))
    cum_f, cum_b = cums
    k_stack = _head_stack(k.astype(BF16), GLA_DK)
    yield

    row = lax.broadcasted_iota(jnp.int32, (C, 1), 0)
    pair_m = pair_ref[...]
    att = jnp.where(pair_m == 0, 2.0 * _dot_nt(q.astype(BF16), k_stack), 0.0)
    m = C // 2
    while m >= 1:
        blk = 2 * m
        upper = (row % blk) >= m
        if m == 1:
            qm = (q * jnp.exp2(jnp.where(upper, la_f, la_b))).astype(BF16)
            km_stack = k_stack
        else:
            ref_f = _block_row_bcast(cum_f, blk, m - 1)
            ref_b = _block_row_bcast(cum_b, blk, m)
            eq = jnp.where(upper, cum_f - ref_f, cum_b - ref_b)
            ek = jnp.where(upper, ref_b - cum_b, ref_f - cum_f)
            qm = (q * jnp.exp2(eq)).astype(BF16)
            km_stack = _head_stack((k * jnp.exp2(ek)).astype(BF16), GLA_DK)
        yield
        att = jnp.where(pair_m == m, _dot_nt(qm, km_stack), att)
        m //= 2

    v = v_ref[pl.ds(r0, C), :]
    v_stack = _head_stack(v, GLA_DV)
    qcs = [(q * jnp.exp2(cum)).astype(BF16) for cum in cums]
    yield
    oacc_ref[pl.ds(r0, C), :] = _dot(att.astype(BF16), v_stack)
    for d, cum in enumerate(cums):
        last = cum[C - 1:C, :] if d == 0 else cum[0:1, :]
        qst_ref[d, c] = _head_stack(qcs[d], GLA_DK)
        kct = (k * jnp.exp2(last - cum)).T.astype(BF16)
        decay_col = jnp.exp2(jnp.broadcast_to(last, (8, GLA_QK))).T[:, 0:1]
        dc_ref[d, c] = jnp.broadcast_to(decay_col, (GLA_QK, GLA_DV))
        yield
        ds_ref[d, c] = jnp.concatenate(
            [_dot(kct[h * GLA_DK:(h + 1) * GLA_DK, :], v[:, h * GLA_DV:(h + 1) * GLA_DV])
             for h in range(GLA_HEADS)], axis=0)


def _gla_phase_c(c, og_ref, gg, oacc_ref, qst_ref, sb_ref, o_ref):
    C = GLA_CHUNK
    r0 = pl.multiple_of(c * C, C)
    q_cat = jnp.concatenate([qst_ref[0, c], qst_ref[1, c]], axis=1)
    s_cat = jnp.concatenate([sb_ref[0, c], sb_ref[1, c]], axis=0)
    inter = _dot(q_cat, s_cat)
    yield
    for h in range(GLA_HEADS):
        sl = slice(h * GLA_DV, (h + 1) * GLA_DV)
        o = oacc_ref[pl.ds(r0, C), sl] + inter[h * C:(h + 1) * C, :]
        o_ref[pl.ds(r0, C), sl] = (_rms(o, gg) * _silu(og_ref[pl.ds(r0, C), sl])).astype(BF16)


def _gla_kernel(*refs, has_state):
    it = iter(refs)
    q_ref, k_ref, v_ref, la_ref, og_ref = (next(it) for _ in range(5))
    s0_ref = next(it) if has_state else None
    tri_ref, pair_ref, gg_ref, o_ref, sout_ref = (next(it) for _ in range(5))
    oacc_ref, qst_ref, ds_ref, dc_ref, sb_ref, st_ref = (next(it) for _ in range(6))

    nc = q_ref.shape[0] // GLA_CHUNK
    per_step = 4

    def phase_a(i, carry):
        _run_skewed([_gla_phase_a(per_step * i + u, q_ref, k_ref, v_ref, la_ref, tri_ref, pair_ref, oacc_ref,
                                  qst_ref, ds_ref, dc_ref) for u in range(per_step)])
        return carry

    lax.fori_loop(0, nc // per_step, phase_a, 0)

    for d in range(2):
        for h in range(GLA_HEADS):
            rows = slice(h * GLA_DK, (h + 1) * GLA_DK)
            st_ref[d, rows, :] = s0_ref[d, h] if has_state else jnp.zeros((GLA_DK, GLA_DV), F32)

    def phase_b(c, carry):
        for d in range(2):
            cc = c if d == 0 else nc - 1 - c
            s = st_ref[d]
            sb_ref[d, cc] = s.astype(BF16)
            st_ref[d] = s * dc_ref[d, cc] + ds_ref[d, cc]
        return carry

    lax.fori_loop(0, nc, phase_b, 0)
    for d in range(2):
        for h in range(GLA_HEADS):
            sout_ref[d, h] = st_ref[d, h * GLA_DK:(h + 1) * GLA_DK, :]

    gg = gg_ref[...]

    def phase_c(i, carry):
        _run_skewed([_gla_phase_c(per_step * i + u, og_ref, gg, oacc_ref, qst_ref, sb_ref, o_ref)
                     for u in range(per_step)])
        return carry

    lax.fori_loop(0, nc // per_step, phase_c, 0)


def _gla(gq, gk, gv, la, og, state, layer_j, tri, gg, nbatch, t, row_off):
    assert t % (2 * GLA_CHUNK) == 0
    b0 = row_off // t
    nc = t // GLA_CHUNK
    seq = lambda w: pl.BlockSpec((t, w), lambda b: (b0 + b, 0))
    in_specs = [seq(GLA_QK), seq(GLA_QK), seq(GLA_VW), seq(2 * GLA_QK), seq(GLA_VW)]
    args = [gq, gk, gv, la, og]
    if state is not None:
        in_specs.append(pl.BlockSpec((None, None, 2, GLA_HEADS, GLA_DK, GLA_DV),
                                     lambda b: (b, layer_j, 0, 0, 0, 0)))
        args.append(state)
    tri, pair_m = tri
    in_specs += [_resident(tri.shape, _const_map(3)), _resident(pair_m.shape, _const_map(2)),
                 _layer_spec(gg, layer_j)]
    args += [tri, pair_m, gg]
    return pl.pallas_call(
        functools.partial(_gla_kernel, has_state=state is not None),
        out_shape=[jax.ShapeDtypeStruct((nbatch * t, GLA_VW), BF16),
                   jax.ShapeDtypeStruct((nbatch, 2, GLA_HEADS, GLA_DK, GLA_DV), F32)],
        grid=(nbatch,),
        in_specs=in_specs,
        out_specs=[pl.BlockSpec((t, GLA_VW), lambda b: (b, 0)),
                   pl.BlockSpec((None, 2, GLA_HEADS, GLA_DK, GLA_DV), lambda b: (b, 0, 0, 0, 0))],
        scratch_shapes=[
            pltpu.VMEM((t, GLA_VW), F32),
            pltpu.VMEM((2, nc, GLA_HEADS * GLA_CHUNK, GLA_QK), BF16),
            pltpu.VMEM((2, nc, GLA_QK, GLA_DV), F32),
            pltpu.VMEM((2, nc, GLA_QK, GLA_DV), F32),
            pltpu.VMEM((2, nc, GLA_QK, GLA_DV), BF16),
            pltpu.VMEM((2, GLA_QK, GLA_DV), F32),
        ],
        compiler_params=_cparams("arbitrary"),
        name="gla_scan",
    )(*args)


def _mla_kernel(*refs, has_ctx):
    it = iter(refs)
    q_ref, k_ref, v_ref = next(it), next(it), next(it)
    kc_ref, vc_ref = (next(it), next(it)) if has_ctx else (None, None)
    o_ref = next(it)
    def scores(hd):
        sl = slice(hd * HEAD_PAD, (hd + 1) * HEAD_PAD)
        q = q_ref[:, sl]
        return [_dot_nt(r[:, sl], q) for r in ((k_ref, kc_ref) if has_ctx else (k_ref,))]

    def softmax(ss):
        mx = functools.reduce(jnp.maximum, [jnp.max(s, axis=0, keepdims=True) for s in ss])
        ps = [jnp.exp2(s - mx) for s in ss]
        den = sum(jnp.sum(p, axis=0, keepdims=True) for p in ps)
        return [p.astype(BF16) for p in ps], den

    def values(hd, ps, den):
        vrows = slice(hd * MLA_V, (hd + 1) * MLA_V)
        o = sum(_dot(r[vrows, :], p) for r, p in zip((v_ref, vc_ref), ps))
        return o / den

    s_q, p_q, outs = {}, {}, []
    for step in range(MLA_HEADS + 2):
        if step < MLA_HEADS:
            s_q[step] = scores(step)
        if 0 <= step - 1 < MLA_HEADS:
            p_q[step - 1] = softmax(s_q.pop(step - 1))
        if 0 <= step - 2 < MLA_HEADS:
            outs.append(values(step - 2, *p_q.pop(step - 2)))
    o_ref[...] = jnp.concatenate(outs, axis=0).T.astype(BF16)


def _mla(qm, km, vm, kc, vc, layer_j, nbatch, t, row_off, tq):
    nq = t // tq
    q0, b0 = row_off // tq, row_off // t
    qspec = lambda w: pl.BlockSpec((tq, w), lambda b, i: (q0 + b * nq + i, 0))
    kspec = lambda w: pl.BlockSpec((t, w), lambda b, i: (b0 + b, 0))
    vspec = pl.BlockSpec((MLA_HEADS * MLA_V, t), lambda b, i: (0, b0 + b))
    in_specs = [qspec(MLA_HEADS * HEAD_PAD), kspec(MLA_HEADS * HEAD_PAD), vspec]
    args = [qm, km, vm]
    if kc is not None:
        in_specs += [pl.BlockSpec((None, None) + kc.shape[2:], lambda b, i: (layer_j, b, 0, 0)),
                     pl.BlockSpec((None, None) + vc.shape[2:], lambda b, i: (layer_j, b, 0, 0))]
        args += [kc, vc]
    return pl.pallas_call(
        functools.partial(_mla_kernel, has_ctx=kc is not None),
        out_shape=jax.ShapeDtypeStruct((nbatch * t, MLA_HEADS * MLA_V), BF16),
        grid=(nbatch, nq),
        in_specs=in_specs,
        out_specs=pl.BlockSpec((tq, MLA_HEADS * MLA_V), lambda b, i: (b * nq + i, 0)),
        compiler_params=_cparams("arbitrary", "arbitrary"),
        name="mla_attention",
    )(*args)


def _swap_signed(a):
    pairs = a.reshape(a.shape[:-1] + (a.shape[-1] // 2, 2))
    return jnp.stack([-pairs[..., 1], pairs[..., 0]], axis=-1).reshape(a.shape)


def _swap_pairs(a):
    pairs = a.reshape(a.shape[:-1] + (a.shape[-1] // 2, 2))
    return pairs[..., ::-1].reshape(a.shape)


def _pack_even_weights(cos, sin, even_w_in, gla_gate_w2, gla_gate_b, mla_qa_g, mla_qb_w, mla_qn_g,
                       mla_kva_g, mla_kvb_w, mla_kn_g):
    ne, d, _ = even_w_in.shape
    o = 0
    parts = {}
    for name, width in (("q", GLA_QK), ("k", GLA_QK), ("v", GLA_VW), ("og", GLA_VW),
                        ("gl", 2 * GLA_GATE_RANK), ("cq", MLA_Q_RANK), ("ckv", MLA_KV_RANK),
                        ("kr", MLA_ROPE)):
        parts[name] = even_w_in[:, :, o:o + width]
        o += width
    zeros = lambda n: jnp.zeros((ne, d, n), even_w_in.dtype)
    kr, kr_sw = parts["kr"], _swap_signed(parts["kr"])
    blk_a = jnp.concatenate([parts["gl"], zeros(MLA_NOPE - 2 * GLA_GATE_RANK), kr, kr_sw], axis=2)
    blk_b = jnp.concatenate([zeros(MLA_NOPE), kr_sw, kr], axis=2)
    w_in = jnp.concatenate([parts["cq"], parts["ckv"], blk_a, blk_b, parts["q"], parts["k"], parts["v"],
                            parts["og"]], axis=2).astype(BF16)
    assert w_in.shape[2] == _C_END

    w2_f = jnp.pad(gla_gate_w2[:, 0], ((0, 0), (0, 0), (0, GLA_QK)))
    w2_b = jnp.pad(gla_gate_w2[:, 1], ((0, 0), (0, 0), (GLA_QK, 0)))
    w2 = jnp.pad(jnp.concatenate([w2_f, w2_b], axis=1), ((0, 0), (0, HEAD_PAD - 2 * GLA_GATE_RANK), (0, 0)))

    pad_head = lambda a: jnp.pad(a, [(0, 0)] * (a.ndim - 1) + [(0, HEAD_PAD - a.shape[-1])])
    qb = jnp.concatenate([mla_qb_w, _swap_signed(mla_qb_w[..., MLA_NOPE:])], axis=-1)
    qb = qb.reshape(ne, MLA_Q_RANK, MLA_HEADS * HEAD_PAD)
    kvb_k = pad_head(mla_kvb_w[..., :MLA_NOPE]).reshape(ne, MLA_KV_RANK, MLA_HEADS * HEAD_PAD)
    kvb_v = mla_kvb_w[..., MLA_NOPE:].reshape(ne, MLA_KV_RANK, MLA_HEADS * MLA_V)

    def rot_tables(g):
        n = cos.shape[0]
        g_n = jnp.broadcast_to(g[:, None, :MLA_NOPE], (ne, n, MLA_NOPE))
        g_c = g[:, None, MLA_NOPE:] * cos
        g_s = _swap_pairs(g[:, MLA_NOPE:])[:, None, :] * sin
        return g_n, g_c, g_s

    qn_n, qn_c, qn_s = rot_tables(mla_qn_g)
    kn_n, kn_c, kn_s = rot_tables(mla_kn_g)
    kzero = jnp.zeros_like(kn_n)
    return {
        "w_in": w_in,
        "w2": w2.astype(BF16),
        "gate_b": gla_gate_b.reshape(ne, 1, 2 * GLA_QK),
        "qa_g": mla_qa_g[:, None, :],
        "qb": qb.astype(BF16),
        "kva_g": mla_kva_g[:, None, :],
        "kvb_k": kvb_k.astype(BF16),
        "kvb_v": kvb_v.astype(BF16),
        "kn_g": pad_head(mla_kn_g[:, :MLA_NOPE])[:, None, :],
        "kn_g_ctx": jnp.concatenate([mla_kn_g, mla_kn_g[:, MLA_NOPE:]], axis=1)[:, None, :],
        "t_q": jnp.concatenate([qn_n, qn_c, qn_s], axis=2) * (MLA_QK_DIM ** -0.5 * math.log2(math.e)),
        "t_a": jnp.concatenate([kzero, kn_c, kn_s], axis=2),
        "t_b": jnp.concatenate([kzero, kn_s, kn_c], axis=2),
    }


def _rope_tables(n, tm):
    pairs = MLA_ROPE // 4
    pos = np.arange(n)
    inv = ROPE_BASE ** (-jnp.arange(pairs, dtype=F32) / pairs)
    row = jnp.asarray(pos // GRID_W, F32)
    col = jnp.asarray(pos % GRID_W, F32)
    ang = jnp.concatenate([row[:, None] * inv, col[:, None] * inv], axis=-1)
    cos = jnp.repeat(jnp.cos(ang), 2, axis=-1)
    sin = jnp.repeat(jnp.sin(ang), 2, axis=-1)
    cos = jnp.pad(cos, ((tm, 0), (0, 0)), constant_values=1.0)
    sin = jnp.pad(sin, ((tm, 0), (0, 0)))
    return cos, sin


def _tri_masks():
    r = np.arange(GLA_CHUNK)
    lower = (r[None, :] <= r[:, None]).astype(np.float32)
    diff = r[:, None] ^ r[None, :]
    level = np.where(diff > 0, 1 << np.floor(np.log2(np.maximum(diff, 1))).astype(np.int64), 0)
    pair_m = np.tile(level, (1, GLA_HEADS)).astype(np.int32)
    return jnp.asarray(np.stack([lower, lower.T]), BF16), jnp.asarray(pair_m)


def _pick_tile(pref, *sizes):
    tm = pref
    while any(s % tm for s in sizes):
        tm //= 2
    return tm


def kernel(x_prompt, x_sample, c, c_ctx, cache_ckv, cache_krope, state_gla, ada_w, ada_b, norm_g,
           ffn1_wg, ffn1_wu, ffn1_wd, ffn2_wg, ffn2_wu, ffn2_wd, even_w_in, even_w_out,
           gla_gate_w2, gla_gate_b, gla_norm_g, mla_qa_g, mla_qb_w, mla_kva_g, mla_kvb_w,
           mla_qn_g, mla_kn_g, odd_w_in, odd_v_g, odd_ws, odd_bs, odd_w_out):
    batch, seq, d = x_prompt.shape
    dec_batch, dec_seq, _ = x_sample.shape
    depth = ada_w.shape[0]
    n_even = even_w_in.shape[0]
    m_ctx, m_dec = batch * seq, dec_batch * dec_seq
    assert 1 + dec_batch <= COND_PAD and seq % CMLP_CHUNK == 0 and dec_seq % CMLP_CHUNK == 0

    rows_ffn = _Rows(m_ctx, dec_batch, dec_seq, _pick_tile(512, m_ctx, dec_seq))
    rows_mix = _Rows(m_ctx, dec_batch, dec_seq, _pick_tile(512, m_ctx, dec_seq))

    cond = jnp.concatenate([c_ctx[None, :], c, jnp.zeros((COND_PAD - 1 - dec_batch, d), F32)], axis=0)
    mod = _modulation_all(cond, ada_w, ada_b).reshape(depth, COND_PAD, N_MOD, d)

    cos_t, sin_t = _rope_tables(dec_seq, rows_mix.tm)
    tri = _tri_masks()
    krope_blk = jnp.concatenate([jnp.zeros(cache_krope.shape[:-1] + (MLA_NOPE,), F32), cache_krope,
                                 cache_krope], axis=-1)
    even_wts = _pack_even_weights(cos_t, sin_t, even_w_in, gla_gate_w2, gla_gate_b, mla_qa_g, mla_qb_w,
                                  mla_qn_g, mla_kva_g, mla_kvb_w, mla_kn_g)
    kc, vc = _ctx_kv(cache_ckv, krope_blk, even_wts["kvb_k"], even_wts["kvb_v"], even_wts["kn_g_ctx"])

    ffn1 = (ffn1_wg, ffn1_wu, ffn1_wd)
    ffn2 = (ffn2_wg, ffn2_wu, ffn2_wd)
    w_out_bf = even_w_out.astype(BF16)
    odd_wts = (odd_w_in, odd_v_g[:, None, :], odd_ws.astype(BF16), jnp.swapaxes(odd_bs, 1, 2), odd_w_out)
    norms = norm_g.reshape(depth * 3, 1, d)
    gg = gla_norm_g[:, None, :]

    x = (x_prompt.reshape(m_ctx, d), x_sample.reshape(m_dec, d))
    new_ckv, new_krope, new_gla = [], [], []
    for i in range(depth):
        j = i // 2
        x = _ffn(x, mod, i, 0, (norms, 3 * i), *ffn1, rows_ffn)
        g_mix = (norms, 3 * i + 1)
        premix = None
        if i % 2 == 0:
            gq, gk, gv, og, la, qm, km, vm, ckv, kr = _even_in(x, mod, i, g_mix, even_wts, j, rows_mix)
            gla_ctx, st = _gla(gq, gk, gv, la, og, None, j, tri, gg, batch, seq, 0)
            gla_dec, _ = _gla(gq, gk, gv, la, og, state_gla, j, tri, gg, dec_batch, dec_seq, m_ctx)
            mla_ctx = _mla(qm, km, vm, None, None, j, batch, seq, 0, _pick_tile(256, seq))
            mla_dec = _mla(qm, km, vm, kc, vc, j, dec_batch, dec_seq, m_ctx, _pick_tile(512, dec_seq))
            premix = ((gla_ctx, gla_dec), (mla_ctx, mla_dec), w_out_bf, j)
            new_ckv.append(ckv.reshape(batch, seq, MLA_KV_RANK))
            new_krope.append(kr.reshape(batch, seq, MLA_ROPE))
            new_gla.append(st)
        else:
            x = _odd_mixer(x, mod, i, j, g_mix, *odd_wts, rows_mix)
        x = _ffn(x, mod, i, 6, (norms, 3 * i + 2), *ffn2, rows_ffn, split_out=(i == depth - 1),
                 premix=premix)

    y_prompt, y_sample = x
    return (y_prompt.reshape(batch, seq, d), y_sample.reshape(dec_batch, dec_seq, d),
            jnp.stack(new_ckv, axis=1), jnp.stack(new_krope, axis=1), jnp.stack(new_gla, axis=1))
```

```python
import functools
import math

import numpy as np
import jax
import jax.numpy as jnp
from jax import lax
from jax.experimental import pallas as pl
from jax.experimental.pallas import tpu as pltpu

F32 = jnp.float32
BF16 = jnp.bfloat16

EPS = 1e-6
N_MOD = 9
GRID_W = 64
ROPE_BASE = 10000.0
GLA_HEADS = 4
GLA_DK = 64
GLA_DV = 128
GLA_QK = GLA_HEADS * GLA_DK
GLA_VW = GLA_HEADS * GLA_DV
GLA_GATE_RANK = 16
GLA_GATE_NORM = 16.0
GLA_CHUNK = 64
GLA_CHUNKS_PER_STEP = 4
MLA_HEADS = 8
MLA_NOPE = 64
MLA_ROPE = 32
MLA_V = 64
MLA_QK_DIM = MLA_NOPE + MLA_ROPE
MLA_Q_RANK = 384
MLA_KV_RANK = 256
HEAD_PAD = 128
CMLP_CHUNK = 128
CMLP_GROUPS = 4
COND_PAD = 16

VMEM_LIMIT = 56 * 1024 * 1024
ROW_TILE = 512


def _cparams(*sem):
    return pltpu.CompilerParams(dimension_semantics=sem, vmem_limit_bytes=VMEM_LIMIT)


def _resident(shape, index_map):
    return pl.BlockSpec(shape, index_map, pipeline_mode=pl.Buffered(1))


def _const_map(nd):
    return lambda *_: (0,) * nd


def _layer_spec(stack, layer):
    nd = stack.ndim - 1
    return _resident((None,) + stack.shape[1:], lambda *_: (layer,) + (0,) * nd)


def _silu(x):
    return x * jax.nn.sigmoid(x)


def _rms(x, g):
    return x * lax.rsqrt(jnp.mean(x * x, axis=-1, keepdims=True) + EPS) * g


def _modulate(x, g, shift, scale):
    return _rms(x, g) * (1.0 + scale) + shift


def _dot(a, b):
    return jnp.dot(a, b, preferred_element_type=F32)


def _dot_nt(a, b):
    return lax.dot_general(a, b, (((1,), (1,)), ((), ())), preferred_element_type=F32)


def _mod_kernel(c_ref, w_ref, b_ref, o_ref):
    s = _silu(c_ref[...]).astype(BF16)
    o_ref[...] = _dot(s, w_ref[...].astype(BF16)) + b_ref[...]


def _modulation_all(cond, ada_w, ada_b):
    depth, d, n = ada_w.shape
    tn = n // 4
    return pl.pallas_call(
        _mod_kernel,
        out_shape=jax.ShapeDtypeStruct((depth, COND_PAD, n), F32),
        grid=(depth, n // tn),
        in_specs=[
            pl.BlockSpec((COND_PAD, d), lambda l, j: (0, 0)),
            pl.BlockSpec((None, d, tn), lambda l, j: (l, 0, j)),
            pl.BlockSpec((None, 1, tn), lambda l, j: (l, 0, j)),
        ],
        out_specs=pl.BlockSpec((None, COND_PAD, tn), lambda l, j: (l, 0, j)),
        compiler_params=_cparams("arbitrary", "arbitrary"),
        name="adaln_modulation",
    )(cond, ada_w, ada_b.reshape(depth, 1, n))


class _Rows:
    def __init__(self, m_ctx, dec_batch, dec_seq, tm):
        assert m_ctx % tm == 0 and dec_seq % tm == 0
        self.tm = tm
        self.m_ctx = m_ctx
        self.m = m_ctx + dec_batch * dec_seq
        self.nb_ctx = m_ctx // tm
        self.nb_seq = dec_seq // tm
        self.nblocks = self.m // tm

    def cond(self, i):
        return jnp.where(i < self.nb_ctx, 0, 1 + (i - self.nb_ctx) // self.nb_seq)

    def rope_block(self, i):
        return jnp.where(i < self.nb_ctx, 0, 1 + (i - self.nb_ctx) % self.nb_seq)

    def mod_spec(self, layer, d):
        return pl.BlockSpec((None, None, N_MOD, d), lambda i: (layer, self.cond(i), 0, 0))

    def row_spec(self, width):
        return pl.BlockSpec((self.tm, width), lambda i: (i, 0))

    def ctx_spec(self, width):
        return pl.BlockSpec((self.tm, width), lambda i: (jnp.minimum(i, self.nb_ctx - 1), 0))

    def dec_spec(self, width):
        return pl.BlockSpec((self.tm, width), lambda i: (jnp.maximum(i - self.nb_ctx, 0), 0))

    def split_specs(self, width):
        return [self.ctx_spec(width), self.dec_spec(width)]

    def split_shapes(self, width, dtype):
        return [jax.ShapeDtypeStruct((self.m_ctx, width), dtype),
                jax.ShapeDtypeStruct((self.m - self.m_ctx, width), dtype)]


def _load_split(refs, is_ctx):
    if len(refs) == 1:
        return refs[0][...]
    return jnp.where(is_ctx, refs[0][...], refs[1][...])


STAGE_CHUNKS = 16
STAGE_SLOTS = 8


def _stage_weights_bf16(layer, hbm_refs, vmem_refs, stage_refs, sem):
    chunks = []
    for src, dst, stage in zip(hbm_refs, vmem_refs, stage_refs):
        rows = stage.shape[1]
        assert dst.shape[0] % rows == 0
        chunks += [(src, dst, stage, r0) for r0 in range(0, dst.shape[0], rows)]
    slots = sem.shape[0]
    ahead = slots - 1

    def copy(i):
        src, _, stage, r0 = chunks[i]
        return pltpu.make_async_copy(src.at[layer, pl.ds(r0, stage.shape[1]), :], stage.at[i % slots],
                                     sem.at[i % slots])

    for i in range(min(ahead, len(chunks))):
        copy(i).start()
    for i, (_, dst, stage, r0) in enumerate(chunks):
        if i + ahead < len(chunks):
            copy(i + ahead).start()
        copy(i).wait()
        dst[pl.ds(r0, stage.shape[1]), :] = stage[i % slots].astype(BF16)


def _ffn_kernel(*refs, layer, row0, nb_ctx, split_in, split_out, premix):
    it = iter(refs)
    x_refs = [next(it) for _ in range(2 if split_in else 1)]
    mod_ref, g_ref = next(it), next(it)
    w_hbm = [next(it) for _ in range(3)]
    if premix:
        a_refs, b_refs = [next(it), next(it)], [next(it), next(it)]
        wo_ref = next(it)
    o_refs = [next(it) for _ in range(2 if split_out else 1)]
    wg_ref, wu_ref, wd_ref = w_vmem = [next(it) for _ in range(3)]
    stage_in, stage_down, sem = next(it), next(it), next(it)

    @pl.when(pl.program_id(0) == 0)
    def _():
        _stage_weights_bf16(layer, w_hbm, w_vmem, [stage_in, stage_in, stage_down], sem)

    is_ctx = pl.program_id(0) < nb_ctx
    shift = mod_ref[row0:row0 + 1, :]
    scale = mod_ref[row0 + 1:row0 + 2, :]
    gate = mod_ref[row0 + 2:row0 + 3, :]
    x = _load_split(x_refs, is_ctx)
    if premix:
        half = wo_ref.shape[0] // 2
        mix = _dot(_load_split(a_refs, is_ctx), wo_ref[:half, :])
        mix = mix + _dot(_load_split(b_refs, is_ctx), wo_ref[half:, :])
        x = x + mod_ref[5:6, :] * mix
    h = _modulate(x, g_ref[...], shift, scale).astype(BF16)
    a = _dot(h, wg_ref[...])
    u = _dot(h, wu_ref[...])
    act = (_silu(a) * u).astype(BF16)
    y = _dot(act, wd_ref[...])
    out = x + (0.5 * gate) * y
    if split_out:
        @pl.when(is_ctx)
        def _():
            o_refs[0][...] = out

        @pl.when(jnp.logical_not(is_ctx))
        def _():
            o_refs[1][...] = out
    else:
        o_refs[0][...] = out


def _ffn(xs, mod, layer, row0, g, wg, wu, wd, rows, split_out=False, premix=None):
    split_in = isinstance(xs, (tuple, list))
    xs = list(xs) if split_in else [xs]
    d = xs[0].shape[1]
    f = wg.shape[2]
    hbm = pl.BlockSpec(memory_space=pl.ANY)
    in_specs = (rows.split_specs(d) if split_in else [rows.row_spec(d)]) + [
        rows.mod_spec(layer, d),
        _layer_spec(*g),
        hbm, hbm, hbm,
    ]
    args = xs + [mod, g[0], wg, wu, wd]
    if premix is not None:
        gla, mla, w_out, j = premix
        in_specs += rows.split_specs(gla[0].shape[1]) + rows.split_specs(mla[0].shape[1])
        in_specs.append(_layer_spec(w_out, j))
        args += list(gla) + list(mla) + [w_out]
    return pl.pallas_call(
        functools.partial(_ffn_kernel, layer=layer, row0=row0, nb_ctx=rows.nb_ctx, split_in=split_in,
                          split_out=split_out, premix=premix is not None),
        out_shape=rows.split_shapes(d, F32) if split_out else jax.ShapeDtypeStruct((rows.m, d), F32),
        grid=(rows.nblocks,),
        in_specs=in_specs,
        out_specs=rows.split_specs(d) if split_out else rows.row_spec(d),
        scratch_shapes=[
            pltpu.VMEM((d, f), BF16), pltpu.VMEM((d, f), BF16), pltpu.VMEM((f, d), BF16),
            pltpu.VMEM((STAGE_SLOTS, d // STAGE_CHUNKS, f), F32),
            pltpu.VMEM((STAGE_SLOTS, f // STAGE_CHUNKS, d), F32),
            pltpu.SemaphoreType.DMA((STAGE_SLOTS,)),
        ],
        compiler_params=_cparams("arbitrary"),
        name="ffn_swiglu",
    )(*args)


def _gelu_tanh(x):
    c = math.sqrt(2.0 / math.pi)
    return x * (0.5 * (1.0 + jnp.tanh(c * (x + 0.044715 * (x * x * x)))))


def _odd_kernel(x_ref, mod_ref, g_ref, win_hbm, vg_ref, ws_ref, bst_ref, wout_hbm, o_ref,
                win_ref, wout_ref, stage_in, stage_out, sem, *, layer_j):
    @pl.when(pl.program_id(0) == 0)
    def _():
        _stage_weights_bf16(layer_j, [win_hbm, wout_hbm], [win_ref, wout_ref], [stage_in, stage_out], sem)

    x = x_ref[...]
    tm = x.shape[0]
    shift, scale, gate = mod_ref[3:4, :], mod_ref[4:5, :], mod_ref[5:6, :]
    h = _modulate(x, g_ref[...], shift, scale).astype(BF16)
    uv = _gelu_tanh(_dot(h, win_ref[...]))
    width = uv.shape[1] // 2
    u = uv[:, :width]
    v = _rms(uv[:, width:], vg_ref[...]).astype(BF16)
    gw = width // CMLP_GROUPS
    chunks = []
    for c in range(tm // CMLP_CHUNK):
        r0 = c * CMLP_CHUNK
        groups = []
        for g in range(CMLP_GROUPS):
            vg = v[r0:r0 + CMLP_CHUNK, g * gw:(g + 1) * gw]
            mixed = _dot(ws_ref[g], vg) + bst_ref[:, g:g + 1]
            groups.append(u[r0:r0 + CMLP_CHUNK, g * gw:(g + 1) * gw] * mixed)
        chunks.append(jnp.concatenate(groups, axis=1))
    z = jnp.concatenate(chunks, axis=0).astype(BF16)
    o_ref[...] = x + gate * _dot(z, wout_ref[...])


def _odd_mixer(x, mod, layer, layer_j, g, w_in, v_g, w_s, b_s_t, w_out, rows):
    m, d = x.shape
    n_in, width = w_in.shape[2], w_out.shape[1]
    hbm = pl.BlockSpec(memory_space=pl.ANY)
    return pl.pallas_call(
        functools.partial(_odd_kernel, layer_j=layer_j),
        out_shape=jax.ShapeDtypeStruct((m, d), F32),
        grid=(rows.nblocks,),
        in_specs=[
            rows.row_spec(d),
            rows.mod_spec(layer, d),
            _layer_spec(*g),
            hbm, _layer_spec(v_g, layer_j), _layer_spec(w_s, layer_j), _layer_spec(b_s_t, layer_j), hbm,
        ],
        out_specs=rows.row_spec(d),
        scratch_shapes=[
            pltpu.VMEM((d, n_in), BF16), pltpu.VMEM((width, d), BF16),
            pltpu.VMEM((STAGE_SLOTS, d // STAGE_CHUNKS, n_in), F32),
            pltpu.VMEM((STAGE_SLOTS, width // STAGE_CHUNKS, d), F32),
            pltpu.SemaphoreType.DMA((STAGE_SLOTS,)),
        ],
        compiler_params=_cparams("arbitrary"),
        name="odd_gmlp",
    )(x, mod, g[0], w_in, v_g, w_s, b_s_t, w_out)


_C_CQ = 0
_C_CKV = _C_CQ + MLA_Q_RANK
_C_KA = _C_CKV + MLA_KV_RANK
_C_KB = _C_KA + HEAD_PAD
_C_Q = _C_KB + HEAD_PAD
_C_K = _C_Q + GLA_QK
_C_V = _C_K + GLA_QK
_C_OG = _C_V + GLA_VW
_C_END = _C_OG + GLA_VW


def _head_scale(x, live):
    ss = jnp.sum(jnp.where(live, x * x, 0.0), axis=-1, keepdims=True)
    return lax.rsqrt(ss * (1.0 / MLA_QK_DIM) + EPS)


def _even_in_kernel(x_ref, mod_ref, g_ref, win_ref, w2_ref, gb_ref, qag_ref, qb_ref, kvag_ref,
                    kvbk_ref, kvbv_ref, kng_ref, tq_ref, ta_ref, tb_ref,
                    gq_ref, gk_ref, gv_ref, og_ref, la_ref, qm_ref, km_ref, vm_ref, ckv_ref, kr_ref,
                    *, nb_ctx):
    is_ctx = pl.program_id(0) < nb_ctx
    shift, scale = mod_ref[3:4, :], mod_ref[4:5, :]
    h = _modulate(x_ref[...], g_ref[...], shift, scale).astype(BF16)

    proj_q = _dot(h, win_ref[:, _C_CQ:_C_CKV])
    proj_kv = _dot(h, win_ref[:, _C_CKV:_C_Q])
    ckv = _rms(proj_kv[:, :MLA_KV_RANK], kvag_ref[...])
    ka = proj_kv[:, _C_KA - _C_CKV:_C_KB - _C_CKV]
    kb = proj_kv[:, _C_KB - _C_CKV:]

    @pl.when(is_ctx)
    def _():
        ckv_ref[...] = ckv
        kr_ref[...] = ka[:, MLA_NOPE:MLA_QK_DIM]

    cq = _rms(proj_q, qag_ref[...]).astype(BF16)
    ckv_bf = ckv.astype(BF16)
    qn = _dot(cq, qb_ref[...])
    kn = _dot(ckv_bf, kvbk_ref[...])
    vm_t = _dot(ckv_bf, kvbv_ref[...])
    z = _dot(ka.astype(BF16), w2_ref[...]) + gb_ref[...]

    lane = lax.broadcasted_iota(jnp.int32, ka.shape, 1)
    live = lane < MLA_QK_DIM
    tq = tq_ref[...]

    def q_heads(heads):
        for hd in heads:
            sl = slice(hd * HEAD_PAD, (hd + 1) * HEAD_PAD)
            qh = qn[:, sl]
            qm_ref[:, sl] = (qh * tq * _head_scale(qh, live)).astype(BF16)

    is_rope = live & (lane >= MLA_NOPE)
    ss_rope = jnp.sum(jnp.where(is_rope, ka * ka, 0.0), axis=-1, keepdims=True)
    rot = ka * ta_ref[...] + kb * tb_ref[...]
    kng = kng_ref[...]

    def k_heads(heads):
        for hd in heads:
            sl = slice(hd * HEAD_PAD, (hd + 1) * HEAD_PAD)
            kh = kn[:, sl]
            ss = jnp.sum(kh * kh, axis=-1, keepdims=True) + ss_rope
            r = lax.rsqrt(ss * (1.0 / MLA_QK_DIM) + EPS)
            km_ref[:, sl] = ((kh * kng + rot) * r).astype(BF16)

    half = MLA_HEADS // 2
    gqk = _dot(h, win_ref[:, _C_Q:_C_V])
    q_heads(range(half))
    gq_ref[...] = gqk[:, :GLA_QK] * (GLA_DK ** -0.5)
    gk_ref[...] = gqk[:, GLA_QK:]
    gv = _dot(h, win_ref[:, _C_V:_C_OG])
    q_heads(range(half, MLA_HEADS))
    gv_ref[...] = gv.astype(BF16)
    og = _dot(h, win_ref[:, _C_OG:])
    k_heads(range(half))
    og_ref[...] = og
    vm_ref[...] = vm_t.T.astype(BF16)
    k_heads(range(half, MLA_HEADS))
    la_ref[...] = ((jnp.minimum(z, 0.0) - jnp.log(1.0 + jnp.exp(-jnp.abs(z))))
                   * (math.log2(math.e) / GLA_GATE_NORM))


def _even_in(x, mod, layer, g, wts, layer_j, rows):
    m, d = x.shape
    tm = rows.tm
    rope_spec = pl.BlockSpec((None, tm, HEAD_PAD), lambda i: (layer_j, rows.rope_block(i), 0))
    widths = [GLA_QK, GLA_QK, GLA_VW, GLA_VW, 2 * GLA_QK, MLA_HEADS * HEAD_PAD, MLA_HEADS * HEAD_PAD]
    dtypes = [F32, F32, BF16, F32, F32, BF16, BF16]
    names = ["w_in", "w2", "gate_b", "qa_g", "qb", "kva_g", "kvb_k", "kvb_v", "kn_g"]
    return pl.pallas_call(
        functools.partial(_even_in_kernel, nb_ctx=rows.nb_ctx),
        out_shape=[jax.ShapeDtypeStruct((m, w), t) for w, t in zip(widths, dtypes)]
        + [jax.ShapeDtypeStruct((MLA_HEADS * MLA_V, m), BF16),
           jax.ShapeDtypeStruct((rows.m_ctx, MLA_KV_RANK), F32),
           jax.ShapeDtypeStruct((rows.m_ctx, MLA_ROPE), F32)],
        grid=(rows.nblocks,),
        in_specs=[rows.row_spec(d), rows.mod_spec(layer, d), _layer_spec(*g)]
        + [_layer_spec(wts[n], layer_j) for n in names] + [rope_spec] * 3,
        out_specs=[rows.row_spec(w) for w in widths]
        + [pl.BlockSpec((MLA_HEADS * MLA_V, tm), lambda i: (0, i)),
           rows.ctx_spec(MLA_KV_RANK), rows.ctx_spec(MLA_ROPE)],
        compiler_params=_cparams("arbitrary"),
        name="even_in_proj",
    )(x, mod, g[0], *[wts[n] for n in names], wts["t_q"], wts["t_a"], wts["t_b"])


def _ctx_kv_kernel(ckv_ref, kr_ref, kvbk_ref, kvbv_ref, kng_ref, k_ref, v_ref):
    ckv_bf = ckv_ref[...].astype(BF16)
    v_ref[...] = _dot(ckv_bf, kvbv_ref[...]).T.astype(BF16)
    kn = _dot(ckv_bf, kvbk_ref[...])
    kr = kr_ref[...]
    lane = lax.broadcasted_iota(jnp.int32, kr.shape, 1)
    live = lane < MLA_QK_DIM
    kng = kng_ref[...]
    for hd in range(MLA_HEADS):
        sl = slice(hd * HEAD_PAD, (hd + 1) * HEAD_PAD)
        kh = kn[:, sl] + kr
        k_ref[:, sl] = (kh * kng * _head_scale(kh, live)).astype(BF16)


def _ctx_kv(cache_ckv, cache_krope_blk, kvb_k, kvb_v, kn_g):
    nb, ne, p, r = cache_ckv.shape
    return pl.pallas_call(
        _ctx_kv_kernel,
        out_shape=[jax.ShapeDtypeStruct((ne, nb, p, MLA_HEADS * HEAD_PAD), BF16),
                   jax.ShapeDtypeStruct((ne, nb, MLA_HEADS * MLA_V, p), BF16)],
        grid=(ne, nb),
        in_specs=[
            pl.BlockSpec((None, None, p, r), lambda j, b: (b, j, 0, 0)),
            pl.BlockSpec((None, None, p, HEAD_PAD), lambda j, b: (b, j, 0, 0)),
            pl.BlockSpec((None,) + kvb_k.shape[1:], lambda j, b: (j, 0, 0)),
            pl.BlockSpec((None,) + kvb_v.shape[1:], lambda j, b: (j, 0, 0)),
            pl.BlockSpec((None,) + kn_g.shape[1:], lambda j, b: (j, 0, 0)),
        ],
        out_specs=[pl.BlockSpec((None, None, p, MLA_HEADS * HEAD_PAD), lambda j, b: (j, b, 0, 0)),
                   pl.BlockSpec((None, None, MLA_HEADS * MLA_V, p), lambda j, b: (j, b, 0, 0))],
        compiler_params=_cparams("arbitrary", "arbitrary"),
        name="ctx_kv",
    )(cache_ckv, cache_krope_blk, kvb_k, kvb_v, kn_g)


def _split3(x):
    hi = x.astype(BF16)
    r = x - hi.astype(F32)
    mid = r.astype(BF16)
    lo = (r - mid.astype(F32)).astype(BF16)
    return hi, mid, lo


def _block_row_bcast(a, blk, off):
    c, n = a.shape
    if blk >= 8:
        pieces = [jnp.broadcast_to(a[b * blk + off:b * blk + off + 1, :], (blk, n))
                  for b in range(c // blk)]
        return pieces[0] if len(pieces) == 1 else jnp.concatenate(pieces, axis=0)
    a3 = a.reshape(c // 8, 8, n)
    sub = lax.broadcasted_iota(jnp.int32, a3.shape, 1) // blk
    out = jnp.broadcast_to(a3[:, off:off + 1, :], a3.shape)
    for s in range(1, 8 // blk):
        cand = jnp.broadcast_to(a3[:, s * blk + off:s * blk + off + 1, :], a3.shape)
        out = jnp.where(sub == s, cand, out)
    return out.reshape(c, n)


def _head_stack(a, head_w):
    lane_head = lax.broadcasted_iota(jnp.int32, a.shape, 1) // head_w
    zero = jnp.zeros_like(a)
    return jnp.concatenate([jnp.where(lane_head == h, a, zero) for h in range(GLA_HEADS)], axis=0)


def _run_skewed(stage_gens, skew=1):
    done = [False] * len(stage_gens)
    tick = 0
    while not all(done):
        for u, gen in enumerate(stage_gens):
            if tick >= u * skew and not done[u]:
                try:
                    next(gen)
                except StopIteration:
                    done[u] = True
        tick += 1


def _gla_phase_a(c, q_ref, k_ref, v_ref, la_ref, tri_ref, pair_ref, oacc_ref, qst_ref, ds_ref, dc_ref):
    C = GLA_CHUNK
    r0 = pl.multiple_of(c * C, C)
    q = q_ref[pl.ds(r0, C), :]
    k = k_ref[pl.ds(r0, C), :]
    la = la_ref[pl.ds(r0, C), :]
    la_f, la_b = la[:, :GLA_QK], la[:, GLA_QK:]
    cums = []
    for d, la_d in enumerate((la_f, la_b)):
        hi, mid, lo = _split3(la_d)
        tri = tri_ref[d]
        cums.append(_dot(tri, hi) + _dot(tri, mid) + _dot(tri, lo))
    cum_f, cum_b = cums
    k_stack = _head_stack(k.astype(BF16), GLA_DK)
    yield

    row = lax.broadcasted_iota(jnp.int32, (C, 1), 0)
    pair_m = pair_ref[...]
    att = jnp.where(pair_m == 0, 2.0 * _dot_nt(q.astype(BF16), k_stack), 0.0)
    m = C // 2
    while m >= 1:
        blk = 2 * m
        upper = (row % blk) >= m
        if m == 1:
            qm = (q * jnp.exp2(jnp.where(upper, la_f, la_b))).astype(BF16)
            km_stack = k_stack
        else:
            ref_f = _block_row_bcast(cum_f, blk, m - 1)
            ref_b = _block_row_bcast(cum_b, blk, m)
            eq = jnp.where(upper, cum_f - ref_f, cum_b - ref_b)
            ek = jnp.where(upper, ref_b - cum_b, ref_f - cum_f)
            qm = (q * jnp.exp2(eq)).astype(BF16)
            km_stack = _head_stack((k * jnp.exp2(ek)).astype(BF16), GLA_DK)
        yield
        att = jnp.where(pair_m == m, _dot_nt(qm, km_stack), att)
        m //= 2

    v = v_ref[pl.ds(r0, C), :]
    v_stack = _head_stack(v, GLA_DV)
    qcs = [(q * jnp.exp2(cum)).astype(BF16) for cum in cums]
    yield
    oacc_ref[pl.ds(r0, C), :] = _dot(att.astype(BF16), v_stack)
    for d, cum in enumerate(cums):
        last = cum[C - 1:C, :] if d == 0 else cum[0:1, :]
        qst_ref[d, c] = _head_stack(qcs[d], GLA_DK)
        kct = (k * jnp.exp2(last - cum)).T.astype(BF16)
        decay_col = jnp.exp2(jnp.broadcast_to(last, (8, GLA_QK))).T[:, 0:1]
        dc_ref[d, c] = jnp.broadcast_to(decay_col, (GLA_QK, GLA_DV))
        yield
        ds_ref[d, c] = jnp.concatenate(
            [_dot(kct[h * GLA_DK:(h + 1) * GLA_DK, :], v[:, h * GLA_DV:(h + 1) * GLA_DV])
             for h in range(GLA_HEADS)], axis=0)


def _gla_phase_c(c, og_ref, gg, oacc_ref, qst_ref, sb_ref, o_ref):
    C = GLA_CHUNK
    r0 = pl.multiple_of(c * C, C)
    q_cat = jnp.concatenate([qst_ref[0, c], qst_ref[1, c]], axis=1)
    s_cat = jnp.concatenate([sb_ref[0, c], sb_ref[1, c]], axis=0)
    inter = _dot(q_cat, s_cat)
    yield
    for h in range(GLA_HEADS):
        sl = slice(h * GLA_DV, (h + 1) * GLA_DV)
        o = oacc_ref[pl.ds(r0, C), sl] + inter[h * C:(h + 1) * C, :]
        o_ref[pl.ds(r0, C), sl] = (_rms(o, gg) * _silu(og_ref[pl.ds(r0, C), sl])).astype(BF16)


def _gla_kernel(*refs, has_state):
    it = iter(refs)
    q_ref, k_ref, v_ref, la_ref, og_ref = (next(it) for _ in range(5))
    s0_ref = next(it) if has_state else None
    tri_ref, pair_ref, gg_ref, o_ref, sout_ref = (next(it) for _ in range(5))
    oacc_ref, qst_ref, ds_ref, dc_ref, sb_ref, st_ref = (next(it) for _ in range(6))

    nc = q_ref.shape[0] // GLA_CHUNK
    per_step = GLA_CHUNKS_PER_STEP

    def phase_a(i, carry):
        _run_skewed([_gla_phase_a(per_step * i + u, q_ref, k_ref, v_ref, la_ref, tri_ref, pair_ref, oacc_ref,
                                  qst_ref, ds_ref, dc_ref) for u in range(per_step)])
        return carry

    lax.fori_loop(0, nc // per_step, phase_a, 0)

    for d in range(2):
        for h in range(GLA_HEADS):
            rows = slice(h * GLA_DK, (h + 1) * GLA_DK)
            st_ref[d, rows, :] = s0_ref[d, h] if has_state else jnp.zeros((GLA_DK, GLA_DV), F32)

    def phase_b(c, carry):
        for d in range(2):
            cc = c if d == 0 else nc - 1 - c
            s = st_ref[d]
            sb_ref[d, cc] = s.astype(BF16)
            st_ref[d] = s * dc_ref[d, cc] + ds_ref[d, cc]
        return carry

    lax.fori_loop(0, nc, phase_b, 0)
    for d in range(2):
        for h in range(GLA_HEADS):
            sout_ref[d, h] = st_ref[d, h * GLA_DK:(h + 1) * GLA_DK, :]

    gg = gg_ref[...]

    def phase_c(i, carry):
        _run_skewed([_gla_phase_c(per_step * i + u, og_ref, gg, oacc_ref, qst_ref, sb_ref, o_ref)
                     for u in range(per_step)])
        return carry

    lax.fori_loop(0, nc // per_step, phase_c, 0)


def _gla(gq, gk, gv, la, og, state, layer_j, tri, gg, nbatch, t, row_off):
    assert t % (GLA_CHUNKS_PER_STEP * GLA_CHUNK) == 0
    b0 = row_off // t
    nc = t // GLA_CHUNK
    seq = lambda w: pl.BlockSpec((t, w), lambda b: (b0 + b, 0))
    in_specs = [seq(GLA_QK), seq(GLA_QK), seq(GLA_VW), seq(2 * GLA_QK), seq(GLA_VW)]
    args = [gq, gk, gv, la, og]
    if state is not None:
        in_specs.append(pl.BlockSpec((None, None, 2, GLA_HEADS, GLA_DK, GLA_DV),
                                     lambda b: (b, layer_j, 0, 0, 0, 0)))
        args.append(state)
    tri, pair_m = tri
    in_specs += [_resident(tri.shape, _const_map(3)), _resident(pair_m.shape, _const_map(2)),
                 _layer_spec(gg, layer_j)]
    args += [tri, pair_m, gg]
    return pl.pallas_call(
        functools.partial(_gla_kernel, has_state=state is not None),
        out_shape=[jax.ShapeDtypeStruct((nbatch * t, GLA_VW), BF16),
                   jax.ShapeDtypeStruct((nbatch, 2, GLA_HEADS, GLA_DK, GLA_DV), F32)],
        grid=(nbatch,),
        in_specs=in_specs,
        out_specs=[pl.BlockSpec((t, GLA_VW), lambda b: (b, 0)),
                   pl.BlockSpec((None, 2, GLA_HEADS, GLA_DK, GLA_DV), lambda b: (b, 0, 0, 0, 0))],
        scratch_shapes=[
            pltpu.VMEM((t, GLA_VW), F32),
            pltpu.VMEM((2, nc, GLA_HEADS * GLA_CHUNK, GLA_QK), BF16),
            pltpu.VMEM((2, nc, GLA_QK, GLA_DV), F32),
            pltpu.VMEM((2, nc, GLA_QK, GLA_DV), F32),
            pltpu.VMEM((2, nc, GLA_QK, GLA_DV), BF16),
            pltpu.VMEM((2, GLA_QK, GLA_DV), F32),
        ],
        compiler_params=_cparams("arbitrary"),
        name="gla_scan",
    )(*args)


def _mla_kernel(*refs, has_ctx):
    it = iter(refs)
    q_ref, k_ref, v_ref = next(it), next(it), next(it)
    kc_ref, vc_ref = (next(it), next(it)) if has_ctx else (None, None)
    o_ref = next(it)
    def scores(hd):
        sl = slice(hd * HEAD_PAD, (hd + 1) * HEAD_PAD)
        q = q_ref[:, sl]
        return [_dot_nt(r[:, sl], q) for r in ((k_ref, kc_ref) if has_ctx else (k_ref,))]

    def softmax(ss):
        mx = functools.reduce(jnp.maximum, [jnp.max(s, axis=0, keepdims=True) for s in ss])
        ps = [jnp.exp2(s - mx) for s in ss]
        den = sum(jnp.sum(p, axis=0, keepdims=True) for p in ps)
        return [p.astype(BF16) for p in ps], den

    def values(hd, ps, den):
        vrows = slice(hd * MLA_V, (hd + 1) * MLA_V)
        o = sum(_dot(r[vrows, :], p) for r, p in zip((v_ref, vc_ref), ps))
        return o / den

    s_q, p_q, outs = {}, {}, []
    for step in range(MLA_HEADS + 2):
        if step < MLA_HEADS:
            s_q[step] = scores(step)
        if 0 <= step - 1 < MLA_HEADS:
            p_q[step - 1] = softmax(s_q.pop(step - 1))
        if 0 <= step - 2 < MLA_HEADS:
            outs.append(values(step - 2, *p_q.pop(step - 2)))
    o_ref[...] = jnp.concatenate(outs, axis=0).T.astype(BF16)


def _mla(qm, km, vm, kc, vc, layer_j, nbatch, t, row_off, tq):
    nq = t // tq
    q0, b0 = row_off // tq, row_off // t
    qspec = lambda w: pl.BlockSpec((tq, w), lambda b, i: (q0 + b * nq + i, 0))
    kspec = lambda w: pl.BlockSpec((t, w), lambda b, i: (b0 + b, 0))
    vspec = pl.BlockSpec((MLA_HEADS * MLA_V, t), lambda b, i: (0, b0 + b))
    in_specs = [qspec(MLA_HEADS * HEAD_PAD), kspec(MLA_HEADS * HEAD_PAD), vspec]
    args = [qm, km, vm]
    if kc is not None:
        in_specs += [pl.BlockSpec((None, None) + kc.shape[2:], lambda b, i: (layer_j, b, 0, 0)),
                     pl.BlockSpec((None, None) + vc.shape[2:], lambda b, i: (layer_j, b, 0, 0))]
        args += [kc, vc]
    return pl.pallas_call(
        functools.partial(_mla_kernel, has_ctx=kc is not None),
        out_shape=jax.ShapeDtypeStruct((nbatch * t, MLA_HEADS * MLA_V), BF16),
        grid=(nbatch, nq),
        in_specs=in_specs,
        out_specs=pl.BlockSpec((tq, MLA_HEADS * MLA_V), lambda b, i: (b * nq + i, 0)),
        compiler_params=_cparams("arbitrary", "arbitrary"),
        name="mla_attention",
    )(*args)


def _swap_signed(a):
    pairs = a.reshape(a.shape[:-1] + (a.shape[-1] // 2, 2))
    return jnp.stack([-pairs[..., 1], pairs[..., 0]], axis=-1).reshape(a.shape)


def _swap_pairs(a):
    pairs = a.reshape(a.shape[:-1] + (a.shape[-1] // 2, 2))
    return pairs[..., ::-1].reshape(a.shape)


def _pack_even_weights(cos, sin, even_w_in, gla_gate_w2, gla_gate_b, mla_qa_g, mla_qb_w, mla_qn_g,
                       mla_kva_g, mla_kvb_w, mla_kn_g):
    ne, d, _ = even_w_in.shape
    o = 0
    parts = {}
    for name, width in (("q", GLA_QK), ("k", GLA_QK), ("v", GLA_VW), ("og", GLA_VW),
                        ("gl", 2 * GLA_GATE_RANK), ("cq", MLA_Q_RANK), ("ckv", MLA_KV_RANK),
                        ("kr", MLA_ROPE)):
        parts[name] = even_w_in[:, :, o:o + width]
        o += width
    zeros = lambda n: jnp.zeros((ne, d, n), even_w_in.dtype)
    kr, kr_sw = parts["kr"], _swap_signed(parts["kr"])
    blk_a = jnp.concatenate([parts["gl"], zeros(MLA_NOPE - 2 * GLA_GATE_RANK), kr, kr_sw], axis=2)
    blk_b = jnp.concatenate([zeros(MLA_NOPE), kr_sw, kr], axis=2)
    w_in = jnp.concatenate([parts["cq"], parts["ckv"], blk_a, blk_b, parts["q"], parts["k"], parts["v"],
                            parts["og"]], axis=2).astype(BF16)
    assert w_in.shape[2] == _C_END

    w2_f = jnp.pad(gla_gate_w2[:, 0], ((0, 0), (0, 0), (0, GLA_QK)))
    w2_b = jnp.pad(gla_gate_w2[:, 1], ((0, 0), (0, 0), (GLA_QK, 0)))
    w2 = jnp.pad(jnp.concatenate([w2_f, w2_b], axis=1), ((0, 0), (0, HEAD_PAD - 2 * GLA_GATE_RANK), (0, 0)))

    pad_head = lambda a: jnp.pad(a, [(0, 0)] * (a.ndim - 1) + [(0, HEAD_PAD - a.shape[-1])])
    qb = jnp.concatenate([mla_qb_w, _swap_signed(mla_qb_w[..., MLA_NOPE:])], axis=-1)
    qb = qb.reshape(ne, MLA_Q_RANK, MLA_HEADS * HEAD_PAD)
    kvb_k = pad_head(mla_kvb_w[..., :MLA_NOPE]).reshape(ne, MLA_KV_RANK, MLA_HEADS * HEAD_PAD)
    kvb_v = mla_kvb_w[..., MLA_NOPE:].reshape(ne, MLA_KV_RANK, MLA_HEADS * MLA_V)

    def rot_tables(g):
        n = cos.shape[0]
        g_n = jnp.broadcast_to(g[:, None, :MLA_NOPE], (ne, n, MLA_NOPE))
        g_c = g[:, None, MLA_NOPE:] * cos
        g_s = _swap_pairs(g[:, MLA_NOPE:])[:, None, :] * sin
        return g_n, g_c, g_s

    qn_n, qn_c, qn_s = rot_tables(mla_qn_g)
    kn_n, kn_c, kn_s = rot_tables(mla_kn_g)
    kzero = jnp.zeros_like(kn_n)
    return {
        "w_in": w_in,
        "w2": w2.astype(BF16),
        "gate_b": gla_gate_b.reshape(ne, 1, 2 * GLA_QK),
        "qa_g": mla_qa_g[:, None, :],
        "qb": qb.astype(BF16),
        "kva_g": mla_kva_g[:, None, :],
        "kvb_k": kvb_k.astype(BF16),
        "kvb_v": kvb_v.astype(BF16),
        "kn_g": pad_head(mla_kn_g[:, :MLA_NOPE])[:, None, :],
        "kn_g_ctx": jnp.concatenate([mla_kn_g, mla_kn_g[:, MLA_NOPE:]], axis=1)[:, None, :],
        "t_q": jnp.concatenate([qn_n, qn_c, qn_s], axis=2) * (MLA_QK_DIM ** -0.5 * math.log2(math.e)),
        "t_a": jnp.concatenate([kzero, kn_c, kn_s], axis=2),
        "t_b": jnp.concatenate([kzero, kn_s, kn_c], axis=2),
    }


def _rope_tables(n, tm):
    pairs = MLA_ROPE // 4
    pos = np.arange(n)
    inv = ROPE_BASE ** (-jnp.arange(pairs, dtype=F32) / pairs)
    row = jnp.asarray(pos // GRID_W, F32)
    col = jnp.asarray(pos % GRID_W, F32)
    ang = jnp.concatenate([row[:, None] * inv, col[:, None] * inv], axis=-1)
    cos = jnp.repeat(jnp.cos(ang), 2, axis=-1)
    sin = jnp.repeat(jnp.sin(ang), 2, axis=-1)
    cos = jnp.pad(cos, ((tm, 0), (0, 0)), constant_values=1.0)
    sin = jnp.pad(sin, ((tm, 0), (0, 0)))
    return cos, sin


def _tri_masks():
    r = np.arange(GLA_CHUNK)
    lower = (r[None, :] <= r[:, None]).astype(np.float32)
    diff = r[:, None] ^ r[None, :]
    level = np.where(diff > 0, 1 << np.floor(np.log2(np.maximum(diff, 1))).astype(np.int64), 0)
    pair_m = np.tile(level, (1, GLA_HEADS)).astype(np.int32)
    return jnp.asarray(np.stack([lower, lower.T]), BF16), jnp.asarray(pair_m)


def _pick_tile(pref, *sizes):
    tm = pref
    while any(s % tm for s in sizes):
        tm //= 2
    return tm


def kernel(x_prompt, x_sample, c, c_ctx, cache_ckv, cache_krope, state_gla, ada_w, ada_b, norm_g,
           ffn1_wg, ffn1_wu, ffn1_wd, ffn2_wg, ffn2_wu, ffn2_wd, even_w_in, even_w_out,
           gla_gate_w2, gla_gate_b, gla_norm_g, mla_qa_g, mla_qb_w, mla_kva_g, mla_kvb_w,
           mla_qn_g, mla_kn_g, odd_w_in, odd_v_g, odd_ws, odd_bs, odd_w_out):
    batch, seq, d = x_prompt.shape
    dec_batch, dec_seq, _ = x_sample.shape
    depth = ada_w.shape[0]
    n_even = even_w_in.shape[0]
    m_ctx, m_dec = batch * seq, dec_batch * dec_seq
    assert 1 + dec_batch <= COND_PAD and seq % CMLP_CHUNK == 0 and dec_seq % CMLP_CHUNK == 0

    rows = _Rows(m_ctx, dec_batch, dec_seq, _pick_tile(ROW_TILE, m_ctx, dec_seq))

    cond = jnp.concatenate([c_ctx[None, :], c, jnp.zeros((COND_PAD - 1 - dec_batch, d), F32)], axis=0)
    mod = _modulation_all(cond, ada_w, ada_b).reshape(depth, COND_PAD, N_MOD, d)

    cos_t, sin_t = _rope_tables(dec_seq, rows.tm)
    tri = _tri_masks()
    krope_blk = jnp.concatenate([jnp.zeros(cache_krope.shape[:-1] + (MLA_NOPE,), F32), cache_krope,
                                 cache_krope], axis=-1)
    even_wts = _pack_even_weights(cos_t, sin_t, even_w_in, gla_gate_w2, gla_gate_b, mla_qa_g, mla_qb_w,
                                  mla_qn_g, mla_kva_g, mla_kvb_w, mla_kn_g)
    kc, vc = _ctx_kv(cache_ckv, krope_blk, even_wts["kvb_k"], even_wts["kvb_v"], even_wts["kn_g_ctx"])

    ffn1 = (ffn1_wg, ffn1_wu, ffn1_wd)
    ffn2 = (ffn2_wg, ffn2_wu, ffn2_wd)
    w_out_bf = even_w_out.astype(BF16)
    odd_wts = (odd_w_in, odd_v_g[:, None, :], odd_ws.astype(BF16), jnp.swapaxes(odd_bs, 1, 2), odd_w_out)
    norms = norm_g.reshape(depth * 3, 1, d)
    gg = gla_norm_g[:, None, :]

    x = (x_prompt.reshape(m_ctx, d), x_sample.reshape(m_dec, d))
    new_ckv, new_krope, new_gla = [], [], []
    for i in range(depth):
        j = i // 2
        x = _ffn(x, mod, i, 0, (norms, 3 * i), *ffn1, rows)
        g_mix = (norms, 3 * i + 1)
        premix = None
        if i % 2 == 0:
            gq, gk, gv, og, la, qm, km, vm, ckv, kr = _even_in(x, mod, i, g_mix, even_wts, j, rows)
            gla_ctx, st = _gla(gq, gk, gv, la, og, None, j, tri, gg, batch, seq, 0)
            gla_dec, _ = _gla(gq, gk, gv, la, og, state_gla, j, tri, gg, dec_batch, dec_seq, m_ctx)
            mla_ctx = _mla(qm, km, vm, None, None, j, batch, seq, 0, _pick_tile(256, seq))
            mla_dec = _mla(qm, km, vm, kc, vc, j, dec_batch, dec_seq, m_ctx, _pick_tile(512, dec_seq))
            premix = ((gla_ctx, gla_dec), (mla_ctx, mla_dec), w_out_bf, j)
            new_ckv.append(ckv.reshape(batch, seq, MLA_KV_RANK))
            new_krope.append(kr.reshape(batch, seq, MLA_ROPE))
            new_gla.append(st)
        else:
            x = _odd_mixer(x, mod, i, j, g_mix, *odd_wts, rows)
        x = _ffn(x, mod, i, 6, (norms, 3 * i + 2), *ffn2, rows, split_out=(i == depth - 1),
                 premix=premix)

    y_prompt, y_sample = x
    return (y_prompt.reshape(batch, seq, d), y_sample.reshape(dec_batch, dec_seq, d),
            jnp.stack(new_ckv, axis=1), jnp.stack(new_krope, axis=1), jnp.stack(new_gla, axis=1))
```

```python
import functools
import math

import numpy as np
import jax
import jax.numpy as jnp
from jax import lax
from jax.experimental import pallas as pl
from jax.experimental.pallas import tpu as pltpu

F32 = jnp.float32
BF16 = jnp.bfloat16

EPS = 1e-6
N_MOD = 9
GRID_W = 64
ROPE_BASE = 10000.0
GLA_HEADS = 4
GLA_DK = 64
GLA_DV = 128
GLA_QK = GLA_HEADS * GLA_DK
GLA_VW = GLA_HEADS * GLA_DV
GLA_GATE_RANK = 16
GLA_GATE_NORM = 16.0
GLA_CHUNK = 64
GLA_CHUNKS_PER_STEP = 8
MLA_HEADS = 8
MLA_NOPE = 64
MLA_ROPE = 32
MLA_V = 64
MLA_QK_DIM = MLA_NOPE + MLA_ROPE
MLA_Q_RANK = 384
MLA_KV_RANK = 256
HEAD_PAD = 128
CMLP_CHUNK = 128
CMLP_GROUPS = 4
COND_PAD = 16

VMEM_LIMIT = 56 * 1024 * 1024
ROW_TILE = 512


def _cparams(*sem):
    return pltpu.CompilerParams(dimension_semantics=sem, vmem_limit_bytes=VMEM_LIMIT)


def _resident(shape, index_map):
    return pl.BlockSpec(shape, index_map, pipeline_mode=pl.Buffered(1))


def _const_map(nd):
    return lambda *_: (0,) * nd


def _layer_spec(stack, layer):
    nd = stack.ndim - 1
    return _resident((None,) + stack.shape[1:], lambda *_: (layer,) + (0,) * nd)


def _silu(x):
    return x * jax.nn.sigmoid(x)


def _rms(x, g):
    return x * lax.rsqrt(jnp.mean(x * x, axis=-1, keepdims=True) + EPS) * g


def _modulate(x, g, shift, scale):
    return _rms(x, g) * (1.0 + scale) + shift


def _dot(a, b):
    return jnp.dot(a, b, preferred_element_type=F32)


def _dot_nt(a, b):
    return lax.dot_general(a, b, (((1,), (1,)), ((), ())), preferred_element_type=F32)


def _mod_kernel(c_ref, w_ref, b_ref, o_ref):
    s = _silu(c_ref[...]).astype(BF16)
    o_ref[...] = _dot(s, w_ref[...].astype(BF16)) + b_ref[...]


def _modulation_all(cond, ada_w, ada_b):
    depth, d, n = ada_w.shape
    tn = n // 4
    return pl.pallas_call(
        _mod_kernel,
        out_shape=jax.ShapeDtypeStruct((depth, COND_PAD, n), F32),
        grid=(depth, n // tn),
        in_specs=[
            pl.BlockSpec((COND_PAD, d), lambda l, j: (0, 0)),
            pl.BlockSpec((None, d, tn), lambda l, j: (l, 0, j)),
            pl.BlockSpec((None, 1, tn), lambda l, j: (l, 0, j)),
        ],
        out_specs=pl.BlockSpec((None, COND_PAD, tn), lambda l, j: (l, 0, j)),
        compiler_params=_cparams("arbitrary", "arbitrary"),
        name="adaln_modulation",
    )(cond, ada_w, ada_b.reshape(depth, 1, n))


class _Rows:
    def __init__(self, m_ctx, dec_batch, dec_seq, tm):
        assert m_ctx % tm == 0 and dec_seq % tm == 0
        self.tm = tm
        self.m_ctx = m_ctx
        self.m = m_ctx + dec_batch * dec_seq
        self.nb_ctx = m_ctx // tm
        self.nb_seq = dec_seq // tm
        self.nblocks = self.m // tm

    def cond(self, i):
        return jnp.where(i < self.nb_ctx, 0, 1 + (i - self.nb_ctx) // self.nb_seq)

    def rope_block(self, i):
        return jnp.where(i < self.nb_ctx, 0, 1 + (i - self.nb_ctx) % self.nb_seq)

    def mod_spec(self, layer, d):
        return pl.BlockSpec((None, None, N_MOD, d), lambda i: (layer, self.cond(i), 0, 0))

    def row_spec(self, width):
        return pl.BlockSpec((self.tm, width), lambda i: (i, 0))

    def ctx_spec(self, width):
        return pl.BlockSpec((self.tm, width), lambda i: (jnp.minimum(i, self.nb_ctx - 1), 0))

    def dec_spec(self, width):
        return pl.BlockSpec((self.tm, width), lambda i: (jnp.maximum(i - self.nb_ctx, 0), 0))

    def split_specs(self, width):
        return [self.ctx_spec(width), self.dec_spec(width)]

    def split_shapes(self, width, dtype):
        return [jax.ShapeDtypeStruct((self.m_ctx, width), dtype),
                jax.ShapeDtypeStruct((self.m - self.m_ctx, width), dtype)]


def _load_split(refs, is_ctx):
    if len(refs) == 1:
        return refs[0][...]
    return jnp.where(is_ctx, refs[0][...], refs[1][...])


STAGE_CHUNKS = 16
STAGE_SLOTS = 8


def _stage_weights_bf16(layer, hbm_refs, vmem_refs, stage_refs, sem):
    chunks = []
    for src, dst, stage in zip(hbm_refs, vmem_refs, stage_refs):
        rows = stage.shape[1]
        assert dst.shape[0] % rows == 0
        chunks += [(src, dst, stage, r0) for r0 in range(0, dst.shape[0], rows)]
    slots = sem.shape[0]
    ahead = slots - 1

    def copy(i):
        src, _, stage, r0 = chunks[i]
        return pltpu.make_async_copy(src.at[layer, pl.ds(r0, stage.shape[1]), :], stage.at[i % slots],
                                     sem.at[i % slots])

    for i in range(min(ahead, len(chunks))):
        copy(i).start()
    for i, (_, dst, stage, r0) in enumerate(chunks):
        if i + ahead < len(chunks):
            copy(i + ahead).start()
        copy(i).wait()
        dst[pl.ds(r0, stage.shape[1]), :] = stage[i % slots].astype(BF16)


def _ffn_kernel(*refs, layer, row0, nb_ctx, split_in, split_out, premix):
    it = iter(refs)
    x_refs = [next(it) for _ in range(2 if split_in else 1)]
    mod_ref, g_ref = next(it), next(it)
    w_hbm = [next(it) for _ in range(3)]
    if premix:
        a_refs, b_refs = [next(it), next(it)], [next(it), next(it)]
        wo_ref = next(it)
    o_refs = [next(it) for _ in range(2 if split_out else 1)]
    wg_ref, wu_ref, wd_ref = w_vmem = [next(it) for _ in range(3)]
    stage_in, stage_down, sem = next(it), next(it), next(it)

    @pl.when(pl.program_id(0) == 0)
    def _():
        _stage_weights_bf16(layer, w_hbm, w_vmem, [stage_in, stage_in, stage_down], sem)

    is_ctx = pl.program_id(0) < nb_ctx
    shift = mod_ref[row0:row0 + 1, :]
    scale = mod_ref[row0 + 1:row0 + 2, :]
    gate = mod_ref[row0 + 2:row0 + 3, :]
    x = _load_split(x_refs, is_ctx)
    if premix:
        half = wo_ref.shape[0] // 2
        mix = _dot(_load_split(a_refs, is_ctx), wo_ref[:half, :])
        mix = mix + _dot(_load_split(b_refs, is_ctx), wo_ref[half:, :])
        x = x + mod_ref[5:6, :] * mix
    h = _modulate(x, g_ref[...], shift, scale).astype(BF16)
    a = _dot(h, wg_ref[...])
    u = _dot(h, wu_ref[...])
    act = (_silu(a) * u).astype(BF16)
    y = _dot(act, wd_ref[...])
    out = x + (0.5 * gate) * y
    if split_out:
        @pl.when(is_ctx)
        def _():
            o_refs[0][...] = out

        @pl.when(jnp.logical_not(is_ctx))
        def _():
            o_refs[1][...] = out
    else:
        o_refs[0][...] = out


def _ffn(xs, mod, layer, row0, g, wg, wu, wd, rows, split_out=False, premix=None):
    split_in = isinstance(xs, (tuple, list))
    xs = list(xs) if split_in else [xs]
    d = xs[0].shape[1]
    f = wg.shape[2]
    hbm = pl.BlockSpec(memory_space=pl.ANY)
    in_specs = (rows.split_specs(d) if split_in else [rows.row_spec(d)]) + [
        rows.mod_spec(layer, d),
        _layer_spec(*g),
        hbm, hbm, hbm,
    ]
    args = xs + [mod, g[0], wg, wu, wd]
    if premix is not None:
        gla, mla, w_out, j = premix
        in_specs += rows.split_specs(gla[0].shape[1]) + rows.split_specs(mla[0].shape[1])
        in_specs.append(_layer_spec(w_out, j))
        args += list(gla) + list(mla) + [w_out]
    return pl.pallas_call(
        functools.partial(_ffn_kernel, layer=layer, row0=row0, nb_ctx=rows.nb_ctx, split_in=split_in,
                          split_out=split_out, premix=premix is not None),
        out_shape=rows.split_shapes(d, F32) if split_out else jax.ShapeDtypeStruct((rows.m, d), F32),
        grid=(rows.nblocks,),
        in_specs=in_specs,
        out_specs=rows.split_specs(d) if split_out else rows.row_spec(d),
        scratch_shapes=[
            pltpu.VMEM((d, f), BF16), pltpu.VMEM((d, f), BF16), pltpu.VMEM((f, d), BF16),
            pltpu.VMEM((STAGE_SLOTS, d // STAGE_CHUNKS, f), F32),
            pltpu.VMEM((STAGE_SLOTS, f // STAGE_CHUNKS, d), F32),
            pltpu.SemaphoreType.DMA((STAGE_SLOTS,)),
        ],
        compiler_params=_cparams("arbitrary"),
        name="ffn_swiglu",
    )(*args)


def _gelu_tanh(x):
    c = math.sqrt(2.0 / math.pi)
    return x * (0.5 * (1.0 + jnp.tanh(c * (x + 0.044715 * (x * x * x)))))


def _odd_kernel(x_ref, mod_ref, g_ref, win_hbm, vg_ref, ws_ref, bst_ref, wout_hbm, o_ref,
                win_ref, wout_ref, stage_in, stage_out, sem, *, layer_j):
    @pl.when(pl.program_id(0) == 0)
    def _():
        _stage_weights_bf16(layer_j, [win_hbm, wout_hbm], [win_ref, wout_ref], [stage_in, stage_out], sem)

    x = x_ref[...]
    tm = x.shape[0]
    shift, scale, gate = mod_ref[3:4, :], mod_ref[4:5, :], mod_ref[5:6, :]
    h = _modulate(x, g_ref[...], shift, scale).astype(BF16)
    uv = _gelu_tanh(_dot(h, win_ref[...]))
    width = uv.shape[1] // 2
    u = uv[:, :width]
    v = _rms(uv[:, width:], vg_ref[...]).astype(BF16)
    gw = width // CMLP_GROUPS
    chunks = []
    for c in range(tm // CMLP_CHUNK):
        r0 = c * CMLP_CHUNK
        groups = []
        for g in range(CMLP_GROUPS):
            vg = v[r0:r0 + CMLP_CHUNK, g * gw:(g + 1) * gw]
            mixed = _dot(ws_ref[g], vg) + bst_ref[:, g:g + 1]
            groups.append(u[r0:r0 + CMLP_CHUNK, g * gw:(g + 1) * gw] * mixed)
        chunks.append(jnp.concatenate(groups, axis=1))
    z = jnp.concatenate(chunks, axis=0).astype(BF16)
    o_ref[...] = x + gate * _dot(z, wout_ref[...])


def _odd_mixer(x, mod, layer, layer_j, g, w_in, v_g, w_s, b_s_t, w_out, rows):
    m, d = x.shape
    n_in, width = w_in.shape[2], w_out.shape[1]
    hbm = pl.BlockSpec(memory_space=pl.ANY)
    return pl.pallas_call(
        functools.partial(_odd_kernel, layer_j=layer_j),
        out_shape=jax.ShapeDtypeStruct((m, d), F32),
        grid=(rows.nblocks,),
        in_specs=[
            rows.row_spec(d),
            rows.mod_spec(layer, d),
            _layer_spec(*g),
            hbm, _layer_spec(v_g, layer_j), _layer_spec(w_s, layer_j), _layer_spec(b_s_t, layer_j), hbm,
        ],
        out_specs=rows.row_spec(d),
        scratch_shapes=[
            pltpu.VMEM((d, n_in), BF16), pltpu.VMEM((width, d), BF16),
            pltpu.VMEM((STAGE_SLOTS, d // STAGE_CHUNKS, n_in), F32),
            pltpu.VMEM((STAGE_SLOTS, width // STAGE_CHUNKS, d), F32),
            pltpu.SemaphoreType.DMA((STAGE_SLOTS,)),
        ],
        compiler_params=_cparams("arbitrary"),
        name="odd_gmlp",
    )(x, mod, g[0], w_in, v_g, w_s, b_s_t, w_out)


_C_CQ = 0
_C_CKV = _C_CQ + MLA_Q_RANK
_C_KA = _C_CKV + MLA_KV_RANK
_C_KB = _C_KA + HEAD_PAD
_C_Q = _C_KB + HEAD_PAD
_C_K = _C_Q + GLA_QK
_C_V = _C_K + GLA_QK
_C_OG = _C_V + GLA_VW
_C_END = _C_OG + GLA_VW


def _head_scale(x, live):
    ss = jnp.sum(jnp.where(live, x * x, 0.0), axis=-1, keepdims=True)
    return lax.rsqrt(ss * (1.0 / MLA_QK_DIM) + EPS)


def _even_in_kernel(x_ref, mod_ref, g_ref, win_ref, w2_ref, gb_ref, qag_ref, qb_ref, kvag_ref,
                    kvbk_ref, kvbv_ref, kng_ref, tq_ref, ta_ref, tb_ref,
                    gq_ref, gk_ref, gv_ref, og_ref, la_ref, qm_ref, km_ref, vm_ref, ckv_ref, kr_ref,
                    *, nb_ctx):
    is_ctx = pl.program_id(0) < nb_ctx
    shift, scale = mod_ref[3:4, :], mod_ref[4:5, :]
    h = _modulate(x_ref[...], g_ref[...], shift, scale).astype(BF16)

    proj_q = _dot(h, win_ref[:, _C_CQ:_C_CKV])
    proj_kv = _dot(h, win_ref[:, _C_CKV:_C_Q])
    ckv = _rms(proj_kv[:, :MLA_KV_RANK], kvag_ref[...])
    ka = proj_kv[:, _C_KA - _C_CKV:_C_KB - _C_CKV]
    kb = proj_kv[:, _C_KB - _C_CKV:]

    @pl.when(is_ctx)
    def _():
        ckv_ref[...] = ckv
        kr_ref[...] = ka[:, MLA_NOPE:MLA_QK_DIM]

    cq = _rms(proj_q, qag_ref[...]).astype(BF16)
    ckv_bf = ckv.astype(BF16)
    qn = _dot(cq, qb_ref[...])
    kn = _dot(ckv_bf, kvbk_ref[...])
    vm_t = _dot(ckv_bf, kvbv_ref[...])
    z = _dot(ka.astype(BF16), w2_ref[...]) + gb_ref[...]

    lane = lax.broadcasted_iota(jnp.int32, ka.shape, 1)
    live = lane < MLA_QK_DIM
    tq = tq_ref[...]

    def q_heads(heads):
        for hd in heads:
            sl = slice(hd * HEAD_PAD, (hd + 1) * HEAD_PAD)
            qh = qn[:, sl]
            qm_ref[:, sl] = (qh * tq * _head_scale(qh, live)).astype(BF16)

    is_rope = live & (lane >= MLA_NOPE)
    ss_rope = jnp.sum(jnp.where(is_rope, ka * ka, 0.0), axis=-1, keepdims=True)
    rot = ka * ta_ref[...] + kb * tb_ref[...]
    kng = kng_ref[...]

    def k_heads(heads):
        for hd in heads:
            sl = slice(hd * HEAD_PAD, (hd + 1) * HEAD_PAD)
            kh = kn[:, sl]
            ss = jnp.sum(kh * kh, axis=-1, keepdims=True) + ss_rope
            r = lax.rsqrt(ss * (1.0 / MLA_QK_DIM) + EPS)
            km_ref[:, sl] = ((kh * kng + rot) * r).astype(BF16)

    half = MLA_HEADS // 2
    gqk = _dot(h, win_ref[:, _C_Q:_C_V])
    q_heads(range(half))
    gq_ref[...] = gqk[:, :GLA_QK] * (GLA_DK ** -0.5)
    gk_ref[...] = gqk[:, GLA_QK:]
    gv = _dot(h, win_ref[:, _C_V:_C_OG])
    q_heads(range(half, MLA_HEADS))
    gv_ref[...] = gv.astype(BF16)
    og = _dot(h, win_ref[:, _C_OG:])
    k_heads(range(half))
    og_ref[...] = og
    vm_ref[...] = vm_t.T.astype(BF16)
    k_heads(range(half, MLA_HEADS))
    la_ref[...] = ((jnp.minimum(z, 0.0) - jnp.log(1.0 + jnp.exp(-jnp.abs(z))))
                   * (math.log2(math.e) / GLA_GATE_NORM))


def _even_in(x, mod, layer, g, wts, layer_j, rows):
    m, d = x.shape
    tm = rows.tm
    rope_spec = pl.BlockSpec((None, tm, HEAD_PAD), lambda i: (layer_j, rows.rope_block(i), 0))
    widths = [GLA_QK, GLA_QK, GLA_VW, GLA_VW, 2 * GLA_QK, MLA_HEADS * HEAD_PAD, MLA_HEADS * HEAD_PAD]
    dtypes = [F32, F32, BF16, F32, F32, BF16, BF16]
    names = ["w_in", "w2", "gate_b", "qa_g", "qb", "kva_g", "kvb_k", "kvb_v", "kn_g"]
    return pl.pallas_call(
        functools.partial(_even_in_kernel, nb_ctx=rows.nb_ctx),
        out_shape=[jax.ShapeDtypeStruct((m, w), t) for w, t in zip(widths, dtypes)]
        + [jax.ShapeDtypeStruct((MLA_HEADS * MLA_V, m), BF16),
           jax.ShapeDtypeStruct((rows.m_ctx, MLA_KV_RANK), F32),
           jax.ShapeDtypeStruct((rows.m_ctx, MLA_ROPE), F32)],
        grid=(rows.nblocks,),
        in_specs=[rows.row_spec(d), rows.mod_spec(layer, d), _layer_spec(*g)]
        + [_layer_spec(wts[n], layer_j) for n in names] + [rope_spec] * 3,
        out_specs=[rows.row_spec(w) for w in widths]
        + [pl.BlockSpec((MLA_HEADS * MLA_V, tm), lambda i: (0, i)),
           rows.ctx_spec(MLA_KV_RANK), rows.ctx_spec(MLA_ROPE)],
        compiler_params=_cparams("arbitrary"),
        name="even_in_proj",
    )(x, mod, g[0], *[wts[n] for n in names], wts["t_q"], wts["t_a"], wts["t_b"])


def _ctx_kv_kernel(ckv_ref, kr_ref, kvbk_ref, kvbv_ref, kng_ref, k_ref, v_ref):
    ckv_bf = ckv_ref[...].astype(BF16)
    v_ref[...] = _dot(ckv_bf, kvbv_ref[...]).T.astype(BF16)
    kn = _dot(ckv_bf, kvbk_ref[...])
    kr = kr_ref[...]
    lane = lax.broadcasted_iota(jnp.int32, kr.shape, 1)
    live = lane < MLA_QK_DIM
    kng = kng_ref[...]
    for hd in range(MLA_HEADS):
        sl = slice(hd * HEAD_PAD, (hd + 1) * HEAD_PAD)
        kh = kn[:, sl] + kr
        k_ref[:, sl] = (kh * kng * _head_scale(kh, live)).astype(BF16)


def _ctx_kv(cache_ckv, cache_krope_blk, kvb_k, kvb_v, kn_g):
    nb, ne, p, r = cache_ckv.shape
    return pl.pallas_call(
        _ctx_kv_kernel,
        out_shape=[jax.ShapeDtypeStruct((ne, nb, p, MLA_HEADS * HEAD_PAD), BF16),
                   jax.ShapeDtypeStruct((ne, nb, MLA_HEADS * MLA_V, p), BF16)],
        grid=(ne, nb),
        in_specs=[
            pl.BlockSpec((None, None, p, r), lambda j, b: (b, j, 0, 0)),
            pl.BlockSpec((None, None, p, HEAD_PAD), lambda j, b: (b, j, 0, 0)),
            pl.BlockSpec((None,) + kvb_k.shape[1:], lambda j, b: (j, 0, 0)),
            pl.BlockSpec((None,) + kvb_v.shape[1:], lambda j, b: (j, 0, 0)),
            pl.BlockSpec((None,) + kn_g.shape[1:], lambda j, b: (j, 0, 0)),
        ],
        out_specs=[pl.BlockSpec((None, None, p, MLA_HEADS * HEAD_PAD), lambda j, b: (j, b, 0, 0)),
                   pl.BlockSpec((None, None, MLA_HEADS * MLA_V, p), lambda j, b: (j, b, 0, 0))],
        compiler_params=_cparams("arbitrary", "arbitrary"),
        name="ctx_kv",
    )(cache_ckv, cache_krope_blk, kvb_k, kvb_v, kn_g)


def _split3(x):
    hi = x.astype(BF16)
    r = x - hi.astype(F32)
    mid = r.astype(BF16)
    lo = (r - mid.astype(F32)).astype(BF16)
    return hi, mid, lo


def _block_row_bcast(a, blk, off):
    c, n = a.shape
    if blk >= 8:
        pieces = [jnp.broadcast_to(a[b * blk + off:b * blk + off + 1, :], (blk, n))
                  for b in range(c // blk)]
        return pieces[0] if len(pieces) == 1 else jnp.concatenate(pieces, axis=0)
    a3 = a.reshape(c // 8, 8, n)
    sub = lax.broadcasted_iota(jnp.int32, a3.shape, 1) // blk
    out = jnp.broadcast_to(a3[:, off:off + 1, :], a3.shape)
    for s in range(1, 8 // blk):
        cand = jnp.broadcast_to(a3[:, s * blk + off:s * blk + off + 1, :], a3.shape)
        out = jnp.where(sub == s, cand, out)
    return out.reshape(c, n)


def _head_stack(a, head_w):
    lane_head = lax.broadcasted_iota(jnp.int32, a.shape, 1) // head_w
    zero = jnp.zeros_like(a)
    return jnp.concatenate([jnp.where(lane_head == h, a, zero) for h in range(GLA_HEADS)], axis=0)


def _run_skewed(stage_gens, skew=1):
    done = [False] * len(stage_gens)
    tick = 0
    while not all(done):
        for u, gen in enumerate(stage_gens):
            if tick >= u * skew and not done[u]:
                try:
                    next(gen)
                except StopIteration:
                    done[u] = True
        tick += 1


def _gla_phase_a(c, q_ref, k_ref, v_ref, la_ref, tri_ref, pair_ref, oacc_ref, qst_ref, ds_ref, dc_ref):
    C = GLA_CHUNK
    r0 = pl.multiple_of(c * C, C)
    q = q_ref[pl.ds(r0, C), :]
    k = k_ref[pl.ds(r0, C), :]
    la = la_ref[pl.ds(r0, C), :]
    la_f, la_b = la[:, :GLA_QK], la[:, GLA_QK:]
    cums = []
    for d, la_d in enumerate((la_f, la_b)):
        hi, mid, lo = _split3(la_d)
        tri = tri_ref[d]
        cums.append(_dot(tri, hi) + _dot(tri, mid) + _dot(tri, lo))
    cum_f, cum_b = cums
    k_stack = _head_stack(k.astype(BF16), GLA_DK)
    yield

    row = lax.broadcasted_iota(jnp.int32, (C, 1), 0)
    pair_m = pair_ref[...]
    att = jnp.where(pair_m == 0, 2.0 * _dot_nt(q.astype(BF16), k_stack), 0.0)
    m = C // 2
    while m >= 1:
        blk = 2 * m
        upper = (row % blk) >= m
        if m == 1:
            qm = (q * jnp.exp2(jnp.where(upper, la_f, la_b))).astype(BF16)
            km_stack = k_stack
        else:
            ref_f = _block_row_bcast(cum_f, blk, m - 1)
            ref_b = _block_row_bcast(cum_b, blk, m)
            eq = jnp.where(upper, cum_f - ref_f, cum_b - ref_b)
            ek = jnp.where(upper, ref_b - cum_b, ref_f - cum_f)
            qm = (q * jnp.exp2(eq)).astype(BF16)
            km_stack = _head_stack((k * jnp.exp2(ek)).astype(BF16), GLA_DK)
        yield
        att = jnp.where(pair_m == m, _dot_nt(qm, km_stack), att)
        m //= 2

    v = v_ref[pl.ds(r0, C), :]
    v_stack = _head_stack(v, GLA_DV)
    qcs = [(q * jnp.exp2(cum)).astype(BF16) for cum in cums]
    yield
    oacc_ref[pl.ds(r0, C), :] = _dot(att.astype(BF16), v_stack)
    for d, cum in enumerate(cums):
        last = cum[C - 1:C, :] if d == 0 else cum[0:1, :]
        qst_ref[d, c] = _head_stack(qcs[d], GLA_DK)
        kct = (k * jnp.exp2(last - cum)).T.astype(BF16)
        decay_col = jnp.exp2(jnp.broadcast_to(last, (8, GLA_QK))).T[:, 0:1]
        dc_ref[d, c] = jnp.broadcast_to(decay_col, (GLA_QK, GLA_DV))
        yield
        ds_ref[d, c] = jnp.concatenate(
            [_dot(kct[h * GLA_DK:(h + 1) * GLA_DK, :], v[:, h * GLA_DV:(h + 1) * GLA_DV])
             for h in range(GLA_HEADS)], axis=0)


def _gla_phase_c(c, og_ref, gg, oacc_ref, qst_ref, sb_ref, o_ref):
    C = GLA_CHUNK
    r0 = pl.multiple_of(c * C, C)
    q_cat = jnp.concatenate([qst_ref[0, c], qst_ref[1, c]], axis=1)
    s_cat = jnp.concatenate([sb_ref[0, c], sb_ref[1, c]], axis=0)
    inter = _dot(q_cat, s_cat)
    yield
    for h in range(GLA_HEADS):
        sl = slice(h * GLA_DV, (h + 1) * GLA_DV)
        o = oacc_ref[pl.ds(r0, C), sl] + inter[h * C:(h + 1) * C, :]
        o_ref[pl.ds(r0, C), sl] = (_rms(o, gg) * _silu(og_ref[pl.ds(r0, C), sl])).astype(BF16)


def _gla_kernel(*refs, has_state):
    it = iter(refs)
    q_ref, k_ref, v_ref, la_ref, og_ref = (next(it) for _ in range(5))
    s0_ref = next(it) if has_state else None
    tri_ref, pair_ref, gg_ref, o_ref, sout_ref = (next(it) for _ in range(5))
    oacc_ref, qst_ref, ds_ref, dc_ref, sb_ref, st_ref = (next(it) for _ in range(6))

    nc = q_ref.shape[0] // GLA_CHUNK
    per_step = min(GLA_CHUNKS_PER_STEP, nc)

    def phase_a(i, carry):
        _run_skewed([_gla_phase_a(per_step * i + u, q_ref, k_ref, v_ref, la_ref, tri_ref, pair_ref, oacc_ref,
                                  qst_ref, ds_ref, dc_ref) for u in range(per_step)])
        return carry

    lax.fori_loop(0, nc // per_step, phase_a, 0)

    for d in range(2):
        for h in range(GLA_HEADS):
            rows = slice(h * GLA_DK, (h + 1) * GLA_DK)
            st_ref[d, rows, :] = s0_ref[d, h] if has_state else jnp.zeros((GLA_DK, GLA_DV), F32)

    def phase_b(c, carry):
        for d in range(2):
            cc = c if d == 0 else nc - 1 - c
            s = st_ref[d]
            sb_ref[d, cc] = s.astype(BF16)
            st_ref[d] = s * dc_ref[d, cc] + ds_ref[d, cc]
        return carry

    lax.fori_loop(0, nc, phase_b, 0)
    for d in range(2):
        for h in range(GLA_HEADS):
            sout_ref[d, h] = st_ref[d, h * GLA_DK:(h + 1) * GLA_DK, :]

    gg = gg_ref[...]

    def phase_c(i, carry):
        _run_skewed([_gla_phase_c(per_step * i + u, og_ref, gg, oacc_ref, qst_ref, sb_ref, o_ref)
                     for u in range(per_step)])
        return carry

    lax.fori_loop(0, nc // per_step, phase_c, 0)


def _gla(gq, gk, gv, la, og, state, layer_j, tri, gg, nbatch, t, row_off):
    assert (t // GLA_CHUNK) % min(GLA_CHUNKS_PER_STEP, t // GLA_CHUNK) == 0
    b0 = row_off // t
    nc = t // GLA_CHUNK
    seq = lambda w: pl.BlockSpec((t, w), lambda b: (b0 + b, 0))
    in_specs = [seq(GLA_QK), seq(GLA_QK), seq(GLA_VW), seq(2 * GLA_QK), seq(GLA_VW)]
    args = [gq, gk, gv, la, og]
    if state is not None:
        in_specs.append(pl.BlockSpec((None, None, 2, GLA_HEADS, GLA_DK, GLA_DV),
                                     lambda b: (b, layer_j, 0, 0, 0, 0)))
        args.append(state)
    tri, pair_m = tri
    in_specs += [_resident(tri.shape, _const_map(3)), _resident(pair_m.shape, _const_map(2)),
                 _layer_spec(gg, layer_j)]
    args += [tri, pair_m, gg]
    return pl.pallas_call(
        functools.partial(_gla_kernel, has_state=state is not None),
        out_shape=[jax.ShapeDtypeStruct((nbatch * t, GLA_VW), BF16),
                   jax.ShapeDtypeStruct((nbatch, 2, GLA_HEADS, GLA_DK, GLA_DV), F32)],
        grid=(nbatch,),
        in_specs=in_specs,
        out_specs=[pl.BlockSpec((t, GLA_VW), lambda b: (b, 0)),
                   pl.BlockSpec((None, 2, GLA_HEADS, GLA_DK, GLA_DV), lambda b: (b, 0, 0, 0, 0))],
        scratch_shapes=[
            pltpu.VMEM((t, GLA_VW), F32),
            pltpu.VMEM((2, nc, GLA_HEADS * GLA_CHUNK, GLA_QK), BF16),
            pltpu.VMEM((2, nc, GLA_QK, GLA_DV), F32),
            pltpu.VMEM((2, nc, GLA_QK, GLA_DV), F32),
            pltpu.VMEM((2, nc, GLA_QK, GLA_DV), BF16),
            pltpu.VMEM((2, GLA_QK, GLA_DV), F32),
        ],
        compiler_params=_cparams("arbitrary"),
        name="gla_scan",
    )(*args)


def _mla_kernel(*refs, has_ctx):
    it = iter(refs)
    q_ref, k_ref, v_ref = next(it), next(it), next(it)
    kc_ref, vc_ref = (next(it), next(it)) if has_ctx else (None, None)
    o_ref = next(it)
    def scores(hd):
        sl = slice(hd * HEAD_PAD, (hd + 1) * HEAD_PAD)
        q = q_ref[:, sl]
        return [_dot_nt(r[:, sl], q) for r in ((k_ref, kc_ref) if has_ctx else (k_ref,))]

    def softmax(ss):
        mx = functools.reduce(jnp.maximum, [jnp.max(s, axis=0, keepdims=True) for s in ss])
        ps = [jnp.exp2(s - mx) for s in ss]
        den = sum(jnp.sum(p, axis=0, keepdims=True) for p in ps)
        return [p.astype(BF16) for p in ps], den

    def values(hd, ps, den):
        vrows = slice(hd * MLA_V, (hd + 1) * MLA_V)
        o = sum(_dot(r[vrows, :], p) for r, p in zip((v_ref, vc_ref), ps))
        return o / den

    s_q, p_q, outs = {}, {}, []
    for step in range(MLA_HEADS + 2):
        if step < MLA_HEADS:
            s_q[step] = scores(step)
        if 0 <= step - 1 < MLA_HEADS:
            p_q[step - 1] = softmax(s_q.pop(step - 1))
        if 0 <= step - 2 < MLA_HEADS:
            outs.append(values(step - 2, *p_q.pop(step - 2)))
    o_ref[...] = jnp.concatenate(outs, axis=0).T.astype(BF16)


def _mla(qm, km, vm, kc, vc, layer_j, nbatch, t, row_off, tq):
    nq = t // tq
    q0, b0 = row_off // tq, row_off // t
    qspec = lambda w: pl.BlockSpec((tq, w), lambda b, i: (q0 + b * nq + i, 0))
    kspec = lambda w: pl.BlockSpec((t, w), lambda b, i: (b0 + b, 0))
    vspec = pl.BlockSpec((MLA_HEADS * MLA_V, t), lambda b, i: (0, b0 + b))
    in_specs = [qspec(MLA_HEADS * HEAD_PAD), kspec(MLA_HEADS * HEAD_PAD), vspec]
    args = [qm, km, vm]
    if kc is not None:
        in_specs += [pl.BlockSpec((None, None) + kc.shape[2:], lambda b, i: (layer_j, b, 0, 0)),
                     pl.BlockSpec((None, None) + vc.shape[2:], lambda b, i: (layer_j, b, 0, 0))]
        args += [kc, vc]
    return pl.pallas_call(
        functools.partial(_mla_kernel, has_ctx=kc is not None),
        out_shape=jax.ShapeDtypeStruct((nbatch * t, MLA_HEADS * MLA_V), BF16),
        grid=(nbatch, nq),
        in_specs=in_specs,
        out_specs=pl.BlockSpec((tq, MLA_HEADS * MLA_V), lambda b, i: (b * nq + i, 0)),
        compiler_params=_cparams("arbitrary", "arbitrary"),
        name="mla_attention",
    )(*args)


def _swap_signed(a):
    pairs = a.reshape(a.shape[:-1] + (a.shape[-1] // 2, 2))
    return jnp.stack([-pairs[..., 1], pairs[..., 0]], axis=-1).reshape(a.shape)


def _swap_pairs(a):
    pairs = a.reshape(a.shape[:-1] + (a.shape[-1] // 2, 2))
    return pairs[..., ::-1].reshape(a.shape)


def _pack_even_weights(cos, sin, even_w_in, gla_gate_w2, gla_gate_b, mla_qa_g, mla_qb_w, mla_qn_g,
                       mla_kva_g, mla_kvb_w, mla_kn_g):
    ne, d, _ = even_w_in.shape
    o = 0
    parts = {}
    for name, width in (("q", GLA_QK), ("k", GLA_QK), ("v", GLA_VW), ("og", GLA_VW),
                        ("gl", 2 * GLA_GATE_RANK), ("cq", MLA_Q_RANK), ("ckv", MLA_KV_RANK),
                        ("kr", MLA_ROPE)):
        parts[name] = even_w_in[:, :, o:o + width]
        o += width
    zeros = lambda n: jnp.zeros((ne, d, n), even_w_in.dtype)
    kr, kr_sw = parts["kr"], _swap_signed(parts["kr"])
    blk_a = jnp.concatenate([parts["gl"], zeros(MLA_NOPE - 2 * GLA_GATE_RANK), kr, kr_sw], axis=2)
    blk_b = jnp.concatenate([zeros(MLA_NOPE), kr_sw, kr], axis=2)
    w_in = jnp.concatenate([parts["cq"], parts["ckv"], blk_a, blk_b, parts["q"], parts["k"], parts["v"],
                            parts["og"]], axis=2).astype(BF16)
    assert w_in.shape[2] == _C_END

    w2_f = jnp.pad(gla_gate_w2[:, 0], ((0, 0), (0, 0), (0, GLA_QK)))
    w2_b = jnp.pad(gla_gate_w2[:, 1], ((0, 0), (0, 0), (GLA_QK, 0)))
    w2 = jnp.pad(jnp.concatenate([w2_f, w2_b], axis=1), ((0, 0), (0, HEAD_PAD - 2 * GLA_GATE_RANK), (0, 0)))

    pad_head = lambda a: jnp.pad(a, [(0, 0)] * (a.ndim - 1) + [(0, HEAD_PAD - a.shape[-1])])
    qb = jnp.concatenate([mla_qb_w, _swap_signed(mla_qb_w[..., MLA_NOPE:])], axis=-1)
    qb = qb.reshape(ne, MLA_Q_RANK, MLA_HEADS * HEAD_PAD)
    kvb_k = pad_head(mla_kvb_w[..., :MLA_NOPE]).reshape(ne, MLA_KV_RANK, MLA_HEADS * HEAD_PAD)
    kvb_v = mla_kvb_w[..., MLA_NOPE:].reshape(ne, MLA_KV_RANK, MLA_HEADS * MLA_V)

    def rot_tables(g):
        n = cos.shape[0]
        g_n = jnp.broadcast_to(g[:, None, :MLA_NOPE], (ne, n, MLA_NOPE))
        g_c = g[:, None, MLA_NOPE:] * cos
        g_s = _swap_pairs(g[:, MLA_NOPE:])[:, None, :] * sin
        return g_n, g_c, g_s

    qn_n, qn_c, qn_s = rot_tables(mla_qn_g)
    kn_n, kn_c, kn_s = rot_tables(mla_kn_g)
    kzero = jnp.zeros_like(kn_n)
    return {
        "w_in": w_in,
        "w2": w2.astype(BF16),
        "gate_b": gla_gate_b.reshape(ne, 1, 2 * GLA_QK),
        "qa_g": mla_qa_g[:, None, :],
        "qb": qb.astype(BF16),
        "kva_g": mla_kva_g[:, None, :],
        "kvb_k": kvb_k.astype(BF16),
        "kvb_v": kvb_v.astype(BF16),
        "kn_g": pad_head(mla_kn_g[:, :MLA_NOPE])[:, None, :],
        "kn_g_ctx": jnp.concatenate([mla_kn_g, mla_kn_g[:, MLA_NOPE:]], axis=1)[:, None, :],
        "t_q": jnp.concatenate([qn_n, qn_c, qn_s], axis=2) * (MLA_QK_DIM ** -0.5 * math.log2(math.e)),
        "t_a": jnp.concatenate([kzero, kn_c, kn_s], axis=2),
        "t_b": jnp.concatenate([kzero, kn_s, kn_c], axis=2),
    }


def _rope_tables(n, tm):
    pairs = MLA_ROPE // 4
    pos = np.arange(n)
    inv = ROPE_BASE ** (-jnp.arange(pairs, dtype=F32) / pairs)
    row = jnp.asarray(pos // GRID_W, F32)
    col = jnp.asarray(pos % GRID_W, F32)
    ang = jnp.concatenate([row[:, None] * inv, col[:, None] * inv], axis=-1)
    cos = jnp.repeat(jnp.cos(ang), 2, axis=-1)
    sin = jnp.repeat(jnp.sin(ang), 2, axis=-1)
    cos = jnp.pad(cos, ((tm, 0), (0, 0)), constant_values=1.0)
    sin = jnp.pad(sin, ((tm, 0), (0, 0)))
    return cos, sin


def _tri_masks():
    r = np.arange(GLA_CHUNK)
    lower = (r[None, :] <= r[:, None]).astype(np.float32)
    diff = r[:, None] ^ r[None, :]
    level = np.where(diff > 0, 1 << np.floor(np.log2(np.maximum(diff, 1))).astype(np.int64), 0)
    pair_m = np.tile(level, (1, GLA_HEADS)).astype(np.int32)
    return jnp.asarray(np.stack([lower, lower.T]), BF16), jnp.asarray(pair_m)


def _pick_tile(pref, *sizes):
    tm = pref
    while any(s % tm for s in sizes):
        tm //= 2
    return tm


def kernel(x_prompt, x_sample, c, c_ctx, cache_ckv, cache_krope, state_gla, ada_w, ada_b, norm_g,
           ffn1_wg, ffn1_wu, ffn1_wd, ffn2_wg, ffn2_wu, ffn2_wd, even_w_in, even_w_out,
           gla_gate_w2, gla_gate_b, gla_norm_g, mla_qa_g, mla_qb_w, mla_kva_g, mla_kvb_w,
           mla_qn_g, mla_kn_g, odd_w_in, odd_v_g, odd_ws, odd_bs, odd_w_out):
    batch, seq, d = x_prompt.shape
    dec_batch, dec_seq, _ = x_sample.shape
    depth = ada_w.shape[0]
    n_even = even_w_in.shape[0]
    m_ctx, m_dec = batch * seq, dec_batch * dec_seq
    assert 1 + dec_batch <= COND_PAD and seq % CMLP_CHUNK == 0 and dec_seq % CMLP_CHUNK == 0

    rows = _Rows(m_ctx, dec_batch, dec_seq, _pick_tile(ROW_TILE, m_ctx, dec_seq))

    cond = jnp.concatenate([c_ctx[None, :], c, jnp.zeros((COND_PAD - 1 - dec_batch, d), F32)], axis=0)
    mod = _modulation_all(cond, ada_w, ada_b).reshape(depth, COND_PAD, N_MOD, d)

    cos_t, sin_t = _rope_tables(dec_seq, rows.tm)
    tri = _tri_masks()
    krope_blk = jnp.concatenate([jnp.zeros(cache_krope.shape[:-1] + (MLA_NOPE,), F32), cache_krope,
                                 cache_krope], axis=-1)
    even_wts = _pack_even_weights(cos_t, sin_t, even_w_in, gla_gate_w2, gla_gate_b, mla_qa_g, mla_qb_w,
                                  mla_qn_g, mla_kva_g, mla_kvb_w, mla_kn_g)
    kc, vc = _ctx_kv(cache_ckv, krope_blk, even_wts["kvb_k"], even_wts["kvb_v"], even_wts["kn_g_ctx"])

    ffn1 = (ffn1_wg, ffn1_wu, ffn1_wd)
    ffn2 = (ffn2_wg, ffn2_wu, ffn2_wd)
    w_out_bf = even_w_out.astype(BF16)
    odd_wts = (odd_w_in, odd_v_g[:, None, :], odd_ws.astype(BF16), jnp.swapaxes(odd_bs, 1, 2), odd_w_out)
    norms = norm_g.reshape(depth * 3, 1, d)
    gg = gla_norm_g[:, None, :]

    x = (x_prompt.reshape(m_ctx, d), x_sample.reshape(m_dec, d))
    new_ckv, new_krope, new_gla = [], [], []
    for i in range(depth):
        j = i // 2
        x = _ffn(x, mod, i, 0, (norms, 3 * i), *ffn1, rows)
        g_mix = (norms, 3 * i + 1)
        premix = None
        if i % 2 == 0:
            gq, gk, gv, og, la, qm, km, vm, ckv, kr = _even_in(x, mod, i, g_mix, even_wts, j, rows)
            gla_ctx, st = _gla(gq, gk, gv, la, og, None, j, tri, gg, batch, seq, 0)
            gla_dec, _ = _gla(gq, gk, gv, la, og, state_gla, j, tri, gg, dec_batch, dec_seq, m_ctx)
            mla_ctx = _mla(qm, km, vm, None, None, j, batch, seq, 0, _pick_tile(256, seq))
            mla_dec = _mla(qm, km, vm, kc, vc, j, dec_batch, dec_seq, m_ctx, _pick_tile(512, dec_seq))
            premix = ((gla_ctx, gla_dec), (mla_ctx, mla_dec), w_out_bf, j)
            new_ckv.append(ckv.reshape(batch, seq, MLA_KV_RANK))
            new_krope.append(kr.reshape(batch, seq, MLA_ROPE))
            new_gla.append(st)
        else:
            x = _odd_mixer(x, mod, i, j, g_mix, *odd_wts, rows)
        x = _ffn(x, mod, i, 6, (norms, 3 * i + 2), *ffn2, rows, split_out=(i == depth - 1),
                 premix=premix)

    y_prompt, y_sample = x
    return (y_prompt.reshape(batch, seq, d), y_sample.reshape(dec_batch, dec_seq, d),
            jnp.stack(new_ckv, axis=1), jnp.stack(new_krope, axis=1), jnp.stack(new_gla, axis=1))
```

```python
import functools
import math

import numpy as np
import jax
import jax.numpy as jnp
from jax import lax
from jax.experimental import pallas as pl
from jax.experimental.pallas import tpu as pltpu

F32 = jnp.float32
BF16 = jnp.bfloat16

EPS = 1e-6
N_MOD = 9
GRID_W = 64
ROPE_BASE = 10000.0
GLA_HEADS = 4
GLA_DK = 64
GLA_DV = 128
GLA_QK = GLA_HEADS * GLA_DK
GLA_VW = GLA_HEADS * GLA_DV
GLA_GATE_RANK = 16
GLA_GATE_NORM = 16.0
GLA_CHUNK = 64
GLA_CHUNKS_PER_STEP = 8
MLA_HEADS = 8
MLA_NOPE = 64
MLA_ROPE = 32
MLA_V = 64
MLA_QK_DIM = MLA_NOPE + MLA_ROPE
MLA_Q_RANK = 384
MLA_KV_RANK = 256
HEAD_PAD = 128
MLA_QUERY_TILE = 512
CMLP_CHUNK = 128
CMLP_GROUPS = 4
COND_PAD = 16

VMEM_LIMIT = 56 * 1024 * 1024
ROW_TILE = 512


def _cparams(*sem):
    return pltpu.CompilerParams(dimension_semantics=sem, vmem_limit_bytes=VMEM_LIMIT)


def _resident(shape, index_map):
    return pl.BlockSpec(shape, index_map, pipeline_mode=pl.Buffered(1))


def _const_map(nd):
    return lambda *_: (0,) * nd


def _layer_spec(stack, layer):
    nd = stack.ndim - 1
    return _resident((None,) + stack.shape[1:], lambda *_: (layer,) + (0,) * nd)


def _silu(x):
    return x * jax.nn.sigmoid(x)


def _rms(x, g):
    return x * lax.rsqrt(jnp.mean(x * x, axis=-1, keepdims=True) + EPS) * g


def _modulate(x, g, shift, scale):
    return _rms(x, g) * (1.0 + scale) + shift


def _dot(a, b):
    return jnp.dot(a, b, preferred_element_type=F32)


def _dot_nt(a, b):
    return lax.dot_general(a, b, (((1,), (1,)), ((), ())), preferred_element_type=F32)


def _mod_kernel(c_ref, w_ref, b_ref, o_ref):
    s = _silu(c_ref[...]).astype(BF16)
    o_ref[...] = _dot(s, w_ref[...].astype(BF16)) + b_ref[...]


def _modulation_all(cond, ada_w, ada_b):
    depth, d, n = ada_w.shape
    tn = n // 4
    return pl.pallas_call(
        _mod_kernel,
        out_shape=jax.ShapeDtypeStruct((depth, COND_PAD, n), F32),
        grid=(depth, n // tn),
        in_specs=[
            pl.BlockSpec((COND_PAD, d), lambda l, j: (0, 0)),
            pl.BlockSpec((None, d, tn), lambda l, j: (l, 0, j)),
            pl.BlockSpec((None, 1, tn), lambda l, j: (l, 0, j)),
        ],
        out_specs=pl.BlockSpec((None, COND_PAD, tn), lambda l, j: (l, 0, j)),
        compiler_params=_cparams("arbitrary", "arbitrary"),
        name="adaln_modulation",
    )(cond, ada_w, ada_b.reshape(depth, 1, n))


class _Rows:
    def __init__(self, m_ctx, dec_batch, dec_seq, tm):
        assert m_ctx % tm == 0 and dec_seq % tm == 0
        self.tm = tm
        self.m_ctx = m_ctx
        self.m = m_ctx + dec_batch * dec_seq
        self.nb_ctx = m_ctx // tm
        self.nb_seq = dec_seq // tm
        self.nblocks = self.m // tm

    def cond(self, i):
        return jnp.where(i < self.nb_ctx, 0, 1 + (i - self.nb_ctx) // self.nb_seq)

    def rope_block(self, i):
        return jnp.where(i < self.nb_ctx, 0, 1 + (i - self.nb_ctx) % self.nb_seq)

    def mod_spec(self, layer, d):
        return pl.BlockSpec((None, None, N_MOD, d), lambda i: (layer, self.cond(i), 0, 0))

    def row_spec(self, width):
        return pl.BlockSpec((self.tm, width), lambda i: (i, 0))

    def ctx_spec(self, width):
        return pl.BlockSpec((self.tm, width), lambda i: (jnp.minimum(i, self.nb_ctx - 1), 0))

    def dec_spec(self, width):
        return pl.BlockSpec((self.tm, width), lambda i: (jnp.maximum(i - self.nb_ctx, 0), 0))

    def split_specs(self, width):
        return [self.ctx_spec(width), self.dec_spec(width)]

    def split_shapes(self, width, dtype):
        return [jax.ShapeDtypeStruct((self.m_ctx, width), dtype),
                jax.ShapeDtypeStruct((self.m - self.m_ctx, width), dtype)]


def _load_split(refs, is_ctx):
    if len(refs) == 1:
        return refs[0][...]
    return jnp.where(is_ctx, refs[0][...], refs[1][...])


STAGE_CHUNKS = 16
STAGE_SLOTS = 8


def _stage_weights_bf16(layer, hbm_refs, vmem_refs, stage_refs, sem):
    chunks = []
    for src, dst, stage in zip(hbm_refs, vmem_refs, stage_refs):
        rows = stage.shape[1]
        assert dst.shape[0] % rows == 0
        chunks += [(src, dst, stage, r0) for r0 in range(0, dst.shape[0], rows)]
    slots = sem.shape[0]
    ahead = slots - 1

    def copy(i):
        src, _, stage, r0 = chunks[i]
        return pltpu.make_async_copy(src.at[layer, pl.ds(r0, stage.shape[1]), :], stage.at[i % slots],
                                     sem.at[i % slots])

    for i in range(min(ahead, len(chunks))):
        copy(i).start()
    for i, (_, dst, stage, r0) in enumerate(chunks):
        if i + ahead < len(chunks):
            copy(i + ahead).start()
        copy(i).wait()
        dst[pl.ds(r0, stage.shape[1]), :] = stage[i % slots].astype(BF16)


def _ffn_kernel(*refs, layer, row0, nb_ctx, split_in, split_out, premix):
    it = iter(refs)
    x_refs = [next(it) for _ in range(2 if split_in else 1)]
    mod_ref, g_ref = next(it), next(it)
    w_hbm = [next(it) for _ in range(3)]
    if premix:
        a_refs, b_refs = [next(it), next(it)], [next(it), next(it)]
        wo_ref = next(it)
    o_refs = [next(it) for _ in range(2 if split_out else 1)]
    wg_ref, wu_ref, wd_ref = w_vmem = [next(it) for _ in range(3)]
    stage_in, stage_down, sem = next(it), next(it), next(it)

    @pl.when(pl.program_id(0) == 0)
    def _():
        _stage_weights_bf16(layer, w_hbm, w_vmem, [stage_in, stage_in, stage_down], sem)

    is_ctx = pl.program_id(0) < nb_ctx
    shift = mod_ref[row0:row0 + 1, :]
    scale = mod_ref[row0 + 1:row0 + 2, :]
    gate = mod_ref[row0 + 2:row0 + 3, :]
    x = _load_split(x_refs, is_ctx)
    if premix:
        half = wo_ref.shape[0] // 2
        mix = _dot(_load_split(a_refs, is_ctx), wo_ref[:half, :])
        mix = mix + _dot(_load_split(b_refs, is_ctx), wo_ref[half:, :])
        x = x + mod_ref[5:6, :] * mix
    h = _modulate(x, g_ref[...], shift, scale).astype(BF16)
    a = _dot(h, wg_ref[...])
    u = _dot(h, wu_ref[...])
    act = (_silu(a) * u).astype(BF16)
    y = _dot(act, wd_ref[...])
    out = x + (0.5 * gate) * y
    if split_out:
        @pl.when(is_ctx)
        def _():
            o_refs[0][...] = out

        @pl.when(jnp.logical_not(is_ctx))
        def _():
            o_refs[1][...] = out
    else:
        o_refs[0][...] = out


def _ffn(xs, mod, layer, row0, g, wg, wu, wd, rows, split_out=False, premix=None):
    split_in = isinstance(xs, (tuple, list))
    xs = list(xs) if split_in else [xs]
    d = xs[0].shape[1]
    f = wg.shape[2]
    hbm = pl.BlockSpec(memory_space=pl.ANY)
    in_specs = (rows.split_specs(d) if split_in else [rows.row_spec(d)]) + [
        rows.mod_spec(layer, d),
        _layer_spec(*g),
        hbm, hbm, hbm,
    ]
    args = xs + [mod, g[0], wg, wu, wd]
    if premix is not None:
        gla, mla, w_out, j = premix
        in_specs += rows.split_specs(gla[0].shape[1]) + rows.split_specs(mla[0].shape[1])
        in_specs.append(_layer_spec(w_out, j))
        args += list(gla) + list(mla) + [w_out]
    return pl.pallas_call(
        functools.partial(_ffn_kernel, layer=layer, row0=row0, nb_ctx=rows.nb_ctx, split_in=split_in,
                          split_out=split_out, premix=premix is not None),
        out_shape=rows.split_shapes(d, F32) if split_out else jax.ShapeDtypeStruct((rows.m, d), F32),
        grid=(rows.nblocks,),
        in_specs=in_specs,
        out_specs=rows.split_specs(d) if split_out else rows.row_spec(d),
        scratch_shapes=[
            pltpu.VMEM((d, f), BF16), pltpu.VMEM((d, f), BF16), pltpu.VMEM((f, d), BF16),
            pltpu.VMEM((STAGE_SLOTS, d // STAGE_CHUNKS, f), F32),
            pltpu.VMEM((STAGE_SLOTS, f // STAGE_CHUNKS, d), F32),
            pltpu.SemaphoreType.DMA((STAGE_SLOTS,)),
        ],
        compiler_params=_cparams("arbitrary"),
        name="ffn_swiglu",
    )(*args)


def _gelu_tanh(x):
    c = math.sqrt(2.0 / math.pi)
    return x * (0.5 * (1.0 + jnp.tanh(c * (x + 0.044715 * (x * x * x)))))


def _odd_kernel(x_ref, mod_ref, g_ref, win_hbm, vg_ref, ws_ref, bst_ref, wout_hbm, o_ref,
                win_ref, wout_ref, stage_in, stage_out, sem, *, layer_j):
    @pl.when(pl.program_id(0) == 0)
    def _():
        _stage_weights_bf16(layer_j, [win_hbm, wout_hbm], [win_ref, wout_ref], [stage_in, stage_out], sem)

    x = x_ref[...]
    tm = x.shape[0]
    shift, scale, gate = mod_ref[3:4, :], mod_ref[4:5, :], mod_ref[5:6, :]
    h = _modulate(x, g_ref[...], shift, scale).astype(BF16)
    uv = _gelu_tanh(_dot(h, win_ref[...]))
    width = uv.shape[1] // 2
    u = uv[:, :width]
    v = _rms(uv[:, width:], vg_ref[...]).astype(BF16)
    gw = width // CMLP_GROUPS
    chunks = []
    for c in range(tm // CMLP_CHUNK):
        r0 = c * CMLP_CHUNK
        groups = []
        for g in range(CMLP_GROUPS):
            vg = v[r0:r0 + CMLP_CHUNK, g * gw:(g + 1) * gw]
            mixed = _dot(ws_ref[g], vg) + bst_ref[:, g:g + 1]
            groups.append(u[r0:r0 + CMLP_CHUNK, g * gw:(g + 1) * gw] * mixed)
        chunks.append(jnp.concatenate(groups, axis=1))
    z = jnp.concatenate(chunks, axis=0).astype(BF16)
    o_ref[...] = x + gate * _dot(z, wout_ref[...])


def _odd_mixer(x, mod, layer, layer_j, g, w_in, v_g, w_s, b_s_t, w_out, rows):
    m, d = x.shape
    n_in, width = w_in.shape[2], w_out.shape[1]
    hbm = pl.BlockSpec(memory_space=pl.ANY)
    return pl.pallas_call(
        functools.partial(_odd_kernel, layer_j=layer_j),
        out_shape=jax.ShapeDtypeStruct((m, d), F32),
        grid=(rows.nblocks,),
        in_specs=[
            rows.row_spec(d),
            rows.mod_spec(layer, d),
            _layer_spec(*g),
            hbm, _layer_spec(v_g, layer_j), _layer_spec(w_s, layer_j), _layer_spec(b_s_t, layer_j), hbm,
        ],
        out_specs=rows.row_spec(d),
        scratch_shapes=[
            pltpu.VMEM((d, n_in), BF16), pltpu.VMEM((width, d), BF16),
            pltpu.VMEM((STAGE_SLOTS, d // STAGE_CHUNKS, n_in), F32),
            pltpu.VMEM((STAGE_SLOTS, width // STAGE_CHUNKS, d), F32),
            pltpu.SemaphoreType.DMA((STAGE_SLOTS,)),
        ],
        compiler_params=_cparams("arbitrary"),
        name="odd_gmlp",
    )(x, mod, g[0], w_in, v_g, w_s, b_s_t, w_out)


_C_CQ = 0
_C_CKV = _C_CQ + MLA_Q_RANK
_C_KA = _C_CKV + MLA_KV_RANK
_C_KB = _C_KA + HEAD_PAD
_C_Q = _C_KB + HEAD_PAD
_C_K = _C_Q + GLA_QK
_C_V = _C_K + GLA_QK
_C_OG = _C_V + GLA_VW
_C_END = _C_OG + GLA_VW


def _head_scale(x, live):
    ss = jnp.sum(jnp.where(live, x * x, 0.0), axis=-1, keepdims=True)
    return lax.rsqrt(ss * (1.0 / MLA_QK_DIM) + EPS)


def _even_in_kernel(x_ref, mod_ref, g_ref, win_ref, w2_ref, gb_ref, qag_ref, qb_ref, kvag_ref,
                    kvbk_ref, kvbv_ref, kng_ref, tq_ref, ta_ref, tb_ref,
                    gq_ref, gk_ref, gv_ref, og_ref, la_ref, qm_ref, km_ref, vm_ref, ckv_ref, kr_ref,
                    *, nb_ctx):
    is_ctx = pl.program_id(0) < nb_ctx
    shift, scale = mod_ref[3:4, :], mod_ref[4:5, :]
    h = _modulate(x_ref[...], g_ref[...], shift, scale).astype(BF16)

    proj_q = _dot(h, win_ref[:, _C_CQ:_C_CKV])
    proj_kv = _dot(h, win_ref[:, _C_CKV:_C_Q])
    ckv = _rms(proj_kv[:, :MLA_KV_RANK], kvag_ref[...])
    ka = proj_kv[:, _C_KA - _C_CKV:_C_KB - _C_CKV]
    kb = proj_kv[:, _C_KB - _C_CKV:]

    @pl.when(is_ctx)
    def _():
        ckv_ref[...] = ckv
        kr_ref[...] = ka[:, MLA_NOPE:MLA_QK_DIM]

    cq = _rms(proj_q, qag_ref[...]).astype(BF16)
    ckv_bf = ckv.astype(BF16)
    qn = _dot(cq, qb_ref[...])
    kn = _dot(ckv_bf, kvbk_ref[...])
    vm_t = _dot(ckv_bf, kvbv_ref[...])
    z = _dot(ka.astype(BF16), w2_ref[...]) + gb_ref[...]

    lane = lax.broadcasted_iota(jnp.int32, ka.shape, 1)
    live = lane < MLA_QK_DIM
    tq = tq_ref[...]

    def q_heads(heads):
        for hd in heads:
            sl = slice(hd * HEAD_PAD, (hd + 1) * HEAD_PAD)
            qh = qn[:, sl]
            qm_ref[:, sl] = (qh * tq * _head_scale(qh, live)).astype(BF16)

    is_rope = live & (lane >= MLA_NOPE)
    ss_rope = jnp.sum(jnp.where(is_rope, ka * ka, 0.0), axis=-1, keepdims=True)
    rot = ka * ta_ref[...] + kb * tb_ref[...]
    kng = kng_ref[...]

    def k_heads(heads):
        for hd in heads:
            sl = slice(hd * HEAD_PAD, (hd + 1) * HEAD_PAD)
            kh = kn[:, sl]
            ss = jnp.sum(kh * kh, axis=-1, keepdims=True) + ss_rope
            r = lax.rsqrt(ss * (1.0 / MLA_QK_DIM) + EPS)
            km_ref[:, sl] = ((kh * kng + rot) * r).astype(BF16)

    half = MLA_HEADS // 2
    gqk = _dot(h, win_ref[:, _C_Q:_C_V])
    q_heads(range(half))
    gq_ref[...] = gqk[:, :GLA_QK] * (GLA_DK ** -0.5)
    gk_ref[...] = gqk[:, GLA_QK:]
    gv = _dot(h, win_ref[:, _C_V:_C_OG])
    q_heads(range(half, MLA_HEADS))
    gv_ref[...] = gv.astype(BF16)
    og = _dot(h, win_ref[:, _C_OG:])
    k_heads(range(half))
    og_ref[...] = og
    vm_ref[...] = vm_t.T.astype(BF16)
    k_heads(range(half, MLA_HEADS))
    la_ref[...] = ((jnp.minimum(z, 0.0) - jnp.log(1.0 + jnp.exp(-jnp.abs(z))))
                   * (math.log2(math.e) / GLA_GATE_NORM))


def _even_in(x, mod, layer, g, wts, layer_j, rows):
    m, d = x.shape
    tm = rows.tm
    rope_spec = pl.BlockSpec((None, tm, HEAD_PAD), lambda i: (layer_j, rows.rope_block(i), 0))
    widths = [GLA_QK, GLA_QK, GLA_VW, GLA_VW, 2 * GLA_QK, MLA_HEADS * HEAD_PAD, MLA_HEADS * HEAD_PAD]
    dtypes = [F32, F32, BF16, F32, F32, BF16, BF16]
    names = ["w_in", "w2", "gate_b", "qa_g", "qb", "kva_g", "kvb_k", "kvb_v", "kn_g"]
    return pl.pallas_call(
        functools.partial(_even_in_kernel, nb_ctx=rows.nb_ctx),
        out_shape=[jax.ShapeDtypeStruct((m, w), t) for w, t in zip(widths, dtypes)]
        + [jax.ShapeDtypeStruct((MLA_HEADS * MLA_V, m), BF16),
           jax.ShapeDtypeStruct((rows.m_ctx, MLA_KV_RANK), F32),
           jax.ShapeDtypeStruct((rows.m_ctx, MLA_ROPE), F32)],
        grid=(rows.nblocks,),
        in_specs=[rows.row_spec(d), rows.mod_spec(layer, d), _layer_spec(*g)]
        + [_layer_spec(wts[n], layer_j) for n in names] + [rope_spec] * 3,
        out_specs=[rows.row_spec(w) for w in widths]
        + [pl.BlockSpec((MLA_HEADS * MLA_V, tm), lambda i: (0, i)),
           rows.ctx_spec(MLA_KV_RANK), rows.ctx_spec(MLA_ROPE)],
        compiler_params=_cparams("arbitrary"),
        name="even_in_proj",
    )(x, mod, g[0], *[wts[n] for n in names], wts["t_q"], wts["t_a"], wts["t_b"])


def _ctx_kv_kernel(ckv_ref, kr_ref, kvbk_ref, kvbv_ref, kng_ref, k_ref, v_ref):
    ckv_bf = ckv_ref[...].astype(BF16)
    v_ref[...] = _dot(ckv_bf, kvbv_ref[...]).T.astype(BF16)
    kn = _dot(ckv_bf, kvbk_ref[...])
    kr = kr_ref[...]
    lane = lax.broadcasted_iota(jnp.int32, kr.shape, 1)
    live = lane < MLA_QK_DIM
    kng = kng_ref[...]
    for hd in range(MLA_HEADS):
        sl = slice(hd * HEAD_PAD, (hd + 1) * HEAD_PAD)
        kh = kn[:, sl] + kr
        k_ref[:, sl] = (kh * kng * _head_scale(kh, live)).astype(BF16)


def _ctx_kv(cache_ckv, cache_krope_blk, kvb_k, kvb_v, kn_g):
    nb, ne, p, r = cache_ckv.shape
    return pl.pallas_call(
        _ctx_kv_kernel,
        out_shape=[jax.ShapeDtypeStruct((ne, nb, p, MLA_HEADS * HEAD_PAD), BF16),
                   jax.ShapeDtypeStruct((ne, nb, MLA_HEADS * MLA_V, p), BF16)],
        grid=(ne, nb),
        in_specs=[
            pl.BlockSpec((None, None, p, r), lambda j, b: (b, j, 0, 0)),
            pl.BlockSpec((None, None, p, HEAD_PAD), lambda j, b: (b, j, 0, 0)),
            pl.BlockSpec((None,) + kvb_k.shape[1:], lambda j, b: (j, 0, 0)),
            pl.BlockSpec((None,) + kvb_v.shape[1:], lambda j, b: (j, 0, 0)),
            pl.BlockSpec((None,) + kn_g.shape[1:], lambda j, b: (j, 0, 0)),
        ],
        out_specs=[pl.BlockSpec((None, None, p, MLA_HEADS * HEAD_PAD), lambda j, b: (j, b, 0, 0)),
                   pl.BlockSpec((None, None, MLA_HEADS * MLA_V, p), lambda j, b: (j, b, 0, 0))],
        compiler_params=_cparams("arbitrary", "arbitrary"),
        name="ctx_kv",
    )(cache_ckv, cache_krope_blk, kvb_k, kvb_v, kn_g)


def _split3(x):
    hi = x.astype(BF16)
    r = x - hi.astype(F32)
    mid = r.astype(BF16)
    lo = (r - mid.astype(F32)).astype(BF16)
    return hi, mid, lo


def _block_row_bcast(a, blk, off):
    c, n = a.shape
    if blk >= 8:
        pieces = [jnp.broadcast_to(a[b * blk + off:b * blk + off + 1, :], (blk, n))
                  for b in range(c // blk)]
        return pieces[0] if len(pieces) == 1 else jnp.concatenate(pieces, axis=0)
    a3 = a.reshape(c // 8, 8, n)
    sub = lax.broadcasted_iota(jnp.int32, a3.shape, 1) // blk
    out = jnp.broadcast_to(a3[:, off:off + 1, :], a3.shape)
    for s in range(1, 8 // blk):
        cand = jnp.broadcast_to(a3[:, s * blk + off:s * blk + off + 1, :], a3.shape)
        out = jnp.where(sub == s, cand, out)
    return out.reshape(c, n)


def _head_stack(a, head_w):
    lane_head = lax.broadcasted_iota(jnp.int32, a.shape, 1) // head_w
    zero = jnp.zeros_like(a)
    return jnp.concatenate([jnp.where(lane_head == h, a, zero) for h in range(GLA_HEADS)], axis=0)


def _run_skewed(stage_gens, skew=1):
    done = [False] * len(stage_gens)
    tick = 0
    while not all(done):
        for u, gen in enumerate(stage_gens):
            if tick >= u * skew and not done[u]:
                try:
                    next(gen)
                except StopIteration:
                    done[u] = True
        tick += 1


def _gla_phase_a(c, q_ref, k_ref, v_ref, la_ref, tri_ref, pair_ref, oacc_ref, qst_ref, ds_ref, dc_ref):
    C = GLA_CHUNK
    r0 = pl.multiple_of(c * C, C)
    q = q_ref[pl.ds(r0, C), :]
    k = k_ref[pl.ds(r0, C), :]
    la = la_ref[pl.ds(r0, C), :]
    la_f, la_b = la[:, :GLA_QK], la[:, GLA_QK:]
    cums = []
    for d, la_d in enumerate((la_f, la_b)):
        hi, mid, lo = _split3(la_d)
        tri = tri_ref[d]
        cums.append(_dot(tri, hi) + _dot(tri, mid) + _dot(tri, lo))
    cum_f, cum_b = cums
    k_stack = _head_stack(k.astype(BF16), GLA_DK)
    yield

    row = lax.broadcasted_iota(jnp.int32, (C, 1), 0)
    pair_m = pair_ref[...]
    att = jnp.where(pair_m == 0, 2.0 * _dot_nt(q.astype(BF16), k_stack), 0.0)
    m = C // 2
    while m >= 1:
        blk = 2 * m
        upper = (row % blk) >= m
        if m == 1:
            qm = (q * jnp.exp2(jnp.where(upper, la_f, la_b))).astype(BF16)
            km_stack = k_stack
        else:
            ref_f = _block_row_bcast(cum_f, blk, m - 1)
            ref_b = _block_row_bcast(cum_b, blk, m)
            eq = jnp.where(upper, cum_f - ref_f, cum_b - ref_b)
            ek = jnp.where(upper, ref_b - cum_b, ref_f - cum_f)
            qm = (q * jnp.exp2(eq)).astype(BF16)
            km_stack = _head_stack((k * jnp.exp2(ek)).astype(BF16), GLA_DK)
        yield
        att = jnp.where(pair_m == m, _dot_nt(qm, km_stack), att)
        m //= 2

    v = v_ref[pl.ds(r0, C), :]
    v_stack = _head_stack(v, GLA_DV)
    qcs = [(q * jnp.exp2(cum)).astype(BF16) for cum in cums]
    yield
    oacc_ref[pl.ds(r0, C), :] = _dot(att.astype(BF16), v_stack)
    for d, cum in enumerate(cums):
        last = cum[C - 1:C, :] if d == 0 else cum[0:1, :]
        qst_ref[d, c] = _head_stack(qcs[d], GLA_DK)
        kct = (k * jnp.exp2(last - cum)).T.astype(BF16)
        decay_col = jnp.exp2(jnp.broadcast_to(last, (8, GLA_QK))).T[:, 0:1]
        dc_ref[d, c] = jnp.broadcast_to(decay_col, (GLA_QK, GLA_DV))
        yield
        ds_ref[d, c] = jnp.concatenate(
            [_dot(kct[h * GLA_DK:(h + 1) * GLA_DK, :], v[:, h * GLA_DV:(h + 1) * GLA_DV])
             for h in range(GLA_HEADS)], axis=0)


def _gla_phase_c(c, og_ref, gg, oacc_ref, qst_ref, sb_ref, o_ref):
    C = GLA_CHUNK
    r0 = pl.multiple_of(c * C, C)
    q_cat = jnp.concatenate([qst_ref[0, c], qst_ref[1, c]], axis=1)
    s_cat = jnp.concatenate([sb_ref[0, c], sb_ref[1, c]], axis=0)
    inter = _dot(q_cat, s_cat)
    yield
    for h in range(GLA_HEADS):
        sl = slice(h * GLA_DV, (h + 1) * GLA_DV)
        o = oacc_ref[pl.ds(r0, C), sl] + inter[h * C:(h + 1) * C, :]
        o_ref[pl.ds(r0, C), sl] = (_rms(o, gg) * _silu(og_ref[pl.ds(r0, C), sl])).astype(BF16)


def _gla_kernel(*refs, has_state, nseq):
    it = iter(refs)
    q_ref, k_ref, v_ref, la_ref, og_ref = (next(it) for _ in range(5))
    s0_ref = next(it) if has_state else None
    tri_ref, pair_ref, gg_ref, o_ref, sout_ref = (next(it) for _ in range(5))
    oacc_ref, qst_ref, ds_ref, dc_ref, sb_ref, st_ref = (next(it) for _ in range(6))

    n_chunks = q_ref.shape[0] // GLA_CHUNK
    nc = n_chunks // nseq
    per_step = min(GLA_CHUNKS_PER_STEP, n_chunks)

    def phase_a(i, carry):
        _run_skewed([_gla_phase_a(per_step * i + u, q_ref, k_ref, v_ref, la_ref, tri_ref, pair_ref, oacc_ref,
                                  qst_ref, ds_ref, dc_ref) for u in range(per_step)])
        return carry

    lax.fori_loop(0, n_chunks // per_step, phase_a, 0)

    for s in range(nseq):
        for d in range(2):
            for h in range(GLA_HEADS):
                rows = slice(h * GLA_DK, (h + 1) * GLA_DK)
                st_ref[d, rows, :] = s0_ref[s, d, h] if has_state else jnp.zeros((GLA_DK, GLA_DV), F32)

        def phase_b(c, carry, c0=s * nc):
            for d in range(2):
                cc = c0 + (c if d == 0 else nc - 1 - c)
                state = st_ref[d]
                sb_ref[d, cc] = state.astype(BF16)
                st_ref[d] = state * dc_ref[d, cc] + ds_ref[d, cc]
            return carry

        lax.fori_loop(0, nc, phase_b, 0)
        for d in range(2):
            for h in range(GLA_HEADS):
                sout_ref[s, d, h] = st_ref[d, h * GLA_DK:(h + 1) * GLA_DK, :]

    gg = gg_ref[...]

    def phase_c(i, carry):
        _run_skewed([_gla_phase_c(per_step * i + u, og_ref, gg, oacc_ref, qst_ref, sb_ref, o_ref)
                     for u in range(per_step)])
        return carry

    lax.fori_loop(0, n_chunks // per_step, phase_c, 0)


def _gla(gq, gk, gv, la, og, state, layer_j, tri, gg, nbatch, t, row_off):
    nseq = max(1, min(nbatch, GLA_CHUNKS_PER_STEP * GLA_CHUNK // t))
    rows = nseq * t
    nc = rows // GLA_CHUNK
    assert nbatch % nseq == 0 and row_off % rows == 0 and t % GLA_CHUNK == 0
    assert nc % min(GLA_CHUNKS_PER_STEP, nc) == 0
    b0 = row_off // rows
    seq = lambda w: pl.BlockSpec((rows, w), lambda b: (b0 + b, 0))
    in_specs = [seq(GLA_QK), seq(GLA_QK), seq(GLA_VW), seq(2 * GLA_QK), seq(GLA_VW)]
    args = [gq, gk, gv, la, og]
    if state is not None:
        in_specs.append(pl.BlockSpec((nseq, None, 2, GLA_HEADS, GLA_DK, GLA_DV),
                                     lambda b: (b, layer_j, 0, 0, 0, 0)))
        args.append(state)
    tri, pair_m = tri
    in_specs += [_resident(tri.shape, _const_map(3)), _resident(pair_m.shape, _const_map(2)),
                 _layer_spec(gg, layer_j)]
    args += [tri, pair_m, gg]
    return pl.pallas_call(
        functools.partial(_gla_kernel, has_state=state is not None, nseq=nseq),
        out_shape=[jax.ShapeDtypeStruct((nbatch * t, GLA_VW), BF16),
                   jax.ShapeDtypeStruct((nbatch, 2, GLA_HEADS, GLA_DK, GLA_DV), F32)],
        grid=(nbatch // nseq,),
        in_specs=in_specs,
        out_specs=[pl.BlockSpec((rows, GLA_VW), lambda b: (b, 0)),
                   pl.BlockSpec((nseq, 2, GLA_HEADS, GLA_DK, GLA_DV), lambda b: (b, 0, 0, 0, 0))],
        scratch_shapes=[
            pltpu.VMEM((rows, GLA_VW), F32),
            pltpu.VMEM((2, nc, GLA_HEADS * GLA_CHUNK, GLA_QK), BF16),
            pltpu.VMEM((2, nc, GLA_QK, GLA_DV), F32),
            pltpu.VMEM((2, nc, GLA_QK, GLA_DV), F32),
            pltpu.VMEM((2, nc, GLA_QK, GLA_DV), BF16),
            pltpu.VMEM((2, GLA_QK, GLA_DV), F32),
        ],
        compiler_params=_cparams("arbitrary"),
        name="gla_scan",
    )(*args)


def _mla_kernel(*refs, has_ctx, nseq):
    it = iter(refs)
    q_ref, k_ref, v_ref = next(it), next(it), next(it)
    kc_ref, vc_ref = (next(it), next(it)) if has_ctx else (None, None)
    o_ref = next(it)
    tq = q_ref.shape[0] // nseq
    tk = k_ref.shape[0] // nseq
    def scores(s, hd):
        sl = slice(hd * HEAD_PAD, (hd + 1) * HEAD_PAD)
        q = q_ref[s * tq:(s + 1) * tq, sl]
        own = _dot_nt(k_ref[s * tk:(s + 1) * tk, sl], q)
        return [own, _dot_nt(kc_ref[:, sl], q)] if has_ctx else [own]

    def softmax(ss):
        mx = functools.reduce(jnp.maximum, [jnp.max(x, axis=0, keepdims=True) for x in ss])
        ps = [jnp.exp2(x - mx) for x in ss]
        den = sum(jnp.sum(p, axis=0, keepdims=True) for p in ps)
        return [p.astype(BF16) for p in ps], den

    def values(s, hd, ps, den):
        vrows = slice(hd * MLA_V, (hd + 1) * MLA_V)
        o = _dot(v_ref[vrows, s * tk:(s + 1) * tk], ps[0])
        if has_ctx:
            o = o + _dot(vc_ref[vrows, :], ps[1])
        return o / den

    items = [(s, hd) for s in range(nseq) for hd in range(MLA_HEADS)]
    s_q, p_q, outs = {}, {}, []
    for step in range(len(items) + 2):
        if step < len(items):
            s_q[step] = scores(*items[step])
        if 0 <= step - 1 < len(items):
            p_q[step - 1] = softmax(s_q.pop(step - 1))
        if 0 <= step - 2 < len(items):
            outs.append(values(*items[step - 2], *p_q.pop(step - 2)))
    for s in range(nseq):
        heads = outs[s * MLA_HEADS:(s + 1) * MLA_HEADS]
        o_ref[s * tq:(s + 1) * tq, :] = jnp.concatenate(heads, axis=0).T.astype(BF16)


def _mla(qm, km, vm, kc, vc, layer_j, nbatch, t, row_off, tq):
    nseq = max(1, min(nbatch, tq // t)) if kc is None else 1
    tq = min(tq, t)
    nq = t // tq
    qrows, krows = nseq * tq, nseq * t
    assert t % tq == 0 and nbatch % nseq == 0 and row_off % krows == 0 and (nseq == 1 or nq == 1)
    q0, b0 = row_off // qrows, row_off // krows
    qspec = lambda w: pl.BlockSpec((qrows, w), lambda b, i: (q0 + b * nq + i, 0))
    kspec = lambda w: pl.BlockSpec((krows, w), lambda b, i: (b0 + b, 0))
    vspec = pl.BlockSpec((MLA_HEADS * MLA_V, krows), lambda b, i: (0, b0 + b))
    in_specs = [qspec(MLA_HEADS * HEAD_PAD), kspec(MLA_HEADS * HEAD_PAD), vspec]
    args = [qm, km, vm]
    if kc is not None:
        in_specs += [pl.BlockSpec((None, None) + kc.shape[2:], lambda b, i: (layer_j, b, 0, 0)),
                     pl.BlockSpec((None, None) + vc.shape[2:], lambda b, i: (layer_j, b, 0, 0))]
        args += [kc, vc]
    return pl.pallas_call(
        functools.partial(_mla_kernel, has_ctx=kc is not None, nseq=nseq),
        out_shape=jax.ShapeDtypeStruct((nbatch * t, MLA_HEADS * MLA_V), BF16),
        grid=(nbatch // nseq, nq),
        in_specs=in_specs,
        out_specs=pl.BlockSpec((qrows, MLA_HEADS * MLA_V), lambda b, i: (b * nq + i, 0)),
        compiler_params=_cparams("arbitrary", "arbitrary"),
        name="mla_attention",
    )(*args)


def _swap_signed(a):
    pairs = a.reshape(a.shape[:-1] + (a.shape[-1] // 2, 2))
    return jnp.stack([-pairs[..., 1], pairs[..., 0]], axis=-1).reshape(a.shape)


def _swap_pairs(a):
    pairs = a.reshape(a.shape[:-1] + (a.shape[-1] // 2, 2))
    return pairs[..., ::-1].reshape(a.shape)


def _pack_even_weights(cos, sin, even_w_in, gla_gate_w2, gla_gate_b, mla_qa_g, mla_qb_w, mla_qn_g,
                       mla_kva_g, mla_kvb_w, mla_kn_g):
    ne, d, _ = even_w_in.shape
    o = 0
    parts = {}
    for name, width in (("q", GLA_QK), ("k", GLA_QK), ("v", GLA_VW), ("og", GLA_VW),
                        ("gl", 2 * GLA_GATE_RANK), ("cq", MLA_Q_RANK), ("ckv", MLA_KV_RANK),
                        ("kr", MLA_ROPE)):
        parts[name] = even_w_in[:, :, o:o + width]
        o += width
    zeros = lambda n: jnp.zeros((ne, d, n), even_w_in.dtype)
    kr, kr_sw = parts["kr"], _swap_signed(parts["kr"])
    blk_a = jnp.concatenate([parts["gl"], zeros(MLA_NOPE - 2 * GLA_GATE_RANK), kr, kr_sw], axis=2)
    blk_b = jnp.concatenate([zeros(MLA_NOPE), kr_sw, kr], axis=2)
    w_in = jnp.concatenate([parts["cq"], parts["ckv"], blk_a, blk_b, parts["q"], parts["k"], parts["v"],
                            parts["og"]], axis=2).astype(BF16)
    assert w_in.shape[2] == _C_END

    w2_f = jnp.pad(gla_gate_w2[:, 0], ((0, 0), (0, 0), (0, GLA_QK)))
    w2_b = jnp.pad(gla_gate_w2[:, 1], ((0, 0), (0, 0), (GLA_QK, 0)))
    w2 = jnp.pad(jnp.concatenate([w2_f, w2_b], axis=1), ((0, 0), (0, HEAD_PAD - 2 * GLA_GATE_RANK), (0, 0)))

    pad_head = lambda a: jnp.pad(a, [(0, 0)] * (a.ndim - 1) + [(0, HEAD_PAD - a.shape[-1])])
    qb = jnp.concatenate([mla_qb_w, _swap_signed(mla_qb_w[..., MLA_NOPE:])], axis=-1)
    qb = qb.reshape(ne, MLA_Q_RANK, MLA_HEADS * HEAD_PAD)
    kvb_k = pad_head(mla_kvb_w[..., :MLA_NOPE]).reshape(ne, MLA_KV_RANK, MLA_HEADS * HEAD_PAD)
    kvb_v = mla_kvb_w[..., MLA_NOPE:].reshape(ne, MLA_KV_RANK, MLA_HEADS * MLA_V)

    def rot_tables(g):
        n = cos.shape[0]
        g_n = jnp.broadcast_to(g[:, None, :MLA_NOPE], (ne, n, MLA_NOPE))
        g_c = g[:, None, MLA_NOPE:] * cos
        g_s = _swap_pairs(g[:, MLA_NOPE:])[:, None, :] * sin
        return g_n, g_c, g_s

    qn_n, qn_c, qn_s = rot_tables(mla_qn_g)
    kn_n, kn_c, kn_s = rot_tables(mla_kn_g)
    kzero = jnp.zeros_like(kn_n)
    return {
        "w_in": w_in,
        "w2": w2.astype(BF16),
        "gate_b": gla_gate_b.reshape(ne, 1, 2 * GLA_QK),
        "qa_g": mla_qa_g[:, None, :],
        "qb": qb.astype(BF16),
        "kva_g": mla_kva_g[:, None, :],
        "kvb_k": kvb_k.astype(BF16),
        "kvb_v": kvb_v.astype(BF16),
        "kn_g": pad_head(mla_kn_g[:, :MLA_NOPE])[:, None, :],
        "kn_g_ctx": jnp.concatenate([mla_kn_g, mla_kn_g[:, MLA_NOPE:]], axis=1)[:, None, :],
        "t_q": jnp.concatenate([qn_n, qn_c, qn_s], axis=2) * (MLA_QK_DIM ** -0.5 * math.log2(math.e)),
        "t_a": jnp.concatenate([kzero, kn_c, kn_s], axis=2),
        "t_b": jnp.concatenate([kzero, kn_s, kn_c], axis=2),
    }


def _rope_tables(n, tm):
    pairs = MLA_ROPE // 4
    pos = np.arange(n)
    inv = ROPE_BASE ** (-jnp.arange(pairs, dtype=F32) / pairs)
    row = jnp.asarray(pos // GRID_W, F32)
    col = jnp.asarray(pos % GRID_W, F32)
    ang = jnp.concatenate([row[:, None] * inv, col[:, None] * inv], axis=-1)
    cos = jnp.repeat(jnp.cos(ang), 2, axis=-1)
    sin = jnp.repeat(jnp.sin(ang), 2, axis=-1)
    cos = jnp.pad(cos, ((tm, 0), (0, 0)), constant_values=1.0)
    sin = jnp.pad(sin, ((tm, 0), (0, 0)))
    return cos, sin


def _tri_masks():
    r = np.arange(GLA_CHUNK)
    lower = (r[None, :] <= r[:, None]).astype(np.float32)
    diff = r[:, None] ^ r[None, :]
    level = np.where(diff > 0, 1 << np.floor(np.log2(np.maximum(diff, 1))).astype(np.int64), 0)
    pair_m = np.tile(level, (1, GLA_HEADS)).astype(np.int32)
    return jnp.asarray(np.stack([lower, lower.T]), BF16), jnp.asarray(pair_m)


def _pick_tile(pref, *sizes):
    tm = pref
    while any(s % tm for s in sizes):
        tm //= 2
    return tm


def kernel(x_prompt, x_sample, c, c_ctx, cache_ckv, cache_krope, state_gla, ada_w, ada_b, norm_g,
           ffn1_wg, ffn1_wu, ffn1_wd, ffn2_wg, ffn2_wu, ffn2_wd, even_w_in, even_w_out,
           gla_gate_w2, gla_gate_b, gla_norm_g, mla_qa_g, mla_qb_w, mla_kva_g, mla_kvb_w,
           mla_qn_g, mla_kn_g, odd_w_in, odd_v_g, odd_ws, odd_bs, odd_w_out):
    batch, seq, d = x_prompt.shape
    dec_batch, dec_seq, _ = x_sample.shape
    depth = ada_w.shape[0]
    n_even = even_w_in.shape[0]
    m_ctx, m_dec = batch * seq, dec_batch * dec_seq
    assert 1 + dec_batch <= COND_PAD and seq % CMLP_CHUNK == 0 and dec_seq % CMLP_CHUNK == 0

    rows = _Rows(m_ctx, dec_batch, dec_seq, _pick_tile(ROW_TILE, m_ctx, dec_seq))

    cond = jnp.concatenate([c_ctx[None, :], c, jnp.zeros((COND_PAD - 1 - dec_batch, d), F32)], axis=0)
    mod = _modulation_all(cond, ada_w, ada_b).reshape(depth, COND_PAD, N_MOD, d)

    cos_t, sin_t = _rope_tables(dec_seq, rows.tm)
    tri = _tri_masks()
    krope_blk = jnp.concatenate([jnp.zeros(cache_krope.shape[:-1] + (MLA_NOPE,), F32), cache_krope,
                                 cache_krope], axis=-1)
    even_wts = _pack_even_weights(cos_t, sin_t, even_w_in, gla_gate_w2, gla_gate_b, mla_qa_g, mla_qb_w,
                                  mla_qn_g, mla_kva_g, mla_kvb_w, mla_kn_g)
    kc, vc = _ctx_kv(cache_ckv, krope_blk, even_wts["kvb_k"], even_wts["kvb_v"], even_wts["kn_g_ctx"])

    ffn1 = (ffn1_wg, ffn1_wu, ffn1_wd)
    ffn2 = (ffn2_wg, ffn2_wu, ffn2_wd)
    w_out_bf = even_w_out.astype(BF16)
    odd_wts = (odd_w_in, odd_v_g[:, None, :], odd_ws.astype(BF16), jnp.swapaxes(odd_bs, 1, 2), odd_w_out)
    norms = norm_g.reshape(depth * 3, 1, d)
    gg = gla_norm_g[:, None, :]

    x = (x_prompt.reshape(m_ctx, d), x_sample.reshape(m_dec, d))
    new_ckv, new_krope, new_gla = [], [], []
    for i in range(depth):
        j = i // 2
        x = _ffn(x, mod, i, 0, (norms, 3 * i), *ffn1, rows)
        g_mix = (norms, 3 * i + 1)
        premix = None
        if i % 2 == 0:
            gq, gk, gv, og, la, qm, km, vm, ckv, kr = _even_in(x, mod, i, g_mix, even_wts, j, rows)
            gla_ctx, st = _gla(gq, gk, gv, la, og, None, j, tri, gg, batch, seq, 0)
            gla_dec, _ = _gla(gq, gk, gv, la, og, state_gla, j, tri, gg, dec_batch, dec_seq, m_ctx)
            mla_ctx = _mla(qm, km, vm, None, None, j, batch, seq, 0, MLA_QUERY_TILE)
            mla_dec = _mla(qm, km, vm, kc, vc, j, dec_batch, dec_seq, m_ctx, _pick_tile(MLA_QUERY_TILE, dec_seq))
            premix = ((gla_ctx, gla_dec), (mla_ctx, mla_dec), w_out_bf, j)
            new_ckv.append(ckv.reshape(batch, seq, MLA_KV_RANK))
            new_krope.append(kr.reshape(batch, seq, MLA_ROPE))
            new_gla.append(st)
        else:
            x = _odd_mixer(x, mod, i, j, g_mix, *odd_wts, rows)
        x = _ffn(x, mod, i, 6, (norms, 3 * i + 2), *ffn2, rows, split_out=(i == depth - 1),
                 premix=premix)

    y_prompt, y_sample = x
    return (y_prompt.reshape(batch, seq, d), y_sample.reshape(dec_batch, dec_seq, d),
            jnp.stack(new_ckv, axis=1), jnp.stack(new_krope, axis=1), jnp.stack(new_gla, axis=1))
```

```python
import functools
import math

import numpy as np
import jax
import jax.numpy as jnp
from jax import lax
from jax.experimental import pallas as pl
from jax.experimental.pallas import tpu as pltpu

F32 = jnp.float32
BF16 = jnp.bfloat16

EPS = 1e-6
N_MOD = 9
GRID_W = 64
ROPE_BASE = 10000.0
GLA_HEADS = 4
GLA_DK = 64
GLA_DV = 128
GLA_QK = GLA_HEADS * GLA_DK
GLA_VW = GLA_HEADS * GLA_DV
GLA_GATE_RANK = 16
GLA_GATE_NORM = 16.0
GLA_CHUNK = 64
GLA_CHUNKS_PER_STEP = 16
MLA_HEADS = 8
MLA_NOPE = 64
MLA_ROPE = 32
MLA_V = 64
MLA_QK_DIM = MLA_NOPE + MLA_ROPE
MLA_Q_RANK = 384
MLA_KV_RANK = 256
HEAD_PAD = 128
MLA_QUERY_TILE = 512
MLA_SHARED_ROWS = 1024
CMLP_CHUNK = 128
CMLP_GROUPS = 4
COND_PAD = 16
SUBLANES = 8
ADALN_COL_TILES = 4

VMEM_LIMIT = 56 * 1024 * 1024
ROW_TILE = 512


def _cparams(*sem):
    return pltpu.CompilerParams(dimension_semantics=sem, vmem_limit_bytes=VMEM_LIMIT)


def _resident(shape, index_map):
    return pl.BlockSpec(shape, index_map, pipeline_mode=pl.Buffered(1))


def _const_map(nd):
    return lambda *_: (0,) * nd


def _layer_spec(stack, layer):
    nd = stack.ndim - 1
    return _resident((None,) + stack.shape[1:], lambda *_: (layer,) + (0,) * nd)


def _silu(x):
    return x * jax.nn.sigmoid(x)


def _rms(x, g):
    return x * lax.rsqrt(jnp.mean(x * x, axis=-1, keepdims=True) + EPS) * g


def _modulate(x, g, shift, scale):
    return _rms(x, g) * (1.0 + scale) + shift


def _dot(a, b):
    return jnp.dot(a, b, preferred_element_type=F32)


def _dot_nt(a, b):
    return lax.dot_general(a, b, (((1,), (1,)), ((), ())), preferred_element_type=F32)


def _mod_kernel(c_ref, w_ref, b_ref, o_ref):
    s = _silu(c_ref[...]).astype(BF16)
    o_ref[...] = _dot(s, w_ref[...].astype(BF16)) + b_ref[...]


def _modulation_all(cond, ada_w, ada_b):
    depth, d, n = ada_w.shape
    tn = n // ADALN_COL_TILES
    return pl.pallas_call(
        _mod_kernel,
        out_shape=jax.ShapeDtypeStruct((depth, COND_PAD, n), F32),
        grid=(depth, n // tn),
        in_specs=[
            pl.BlockSpec((COND_PAD, d), lambda l, j: (0, 0)),
            pl.BlockSpec((None, d, tn), lambda l, j: (l, 0, j)),
            pl.BlockSpec((None, 1, tn), lambda l, j: (l, 0, j)),
        ],
        out_specs=pl.BlockSpec((None, COND_PAD, tn), lambda l, j: (l, 0, j)),
        compiler_params=_cparams("arbitrary", "arbitrary"),
        name="adaln_modulation",
    )(cond, ada_w, ada_b.reshape(depth, 1, n))


class _Rows:
    def __init__(self, m_ctx, dec_batch, dec_seq, tm):
        assert m_ctx % tm == 0 and dec_seq % tm == 0
        self.tm = tm
        self.m_ctx = m_ctx
        self.m = m_ctx + dec_batch * dec_seq
        self.nb_ctx = m_ctx // tm
        self.nb_seq = dec_seq // tm
        self.nblocks = self.m // tm

    def cond(self, i):
        return jnp.where(i < self.nb_ctx, 0, 1 + (i - self.nb_ctx) // self.nb_seq)

    def rope_block(self, i):
        return jnp.where(i < self.nb_ctx, 0, 1 + (i - self.nb_ctx) % self.nb_seq)

    def mod_spec(self, layer, d):
        return pl.BlockSpec((None, None, N_MOD, d), lambda i: (layer, self.cond(i), 0, 0))

    def row_spec(self, width):
        return pl.BlockSpec((self.tm, width), lambda i: (i, 0))

    def ctx_spec(self, width):
        return pl.BlockSpec((self.tm, width), lambda i: (jnp.minimum(i, self.nb_ctx - 1), 0))

    def dec_spec(self, width):
        return pl.BlockSpec((self.tm, width), lambda i: (jnp.maximum(i - self.nb_ctx, 0), 0))

    def split_specs(self, width):
        return [self.ctx_spec(width), self.dec_spec(width)]

    def split_shapes(self, width, dtype):
        return [jax.ShapeDtypeStruct((self.m_ctx, width), dtype),
                jax.ShapeDtypeStruct((self.m - self.m_ctx, width), dtype)]


def _load_split(refs, is_ctx):
    if len(refs) == 1:
        return refs[0][...]
    return jnp.where(is_ctx, refs[0][...], refs[1][...])


STAGE_CHUNKS = 16
STAGE_SLOTS = 8


def _stage_weights_bf16(layer, hbm_refs, vmem_refs, stage_refs, sem):
    chunks = []
    for src, dst, stage in zip(hbm_refs, vmem_refs, stage_refs):
        rows = stage.shape[1]
        assert dst.shape[0] % rows == 0
        chunks += [(src, dst, stage, r0) for r0 in range(0, dst.shape[0], rows)]
    slots = sem.shape[0]
    ahead = slots - 1

    def copy(i):
        src, _, stage, r0 = chunks[i]
        return pltpu.make_async_copy(src.at[layer, pl.ds(r0, stage.shape[1]), :], stage.at[i % slots],
                                     sem.at[i % slots])

    for i in range(min(ahead, len(chunks))):
        copy(i).start()
    for i, (_, dst, stage, r0) in enumerate(chunks):
        if i + ahead < len(chunks):
            copy(i + ahead).start()
        copy(i).wait()
        dst[pl.ds(r0, stage.shape[1]), :] = stage[i % slots].astype(BF16)


def _ffn_kernel(*refs, layer, row0, nb_ctx, split_in, split_out, premix):
    it = iter(refs)
    x_refs = [next(it) for _ in range(2 if split_in else 1)]
    mod_ref, g_ref = next(it), next(it)
    w_hbm = [next(it) for _ in range(3)]
    if premix:
        a_refs, b_refs = [next(it), next(it)], [next(it), next(it)]
        wo_ref = next(it)
    o_refs = [next(it) for _ in range(2 if split_out else 1)]
    wg_ref, wu_ref, wd_ref = w_vmem = [next(it) for _ in range(3)]
    stage_in, stage_down, sem = next(it), next(it), next(it)

    @pl.when(pl.program_id(0) == 0)
    def _():
        _stage_weights_bf16(layer, w_hbm, w_vmem, [stage_in, stage_in, stage_down], sem)

    is_ctx = pl.program_id(0) < nb_ctx
    shift = mod_ref[row0:row0 + 1, :]
    scale = mod_ref[row0 + 1:row0 + 2, :]
    gate = mod_ref[row0 + 2:row0 + 3, :]
    x = _load_split(x_refs, is_ctx)
    if premix:
        half = wo_ref.shape[0] // 2
        mix = _dot(_load_split(a_refs, is_ctx), wo_ref[:half, :])
        mix = mix + _dot(_load_split(b_refs, is_ctx), wo_ref[half:, :])
        x = x + mod_ref[5:6, :] * mix
    h = _modulate(x, g_ref[...], shift, scale).astype(BF16)
    a = _dot(h, wg_ref[...])
    u = _dot(h, wu_ref[...])
    act = (_silu(a) * u).astype(BF16)
    y = _dot(act, wd_ref[...])
    out = x + (0.5 * gate) * y
    if split_out:
        @pl.when(is_ctx)
        def _():
            o_refs[0][...] = out

        @pl.when(jnp.logical_not(is_ctx))
        def _():
            o_refs[1][...] = out
    else:
        o_refs[0][...] = out


def _ffn(xs, mod, layer, row0, g, wg, wu, wd, rows, split_out=False, premix=None):
    split_in = isinstance(xs, (tuple, list))
    xs = list(xs) if split_in else [xs]
    d = xs[0].shape[1]
    f = wg.shape[2]
    hbm = pl.BlockSpec(memory_space=pl.ANY)
    in_specs = (rows.split_specs(d) if split_in else [rows.row_spec(d)]) + [
        rows.mod_spec(layer, d),
        _layer_spec(*g),
        hbm, hbm, hbm,
    ]
    args = xs + [mod, g[0], wg, wu, wd]
    if premix is not None:
        gla, mla, w_out, j = premix
        in_specs += rows.split_specs(gla[0].shape[1]) + rows.split_specs(mla[0].shape[1])
        in_specs.append(_layer_spec(w_out, j))
        args += list(gla) + list(mla) + [w_out]
    return pl.pallas_call(
        functools.partial(_ffn_kernel, layer=layer, row0=row0, nb_ctx=rows.nb_ctx, split_in=split_in,
                          split_out=split_out, premix=premix is not None),
        out_shape=rows.split_shapes(d, F32) if split_out else jax.ShapeDtypeStruct((rows.m, d), F32),
        grid=(rows.nblocks,),
        in_specs=in_specs,
        out_specs=rows.split_specs(d) if split_out else rows.row_spec(d),
        scratch_shapes=[
            pltpu.VMEM((d, f), BF16), pltpu.VMEM((d, f), BF16), pltpu.VMEM((f, d), BF16),
            pltpu.VMEM((STAGE_SLOTS, d // STAGE_CHUNKS, f), F32),
            pltpu.VMEM((STAGE_SLOTS, f // STAGE_CHUNKS, d), F32),
            pltpu.SemaphoreType.DMA((STAGE_SLOTS,)),
        ],
        compiler_params=_cparams("arbitrary"),
        name="ffn_swiglu",
    )(*args)


def _gelu_tanh(x):
    c = math.sqrt(2.0 / math.pi)
    return x * (0.5 * (1.0 + jnp.tanh(c * (x + 0.044715 * (x * x * x)))))


def _odd_kernel(x_ref, mod_ref, g_ref, win_hbm, vg_ref, ws_ref, bst_ref, wout_hbm, o_ref,
                win_ref, wout_ref, stage_in, stage_out, sem, *, layer_j):
    @pl.when(pl.program_id(0) == 0)
    def _():
        _stage_weights_bf16(layer_j, [win_hbm, wout_hbm], [win_ref, wout_ref], [stage_in, stage_out], sem)

    x = x_ref[...]
    tm = x.shape[0]
    shift, scale, gate = mod_ref[3:4, :], mod_ref[4:5, :], mod_ref[5:6, :]
    h = _modulate(x, g_ref[...], shift, scale).astype(BF16)
    uv = _gelu_tanh(_dot(h, win_ref[...]))
    width = uv.shape[1] // 2
    u = uv[:, :width]
    v = _rms(uv[:, width:], vg_ref[...]).astype(BF16)
    gw = width // CMLP_GROUPS
    chunks = []
    for c in range(tm // CMLP_CHUNK):
        r0 = c * CMLP_CHUNK
        groups = []
        for g in range(CMLP_GROUPS):
            vg = v[r0:r0 + CMLP_CHUNK, g * gw:(g + 1) * gw]
            mixed = _dot(ws_ref[g], vg) + bst_ref[:, g:g + 1]
            groups.append(u[r0:r0 + CMLP_CHUNK, g * gw:(g + 1) * gw] * mixed)
        chunks.append(jnp.concatenate(groups, axis=1))
    z = jnp.concatenate(chunks, axis=0).astype(BF16)
    o_ref[...] = x + gate * _dot(z, wout_ref[...])


def _odd_mixer(x, mod, layer, layer_j, g, w_in, v_g, w_s, b_s_t, w_out, rows):
    m, d = x.shape
    n_in, width = w_in.shape[2], w_out.shape[1]
    hbm = pl.BlockSpec(memory_space=pl.ANY)
    return pl.pallas_call(
        functools.partial(_odd_kernel, layer_j=layer_j),
        out_shape=jax.ShapeDtypeStruct((m, d), F32),
        grid=(rows.nblocks,),
        in_specs=[
            rows.row_spec(d),
            rows.mod_spec(layer, d),
            _layer_spec(*g),
            hbm, _layer_spec(v_g, layer_j), _layer_spec(w_s, layer_j), _layer_spec(b_s_t, layer_j), hbm,
        ],
        out_specs=rows.row_spec(d),
        scratch_shapes=[
            pltpu.VMEM((d, n_in), BF16), pltpu.VMEM((width, d), BF16),
            pltpu.VMEM((STAGE_SLOTS, d // STAGE_CHUNKS, n_in), F32),
            pltpu.VMEM((STAGE_SLOTS, width // STAGE_CHUNKS, d), F32),
            pltpu.SemaphoreType.DMA((STAGE_SLOTS,)),
        ],
        compiler_params=_cparams("arbitrary"),
        name="odd_gmlp",
    )(x, mod, g[0], w_in, v_g, w_s, b_s_t, w_out)


_C_CQ = 0
_C_CKV = _C_CQ + MLA_Q_RANK
_C_KA = _C_CKV + MLA_KV_RANK
_C_KB = _C_KA + HEAD_PAD
_C_Q = _C_KB + HEAD_PAD
_C_K = _C_Q + GLA_QK
_C_V = _C_K + GLA_QK
_C_OG = _C_V + GLA_VW
_C_END = _C_OG + GLA_VW


def _head_scale(x, live):
    ss = jnp.sum(jnp.where(live, x * x, 0.0), axis=-1, keepdims=True)
    return lax.rsqrt(ss * (1.0 / MLA_QK_DIM) + EPS)


def _even_in_kernel(x_ref, mod_ref, g_ref, win_ref, w2_ref, gb_ref, qag_ref, qb_ref, kvag_ref,
                    kvbk_ref, kvbv_ref, kng_ref, tq_ref, ta_ref, tb_ref,
                    gq_ref, gk_ref, gv_ref, og_ref, la_ref, qm_ref, km_ref, vm_ref, ckv_ref, kr_ref,
                    *, nb_ctx):
    is_ctx = pl.program_id(0) < nb_ctx
    shift, scale = mod_ref[3:4, :], mod_ref[4:5, :]
    h = _modulate(x_ref[...], g_ref[...], shift, scale).astype(BF16)

    proj_q = _dot(h, win_ref[:, _C_CQ:_C_CKV])
    proj_kv = _dot(h, win_ref[:, _C_CKV:_C_Q])
    ckv = _rms(proj_kv[:, :MLA_KV_RANK], kvag_ref[...])
    ka = proj_kv[:, _C_KA - _C_CKV:_C_KB - _C_CKV]
    kb = proj_kv[:, _C_KB - _C_CKV:]

    @pl.when(is_ctx)
    def _():
        ckv_ref[...] = ckv
        kr_ref[...] = ka[:, MLA_NOPE:MLA_QK_DIM]

    cq = _rms(proj_q, qag_ref[...]).astype(BF16)
    ckv_bf = ckv.astype(BF16)
    qn = _dot(cq, qb_ref[...])
    kn = _dot(ckv_bf, kvbk_ref[...])
    vm_t = _dot(ckv_bf, kvbv_ref[...])
    z = _dot(ka.astype(BF16), w2_ref[...]) + gb_ref[...]

    lane = lax.broadcasted_iota(jnp.int32, ka.shape, 1)
    live = lane < MLA_QK_DIM
    tq = tq_ref[...]

    def q_heads(heads):
        for hd in heads:
            sl = slice(hd * HEAD_PAD, (hd + 1) * HEAD_PAD)
            qh = qn[:, sl]
            qm_ref[:, sl] = (qh * tq * _head_scale(qh, live)).astype(BF16)

    is_rope = live & (lane >= MLA_NOPE)
    ss_rope = jnp.sum(jnp.where(is_rope, ka * ka, 0.0), axis=-1, keepdims=True)
    rot = ka * ta_ref[...] + kb * tb_ref[...]
    kng = kng_ref[...]

    def k_heads(heads):
        for hd in heads:
            sl = slice(hd * HEAD_PAD, (hd + 1) * HEAD_PAD)
            kh = kn[:, sl]
            ss = jnp.sum(kh * kh, axis=-1, keepdims=True) + ss_rope
            r = lax.rsqrt(ss * (1.0 / MLA_QK_DIM) + EPS)
            km_ref[:, sl] = ((kh * kng + rot) * r).astype(BF16)

    half = MLA_HEADS // 2
    gqk = _dot(h, win_ref[:, _C_Q:_C_V])
    q_heads(range(half))
    gq_ref[...] = gqk[:, :GLA_QK] * (GLA_DK ** -0.5)
    gk_ref[...] = gqk[:, GLA_QK:]
    gv = _dot(h, win_ref[:, _C_V:_C_OG])
    q_heads(range(half, MLA_HEADS))
    gv_ref[...] = gv.astype(BF16)
    og = _dot(h, win_ref[:, _C_OG:])
    k_heads(range(half))
    og_ref[...] = og
    vm_ref[...] = vm_t.T.astype(BF16)
    k_heads(range(half, MLA_HEADS))
    la_ref[...] = ((jnp.minimum(z, 0.0) - jnp.log(1.0 + jnp.exp(-jnp.abs(z))))
                   * (math.log2(math.e) / GLA_GATE_NORM))


def _even_in(x, mod, layer, g, wts, layer_j, rows):
    m, d = x.shape
    tm = rows.tm
    rope_spec = pl.BlockSpec((None, tm, HEAD_PAD), lambda i: (layer_j, rows.rope_block(i), 0))
    widths = [GLA_QK, GLA_QK, GLA_VW, GLA_VW, 2 * GLA_QK, MLA_HEADS * HEAD_PAD, MLA_HEADS * HEAD_PAD]
    dtypes = [F32, F32, BF16, F32, F32, BF16, BF16]
    names = ["w_in", "w2", "gate_b", "qa_g", "qb", "kva_g", "kvb_k", "kvb_v", "kn_g"]
    return pl.pallas_call(
        functools.partial(_even_in_kernel, nb_ctx=rows.nb_ctx),
        out_shape=[jax.ShapeDtypeStruct((m, w), t) for w, t in zip(widths, dtypes)]
        + [jax.ShapeDtypeStruct((MLA_HEADS * MLA_V, m), BF16),
           jax.ShapeDtypeStruct((rows.m_ctx, MLA_KV_RANK), F32),
           jax.ShapeDtypeStruct((rows.m_ctx, MLA_ROPE), F32)],
        grid=(rows.nblocks,),
        in_specs=[rows.row_spec(d), rows.mod_spec(layer, d), _layer_spec(*g)]
        + [_layer_spec(wts[n], layer_j) for n in names] + [rope_spec] * 3,
        out_specs=[rows.row_spec(w) for w in widths]
        + [pl.BlockSpec((MLA_HEADS * MLA_V, tm), lambda i: (0, i)),
           rows.ctx_spec(MLA_KV_RANK), rows.ctx_spec(MLA_ROPE)],
        compiler_params=_cparams("arbitrary"),
        name="even_in_proj",
    )(x, mod, g[0], *[wts[n] for n in names], wts["t_q"], wts["t_a"], wts["t_b"])


def _ctx_kv_kernel(ckv_ref, kr_ref, kvbk_ref, kvbv_ref, kng_ref, k_ref, v_ref):
    ckv_bf = ckv_ref[...].astype(BF16)
    v_ref[...] = _dot(ckv_bf, kvbv_ref[...]).T.astype(BF16)
    kn = _dot(ckv_bf, kvbk_ref[...])
    kr = kr_ref[...]
    lane = lax.broadcasted_iota(jnp.int32, kr.shape, 1)
    live = lane < MLA_QK_DIM
    kng = kng_ref[...]
    for hd in range(MLA_HEADS):
        sl = slice(hd * HEAD_PAD, (hd + 1) * HEAD_PAD)
        kh = kn[:, sl] + kr
        k_ref[:, sl] = (kh * kng * _head_scale(kh, live)).astype(BF16)


def _ctx_kv(cache_ckv, cache_krope_blk, kvb_k, kvb_v, kn_g):
    nb, ne, p, r = cache_ckv.shape
    return pl.pallas_call(
        _ctx_kv_kernel,
        out_shape=[jax.ShapeDtypeStruct((ne, nb, p, MLA_HEADS * HEAD_PAD), BF16),
                   jax.ShapeDtypeStruct((ne, nb, MLA_HEADS * MLA_V, p), BF16)],
        grid=(ne, nb),
        in_specs=[
            pl.BlockSpec((None, None, p, r), lambda j, b: (b, j, 0, 0)),
            pl.BlockSpec((None, None, p, HEAD_PAD), lambda j, b: (b, j, 0, 0)),
            pl.BlockSpec((None,) + kvb_k.shape[1:], lambda j, b: (j, 0, 0)),
            pl.BlockSpec((None,) + kvb_v.shape[1:], lambda j, b: (j, 0, 0)),
            pl.BlockSpec((None,) + kn_g.shape[1:], lambda j, b: (j, 0, 0)),
        ],
        out_specs=[pl.BlockSpec((None, None, p, MLA_HEADS * HEAD_PAD), lambda j, b: (j, b, 0, 0)),
                   pl.BlockSpec((None, None, MLA_HEADS * MLA_V, p), lambda j, b: (j, b, 0, 0))],
        compiler_params=_cparams("arbitrary", "arbitrary"),
        name="ctx_kv",
    )(cache_ckv, cache_krope_blk, kvb_k, kvb_v, kn_g)


def _split3(x):
    hi = x.astype(BF16)
    r = x - hi.astype(F32)
    mid = r.astype(BF16)
    lo = (r - mid.astype(F32)).astype(BF16)
    return hi, mid, lo


def _block_row_bcast(a, blk, off):
    c, n = a.shape
    if blk >= 8:
        pieces = [jnp.broadcast_to(a[b * blk + off:b * blk + off + 1, :], (blk, n))
                  for b in range(c // blk)]
        return pieces[0] if len(pieces) == 1 else jnp.concatenate(pieces, axis=0)
    a3 = a.reshape(c // 8, 8, n)
    sub = lax.broadcasted_iota(jnp.int32, a3.shape, 1) // blk
    out = jnp.broadcast_to(a3[:, off:off + 1, :], a3.shape)
    for s in range(1, 8 // blk):
        cand = jnp.broadcast_to(a3[:, s * blk + off:s * blk + off + 1, :], a3.shape)
        out = jnp.where(sub == s, cand, out)
    return out.reshape(c, n)


def _head_stack(a, head_w):
    lane_head = lax.broadcasted_iota(jnp.int32, a.shape, 1) // head_w
    zero = jnp.zeros_like(a)
    return jnp.concatenate([jnp.where(lane_head == h, a, zero) for h in range(GLA_HEADS)], axis=0)


def _seqs_per_step(max_rows, nbatch, t, row_off):
    nseq = max(1, min(nbatch, max_rows // t))
    while nbatch % nseq or row_off % (nseq * t):
        nseq -= 1
    return nseq


def _run_skewed(stage_gens, skew=1):
    done = [False] * len(stage_gens)
    tick = 0
    while not all(done):
        for u, gen in enumerate(stage_gens):
            if tick >= u * skew and not done[u]:
                try:
                    next(gen)
                except StopIteration:
                    done[u] = True
        tick += 1


def _gla_phase_a(c, q_ref, k_ref, v_ref, la_ref, tri_ref, pair_ref, oacc_ref, qst_ref, ds_ref, dc_ref):
    C = GLA_CHUNK
    r0 = pl.multiple_of(c * C, C)
    q = q_ref[pl.ds(r0, C), :]
    k = k_ref[pl.ds(r0, C), :]
    la = la_ref[pl.ds(r0, C), :]
    la_f, la_b = la[:, :GLA_QK], la[:, GLA_QK:]
    cums = []
    for d, la_d in enumerate((la_f, la_b)):
        hi, mid, lo = _split3(la_d)
        tri = tri_ref[d]
        cums.append(_dot(tri, hi) + _dot(tri, mid) + _dot(tri, lo))
    cum_f, cum_b = cums
    k_stack = _head_stack(k.astype(BF16), GLA_DK)
    yield

    row = lax.broadcasted_iota(jnp.int32, (C, 1), 0)
    pair_m = pair_ref[...]
    att = jnp.where(pair_m == 0, 2.0 * _dot_nt(q.astype(BF16), k_stack), 0.0)
    m = C // 2
    while m >= 1:
        blk = 2 * m
        upper = (row % blk) >= m
        if m == 1:
            qm = (q * jnp.exp2(jnp.where(upper, la_f, la_b))).astype(BF16)
            km_stack = k_stack
        else:
            ref_f = _block_row_bcast(cum_f, blk, m - 1)
            ref_b = _block_row_bcast(cum_b, blk, m)
            eq = jnp.where(upper, cum_f - ref_f, cum_b - ref_b)
            ek = jnp.where(upper, ref_b - cum_b, ref_f - cum_f)
            qm = (q * jnp.exp2(eq)).astype(BF16)
            km_stack = _head_stack((k * jnp.exp2(ek)).astype(BF16), GLA_DK)
        yield
        att = jnp.where(pair_m == m, _dot_nt(qm, km_stack), att)
        m //= 2

    v = v_ref[pl.ds(r0, C), :]
    v_stack = _head_stack(v, GLA_DV)
    qcs = [(q * jnp.exp2(cum)).astype(BF16) for cum in cums]
    yield
    oacc_ref[pl.ds(r0, C), :] = _dot(att.astype(BF16), v_stack)
    for d, cum in enumerate(cums):
        last = cum[C - 1:C, :] if d == 0 else cum[0:1, :]
        qst_ref[d, c] = _head_stack(qcs[d], GLA_DK)
        kct = (k * jnp.exp2(last - cum)).T.astype(BF16)
        decay_col = jnp.exp2(jnp.broadcast_to(last, (SUBLANES, GLA_QK))).T[:, 0:1]
        dc_ref[d, c] = jnp.broadcast_to(decay_col, (GLA_QK, GLA_DV))
        yield
        ds_ref[d, c] = jnp.concatenate(
            [_dot(kct[h * GLA_DK:(h + 1) * GLA_DK, :], v[:, h * GLA_DV:(h + 1) * GLA_DV])
             for h in range(GLA_HEADS)], axis=0)


def _gla_phase_c(c, og_ref, gg, oacc_ref, qst_ref, sb_ref, o_ref):
    C = GLA_CHUNK
    r0 = pl.multiple_of(c * C, C)
    q_cat = jnp.concatenate([qst_ref[0, c], qst_ref[1, c]], axis=1)
    s_cat = jnp.concatenate([sb_ref[0, c], sb_ref[1, c]], axis=0)
    inter = _dot(q_cat, s_cat)
    yield
    for h in range(GLA_HEADS):
        sl = slice(h * GLA_DV, (h + 1) * GLA_DV)
        o = oacc_ref[pl.ds(r0, C), sl] + inter[h * C:(h + 1) * C, :]
        o_ref[pl.ds(r0, C), sl] = (_rms(o, gg) * _silu(og_ref[pl.ds(r0, C), sl])).astype(BF16)


def _gla_kernel(*refs, has_state, nseq):
    it = iter(refs)
    q_ref, k_ref, v_ref, la_ref, og_ref = (next(it) for _ in range(5))
    s0_ref = next(it) if has_state else None
    tri_ref, pair_ref, gg_ref, o_ref, sout_ref = (next(it) for _ in range(5))
    oacc_ref, qst_ref, ds_ref, dc_ref, sb_ref, st_ref = (next(it) for _ in range(6))

    n_chunks = q_ref.shape[0] // GLA_CHUNK
    nc = n_chunks // nseq
    per_step = min(GLA_CHUNKS_PER_STEP, n_chunks)

    def phase_a(i, carry):
        _run_skewed([_gla_phase_a(per_step * i + u, q_ref, k_ref, v_ref, la_ref, tri_ref, pair_ref, oacc_ref,
                                  qst_ref, ds_ref, dc_ref) for u in range(per_step)])
        return carry

    lax.fori_loop(0, n_chunks // per_step, phase_a, 0)

    for s in range(nseq):
        for d in range(2):
            for h in range(GLA_HEADS):
                rows = slice(h * GLA_DK, (h + 1) * GLA_DK)
                st_ref[d, rows, :] = s0_ref[s, d, h] if has_state else jnp.zeros((GLA_DK, GLA_DV), F32)

        def phase_b(c, carry, c0=s * nc):
            for d in range(2):
                cc = c0 + (c if d == 0 else nc - 1 - c)
                state = st_ref[d]
                sb_ref[d, cc] = state.astype(BF16)
                st_ref[d] = state * dc_ref[d, cc] + ds_ref[d, cc]
            return carry

        lax.fori_loop(0, nc, phase_b, 0)
        for d in range(2):
            for h in range(GLA_HEADS):
                sout_ref[s, d, h] = st_ref[d, h * GLA_DK:(h + 1) * GLA_DK, :]

    gg = gg_ref[...]

    def phase_c(i, carry):
        _run_skewed([_gla_phase_c(per_step * i + u, og_ref, gg, oacc_ref, qst_ref, sb_ref, o_ref)
                     for u in range(per_step)])
        return carry

    lax.fori_loop(0, n_chunks // per_step, phase_c, 0)


def _gla(gq, gk, gv, la, og, state, layer_j, tri, gg, nbatch, t, row_off):
    nseq = _seqs_per_step(GLA_CHUNKS_PER_STEP * GLA_CHUNK, nbatch, t, row_off)
    rows = nseq * t
    nc = rows // GLA_CHUNK
    assert nbatch % nseq == 0 and row_off % rows == 0 and t % GLA_CHUNK == 0
    assert nc % min(GLA_CHUNKS_PER_STEP, nc) == 0
    b0 = row_off // rows
    seq = lambda w: pl.BlockSpec((rows, w), lambda b: (b0 + b, 0))
    in_specs = [seq(GLA_QK), seq(GLA_QK), seq(GLA_VW), seq(2 * GLA_QK), seq(GLA_VW)]
    args = [gq, gk, gv, la, og]
    if state is not None:
        in_specs.append(pl.BlockSpec((nseq, None, 2, GLA_HEADS, GLA_DK, GLA_DV),
                                     lambda b: (b, layer_j, 0, 0, 0, 0)))
        args.append(state)
    tri, pair_m = tri
    in_specs += [_resident(tri.shape, _const_map(3)), _resident(pair_m.shape, _const_map(2)),
                 _layer_spec(gg, layer_j)]
    args += [tri, pair_m, gg]
    return pl.pallas_call(
        functools.partial(_gla_kernel, has_state=state is not None, nseq=nseq),
        out_shape=[jax.ShapeDtypeStruct((nbatch * t, GLA_VW), BF16),
                   jax.ShapeDtypeStruct((nbatch, 2, GLA_HEADS, GLA_DK, GLA_DV), F32)],
        grid=(nbatch // nseq,),
        in_specs=in_specs,
        out_specs=[pl.BlockSpec((rows, GLA_VW), lambda b: (b, 0)),
                   pl.BlockSpec((nseq, 2, GLA_HEADS, GLA_DK, GLA_DV), lambda b: (b, 0, 0, 0, 0))],
        scratch_shapes=[
            pltpu.VMEM((rows, GLA_VW), F32),
            pltpu.VMEM((2, nc, GLA_HEADS * GLA_CHUNK, GLA_QK), BF16),
            pltpu.VMEM((2, nc, GLA_QK, GLA_DV), F32),
            pltpu.VMEM((2, nc, GLA_QK, GLA_DV), F32),
            pltpu.VMEM((2, nc, GLA_QK, GLA_DV), BF16),
            pltpu.VMEM((2, GLA_QK, GLA_DV), F32),
        ],
        compiler_params=_cparams("arbitrary"),
        name="gla_scan",
    )(*args)


def _mla_kernel(*refs, has_ctx, nseq):
    it = iter(refs)
    q_ref, k_ref, v_ref = next(it), next(it), next(it)
    kc_ref, vc_ref = (next(it), next(it)) if has_ctx else (None, None)
    o_ref = next(it)
    tq = q_ref.shape[0] // nseq
    tk = k_ref.shape[0] // nseq
    def scores(s, hd):
        sl = slice(hd * HEAD_PAD, (hd + 1) * HEAD_PAD)
        q = q_ref[s * tq:(s + 1) * tq, sl]
        own = _dot_nt(k_ref[s * tk:(s + 1) * tk, sl], q)
        return [own, _dot_nt(kc_ref[:, sl], q)] if has_ctx else [own]

    def softmax(ss):
        mx = functools.reduce(jnp.maximum, [jnp.max(x, axis=0, keepdims=True) for x in ss])
        ps = [jnp.exp2(x - mx) for x in ss]
        den = sum(jnp.sum(p, axis=0, keepdims=True) for p in ps)
        return [p.astype(BF16) for p in ps], den

    def values(s, hd, ps, den):
        vrows = slice(hd * MLA_V, (hd + 1) * MLA_V)
        o = _dot(v_ref[vrows, s * tk:(s + 1) * tk], ps[0])
        if has_ctx:
            o = o + _dot(vc_ref[vrows, :], ps[1])
        return o / den

    items = [(s, hd) for s in range(nseq) for hd in range(MLA_HEADS)]
    s_q, p_q, outs = {}, {}, []
    for step in range(len(items) + 2):
        if step < len(items):
            s_q[step] = scores(*items[step])
        if 0 <= step - 1 < len(items):
            p_q[step - 1] = softmax(s_q.pop(step - 1))
        if 0 <= step - 2 < len(items):
            outs.append(values(*items[step - 2], *p_q.pop(step - 2)))
    for s in range(nseq):
        heads = outs[s * MLA_HEADS:(s + 1) * MLA_HEADS]
        o_ref[s * tq:(s + 1) * tq, :] = jnp.concatenate(heads, axis=0).T.astype(BF16)


def _mla(qm, km, vm, kc, vc, layer_j, nbatch, t, row_off, tq):
    nseq = _seqs_per_step(tq, nbatch, t, row_off) if kc is None else 1
    tq = min(tq, t)
    nq = t // tq
    qrows, krows = nseq * tq, nseq * t
    assert t % tq == 0 and nbatch % nseq == 0 and row_off % krows == 0 and (nseq == 1 or nq == 1)
    q0, b0 = row_off // qrows, row_off // krows
    qspec = lambda w: pl.BlockSpec((qrows, w), lambda b, i: (q0 + b * nq + i, 0))
    kspec = lambda w: pl.BlockSpec((krows, w), lambda b, i: (b0 + b, 0))
    vspec = pl.BlockSpec((MLA_HEADS * MLA_V, krows), lambda b, i: (0, b0 + b))
    in_specs = [qspec(MLA_HEADS * HEAD_PAD), kspec(MLA_HEADS * HEAD_PAD), vspec]
    args = [qm, km, vm]
    if kc is not None:
        in_specs += [pl.BlockSpec((None, None) + kc.shape[2:], lambda b, i: (layer_j, b, 0, 0)),
                     pl.BlockSpec((None, None) + vc.shape[2:], lambda b, i: (layer_j, b, 0, 0))]
        args += [kc, vc]
    return pl.pallas_call(
        functools.partial(_mla_kernel, has_ctx=kc is not None, nseq=nseq),
        out_shape=jax.ShapeDtypeStruct((nbatch * t, MLA_HEADS * MLA_V), BF16),
        grid=(nbatch // nseq, nq),
        in_specs=in_specs,
        out_specs=pl.BlockSpec((qrows, MLA_HEADS * MLA_V), lambda b, i: (b * nq + i, 0)),
        compiler_params=_cparams("arbitrary", "arbitrary"),
        name="mla_attention",
    )(*args)


def _swap_signed(a):
    pairs = a.reshape(a.shape[:-1] + (a.shape[-1] // 2, 2))
    return jnp.stack([-pairs[..., 1], pairs[..., 0]], axis=-1).reshape(a.shape)


def _swap_pairs(a):
    pairs = a.reshape(a.shape[:-1] + (a.shape[-1] // 2, 2))
    return pairs[..., ::-1].reshape(a.shape)


def _pack_even_weights(cos, sin, even_w_in, gla_gate_w2, gla_gate_b, mla_qa_g, mla_qb_w, mla_qn_g,
                       mla_kva_g, mla_kvb_w, mla_kn_g):
    ne, d, _ = even_w_in.shape
    o = 0
    parts = {}
    for name, width in (("q", GLA_QK), ("k", GLA_QK), ("v", GLA_VW), ("og", GLA_VW),
                        ("gl", 2 * GLA_GATE_RANK), ("cq", MLA_Q_RANK), ("ckv", MLA_KV_RANK),
                        ("kr", MLA_ROPE)):
        parts[name] = even_w_in[:, :, o:o + width]
        o += width
    zeros = lambda n: jnp.zeros((ne, d, n), even_w_in.dtype)
    kr, kr_sw = parts["kr"], _swap_signed(parts["kr"])
    blk_a = jnp.concatenate([parts["gl"], zeros(MLA_NOPE - 2 * GLA_GATE_RANK), kr, kr_sw], axis=2)
    blk_b = jnp.concatenate([zeros(MLA_NOPE), kr_sw, kr], axis=2)
    w_in = jnp.concatenate([parts["cq"], parts["ckv"], blk_a, blk_b, parts["q"], parts["k"], parts["v"],
                            parts["og"]], axis=2).astype(BF16)
    assert w_in.shape[2] == _C_END

    w2_f = jnp.pad(gla_gate_w2[:, 0], ((0, 0), (0, 0), (0, GLA_QK)))
    w2_b = jnp.pad(gla_gate_w2[:, 1], ((0, 0), (0, 0), (GLA_QK, 0)))
    w2 = jnp.pad(jnp.concatenate([w2_f, w2_b], axis=1), ((0, 0), (0, HEAD_PAD - 2 * GLA_GATE_RANK), (0, 0)))

    pad_head = lambda a: jnp.pad(a, [(0, 0)] * (a.ndim - 1) + [(0, HEAD_PAD - a.shape[-1])])
    qb = jnp.concatenate([mla_qb_w, _swap_signed(mla_qb_w[..., MLA_NOPE:])], axis=-1)
    qb = qb.reshape(ne, MLA_Q_RANK, MLA_HEADS * HEAD_PAD)
    kvb_k = pad_head(mla_kvb_w[..., :MLA_NOPE]).reshape(ne, MLA_KV_RANK, MLA_HEADS * HEAD_PAD)
    kvb_v = mla_kvb_w[..., MLA_NOPE:].reshape(ne, MLA_KV_RANK, MLA_HEADS * MLA_V)

    def rot_tables(g):
        n = cos.shape[0]
        g_n = jnp.broadcast_to(g[:, None, :MLA_NOPE], (ne, n, MLA_NOPE))
        g_c = g[:, None, MLA_NOPE:] * cos
        g_s = _swap_pairs(g[:, MLA_NOPE:])[:, None, :] * sin
        return g_n, g_c, g_s

    qn_n, qn_c, qn_s = rot_tables(mla_qn_g)
    kn_n, kn_c, kn_s = rot_tables(mla_kn_g)
    kzero = jnp.zeros_like(kn_n)
    return {
        "w_in": w_in,
        "w2": w2.astype(BF16),
        "gate_b": gla_gate_b.reshape(ne, 1, 2 * GLA_QK),
        "qa_g": mla_qa_g[:, None, :],
        "qb": qb.astype(BF16),
        "kva_g": mla_kva_g[:, None, :],
        "kvb_k": kvb_k.astype(BF16),
        "kvb_v": kvb_v.astype(BF16),
        "kn_g": pad_head(mla_kn_g[:, :MLA_NOPE])[:, None, :],
        "kn_g_ctx": jnp.concatenate([mla_kn_g, mla_kn_g[:, MLA_NOPE:]], axis=1)[:, None, :],
        "t_q": jnp.concatenate([qn_n, qn_c, qn_s], axis=2) * (MLA_QK_DIM ** -0.5 * math.log2(math.e)),
        "t_a": jnp.concatenate([kzero, kn_c, kn_s], axis=2),
        "t_b": jnp.concatenate([kzero, kn_s, kn_c], axis=2),
    }


def _rope_tables(n, tm):
    pairs = MLA_ROPE // 4
    pos = np.arange(n)
    inv = ROPE_BASE ** (-jnp.arange(pairs, dtype=F32) / pairs)
    row = jnp.asarray(pos // GRID_W, F32)
    col = jnp.asarray(pos % GRID_W, F32)
    ang = jnp.concatenate([row[:, None] * inv, col[:, None] * inv], axis=-1)
    cos = jnp.repeat(jnp.cos(ang), 2, axis=-1)
    sin = jnp.repeat(jnp.sin(ang), 2, axis=-1)
    cos = jnp.pad(cos, ((tm, 0), (0, 0)), constant_values=1.0)
    sin = jnp.pad(sin, ((tm, 0), (0, 0)))
    return cos, sin


def _tri_masks():
    r = np.arange(GLA_CHUNK)
    lower = (r[None, :] <= r[:, None]).astype(np.float32)
    diff = r[:, None] ^ r[None, :]
    level = np.where(diff > 0, 1 << np.floor(np.log2(np.maximum(diff, 1))).astype(np.int64), 0)
    pair_m = np.tile(level, (1, GLA_HEADS)).astype(np.int32)
    return jnp.asarray(np.stack([lower, lower.T]), BF16), jnp.asarray(pair_m)


def _pick_tile(pref, *sizes):
    tm = pref
    while any(s % tm for s in sizes):
        tm //= 2
    return tm


def kernel(x_prompt, x_sample, c, c_ctx, cache_ckv, cache_krope, state_gla, ada_w, ada_b, norm_g,
           ffn1_wg, ffn1_wu, ffn1_wd, ffn2_wg, ffn2_wu, ffn2_wd, even_w_in, even_w_out,
           gla_gate_w2, gla_gate_b, gla_norm_g, mla_qa_g, mla_qb_w, mla_kva_g, mla_kvb_w,
           mla_qn_g, mla_kn_g, odd_w_in, odd_v_g, odd_ws, odd_bs, odd_w_out):
    batch, seq, d = x_prompt.shape
    dec_batch, dec_seq, _ = x_sample.shape
    depth = ada_w.shape[0]
    n_even = even_w_in.shape[0]
    m_ctx, m_dec = batch * seq, dec_batch * dec_seq
    assert 1 + dec_batch <= COND_PAD and seq % CMLP_CHUNK == 0 and dec_seq % CMLP_CHUNK == 0

    rows = _Rows(m_ctx, dec_batch, dec_seq, _pick_tile(ROW_TILE, m_ctx, dec_seq))

    cond = jnp.concatenate([c_ctx[None, :], c, jnp.zeros((COND_PAD - 1 - dec_batch, d), F32)], axis=0)
    mod = _modulation_all(cond, ada_w, ada_b).reshape(depth, COND_PAD, N_MOD, d)

    cos_t, sin_t = _rope_tables(dec_seq, rows.tm)
    tri = _tri_masks()
    krope_blk = jnp.concatenate([jnp.zeros(cache_krope.shape[:-1] + (MLA_NOPE,), F32), cache_krope,
                                 cache_krope], axis=-1)
    even_wts = _pack_even_weights(cos_t, sin_t, even_w_in, gla_gate_w2, gla_gate_b, mla_qa_g, mla_qb_w,
                                  mla_qn_g, mla_kva_g, mla_kvb_w, mla_kn_g)
    kc, vc = _ctx_kv(cache_ckv, krope_blk, even_wts["kvb_k"], even_wts["kvb_v"], even_wts["kn_g_ctx"])

    ffn1 = (ffn1_wg, ffn1_wu, ffn1_wd)
    ffn2 = (ffn2_wg, ffn2_wu, ffn2_wd)
    w_out_bf = even_w_out.astype(BF16)
    odd_wts = (odd_w_in, odd_v_g[:, None, :], odd_ws.astype(BF16), jnp.swapaxes(odd_bs, 1, 2), odd_w_out)
    norms = norm_g.reshape(depth * 3, 1, d)
    gg = gla_norm_g[:, None, :]

    x = (x_prompt.reshape(m_ctx, d), x_sample.reshape(m_dec, d))
    new_ckv, new_krope, new_gla = [], [], []
    for i in range(depth):
        j = i // 2
        x = _ffn(x, mod, i, 0, (norms, 3 * i), *ffn1, rows)
        g_mix = (norms, 3 * i + 1)
        premix = None
        if i % 2 == 0:
            gq, gk, gv, og, la, qm, km, vm, ckv, kr = _even_in(x, mod, i, g_mix, even_wts, j, rows)
            gla_ctx, st = _gla(gq, gk, gv, la, og, None, j, tri, gg, batch, seq, 0)
            gla_dec, _ = _gla(gq, gk, gv, la, og, state_gla, j, tri, gg, dec_batch, dec_seq, m_ctx)
            mla_ctx = _mla(qm, km, vm, None, None, j, batch, seq, 0, MLA_SHARED_ROWS)
            mla_dec = _mla(qm, km, vm, kc, vc, j, dec_batch, dec_seq, m_ctx, _pick_tile(MLA_QUERY_TILE, dec_seq))
            premix = ((gla_ctx, gla_dec), (mla_ctx, mla_dec), w_out_bf, j)
            new_ckv.append(ckv.reshape(batch, seq, MLA_KV_RANK))
            new_krope.append(kr.reshape(batch, seq, MLA_ROPE))
            new_gla.append(st)
        else:
            x = _odd_mixer(x, mod, i, j, g_mix, *odd_wts, rows)
        x = _ffn(x, mod, i, 6, (norms, 3 * i + 2), *ffn2, rows, split_out=(i == depth - 1),
                 premix=premix)

    y_prompt, y_sample = x
    return (y_prompt.reshape(batch, seq, d), y_sample.reshape(dec_batch, dec_seq, d),
            jnp.stack(new_ckv, axis=1), jnp.stack(new_krope, axis=1), jnp.stack(new_gla, axis=1))
```

```python
import functools
import math

import numpy as np
import jax
import jax.numpy as jnp
from jax import lax
from jax.experimental import pallas as pl
from jax.experimental.pallas import tpu as pltpu

F32 = jnp.float32
BF16 = jnp.bfloat16

EPS = 1e-6
N_MOD = 9
GRID_W = 64
ROPE_BASE = 10000.0
GLA_HEADS = 4
GLA_DK = 64
GLA_DV = 128
GLA_QK = GLA_HEADS * GLA_DK
GLA_VW = GLA_HEADS * GLA_DV
GLA_GATE_RANK = 16
GLA_GATE_NORM = 16.0
GLA_CHUNK = 64
GLA_CHUNKS_PER_STEP = 16
MLA_HEADS = 8
MLA_NOPE = 64
MLA_ROPE = 32
MLA_V = 64
MLA_QK_DIM = MLA_NOPE + MLA_ROPE
MLA_Q_RANK = 384
MLA_KV_RANK = 256
HEAD_PAD = 128
MLA_QUERY_TILE = 512
MLA_SHARED_ROWS = 1024
CMLP_CHUNK = 128
CMLP_GROUPS = 4
COND_PAD = 16
SUBLANES = 8
ADALN_COL_TILES = 2

VMEM_LIMIT = 56 * 1024 * 1024
ROW_TILE = 512


def _cparams(*sem):
    return pltpu.CompilerParams(dimension_semantics=sem, vmem_limit_bytes=VMEM_LIMIT)


def _resident(shape, index_map):
    return pl.BlockSpec(shape, index_map, pipeline_mode=pl.Buffered(1))


def _const_map(nd):
    return lambda *_: (0,) * nd


def _layer_spec(stack, layer):
    nd = stack.ndim - 1
    return _resident((None,) + stack.shape[1:], lambda *_: (layer,) + (0,) * nd)


def _silu(x):
    return x * jax.nn.sigmoid(x)


def _rms(x, g):
    return x * lax.rsqrt(jnp.mean(x * x, axis=-1, keepdims=True) + EPS) * g


def _modulate(x, g, shift, scale):
    return _rms(x, g) * (1.0 + scale) + shift


def _dot(a, b):
    return jnp.dot(a, b, preferred_element_type=F32)


def _dot_nt(a, b):
    return lax.dot_general(a, b, (((1,), (1,)), ((), ())), preferred_element_type=F32)


def _mod_kernel(c_ref, w_ref, b_ref, o_ref):
    s = _silu(c_ref[...]).astype(BF16)
    o_ref[...] = _dot(s, w_ref[...].astype(BF16)) + b_ref[...]


def _modulation_all(cond, ada_w, ada_b):
    depth, d, n = ada_w.shape
    tn = n // ADALN_COL_TILES
    return pl.pallas_call(
        _mod_kernel,
        out_shape=jax.ShapeDtypeStruct((depth, COND_PAD, n), F32),
        grid=(depth, n // tn),
        in_specs=[
            pl.BlockSpec((COND_PAD, d), lambda l, j: (0, 0)),
            pl.BlockSpec((None, d, tn), lambda l, j: (l, 0, j)),
            pl.BlockSpec((None, 1, tn), lambda l, j: (l, 0, j)),
        ],
        out_specs=pl.BlockSpec((None, COND_PAD, tn), lambda l, j: (l, 0, j)),
        compiler_params=_cparams("arbitrary", "arbitrary"),
        name="adaln_modulation",
    )(cond, ada_w, ada_b.reshape(depth, 1, n))


class _Rows:
    def __init__(self, m_ctx, dec_batch, dec_seq, tm):
        assert m_ctx % tm == 0 and dec_seq % tm == 0
        self.tm = tm
        self.m_ctx = m_ctx
        self.m = m_ctx + dec_batch * dec_seq
        self.nb_ctx = m_ctx // tm
        self.nb_seq = dec_seq // tm
        self.nblocks = self.m // tm

    def cond(self, i):
        return jnp.where(i < self.nb_ctx, 0, 1 + (i - self.nb_ctx) // self.nb_seq)

    def rope_block(self, i):
        return jnp.where(i < self.nb_ctx, 0, 1 + (i - self.nb_ctx) % self.nb_seq)

    def mod_spec(self, layer, d):
        return pl.BlockSpec((None, None, N_MOD, d), lambda i: (layer, self.cond(i), 0, 0))

    def row_spec(self, width):
        return pl.BlockSpec((self.tm, width), lambda i: (i, 0))

    def ctx_spec(self, width):
        return pl.BlockSpec((self.tm, width), lambda i: (jnp.minimum(i, self.nb_ctx - 1), 0))

    def dec_spec(self, width):
        return pl.BlockSpec((self.tm, width), lambda i: (jnp.maximum(i - self.nb_ctx, 0), 0))

    def split_specs(self, width):
        return [self.ctx_spec(width), self.dec_spec(width)]

    def split_shapes(self, width, dtype):
        return [jax.ShapeDtypeStruct((self.m_ctx, width), dtype),
                jax.ShapeDtypeStruct((self.m - self.m_ctx, width), dtype)]


def _load_split(refs, is_ctx):
    if len(refs) == 1:
        return refs[0][...]
    return jnp.where(is_ctx, refs[0][...], refs[1][...])


STAGE_CHUNKS = 16
STAGE_SLOTS = 8


def _stage_weights_bf16(layer, hbm_refs, vmem_refs, stage_refs, sem):
    chunks = []
    for src, dst, stage in zip(hbm_refs, vmem_refs, stage_refs):
        rows = stage.shape[1]
        assert dst.shape[0] % rows == 0
        chunks += [(src, dst, stage, r0) for r0 in range(0, dst.shape[0], rows)]
    slots = sem.shape[0]
    ahead = slots - 1

    def copy(i):
        src, _, stage, r0 = chunks[i]
        return pltpu.make_async_copy(src.at[layer, pl.ds(r0, stage.shape[1]), :], stage.at[i % slots],
                                     sem.at[i % slots])

    for i in range(min(ahead, len(chunks))):
        copy(i).start()
    for i, (_, dst, stage, r0) in enumerate(chunks):
        if i + ahead < len(chunks):
            copy(i + ahead).start()
        copy(i).wait()
        dst[pl.ds(r0, stage.shape[1]), :] = stage[i % slots].astype(BF16)


def _ffn_kernel(*refs, layer, row0, nb_ctx, split_in, split_out, premix):
    it = iter(refs)
    x_refs = [next(it) for _ in range(2 if split_in else 1)]
    mod_ref, g_ref = next(it), next(it)
    w_hbm = [next(it) for _ in range(3)]
    if premix:
        a_refs, b_refs = [next(it), next(it)], [next(it), next(it)]
        wo_ref = next(it)
    o_refs = [next(it) for _ in range(2 if split_out else 1)]
    wg_ref, wu_ref, wd_ref = w_vmem = [next(it) for _ in range(3)]
    stage_in, stage_down, sem = next(it), next(it), next(it)

    @pl.when(pl.program_id(0) == 0)
    def _():
        _stage_weights_bf16(layer, w_hbm, w_vmem, [stage_in, stage_in, stage_down], sem)

    is_ctx = pl.program_id(0) < nb_ctx
    shift = mod_ref[row0:row0 + 1, :]
    scale = mod_ref[row0 + 1:row0 + 2, :]
    gate = mod_ref[row0 + 2:row0 + 3, :]
    x = _load_split(x_refs, is_ctx)
    if premix:
        half = wo_ref.shape[0] // 2
        mix = _dot(_load_split(a_refs, is_ctx), wo_ref[:half, :])
        mix = mix + _dot(_load_split(b_refs, is_ctx), wo_ref[half:, :])
        x = x + mod_ref[5:6, :] * mix
    h = _modulate(x, g_ref[...], shift, scale).astype(BF16)
    a = _dot(h, wg_ref[...])
    u = _dot(h, wu_ref[...])
    act = (_silu(a) * u).astype(BF16)
    y = _dot(act, wd_ref[...])
    out = x + (0.5 * gate) * y
    if split_out:
        @pl.when(is_ctx)
        def _():
            o_refs[0][...] = out

        @pl.when(jnp.logical_not(is_ctx))
        def _():
            o_refs[1][...] = out
    else:
        o_refs[0][...] = out


def _ffn(xs, mod, layer, row0, g, wg, wu, wd, rows, split_out=False, premix=None):
    split_in = isinstance(xs, (tuple, list))
    xs = list(xs) if split_in else [xs]
    d = xs[0].shape[1]
    f = wg.shape[2]
    hbm = pl.BlockSpec(memory_space=pl.ANY)
    in_specs = (rows.split_specs(d) if split_in else [rows.row_spec(d)]) + [
        rows.mod_spec(layer, d),
        _layer_spec(*g),
        hbm, hbm, hbm,
    ]
    args = xs + [mod, g[0], wg, wu, wd]
    if premix is not None:
        gla, mla, w_out, j = premix
        in_specs += rows.split_specs(gla[0].shape[1]) + rows.split_specs(mla[0].shape[1])
        in_specs.append(_layer_spec(w_out, j))
        args += list(gla) + list(mla) + [w_out]
    return pl.pallas_call(
        functools.partial(_ffn_kernel, layer=layer, row0=row0, nb_ctx=rows.nb_ctx, split_in=split_in,
                          split_out=split_out, premix=premix is not None),
        out_shape=rows.split_shapes(d, F32) if split_out else jax.ShapeDtypeStruct((rows.m, d), F32),
        grid=(rows.nblocks,),
        in_specs=in_specs,
        out_specs=rows.split_specs(d) if split_out else rows.row_spec(d),
        scratch_shapes=[
            pltpu.VMEM((d, f), BF16), pltpu.VMEM((d, f), BF16), pltpu.VMEM((f, d), BF16),
            pltpu.VMEM((STAGE_SLOTS, d // STAGE_CHUNKS, f), F32),
            pltpu.VMEM((STAGE_SLOTS, f // STAGE_CHUNKS, d), F32),
            pltpu.SemaphoreType.DMA((STAGE_SLOTS,)),
        ],
        compiler_params=_cparams("arbitrary"),
        name="ffn_swiglu",
    )(*args)


def _gelu_tanh(x):
    c = math.sqrt(2.0 / math.pi)
    return x * (0.5 * (1.0 + jnp.tanh(c * (x + 0.044715 * (x * x * x)))))


def _odd_kernel(x_ref, mod_ref, g_ref, win_hbm, vg_ref, ws_ref, bst_ref, wout_hbm, o_ref,
                win_ref, wout_ref, stage_in, stage_out, sem, *, layer_j):
    @pl.when(pl.program_id(0) == 0)
    def _():
        _stage_weights_bf16(layer_j, [win_hbm, wout_hbm], [win_ref, wout_ref], [stage_in, stage_out], sem)

    x = x_ref[...]
    tm = x.shape[0]
    shift, scale, gate = mod_ref[3:4, :], mod_ref[4:5, :], mod_ref[5:6, :]
    h = _modulate(x, g_ref[...], shift, scale).astype(BF16)
    uv = _gelu_tanh(_dot(h, win_ref[...]))
    width = uv.shape[1] // 2
    u = uv[:, :width]
    v = _rms(uv[:, width:], vg_ref[...]).astype(BF16)
    gw = width // CMLP_GROUPS
    chunks = []
    for c in range(tm // CMLP_CHUNK):
        r0 = c * CMLP_CHUNK
        groups = []
        for g in range(CMLP_GROUPS):
            vg = v[r0:r0 + CMLP_CHUNK, g * gw:(g + 1) * gw]
            mixed = _dot(ws_ref[g], vg) + bst_ref[:, g:g + 1]
            groups.append(u[r0:r0 + CMLP_CHUNK, g * gw:(g + 1) * gw] * mixed)
        chunks.append(jnp.concatenate(groups, axis=1))
    z = jnp.concatenate(chunks, axis=0).astype(BF16)
    o_ref[...] = x + gate * _dot(z, wout_ref[...])


def _odd_mixer(x, mod, layer, layer_j, g, w_in, v_g, w_s, b_s_t, w_out, rows):
    m, d = x.shape
    n_in, width = w_in.shape[2], w_out.shape[1]
    hbm = pl.BlockSpec(memory_space=pl.ANY)
    return pl.pallas_call(
        functools.partial(_odd_kernel, layer_j=layer_j),
        out_shape=jax.ShapeDtypeStruct((m, d), F32),
        grid=(rows.nblocks,),
        in_specs=[
            rows.row_spec(d),
            rows.mod_spec(layer, d),
            _layer_spec(*g),
            hbm, _layer_spec(v_g, layer_j), _layer_spec(w_s, layer_j), _layer_spec(b_s_t, layer_j), hbm,
        ],
        out_specs=rows.row_spec(d),
        scratch_shapes=[
            pltpu.VMEM((d, n_in), BF16), pltpu.VMEM((width, d), BF16),
            pltpu.VMEM((STAGE_SLOTS, d // STAGE_CHUNKS, n_in), F32),
            pltpu.VMEM((STAGE_SLOTS, width // STAGE_CHUNKS, d), F32),
            pltpu.SemaphoreType.DMA((STAGE_SLOTS,)),
        ],
        compiler_params=_cparams("arbitrary"),
        name="odd_gmlp",
    )(x, mod, g[0], w_in, v_g, w_s, b_s_t, w_out)


_C_CQ = 0
_C_CKV = _C_CQ + MLA_Q_RANK
_C_KA = _C_CKV + MLA_KV_RANK
_C_KB = _C_KA + HEAD_PAD
_C_Q = _C_KB + HEAD_PAD
_C_K = _C_Q + GLA_QK
_C_V = _C_K + GLA_QK
_C_OG = _C_V + GLA_VW
_C_END = _C_OG + GLA_VW


def _head_scale(x, live):
    ss = jnp.sum(jnp.where(live, x * x, 0.0), axis=-1, keepdims=True)
    return lax.rsqrt(ss * (1.0 / MLA_QK_DIM) + EPS)


def _even_in_kernel(x_ref, mod_ref, g_ref, win_ref, w2_ref, gb_ref, qag_ref, qb_ref, kvag_ref,
                    kvbk_ref, kvbv_ref, kng_ref, tq_ref, ta_ref, tb_ref,
                    gq_ref, gk_ref, gv_ref, og_ref, la_ref, qm_ref, km_ref, vm_ref, ckv_ref, kr_ref,
                    *, nb_ctx):
    is_ctx = pl.program_id(0) < nb_ctx
    shift, scale = mod_ref[3:4, :], mod_ref[4:5, :]
    h = _modulate(x_ref[...], g_ref[...], shift, scale).astype(BF16)

    proj_q = _dot(h, win_ref[:, _C_CQ:_C_CKV])
    proj_kv = _dot(h, win_ref[:, _C_CKV:_C_Q])
    ckv = _rms(proj_kv[:, :MLA_KV_RANK], kvag_ref[...])
    ka = proj_kv[:, _C_KA - _C_CKV:_C_KB - _C_CKV]
    kb = proj_kv[:, _C_KB - _C_CKV:]

    @pl.when(is_ctx)
    def _():
        ckv_ref[...] = ckv
        kr_ref[...] = ka[:, MLA_NOPE:MLA_QK_DIM]

    cq = _rms(proj_q, qag_ref[...]).astype(BF16)
    ckv_bf = ckv.astype(BF16)
    qn = _dot(cq, qb_ref[...])
    kn = _dot(ckv_bf, kvbk_ref[...])
    vm_t = _dot(ckv_bf, kvbv_ref[...])
    z = _dot(ka.astype(BF16), w2_ref[...]) + gb_ref[...]

    lane = lax.broadcasted_iota(jnp.int32, ka.shape, 1)
    live = lane < MLA_QK_DIM
    tq = tq_ref[...]

    def q_heads(heads):
        for hd in heads:
            sl = slice(hd * HEAD_PAD, (hd + 1) * HEAD_PAD)
            qh = qn[:, sl]
            qm_ref[:, sl] = (qh * tq * _head_scale(qh, live)).astype(BF16)

    is_rope = live & (lane >= MLA_NOPE)
    ss_rope = jnp.sum(jnp.where(is_rope, ka * ka, 0.0), axis=-1, keepdims=True)
    rot = ka * ta_ref[...] + kb * tb_ref[...]
    kng = kng_ref[...]

    def k_heads(heads):
        for hd in heads:
            sl = slice(hd * HEAD_PAD, (hd + 1) * HEAD_PAD)
            kh = kn[:, sl]
            ss = jnp.sum(kh * kh, axis=-1, keepdims=True) + ss_rope
            r = lax.rsqrt(ss * (1.0 / MLA_QK_DIM) + EPS)
            km_ref[:, sl] = ((kh * kng + rot) * r).astype(BF16)

    half = MLA_HEADS // 2
    gqk = _dot(h, win_ref[:, _C_Q:_C_V])
    q_heads(range(half))
    gq_ref[...] = gqk[:, :GLA_QK] * (GLA_DK ** -0.5)
    gk_ref[...] = gqk[:, GLA_QK:]
    gv = _dot(h, win_ref[:, _C_V:_C_OG])
    q_heads(range(half, MLA_HEADS))
    gv_ref[...] = gv.astype(BF16)
    og = _dot(h, win_ref[:, _C_OG:])
    k_heads(range(half))
    og_ref[...] = og
    vm_ref[...] = vm_t.T.astype(BF16)
    k_heads(range(half, MLA_HEADS))
    la_ref[...] = ((jnp.minimum(z, 0.0) - jnp.log(1.0 + jnp.exp(-jnp.abs(z))))
                   * (math.log2(math.e) / GLA_GATE_NORM))


def _even_in(x, mod, layer, g, wts, layer_j, rows):
    m, d = x.shape
    tm = rows.tm
    rope_spec = pl.BlockSpec((None, tm, HEAD_PAD), lambda i: (layer_j, rows.rope_block(i), 0))
    widths = [GLA_QK, GLA_QK, GLA_VW, GLA_VW, 2 * GLA_QK, MLA_HEADS * HEAD_PAD, MLA_HEADS * HEAD_PAD]
    dtypes = [F32, F32, BF16, F32, F32, BF16, BF16]
    names = ["w_in", "w2", "gate_b", "qa_g", "qb", "kva_g", "kvb_k", "kvb_v", "kn_g"]
    return pl.pallas_call(
        functools.partial(_even_in_kernel, nb_ctx=rows.nb_ctx),
        out_shape=[jax.ShapeDtypeStruct((m, w), t) for w, t in zip(widths, dtypes)]
        + [jax.ShapeDtypeStruct((MLA_HEADS * MLA_V, m), BF16),
           jax.ShapeDtypeStruct((rows.m_ctx, MLA_KV_RANK), F32),
           jax.ShapeDtypeStruct((rows.m_ctx, MLA_ROPE), F32)],
        grid=(rows.nblocks,),
        in_specs=[rows.row_spec(d), rows.mod_spec(layer, d), _layer_spec(*g)]
        + [_layer_spec(wts[n], layer_j) for n in names] + [rope_spec] * 3,
        out_specs=[rows.row_spec(w) for w in widths]
        + [pl.BlockSpec((MLA_HEADS * MLA_V, tm), lambda i: (0, i)),
           rows.ctx_spec(MLA_KV_RANK), rows.ctx_spec(MLA_ROPE)],
        compiler_params=_cparams("arbitrary"),
        name="even_in_proj",
    )(x, mod, g[0], *[wts[n] for n in names], wts["t_q"], wts["t_a"], wts["t_b"])


def _ctx_kv_kernel(ckv_ref, kr_ref, kvbk_ref, kvbv_ref, kng_ref, k_ref, v_ref):
    nb, p, r = ckv_ref.shape
    ckv_bf = ckv_ref[...].reshape(nb * p, r).astype(BF16)
    v_all = _dot(ckv_bf, kvbv_ref[...])
    for b in range(nb):
        v_ref[b] = v_all[b * p:(b + 1) * p, :].T.astype(BF16)
    kn = _dot(ckv_bf, kvbk_ref[...])
    kr = kr_ref[...].reshape(nb * p, HEAD_PAD)
    lane = lax.broadcasted_iota(jnp.int32, kr.shape, 1)
    live = lane < MLA_QK_DIM
    kng = kng_ref[...]
    for hd in range(MLA_HEADS):
        sl = slice(hd * HEAD_PAD, (hd + 1) * HEAD_PAD)
        kh = kn[:, sl] + kr
        k_ref[:, :, sl] = (kh * kng * _head_scale(kh, live)).astype(BF16).reshape(nb, p, HEAD_PAD)


def _ctx_kv(cache_ckv, cache_krope_blk, kvb_k, kvb_v, kn_g):
    nb, ne, p, r = cache_ckv.shape
    return pl.pallas_call(
        _ctx_kv_kernel,
        out_shape=[jax.ShapeDtypeStruct((ne, nb, p, MLA_HEADS * HEAD_PAD), BF16),
                   jax.ShapeDtypeStruct((ne, nb, MLA_HEADS * MLA_V, p), BF16)],
        grid=(ne,),
        in_specs=[
            pl.BlockSpec((nb, None, p, r), lambda j: (0, j, 0, 0)),
            pl.BlockSpec((nb, None, p, HEAD_PAD), lambda j: (0, j, 0, 0)),
            pl.BlockSpec((None,) + kvb_k.shape[1:], lambda j: (j, 0, 0)),
            pl.BlockSpec((None,) + kvb_v.shape[1:], lambda j: (j, 0, 0)),
            pl.BlockSpec((None,) + kn_g.shape[1:], lambda j: (j, 0, 0)),
        ],
        out_specs=[pl.BlockSpec((None, nb, p, MLA_HEADS * HEAD_PAD), lambda j: (j, 0, 0, 0)),
                   pl.BlockSpec((None, nb, MLA_HEADS * MLA_V, p), lambda j: (j, 0, 0, 0))],
        compiler_params=_cparams("arbitrary"),
        name="ctx_kv",
    )(cache_ckv, cache_krope_blk, kvb_k, kvb_v, kn_g)


def _split3(x):
    hi = x.astype(BF16)
    r = x - hi.astype(F32)
    mid = r.astype(BF16)
    lo = (r - mid.astype(F32)).astype(BF16)
    return hi, mid, lo


def _block_row_bcast(a, blk, off):
    c, n = a.shape
    if blk >= 8:
        pieces = [jnp.broadcast_to(a[b * blk + off:b * blk + off + 1, :], (blk, n))
                  for b in range(c // blk)]
        return pieces[0] if len(pieces) == 1 else jnp.concatenate(pieces, axis=0)
    a3 = a.reshape(c // 8, 8, n)
    sub = lax.broadcasted_iota(jnp.int32, a3.shape, 1) // blk
    out = jnp.broadcast_to(a3[:, off:off + 1, :], a3.shape)
    for s in range(1, 8 // blk):
        cand = jnp.broadcast_to(a3[:, s * blk + off:s * blk + off + 1, :], a3.shape)
        out = jnp.where(sub == s, cand, out)
    return out.reshape(c, n)


def _head_stack(a, head_w):
    lane_head = lax.broadcasted_iota(jnp.int32, a.shape, 1) // head_w
    zero = jnp.zeros_like(a)
    return jnp.concatenate([jnp.where(lane_head == h, a, zero) for h in range(GLA_HEADS)], axis=0)


def _seqs_per_step(max_rows, nbatch, t, row_off):
    nseq = max(1, min(nbatch, max_rows // t))
    while nbatch % nseq or row_off % (nseq * t):
        nseq -= 1
    return nseq


def _run_skewed(stage_gens, skew=1):
    done = [False] * len(stage_gens)
    tick = 0
    while not all(done):
        for u, gen in enumerate(stage_gens):
            if tick >= u * skew and not done[u]:
                try:
                    next(gen)
                except StopIteration:
                    done[u] = True
        tick += 1


def _gla_phase_a(c, q_ref, k_ref, v_ref, la_ref, tri_ref, pair_ref, oacc_ref, qst_ref, ds_ref, dc_ref):
    C = GLA_CHUNK
    r0 = pl.multiple_of(c * C, C)
    q = q_ref[pl.ds(r0, C), :]
    k = k_ref[pl.ds(r0, C), :]
    la = la_ref[pl.ds(r0, C), :]
    la_f, la_b = la[:, :GLA_QK], la[:, GLA_QK:]
    cums = []
    for d, la_d in enumerate((la_f, la_b)):
        hi, mid, lo = _split3(la_d)
        tri = tri_ref[d]
        cums.append(_dot(tri, hi) + _dot(tri, mid) + _dot(tri, lo))
    cum_f, cum_b = cums
    k_stack = _head_stack(k.astype(BF16), GLA_DK)
    yield

    row = lax.broadcasted_iota(jnp.int32, (C, 1), 0)
    pair_m = pair_ref[...]
    att = jnp.where(pair_m == 0, 2.0 * _dot_nt(q.astype(BF16), k_stack), 0.0)
    m = C // 2
    while m >= 1:
        blk = 2 * m
        upper = (row % blk) >= m
        if m == 1:
            qm = (q * jnp.exp2(jnp.where(upper, la_f, la_b))).astype(BF16)
            km_stack = k_stack
        else:
            ref_f = _block_row_bcast(cum_f, blk, m - 1)
            ref_b = _block_row_bcast(cum_b, blk, m)
            eq = jnp.where(upper, cum_f - ref_f, cum_b - ref_b)
            ek = jnp.where(upper, ref_b - cum_b, ref_f - cum_f)
            qm = (q * jnp.exp2(eq)).astype(BF16)
            km_stack = _head_stack((k * jnp.exp2(ek)).astype(BF16), GLA_DK)
        yield
        att = jnp.where(pair_m == m, _dot_nt(qm, km_stack), att)
        m //= 2

    v = v_ref[pl.ds(r0, C), :]
    v_stack = _head_stack(v, GLA_DV)
    qcs = [(q * jnp.exp2(cum)).astype(BF16) for cum in cums]
    yield
    oacc_ref[pl.ds(r0, C), :] = _dot(att.astype(BF16), v_stack)
    for d, cum in enumerate(cums):
        last = cum[C - 1:C, :] if d == 0 else cum[0:1, :]
        qst_ref[d, c] = _head_stack(qcs[d], GLA_DK)
        kct = (k * jnp.exp2(last - cum)).T.astype(BF16)
        decay_col = jnp.exp2(jnp.broadcast_to(last, (SUBLANES, GLA_QK))).T[:, 0:1]
        dc_ref[d, c] = jnp.broadcast_to(decay_col, (GLA_QK, GLA_DV))
        yield
        ds_ref[d, c] = jnp.concatenate(
            [_dot(kct[h * GLA_DK:(h + 1) * GLA_DK, :], v[:, h * GLA_DV:(h + 1) * GLA_DV])
             for h in range(GLA_HEADS)], axis=0)


def _gla_phase_c(c, og_ref, gg, oacc_ref, qst_ref, sb_ref, o_ref):
    C = GLA_CHUNK
    r0 = pl.multiple_of(c * C, C)
    q_cat = jnp.concatenate([qst_ref[0, c], qst_ref[1, c]], axis=1)
    s_cat = jnp.concatenate([sb_ref[0, c], sb_ref[1, c]], axis=0)
    inter = _dot(q_cat, s_cat)
    yield
    for h in range(GLA_HEADS):
        sl = slice(h * GLA_DV, (h + 1) * GLA_DV)
        o = oacc_ref[pl.ds(r0, C), sl] + inter[h * C:(h + 1) * C, :]
        o_ref[pl.ds(r0, C), sl] = (_rms(o, gg) * _silu(og_ref[pl.ds(r0, C), sl])).astype(BF16)


def _gla_kernel(*refs, has_state, nseq):
    it = iter(refs)
    q_ref, k_ref, v_ref, la_ref, og_ref = (next(it) for _ in range(5))
    s0_ref = next(it) if has_state else None
    tri_ref, pair_ref, gg_ref, o_ref, sout_ref = (next(it) for _ in range(5))
    oacc_ref, qst_ref, ds_ref, dc_ref, sb_ref, st_ref = (next(it) for _ in range(6))

    n_chunks = q_ref.shape[0] // GLA_CHUNK
    nc = n_chunks // nseq
    per_step = min(GLA_CHUNKS_PER_STEP, n_chunks)

    def phase_a(i, carry):
        _run_skewed([_gla_phase_a(per_step * i + u, q_ref, k_ref, v_ref, la_ref, tri_ref, pair_ref, oacc_ref,
                                  qst_ref, ds_ref, dc_ref) for u in range(per_step)])
        return carry

    lax.fori_loop(0, n_chunks // per_step, phase_a, 0)

    for s in range(nseq):
        for d in range(2):
            for h in range(GLA_HEADS):
                rows = slice(h * GLA_DK, (h + 1) * GLA_DK)
                st_ref[d, rows, :] = s0_ref[s, d, h] if has_state else jnp.zeros((GLA_DK, GLA_DV), F32)

        def phase_b(c, carry, c0=s * nc):
            for d in range(2):
                cc = c0 + (c if d == 0 else nc - 1 - c)
                state = st_ref[d]
                sb_ref[d, cc] = state.astype(BF16)
                st_ref[d] = state * dc_ref[d, cc] + ds_ref[d, cc]
            return carry

        lax.fori_loop(0, nc, phase_b, 0)
        for d in range(2):
            for h in range(GLA_HEADS):
                sout_ref[s, d, h] = st_ref[d, h * GLA_DK:(h + 1) * GLA_DK, :]

    gg = gg_ref[...]

    def phase_c(i, carry):
        _run_skewed([_gla_phase_c(per_step * i + u, og_ref, gg, oacc_ref, qst_ref, sb_ref, o_ref)
                     for u in range(per_step)])
        return carry

    lax.fori_loop(0, n_chunks // per_step, phase_c, 0)


def _gla(gq, gk, gv, la, og, state, layer_j, tri, gg, nbatch, t, row_off):
    nseq = _seqs_per_step(GLA_CHUNKS_PER_STEP * GLA_CHUNK, nbatch, t, row_off)
    rows = nseq * t
    nc = rows // GLA_CHUNK
    assert nbatch % nseq == 0 and row_off % rows == 0 and t % GLA_CHUNK == 0
    assert nc % min(GLA_CHUNKS_PER_STEP, nc) == 0
    b0 = row_off // rows
    seq = lambda w: pl.BlockSpec((rows, w), lambda b: (b0 + b, 0))
    in_specs = [seq(GLA_QK), seq(GLA_QK), seq(GLA_VW), seq(2 * GLA_QK), seq(GLA_VW)]
    args = [gq, gk, gv, la, og]
    if state is not None:
        in_specs.append(pl.BlockSpec((nseq, None, 2, GLA_HEADS, GLA_DK, GLA_DV),
                                     lambda b: (b, layer_j, 0, 0, 0, 0)))
        args.append(state)
    tri, pair_m = tri
    in_specs += [_resident(tri.shape, _const_map(3)), _resident(pair_m.shape, _const_map(2)),
                 _layer_spec(gg, layer_j)]
    args += [tri, pair_m, gg]
    return pl.pallas_call(
        functools.partial(_gla_kernel, has_state=state is not None, nseq=nseq),
        out_shape=[jax.ShapeDtypeStruct((nbatch * t, GLA_VW), BF16),
                   jax.ShapeDtypeStruct((nbatch, 2, GLA_HEADS, GLA_DK, GLA_DV), F32)],
        grid=(nbatch // nseq,),
        in_specs=in_specs,
        out_specs=[pl.BlockSpec((rows, GLA_VW), lambda b: (b, 0)),
                   pl.BlockSpec((nseq, 2, GLA_HEADS, GLA_DK, GLA_DV), lambda b: (b, 0, 0, 0, 0))],
        scratch_shapes=[
            pltpu.VMEM((rows, GLA_VW), F32),
            pltpu.VMEM((2, nc, GLA_HEADS * GLA_CHUNK, GLA_QK), BF16),
            pltpu.VMEM((2, nc, GLA_QK, GLA_DV), F32),
            pltpu.VMEM((2, nc, GLA_QK, GLA_DV), F32),
            pltpu.VMEM((2, nc, GLA_QK, GLA_DV), BF16),
            pltpu.VMEM((2, GLA_QK, GLA_DV), F32),
        ],
        compiler_params=_cparams("arbitrary"),
        name="gla_scan",
    )(*args)


def _mla_kernel(*refs, has_ctx, nseq):
    it = iter(refs)
    q_ref, k_ref, v_ref = next(it), next(it), next(it)
    kc_ref, vc_ref = (next(it), next(it)) if has_ctx else (None, None)
    o_ref = next(it)
    tq = q_ref.shape[0] // nseq
    tk = k_ref.shape[0] // nseq
    def scores(s, hd):
        sl = slice(hd * HEAD_PAD, (hd + 1) * HEAD_PAD)
        q = q_ref[s * tq:(s + 1) * tq, sl]
        own = _dot_nt(k_ref[s * tk:(s + 1) * tk, sl], q)
        return [own, _dot_nt(kc_ref[:, sl], q)] if has_ctx else [own]

    def softmax(ss):
        mx = functools.reduce(jnp.maximum, [jnp.max(x, axis=0, keepdims=True) for x in ss])
        ps = [jnp.exp2(x - mx) for x in ss]
        den = sum(jnp.sum(p, axis=0, keepdims=True) for p in ps)
        return [p.astype(BF16) for p in ps], den

    def values(s, hd, ps, den):
        vrows = slice(hd * MLA_V, (hd + 1) * MLA_V)
        o = _dot(v_ref[vrows, s * tk:(s + 1) * tk], ps[0])
        if has_ctx:
            o = o + _dot(vc_ref[vrows, :], ps[1])
        return o / den

    items = [(s, hd) for s in range(nseq) for hd in range(MLA_HEADS)]
    s_q, p_q, outs = {}, {}, []
    for step in range(len(items) + 2):
        if step < len(items):
            s_q[step] = scores(*items[step])
        if 0 <= step - 1 < len(items):
            p_q[step - 1] = softmax(s_q.pop(step - 1))
        if 0 <= step - 2 < len(items):
            outs.append(values(*items[step - 2], *p_q.pop(step - 2)))
    for s in range(nseq):
        heads = outs[s * MLA_HEADS:(s + 1) * MLA_HEADS]
        o_ref[s * tq:(s + 1) * tq, :] = jnp.concatenate(heads, axis=0).T.astype(BF16)


def _mla(qm, km, vm, kc, vc, layer_j, nbatch, t, row_off, tq):
    nseq = _seqs_per_step(tq, nbatch, t, row_off) if kc is None else 1
    tq = min(tq, t)
    nq = t // tq
    qrows, krows = nseq * tq, nseq * t
    assert t % tq == 0 and nbatch % nseq == 0 and row_off % krows == 0 and (nseq == 1 or nq == 1)
    q0, b0 = row_off // qrows, row_off // krows
    qspec = lambda w: pl.BlockSpec((qrows, w), lambda b, i: (q0 + b * nq + i, 0))
    kspec = lambda w: pl.BlockSpec((krows, w), lambda b, i: (b0 + b, 0))
    vspec = pl.BlockSpec((MLA_HEADS * MLA_V, krows), lambda b, i: (0, b0 + b))
    in_specs = [qspec(MLA_HEADS * HEAD_PAD), kspec(MLA_HEADS * HEAD_PAD), vspec]
    args = [qm, km, vm]
    if kc is not None:
        in_specs += [pl.BlockSpec((None, None) + kc.shape[2:], lambda b, i: (layer_j, b, 0, 0)),
                     pl.BlockSpec((None, None) + vc.shape[2:], lambda b, i: (layer_j, b, 0, 0))]
        args += [kc, vc]
    return pl.pallas_call(
        functools.partial(_mla_kernel, has_ctx=kc is not None, nseq=nseq),
        out_shape=jax.ShapeDtypeStruct((nbatch * t, MLA_HEADS * MLA_V), BF16),
        grid=(nbatch // nseq, nq),
        in_specs=in_specs,
        out_specs=pl.BlockSpec((qrows, MLA_HEADS * MLA_V), lambda b, i: (b * nq + i, 0)),
        compiler_params=_cparams("arbitrary", "arbitrary"),
        name="mla_attention",
    )(*args)


def _swap_signed(a):
    pairs = a.reshape(a.shape[:-1] + (a.shape[-1] // 2, 2))
    return jnp.stack([-pairs[..., 1], pairs[..., 0]], axis=-1).reshape(a.shape)


def _swap_pairs(a):
    pairs = a.reshape(a.shape[:-1] + (a.shape[-1] // 2, 2))
    return pairs[..., ::-1].reshape(a.shape)


def _pack_even_weights(cos, sin, even_w_in, gla_gate_w2, gla_gate_b, mla_qa_g, mla_qb_w, mla_qn_g,
                       mla_kva_g, mla_kvb_w, mla_kn_g):
    ne, d, _ = even_w_in.shape
    o = 0
    parts = {}
    for name, width in (("q", GLA_QK), ("k", GLA_QK), ("v", GLA_VW), ("og", GLA_VW),
                        ("gl", 2 * GLA_GATE_RANK), ("cq", MLA_Q_RANK), ("ckv", MLA_KV_RANK),
                        ("kr", MLA_ROPE)):
        parts[name] = even_w_in[:, :, o:o + width]
        o += width
    zeros = lambda n: jnp.zeros((ne, d, n), even_w_in.dtype)
    kr, kr_sw = parts["kr"], _swap_signed(parts["kr"])
    blk_a = jnp.concatenate([parts["gl"], zeros(MLA_NOPE - 2 * GLA_GATE_RANK), kr, kr_sw], axis=2)
    blk_b = jnp.concatenate([zeros(MLA_NOPE), kr_sw, kr], axis=2)
    w_in = jnp.concatenate([parts["cq"], parts["ckv"], blk_a, blk_b, parts["q"], parts["k"], parts["v"],
                            parts["og"]], axis=2).astype(BF16)
    assert w_in.shape[2] == _C_END

    w2_f = jnp.pad(gla_gate_w2[:, 0], ((0, 0), (0, 0), (0, GLA_QK)))
    w2_b = jnp.pad(gla_gate_w2[:, 1], ((0, 0), (0, 0), (GLA_QK, 0)))
    w2 = jnp.pad(jnp.concatenate([w2_f, w2_b], axis=1), ((0, 0), (0, HEAD_PAD - 2 * GLA_GATE_RANK), (0, 0)))

    pad_head = lambda a: jnp.pad(a, [(0, 0)] * (a.ndim - 1) + [(0, HEAD_PAD - a.shape[-1])])
    qb = jnp.concatenate([mla_qb_w, _swap_signed(mla_qb_w[..., MLA_NOPE:])], axis=-1)
    qb = qb.reshape(ne, MLA_Q_RANK, MLA_HEADS * HEAD_PAD)
    kvb_k = pad_head(mla_kvb_w[..., :MLA_NOPE]).reshape(ne, MLA_KV_RANK, MLA_HEADS * HEAD_PAD)
    kvb_v = mla_kvb_w[..., MLA_NOPE:].reshape(ne, MLA_KV_RANK, MLA_HEADS * MLA_V)

    def rot_tables(g):
        n = cos.shape[0]
        g_n = jnp.broadcast_to(g[:, None, :MLA_NOPE], (ne, n, MLA_NOPE))
        g_c = g[:, None, MLA_NOPE:] * cos
        g_s = _swap_pairs(g[:, MLA_NOPE:])[:, None, :] * sin
        return g_n, g_c, g_s

    qn_n, qn_c, qn_s = rot_tables(mla_qn_g)
    kn_n, kn_c, kn_s = rot_tables(mla_kn_g)
    kzero = jnp.zeros_like(kn_n)
    return {
        "w_in": w_in,
        "w2": w2.astype(BF16),
        "gate_b": gla_gate_b.reshape(ne, 1, 2 * GLA_QK),
        "qa_g": mla_qa_g[:, None, :],
        "qb": qb.astype(BF16),
        "kva_g": mla_kva_g[:, None, :],
        "kvb_k": kvb_k.astype(BF16),
        "kvb_v": kvb_v.astype(BF16),
        "kn_g": pad_head(mla_kn_g[:, :MLA_NOPE])[:, None, :],
        "kn_g_ctx": jnp.concatenate([mla_kn_g, mla_kn_g[:, MLA_NOPE:]], axis=1)[:, None, :],
        "t_q": jnp.concatenate([qn_n, qn_c, qn_s], axis=2) * (MLA_QK_DIM ** -0.5 * math.log2(math.e)),
        "t_a": jnp.concatenate([kzero, kn_c, kn_s], axis=2),
        "t_b": jnp.concatenate([kzero, kn_s, kn_c], axis=2),
    }


def _rope_tables(n, tm):
    pairs = MLA_ROPE // 4
    pos = np.arange(n)
    inv = ROPE_BASE ** (-jnp.arange(pairs, dtype=F32) / pairs)
    row = jnp.asarray(pos // GRID_W, F32)
    col = jnp.asarray(pos % GRID_W, F32)
    ang = jnp.concatenate([row[:, None] * inv, col[:, None] * inv], axis=-1)
    cos = jnp.repeat(jnp.cos(ang), 2, axis=-1)
    sin = jnp.repeat(jnp.sin(ang), 2, axis=-1)
    cos = jnp.pad(cos, ((tm, 0), (0, 0)), constant_values=1.0)
    sin = jnp.pad(sin, ((tm, 0), (0, 0)))
    return cos, sin


def _tri_masks():
    r = np.arange(GLA_CHUNK)
    lower = (r[None, :] <= r[:, None]).astype(np.float32)
    diff = r[:, None] ^ r[None, :]
    level = np.where(diff > 0, 1 << np.floor(np.log2(np.maximum(diff, 1))).astype(np.int64), 0)
    pair_m = np.tile(level, (1, GLA_HEADS)).astype(np.int32)
    return jnp.asarray(np.stack([lower, lower.T]), BF16), jnp.asarray(pair_m)


def _pick_tile(pref, *sizes):
    tm = pref
    while any(s % tm for s in sizes):
        tm //= 2
    return tm


def kernel(x_prompt, x_sample, c, c_ctx, cache_ckv, cache_krope, state_gla, ada_w, ada_b, norm_g,
           ffn1_wg, ffn1_wu, ffn1_wd, ffn2_wg, ffn2_wu, ffn2_wd, even_w_in, even_w_out,
           gla_gate_w2, gla_gate_b, gla_norm_g, mla_qa_g, mla_qb_w, mla_kva_g, mla_kvb_w,
           mla_qn_g, mla_kn_g, odd_w_in, odd_v_g, odd_ws, odd_bs, odd_w_out):
    batch, seq, d = x_prompt.shape
    dec_batch, dec_seq, _ = x_sample.shape
    depth = ada_w.shape[0]
    n_even = even_w_in.shape[0]
    m_ctx, m_dec = batch * seq, dec_batch * dec_seq
    assert 1 + dec_batch <= COND_PAD and seq % CMLP_CHUNK == 0 and dec_seq % CMLP_CHUNK == 0

    rows = _Rows(m_ctx, dec_batch, dec_seq, _pick_tile(ROW_TILE, m_ctx, dec_seq))

    cond = jnp.concatenate([c_ctx[None, :], c, jnp.zeros((COND_PAD - 1 - dec_batch, d), F32)], axis=0)
    mod = _modulation_all(cond, ada_w, ada_b).reshape(depth, COND_PAD, N_MOD, d)

    cos_t, sin_t = _rope_tables(dec_seq, rows.tm)
    tri = _tri_masks()
    krope_blk = jnp.concatenate([jnp.zeros(cache_krope.shape[:-1] + (MLA_NOPE,), F32), cache_krope,
                                 cache_krope], axis=-1)
    even_wts = _pack_even_weights(cos_t, sin_t, even_w_in, gla_gate_w2, gla_gate_b, mla_qa_g, mla_qb_w,
                                  mla_qn_g, mla_kva_g, mla_kvb_w, mla_kn_g)
    kc, vc = _ctx_kv(cache_ckv, krope_blk, even_wts["kvb_k"], even_wts["kvb_v"], even_wts["kn_g_ctx"])

    ffn1 = (ffn1_wg, ffn1_wu, ffn1_wd)
    ffn2 = (ffn2_wg, ffn2_wu, ffn2_wd)
    w_out_bf = even_w_out.astype(BF16)
    odd_wts = (odd_w_in, odd_v_g[:, None, :], odd_ws.astype(BF16), jnp.swapaxes(odd_bs, 1, 2), odd_w_out)
    norms = norm_g.reshape(depth * 3, 1, d)
    gg = gla_norm_g[:, None, :]

    x = (x_prompt.reshape(m_ctx, d), x_sample.reshape(m_dec, d))
    new_ckv, new_krope, new_gla = [], [], []
    for i in range(depth):
        j = i // 2
        x = _ffn(x, mod, i, 0, (norms, 3 * i), *ffn1, rows)
        g_mix = (norms, 3 * i + 1)
        premix = None
        if i % 2 == 0:
            gq, gk, gv, og, la, qm, km, vm, ckv, kr = _even_in(x, mod, i, g_mix, even_wts, j, rows)
            gla_ctx, st = _gla(gq, gk, gv, la, og, None, j, tri, gg, batch, seq, 0)
            gla_dec, _ = _gla(gq, gk, gv, la, og, state_gla, j, tri, gg, dec_batch, dec_seq, m_ctx)
            mla_ctx = _mla(qm, km, vm, None, None, j, batch, seq, 0, MLA_SHARED_ROWS)
            mla_dec = _mla(qm, km, vm, kc, vc, j, dec_batch, dec_seq, m_ctx, _pick_tile(MLA_QUERY_TILE, dec_seq))
            premix = ((gla_ctx, gla_dec), (mla_ctx, mla_dec), w_out_bf, j)
            new_ckv.append(ckv.reshape(batch, seq, MLA_KV_RANK))
            new_krope.append(kr.reshape(batch, seq, MLA_ROPE))
            new_gla.append(st)
        else:
            x = _odd_mixer(x, mod, i, j, g_mix, *odd_wts, rows)
        x = _ffn(x, mod, i, 6, (norms, 3 * i + 2), *ffn2, rows, split_out=(i == depth - 1),
                 premix=premix)

    y_prompt, y_sample = x
    return (y_prompt.reshape(batch, seq, d), y_sample.reshape(dec_batch, dec_seq, d),
            jnp.stack(new_ckv, axis=1), jnp.stack(new_krope, axis=1), jnp.stack(new_gla, axis=1))
```

```python
import functools
import math

import numpy as np
import jax
import jax.numpy as jnp
from jax import lax
from jax.experimental import pallas as pl
from jax.experimental.pallas import tpu as pltpu

F32 = jnp.float32
BF16 = jnp.bfloat16

EPS = 1e-6
N_MOD = 9
GRID_W = 64
ROPE_BASE = 10000.0
GLA_HEADS = 4
GLA_DK = 64
GLA_DV = 128
GLA_QK = GLA_HEADS * GLA_DK
GLA_VW = GLA_HEADS * GLA_DV
GLA_GATE_RANK = 16
GLA_GATE_NORM = 16.0
GLA_CHUNK = 64
GLA_CHUNKS_PER_STEP = 16
MLA_HEADS = 8
MLA_NOPE = 64
MLA_ROPE = 32
MLA_V = 64
MLA_QK_DIM = MLA_NOPE + MLA_ROPE
MLA_Q_RANK = 384
MLA_KV_RANK = 256
HEAD_PAD = 128
MLA_QUERY_TILE = 512
MLA_SHARED_ROWS = 1024
CMLP_CHUNK = 128
CMLP_GROUPS = 4
COND_PAD = 16
SUBLANES = 8
ADALN_COL_TILES = 4

VMEM_LIMIT = 56 * 1024 * 1024
ROW_TILE = 512


def _cparams(*sem):
    return pltpu.CompilerParams(dimension_semantics=sem, vmem_limit_bytes=VMEM_LIMIT)


def _resident(shape, index_map):
    return pl.BlockSpec(shape, index_map, pipeline_mode=pl.Buffered(1))


def _const_map(nd):
    return lambda *_: (0,) * nd


def _layer_spec(stack, layer):
    nd = stack.ndim - 1
    return _resident((None,) + stack.shape[1:], lambda *_: (layer,) + (0,) * nd)


def _silu(x):
    return x * jax.nn.sigmoid(x)


def _rms(x, g):
    return x * lax.rsqrt(jnp.mean(x * x, axis=-1, keepdims=True) + EPS) * g


def _modulate(x, g, shift, scale):
    return _rms(x, g) * (1.0 + scale) + shift


def _dot(a, b):
    return jnp.dot(a, b, preferred_element_type=F32)


def _dot_nt(a, b):
    return lax.dot_general(a, b, (((1,), (1,)), ((), ())), preferred_element_type=F32)


def _mod_kernel(c_ref, w_ref, b_ref, o_ref):
    s = _silu(c_ref[...]).astype(BF16)
    o_ref[...] = _dot(s, w_ref[...].astype(BF16)) + b_ref[...]


def _modulation_all(cond, ada_w, ada_b):
    depth, d, n = ada_w.shape
    tn = n // ADALN_COL_TILES
    return pl.pallas_call(
        _mod_kernel,
        out_shape=jax.ShapeDtypeStruct((depth, COND_PAD, n), F32),
        grid=(depth, n // tn),
        in_specs=[
            pl.BlockSpec((COND_PAD, d), lambda l, j: (0, 0)),
            pl.BlockSpec((None, d, tn), lambda l, j: (l, 0, j)),
            pl.BlockSpec((None, 1, tn), lambda l, j: (l, 0, j)),
        ],
        out_specs=pl.BlockSpec((None, COND_PAD, tn), lambda l, j: (l, 0, j)),
        compiler_params=_cparams("arbitrary", "arbitrary"),
        name="adaln_modulation",
    )(cond, ada_w, ada_b.reshape(depth, 1, n))


class _Rows:
    def __init__(self, m_ctx, dec_batch, dec_seq, tm):
        assert m_ctx % tm == 0 and dec_seq % tm == 0
        self.tm = tm
        self.m_ctx = m_ctx
        self.m = m_ctx + dec_batch * dec_seq
        self.nb_ctx = m_ctx // tm
        self.nb_seq = dec_seq // tm
        self.nblocks = self.m // tm

    def cond(self, i):
        return jnp.where(i < self.nb_ctx, 0, 1 + (i - self.nb_ctx) // self.nb_seq)

    def rope_block(self, i):
        return jnp.where(i < self.nb_ctx, 0, 1 + (i - self.nb_ctx) % self.nb_seq)

    def mod_spec(self, layer, d):
        return pl.BlockSpec((None, None, N_MOD, d), lambda i: (layer, self.cond(i), 0, 0))

    def row_spec(self, width):
        return pl.BlockSpec((self.tm, width), lambda i: (i, 0))

    def ctx_spec(self, width):
        return pl.BlockSpec((self.tm, width), lambda i: (jnp.minimum(i, self.nb_ctx - 1), 0))

    def dec_spec(self, width):
        return pl.BlockSpec((self.tm, width), lambda i: (jnp.maximum(i - self.nb_ctx, 0), 0))

    def split_specs(self, width):
        return [self.ctx_spec(width), self.dec_spec(width)]

    def split_shapes(self, width, dtype):
        return [jax.ShapeDtypeStruct((self.m_ctx, width), dtype),
                jax.ShapeDtypeStruct((self.m - self.m_ctx, width), dtype)]


def _load_split(refs, is_ctx):
    if len(refs) == 1:
        return refs[0][...]
    return jnp.where(is_ctx, refs[0][...], refs[1][...])


STAGE_CHUNKS = 16
STAGE_SLOTS = 8


def _stage_weights_bf16(layer, hbm_refs, vmem_refs, stage_refs, sem):
    chunks = []
    for src, dst, stage in zip(hbm_refs, vmem_refs, stage_refs):
        rows = stage.shape[1]
        assert dst.shape[0] % rows == 0
        chunks += [(src, dst, stage, r0) for r0 in range(0, dst.shape[0], rows)]
    slots = sem.shape[0]
    ahead = slots - 1

    def copy(i):
        src, _, stage, r0 = chunks[i]
        return pltpu.make_async_copy(src.at[layer, pl.ds(r0, stage.shape[1]), :], stage.at[i % slots],
                                     sem.at[i % slots])

    for i in range(min(ahead, len(chunks))):
        copy(i).start()
    for i, (_, dst, stage, r0) in enumerate(chunks):
        if i + ahead < len(chunks):
            copy(i + ahead).start()
        copy(i).wait()
        dst[pl.ds(r0, stage.shape[1]), :] = stage[i % slots].astype(BF16)


def _ffn_kernel(*refs, layer, row0, nb_ctx, split_in, split_out, premix):
    it = iter(refs)
    x_refs = [next(it) for _ in range(2 if split_in else 1)]
    mod_ref, g_ref = next(it), next(it)
    w_hbm = [next(it) for _ in range(3)]
    if premix:
        a_refs, b_refs = [next(it), next(it)], [next(it), next(it)]
        wo_ref = next(it)
    o_refs = [next(it) for _ in range(2 if split_out else 1)]
    wg_ref, wu_ref, wd_ref = w_vmem = [next(it) for _ in range(3)]
    stage_in, stage_down, sem = next(it), next(it), next(it)

    @pl.when(pl.program_id(0) == 0)
    def _():
        _stage_weights_bf16(layer, w_hbm, w_vmem, [stage_in, stage_in, stage_down], sem)

    is_ctx = pl.program_id(0) < nb_ctx
    shift = mod_ref[row0:row0 + 1, :]
    scale = mod_ref[row0 + 1:row0 + 2, :]
    gate = mod_ref[row0 + 2:row0 + 3, :]
    x = _load_split(x_refs, is_ctx)
    if premix:
        half = wo_ref.shape[0] // 2
        mix = _dot(_load_split(a_refs, is_ctx), wo_ref[:half, :])
        mix = mix + _dot(_load_split(b_refs, is_ctx), wo_ref[half:, :])
        x = x + mod_ref[5:6, :] * mix
    h = _modulate(x, g_ref[...], shift, scale).astype(BF16)
    a = _dot(h, wg_ref[...])
    u = _dot(h, wu_ref[...])
    act = (_silu(a) * u).astype(BF16)
    y = _dot(act, wd_ref[...])
    out = x + (0.5 * gate) * y
    if split_out:
        @pl.when(is_ctx)
        def _():
            o_refs[0][...] = out

        @pl.when(jnp.logical_not(is_ctx))
        def _():
            o_refs[1][...] = out
    else:
        o_refs[0][...] = out


def _ffn(xs, mod, layer, row0, g, wg, wu, wd, rows, split_out=False, premix=None):
    split_in = isinstance(xs, (tuple, list))
    xs = list(xs) if split_in else [xs]
    d = xs[0].shape[1]
    f = wg.shape[2]
    hbm = pl.BlockSpec(memory_space=pl.ANY)
    in_specs = (rows.split_specs(d) if split_in else [rows.row_spec(d)]) + [
        rows.mod_spec(layer, d),
        _layer_spec(*g),
        hbm, hbm, hbm,
    ]
    args = xs + [mod, g[0], wg, wu, wd]
    if premix is not None:
        gla, mla, w_out, j = premix
        in_specs += rows.split_specs(gla[0].shape[1]) + rows.split_specs(mla[0].shape[1])
        in_specs.append(_layer_spec(w_out, j))
        args += list(gla) + list(mla) + [w_out]
    return pl.pallas_call(
        functools.partial(_ffn_kernel, layer=layer, row0=row0, nb_ctx=rows.nb_ctx, split_in=split_in,
                          split_out=split_out, premix=premix is not None),
        out_shape=rows.split_shapes(d, F32) if split_out else jax.ShapeDtypeStruct((rows.m, d), F32),
        grid=(rows.nblocks,),
        in_specs=in_specs,
        out_specs=rows.split_specs(d) if split_out else rows.row_spec(d),
        scratch_shapes=[
            pltpu.VMEM((d, f), BF16), pltpu.VMEM((d, f), BF16), pltpu.VMEM((f, d), BF16),
            pltpu.VMEM((STAGE_SLOTS, d // STAGE_CHUNKS, f), F32),
            pltpu.VMEM((STAGE_SLOTS, f // STAGE_CHUNKS, d), F32),
            pltpu.SemaphoreType.DMA((STAGE_SLOTS,)),
        ],
        compiler_params=_cparams("arbitrary"),
        name="ffn_swiglu",
    )(*args)


def _gelu_tanh(x):
    c = math.sqrt(2.0 / math.pi)
    return x * (0.5 * (1.0 + jnp.tanh(c * (x + 0.044715 * (x * x * x)))))


def _odd_kernel(x_ref, mod_ref, g_ref, win_hbm, vg_ref, ws_ref, bst_ref, wout_hbm, o_ref,
                win_ref, wout_ref, stage_in, stage_out, sem, *, layer_j):
    @pl.when(pl.program_id(0) == 0)
    def _():
        _stage_weights_bf16(layer_j, [win_hbm, wout_hbm], [win_ref, wout_ref], [stage_in, stage_out], sem)

    x = x_ref[...]
    tm = x.shape[0]
    shift, scale, gate = mod_ref[3:4, :], mod_ref[4:5, :], mod_ref[5:6, :]
    h = _modulate(x, g_ref[...], shift, scale).astype(BF16)
    uv = _gelu_tanh(_dot(h, win_ref[...]))
    width = uv.shape[1] // 2
    u = uv[:, :width]
    v = _rms(uv[:, width:], vg_ref[...]).astype(BF16)
    gw = width // CMLP_GROUPS
    chunks = []
    for c in range(tm // CMLP_CHUNK):
        r0 = c * CMLP_CHUNK
        groups = []
        for g in range(CMLP_GROUPS):
            vg = v[r0:r0 + CMLP_CHUNK, g * gw:(g + 1) * gw]
            mixed = _dot(ws_ref[g], vg) + bst_ref[:, g:g + 1]
            groups.append(u[r0:r0 + CMLP_CHUNK, g * gw:(g + 1) * gw] * mixed)
        chunks.append(jnp.concatenate(groups, axis=1))
    z = jnp.concatenate(chunks, axis=0).astype(BF16)
    o_ref[...] = x + gate * _dot(z, wout_ref[...])


def _odd_mixer(x, mod, layer, layer_j, g, w_in, v_g, w_s, b_s_t, w_out, rows):
    m, d = x.shape
    n_in, width = w_in.shape[2], w_out.shape[1]
    hbm = pl.BlockSpec(memory_space=pl.ANY)
    return pl.pallas_call(
        functools.partial(_odd_kernel, layer_j=layer_j),
        out_shape=jax.ShapeDtypeStruct((m, d), F32),
        grid=(rows.nblocks,),
        in_specs=[
            rows.row_spec(d),
            rows.mod_spec(layer, d),
            _layer_spec(*g),
            hbm, _layer_spec(v_g, layer_j), _layer_spec(w_s, layer_j), _layer_spec(b_s_t, layer_j), hbm,
        ],
        out_specs=rows.row_spec(d),
        scratch_shapes=[
            pltpu.VMEM((d, n_in), BF16), pltpu.VMEM((width, d), BF16),
            pltpu.VMEM((STAGE_SLOTS, d // STAGE_CHUNKS, n_in), F32),
            pltpu.VMEM((STAGE_SLOTS, width // STAGE_CHUNKS, d), F32),
            pltpu.SemaphoreType.DMA((STAGE_SLOTS,)),
        ],
        compiler_params=_cparams("arbitrary"),
        name="odd_gmlp",
    )(x, mod, g[0], w_in, v_g, w_s, b_s_t, w_out)


_C_CQ = 0
_C_CKV = _C_CQ + MLA_Q_RANK
_C_KA = _C_CKV + MLA_KV_RANK
_C_KB = _C_KA + HEAD_PAD
_C_Q = _C_KB + HEAD_PAD
_C_K = _C_Q + GLA_QK
_C_V = _C_K + GLA_QK
_C_OG = _C_V + GLA_VW
_C_END = _C_OG + GLA_VW


def _head_scale(x, live):
    ss = jnp.sum(jnp.where(live, x * x, 0.0), axis=-1, keepdims=True)
    return lax.rsqrt(ss * (1.0 / MLA_QK_DIM) + EPS)


def _even_in_kernel(x_ref, mod_ref, g_ref, win_ref, w2_ref, gb_ref, qag_ref, qb_ref, kvag_ref,
                    kvbk_ref, kvbv_ref, kng_ref, tq_ref, ta_ref, tb_ref,
                    gq_ref, gk_ref, gv_ref, og_ref, la_ref, qm_ref, km_ref, vm_ref, ckv_ref, kr_ref,
                    *, nb_ctx):
    is_ctx = pl.program_id(0) < nb_ctx
    shift, scale = mod_ref[3:4, :], mod_ref[4:5, :]
    h = _modulate(x_ref[...], g_ref[...], shift, scale).astype(BF16)

    proj_q = _dot(h, win_ref[:, _C_CQ:_C_CKV])
    proj_kv = _dot(h, win_ref[:, _C_CKV:_C_Q])
    ckv = _rms(proj_kv[:, :MLA_KV_RANK], kvag_ref[...])
    ka = proj_kv[:, _C_KA - _C_CKV:_C_KB - _C_CKV]
    kb = proj_kv[:, _C_KB - _C_CKV:]

    @pl.when(is_ctx)
    def _():
        ckv_ref[...] = ckv
        kr_ref[...] = ka[:, MLA_NOPE:MLA_QK_DIM]

    cq = _rms(proj_q, qag_ref[...]).astype(BF16)
    ckv_bf = ckv.astype(BF16)
    qn = _dot(cq, qb_ref[...])
    kn = _dot(ckv_bf, kvbk_ref[...])
    vm_t = _dot(ckv_bf, kvbv_ref[...])
    z = _dot(ka.astype(BF16), w2_ref[...]) + gb_ref[...]

    lane = lax.broadcasted_iota(jnp.int32, ka.shape, 1)
    live = lane < MLA_QK_DIM
    tq = tq_ref[...]

    def q_heads(heads):
        for hd in heads:
            sl = slice(hd * HEAD_PAD, (hd + 1) * HEAD_PAD)
            qh = qn[:, sl]
            qm_ref[:, sl] = (qh * tq * _head_scale(qh, live)).astype(BF16)

    is_rope = live & (lane >= MLA_NOPE)
    ss_rope = jnp.sum(jnp.where(is_rope, ka * ka, 0.0), axis=-1, keepdims=True)
    rot = ka * ta_ref[...] + kb * tb_ref[...]
    kng = kng_ref[...]

    def k_heads(heads):
        for hd in heads:
            sl = slice(hd * HEAD_PAD, (hd + 1) * HEAD_PAD)
            kh = kn[:, sl]
            ss = jnp.sum(kh * kh, axis=-1, keepdims=True) + ss_rope
            r = lax.rsqrt(ss * (1.0 / MLA_QK_DIM) + EPS)
            km_ref[:, sl] = ((kh * kng + rot) * r).astype(BF16)

    half = MLA_HEADS // 2
    gqk = _dot(h, win_ref[:, _C_Q:_C_V])
    q_heads(range(half))
    gq_ref[...] = gqk[:, :GLA_QK] * (GLA_DK ** -0.5)
    gk_ref[...] = gqk[:, GLA_QK:]
    gv = _dot(h, win_ref[:, _C_V:_C_OG])
    q_heads(range(half, MLA_HEADS))
    gv_ref[...] = gv.astype(BF16)
    og = _dot(h, win_ref[:, _C_OG:])
    k_heads(range(half))
    og_ref[...] = og
    vm_ref[...] = vm_t.T.astype(BF16)
    k_heads(range(half, MLA_HEADS))
    la_ref[...] = ((jnp.minimum(z, 0.0) - jnp.log(1.0 + jnp.exp(-jnp.abs(z))))
                   * (math.log2(math.e) / GLA_GATE_NORM))


def _even_in(x, mod, layer, g, wts, layer_j, rows):
    m, d = x.shape
    tm = rows.tm
    rope_spec = pl.BlockSpec((None, tm, HEAD_PAD), lambda i: (layer_j, rows.rope_block(i), 0))
    widths = [GLA_QK, GLA_QK, GLA_VW, GLA_VW, 2 * GLA_QK, MLA_HEADS * HEAD_PAD, MLA_HEADS * HEAD_PAD]
    dtypes = [F32, F32, BF16, F32, F32, BF16, BF16]
    names = ["w_in", "w2", "gate_b", "qa_g", "qb", "kva_g", "kvb_k", "kvb_v", "kn_g"]
    return pl.pallas_call(
        functools.partial(_even_in_kernel, nb_ctx=rows.nb_ctx),
        out_shape=[jax.ShapeDtypeStruct((m, w), t) for w, t in zip(widths, dtypes)]
        + [jax.ShapeDtypeStruct((MLA_HEADS * MLA_V, m), BF16),
           jax.ShapeDtypeStruct((rows.m_ctx, MLA_KV_RANK), F32),
           jax.ShapeDtypeStruct((rows.m_ctx, MLA_ROPE), F32)],
        grid=(rows.nblocks,),
        in_specs=[rows.row_spec(d), rows.mod_spec(layer, d), _layer_spec(*g)]
        + [_layer_spec(wts[n], layer_j) for n in names] + [rope_spec] * 3,
        out_specs=[rows.row_spec(w) for w in widths]
        + [pl.BlockSpec((MLA_HEADS * MLA_V, tm), lambda i: (0, i)),
           rows.ctx_spec(MLA_KV_RANK), rows.ctx_spec(MLA_ROPE)],
        compiler_params=_cparams("arbitrary"),
        name="even_in_proj",
    )(x, mod, g[0], *[wts[n] for n in names], wts["t_q"], wts["t_a"], wts["t_b"])


def _ctx_kv_kernel(ckv_ref, kr_ref, kvbk_ref, kvbv_ref, kng_ref, k_ref, v_ref):
    nb, p, r = ckv_ref.shape
    ckv_bf = ckv_ref[...].reshape(nb * p, r).astype(BF16)
    v_all = _dot(ckv_bf, kvbv_ref[...])
    for b in range(nb):
        v_ref[b] = v_all[b * p:(b + 1) * p, :].T.astype(BF16)
    kn = _dot(ckv_bf, kvbk_ref[...])
    kr = kr_ref[...].reshape(nb * p, HEAD_PAD)
    lane = lax.broadcasted_iota(jnp.int32, kr.shape, 1)
    live = lane < MLA_QK_DIM
    kng = kng_ref[...]
    for hd in range(MLA_HEADS):
        sl = slice(hd * HEAD_PAD, (hd + 1) * HEAD_PAD)
        kh = kn[:, sl] + kr
        k_ref[:, :, sl] = (kh * kng * _head_scale(kh, live)).astype(BF16).reshape(nb, p, HEAD_PAD)


def _ctx_kv(cache_ckv, cache_krope_blk, kvb_k, kvb_v, kn_g):
    nb, ne, p, r = cache_ckv.shape
    return pl.pallas_call(
        _ctx_kv_kernel,
        out_shape=[jax.ShapeDtypeStruct((ne, nb, p, MLA_HEADS * HEAD_PAD), BF16),
                   jax.ShapeDtypeStruct((ne, nb, MLA_HEADS * MLA_V, p), BF16)],
        grid=(ne,),
        in_specs=[
            pl.BlockSpec((nb, None, p, r), lambda j: (0, j, 0, 0)),
            pl.BlockSpec((nb, None, p, HEAD_PAD), lambda j: (0, j, 0, 0)),
            pl.BlockSpec((None,) + kvb_k.shape[1:], lambda j: (j, 0, 0)),
            pl.BlockSpec((None,) + kvb_v.shape[1:], lambda j: (j, 0, 0)),
            pl.BlockSpec((None,) + kn_g.shape[1:], lambda j: (j, 0, 0)),
        ],
        out_specs=[pl.BlockSpec((None, nb, p, MLA_HEADS * HEAD_PAD), lambda j: (j, 0, 0, 0)),
                   pl.BlockSpec((None, nb, MLA_HEADS * MLA_V, p), lambda j: (j, 0, 0, 0))],
        compiler_params=_cparams("arbitrary"),
        name="ctx_kv",
    )(cache_ckv, cache_krope_blk, kvb_k, kvb_v, kn_g)


def _split3(x):
    hi = x.astype(BF16)
    r = x - hi.astype(F32)
    mid = r.astype(BF16)
    lo = (r - mid.astype(F32)).astype(BF16)
    return hi, mid, lo


def _block_row_bcast(a, blk, off):
    c, n = a.shape
    if blk >= 8:
        pieces = [jnp.broadcast_to(a[b * blk + off:b * blk + off + 1, :], (blk, n))
                  for b in range(c // blk)]
        return pieces[0] if len(pieces) == 1 else jnp.concatenate(pieces, axis=0)
    a3 = a.reshape(c // 8, 8, n)
    sub = lax.broadcasted_iota(jnp.int32, a3.shape, 1) // blk
    out = jnp.broadcast_to(a3[:, off:off + 1, :], a3.shape)
    for s in range(1, 8 // blk):
        cand = jnp.broadcast_to(a3[:, s * blk + off:s * blk + off + 1, :], a3.shape)
        out = jnp.where(sub == s, cand, out)
    return out.reshape(c, n)


def _head_stack(a, head_w):
    lane_head = lax.broadcasted_iota(jnp.int32, a.shape, 1) // head_w
    zero = jnp.zeros_like(a)
    return jnp.concatenate([jnp.where(lane_head == h, a, zero) for h in range(GLA_HEADS)], axis=0)


def _seqs_per_step(max_rows, nbatch, t, row_off):
    nseq = max(1, min(nbatch, max_rows // t))
    while nbatch % nseq or row_off % (nseq * t):
        nseq -= 1
    return nseq


def _run_skewed(stage_gens, skew=1):
    done = [False] * len(stage_gens)
    tick = 0
    while not all(done):
        for u, gen in enumerate(stage_gens):
            if tick >= u * skew and not done[u]:
                try:
                    next(gen)
                except StopIteration:
                    done[u] = True
        tick += 1


def _gla_phase_a(c, q_ref, k_ref, v_ref, la_ref, tri_ref, pair_ref, oacc_ref, qst_ref, ds_ref, dc_ref):
    C = GLA_CHUNK
    r0 = pl.multiple_of(c * C, C)
    q = q_ref[pl.ds(r0, C), :]
    k = k_ref[pl.ds(r0, C), :]
    la = la_ref[pl.ds(r0, C), :]
    la_f, la_b = la[:, :GLA_QK], la[:, GLA_QK:]
    cums = []
    for d, la_d in enumerate((la_f, la_b)):
        hi, mid, lo = _split3(la_d)
        tri = tri_ref[d]
        cums.append(_dot(tri, hi) + _dot(tri, mid) + _dot(tri, lo))
    cum_f, cum_b = cums
    k_stack = _head_stack(k.astype(BF16), GLA_DK)
    yield

    row = lax.broadcasted_iota(jnp.int32, (C, 1), 0)
    pair_m = pair_ref[...]
    att = jnp.where(pair_m == 0, 2.0 * _dot_nt(q.astype(BF16), k_stack), 0.0)
    m = C // 2
    while m >= 1:
        blk = 2 * m
        upper = (row % blk) >= m
        if m == 1:
            qm = (q * jnp.exp2(jnp.where(upper, la_f, la_b))).astype(BF16)
            km_stack = k_stack
        else:
            ref_f = _block_row_bcast(cum_f, blk, m - 1)
            ref_b = _block_row_bcast(cum_b, blk, m)
            eq = jnp.where(upper, cum_f - ref_f, cum_b - ref_b)
            ek = jnp.where(upper, ref_b - cum_b, ref_f - cum_f)
            qm = (q * jnp.exp2(eq)).astype(BF16)
            km_stack = _head_stack((k * jnp.exp2(ek)).astype(BF16), GLA_DK)
        yield
        att = jnp.where(pair_m == m, _dot_nt(qm, km_stack), att)
        m //= 2

    v = v_ref[pl.ds(r0, C), :]
    v_stack = _head_stack(v, GLA_DV)
    qcs = [(q * jnp.exp2(cum)).astype(BF16) for cum in cums]
    yield
    oacc_ref[pl.ds(r0, C), :] = _dot(att.astype(BF16), v_stack)
    for d, cum in enumerate(cums):
        last = cum[C - 1:C, :] if d == 0 else cum[0:1, :]
        qst_ref[d, c] = _head_stack(qcs[d], GLA_DK)
        kct = (k * jnp.exp2(last - cum)).T.astype(BF16)
        decay_col = jnp.exp2(jnp.broadcast_to(last, (SUBLANES, GLA_QK))).T[:, 0:1]
        dc_ref[d, c] = jnp.broadcast_to(decay_col, (GLA_QK, GLA_DV))
        yield
        ds_ref[d, c] = jnp.concatenate(
            [_dot(kct[h * GLA_DK:(h + 1) * GLA_DK, :], v[:, h * GLA_DV:(h + 1) * GLA_DV])
             for h in range(GLA_HEADS)], axis=0)


def _gla_phase_c(c, og_ref, gg, oacc_ref, qst_ref, sb_ref, o_ref):
    C = GLA_CHUNK
    r0 = pl.multiple_of(c * C, C)
    q_cat = jnp.concatenate([qst_ref[0, c], qst_ref[1, c]], axis=1)
    s_cat = jnp.concatenate([sb_ref[0, c], sb_ref[1, c]], axis=0)
    inter = _dot(q_cat, s_cat)
    yield
    for h in range(GLA_HEADS):
        sl = slice(h * GLA_DV, (h + 1) * GLA_DV)
        o = oacc_ref[pl.ds(r0, C), sl] + inter[h * C:(h + 1) * C, :]
        o_ref[pl.ds(r0, C), sl] = (_rms(o, gg) * _silu(og_ref[pl.ds(r0, C), sl])).astype(BF16)


def _gla_kernel(*refs, has_state, nseq):
    it = iter(refs)
    q_ref, k_ref, v_ref, la_ref, og_ref = (next(it) for _ in range(5))
    s0_ref = next(it) if has_state else None
    tri_ref, pair_ref, gg_ref, o_ref, sout_ref = (next(it) for _ in range(5))
    oacc_ref, qst_ref, ds_ref, dc_ref, sb_ref, st_ref = (next(it) for _ in range(6))

    n_chunks = q_ref.shape[0] // GLA_CHUNK
    nc = n_chunks // nseq
    per_step = min(GLA_CHUNKS_PER_STEP, n_chunks)

    def phase_a(i, carry):
        _run_skewed([_gla_phase_a(per_step * i + u, q_ref, k_ref, v_ref, la_ref, tri_ref, pair_ref, oacc_ref,
                                  qst_ref, ds_ref, dc_ref) for u in range(per_step)])
        return carry

    lax.fori_loop(0, n_chunks // per_step, phase_a, 0)

    for s in range(nseq):
        for d in range(2):
            for h in range(GLA_HEADS):
                rows = slice(h * GLA_DK, (h + 1) * GLA_DK)
                st_ref[d, rows, :] = s0_ref[s, d, h] if has_state else jnp.zeros((GLA_DK, GLA_DV), F32)

        def phase_b(c, carry, c0=s * nc):
            for d in range(2):
                cc = c0 + (c if d == 0 else nc - 1 - c)
                state = st_ref[d]
                sb_ref[d, cc] = state.astype(BF16)
                st_ref[d] = state * dc_ref[d, cc] + ds_ref[d, cc]
            return carry

        lax.fori_loop(0, nc, phase_b, 0)
        for d in range(2):
            for h in range(GLA_HEADS):
                sout_ref[s, d, h] = st_ref[d, h * GLA_DK:(h + 1) * GLA_DK, :]

    gg = gg_ref[...]

    def phase_c(i, carry):
        _run_skewed([_gla_phase_c(per_step * i + u, og_ref, gg, oacc_ref, qst_ref, sb_ref, o_ref)
                     for u in range(per_step)])
        return carry

    lax.fori_loop(0, n_chunks // per_step, phase_c, 0)


def _gla(gq, gk, gv, la, og, state, layer_j, tri, gg, nbatch, t, row_off):
    nseq = _seqs_per_step(GLA_CHUNKS_PER_STEP * GLA_CHUNK, nbatch, t, row_off)
    rows = nseq * t
    nc = rows // GLA_CHUNK
    assert nbatch % nseq == 0 and row_off % rows == 0 and t % GLA_CHUNK == 0
    assert nc % min(GLA_CHUNKS_PER_STEP, nc) == 0
    b0 = row_off // rows
    seq = lambda w: pl.BlockSpec((rows, w), lambda b: (b0 + b, 0))
    in_specs = [seq(GLA_QK), seq(GLA_QK), seq(GLA_VW), seq(2 * GLA_QK), seq(GLA_VW)]
    args = [gq, gk, gv, la, og]
    if state is not None:
        in_specs.append(pl.BlockSpec((nseq, None, 2, GLA_HEADS, GLA_DK, GLA_DV),
                                     lambda b: (b, layer_j, 0, 0, 0, 0)))
        args.append(state)
    tri, pair_m = tri
    in_specs += [_resident(tri.shape, _const_map(3)), _resident(pair_m.shape, _const_map(2)),
                 _layer_spec(gg, layer_j)]
    args += [tri, pair_m, gg]
    return pl.pallas_call(
        functools.partial(_gla_kernel, has_state=state is not None, nseq=nseq),
        out_shape=[jax.ShapeDtypeStruct((nbatch * t, GLA_VW), BF16),
                   jax.ShapeDtypeStruct((nbatch, 2, GLA_HEADS, GLA_DK, GLA_DV), F32)],
        grid=(nbatch // nseq,),
        in_specs=in_specs,
        out_specs=[pl.BlockSpec((rows, GLA_VW), lambda b: (b, 0)),
                   pl.BlockSpec((nseq, 2, GLA_HEADS, GLA_DK, GLA_DV), lambda b: (b, 0, 0, 0, 0))],
        scratch_shapes=[
            pltpu.VMEM((rows, GLA_VW), F32),
            pltpu.VMEM((2, nc, GLA_HEADS * GLA_CHUNK, GLA_QK), BF16),
            pltpu.VMEM((2, nc, GLA_QK, GLA_DV), F32),
            pltpu.VMEM((2, nc, GLA_QK, GLA_DV), F32),
            pltpu.VMEM((2, nc, GLA_QK, GLA_DV), BF16),
            pltpu.VMEM((2, GLA_QK, GLA_DV), F32),
        ],
        compiler_params=_cparams("arbitrary"),
        name="gla_scan",
    )(*args)


def _mla_kernel(*refs, has_ctx, nseq):
    it = iter(refs)
    q_ref, k_ref, v_ref = next(it), next(it), next(it)
    kc_ref, vc_ref = (next(it), next(it)) if has_ctx else (None, None)
    o_ref = next(it)
    tq = q_ref.shape[0] // nseq
    tk = k_ref.shape[0] // nseq
    def scores(s, hd):
        sl = slice(hd * HEAD_PAD, (hd + 1) * HEAD_PAD)
        q = q_ref[s * tq:(s + 1) * tq, sl]
        own = _dot_nt(k_ref[s * tk:(s + 1) * tk, sl], q)
        return [own, _dot_nt(kc_ref[:, sl], q)] if has_ctx else [own]

    def softmax(ss):
        mx = functools.reduce(jnp.maximum, [jnp.max(x, axis=0, keepdims=True) for x in ss])
        ps = [jnp.exp2(x - mx) for x in ss]
        den = sum(jnp.sum(p, axis=0, keepdims=True) for p in ps)
        return [p.astype(BF16) for p in ps], den

    def values(s, hd, ps, den):
        vrows = slice(hd * MLA_V, (hd + 1) * MLA_V)
        o = _dot(v_ref[vrows, s * tk:(s + 1) * tk], ps[0])
        if has_ctx:
            o = o + _dot(vc_ref[vrows, :], ps[1])
        return o / den

    items = [(s, hd) for s in range(nseq) for hd in range(MLA_HEADS)]
    s_q, p_q, outs = {}, {}, []
    for step in range(len(items) + 2):
        if step < len(items):
            s_q[step] = scores(*items[step])
        if 0 <= step - 1 < len(items):
            p_q[step - 1] = softmax(s_q.pop(step - 1))
        if 0 <= step - 2 < len(items):
            outs.append(values(*items[step - 2], *p_q.pop(step - 2)))
    for s in range(nseq):
        heads = outs[s * MLA_HEADS:(s + 1) * MLA_HEADS]
        o_ref[s * tq:(s + 1) * tq, :] = jnp.concatenate(heads, axis=0).T.astype(BF16)


def _mla(qm, km, vm, kc, vc, layer_j, nbatch, t, row_off, tq):
    nseq = _seqs_per_step(tq, nbatch, t, row_off) if kc is None else 1
    tq = min(tq, t)
    nq = t // tq
    qrows, krows = nseq * tq, nseq * t
    assert t % tq == 0 and nbatch % nseq == 0 and row_off % krows == 0 and (nseq == 1 or nq == 1)
    q0, b0 = row_off // qrows, row_off // krows
    qspec = lambda w: pl.BlockSpec((qrows, w), lambda b, i: (q0 + b * nq + i, 0))
    kspec = lambda w: pl.BlockSpec((krows, w), lambda b, i: (b0 + b, 0))
    vspec = pl.BlockSpec((MLA_HEADS * MLA_V, krows), lambda b, i: (0, b0 + b))
    in_specs = [qspec(MLA_HEADS * HEAD_PAD), kspec(MLA_HEADS * HEAD_PAD), vspec]
    args = [qm, km, vm]
    if kc is not None:
        in_specs += [pl.BlockSpec((None, None) + kc.shape[2:], lambda b, i: (layer_j, b, 0, 0)),
                     pl.BlockSpec((None, None) + vc.shape[2:], lambda b, i: (layer_j, b, 0, 0))]
        args += [kc, vc]
    return pl.pallas_call(
        functools.partial(_mla_kernel, has_ctx=kc is not None, nseq=nseq),
        out_shape=jax.ShapeDtypeStruct((nbatch * t, MLA_HEADS * MLA_V), BF16),
        grid=(nbatch // nseq, nq),
        in_specs=in_specs,
        out_specs=pl.BlockSpec((qrows, MLA_HEADS * MLA_V), lambda b, i: (b * nq + i, 0)),
        compiler_params=_cparams("arbitrary", "arbitrary"),
        name="mla_attention",
    )(*args)


def _swap_signed(a):
    pairs = a.reshape(a.shape[:-1] + (a.shape[-1] // 2, 2))
    return jnp.stack([-pairs[..., 1], pairs[..., 0]], axis=-1).reshape(a.shape)


def _swap_pairs(a):
    pairs = a.reshape(a.shape[:-1] + (a.shape[-1] // 2, 2))
    return pairs[..., ::-1].reshape(a.shape)


def _pack_even_weights(cos, sin, even_w_in, gla_gate_w2, gla_gate_b, mla_qa_g, mla_qb_w, mla_qn_g,
                       mla_kva_g, mla_kvb_w, mla_kn_g):
    ne, d, _ = even_w_in.shape
    o = 0
    parts = {}
    for name, width in (("q", GLA_QK), ("k", GLA_QK), ("v", GLA_VW), ("og", GLA_VW),
                        ("gl", 2 * GLA_GATE_RANK), ("cq", MLA_Q_RANK), ("ckv", MLA_KV_RANK),
                        ("kr", MLA_ROPE)):
        parts[name] = even_w_in[:, :, o:o + width]
        o += width
    zeros = lambda n: jnp.zeros((ne, d, n), even_w_in.dtype)
    kr, kr_sw = parts["kr"], _swap_signed(parts["kr"])
    blk_a = jnp.concatenate([parts["gl"], zeros(MLA_NOPE - 2 * GLA_GATE_RANK), kr, kr_sw], axis=2)
    blk_b = jnp.concatenate([zeros(MLA_NOPE), kr_sw, kr], axis=2)
    w_in = jnp.concatenate([parts["cq"], parts["ckv"], blk_a, blk_b, parts["q"], parts["k"], parts["v"],
                            parts["og"]], axis=2).astype(BF16)
    assert w_in.shape[2] == _C_END

    w2_f = jnp.pad(gla_gate_w2[:, 0], ((0, 0), (0, 0), (0, GLA_QK)))
    w2_b = jnp.pad(gla_gate_w2[:, 1], ((0, 0), (0, 0), (GLA_QK, 0)))
    w2 = jnp.pad(jnp.concatenate([w2_f, w2_b], axis=1), ((0, 0), (0, HEAD_PAD - 2 * GLA_GATE_RANK), (0, 0)))

    pad_head = lambda a: jnp.pad(a, [(0, 0)] * (a.ndim - 1) + [(0, HEAD_PAD - a.shape[-1])])
    qb = jnp.concatenate([mla_qb_w, _swap_signed(mla_qb_w[..., MLA_NOPE:])], axis=-1)
    qb = qb.reshape(ne, MLA_Q_RANK, MLA_HEADS * HEAD_PAD)
    kvb_k = pad_head(mla_kvb_w[..., :MLA_NOPE]).reshape(ne, MLA_KV_RANK, MLA_HEADS * HEAD_PAD)
    kvb_v = mla_kvb_w[..., MLA_NOPE:].reshape(ne, MLA_KV_RANK, MLA_HEADS * MLA_V)

    def rot_tables(g):
        n = cos.shape[0]
        g_n = jnp.broadcast_to(g[:, None, :MLA_NOPE], (ne, n, MLA_NOPE))
        g_c = g[:, None, MLA_NOPE:] * cos
        g_s = _swap_pairs(g[:, MLA_NOPE:])[:, None, :] * sin
        return g_n, g_c, g_s

    qn_n, qn_c, qn_s = rot_tables(mla_qn_g)
    kn_n, kn_c, kn_s = rot_tables(mla_kn_g)
    kzero = jnp.zeros_like(kn_n)
    return {
        "w_in": w_in,
        "w2": w2.astype(BF16),
        "gate_b": gla_gate_b.reshape(ne, 1, 2 * GLA_QK),
        "qa_g": mla_qa_g[:, None, :],
        "qb": qb.astype(BF16),
        "kva_g": mla_kva_g[:, None, :],
        "kvb_k": kvb_k.astype(BF16),
        "kvb_v": kvb_v.astype(BF16),
        "kn_g": pad_head(mla_kn_g[:, :MLA_NOPE])[:, None, :],
        "kn_g_ctx": jnp.concatenate([mla_kn_g, mla_kn_g[:, MLA_NOPE:]], axis=1)[:, None, :],
        "t_q": jnp.concatenate([qn_n, qn_c, qn_s], axis=2) * (MLA_QK_DIM ** -0.5 * math.log2(math.e)),
        "t_a": jnp.concatenate([kzero, kn_c, kn_s], axis=2),
        "t_b": jnp.concatenate([kzero, kn_s, kn_c], axis=2),
    }


def _rope_tables(n, tm):
    pairs = MLA_ROPE // 4
    pos = np.arange(n)
    inv = ROPE_BASE ** (-jnp.arange(pairs, dtype=F32) / pairs)
    row = jnp.asarray(pos // GRID_W, F32)
    col = jnp.asarray(pos % GRID_W, F32)
    ang = jnp.concatenate([row[:, None] * inv, col[:, None] * inv], axis=-1)
    cos = jnp.repeat(jnp.cos(ang), 2, axis=-1)
    sin = jnp.repeat(jnp.sin(ang), 2, axis=-1)
    cos = jnp.pad(cos, ((tm, 0), (0, 0)), constant_values=1.0)
    sin = jnp.pad(sin, ((tm, 0), (0, 0)))
    return cos, sin


def _tri_masks():
    r = np.arange(GLA_CHUNK)
    lower = (r[None, :] <= r[:, None]).astype(np.float32)
    diff = r[:, None] ^ r[None, :]
    level = np.where(diff > 0, 1 << np.floor(np.log2(np.maximum(diff, 1))).astype(np.int64), 0)
    pair_m = np.tile(level, (1, GLA_HEADS)).astype(np.int32)
    return jnp.asarray(np.stack([lower, lower.T]), BF16), jnp.asarray(pair_m)


def _pick_tile(pref, *sizes):
    tm = pref
    while any(s % tm for s in sizes):
        tm //= 2
    return tm


def kernel(x_prompt, x_sample, c, c_ctx, cache_ckv, cache_krope, state_gla, ada_w, ada_b, norm_g,
           ffn1_wg, ffn1_wu, ffn1_wd, ffn2_wg, ffn2_wu, ffn2_wd, even_w_in, even_w_out,
           gla_gate_w2, gla_gate_b, gla_norm_g, mla_qa_g, mla_qb_w, mla_kva_g, mla_kvb_w,
           mla_qn_g, mla_kn_g, odd_w_in, odd_v_g, odd_ws, odd_bs, odd_w_out):
    batch, seq, d = x_prompt.shape
    dec_batch, dec_seq, _ = x_sample.shape
    depth = ada_w.shape[0]
    n_even = even_w_in.shape[0]
    m_ctx, m_dec = batch * seq, dec_batch * dec_seq
    assert 1 + dec_batch <= COND_PAD and seq % CMLP_CHUNK == 0 and dec_seq % CMLP_CHUNK == 0

    rows = _Rows(m_ctx, dec_batch, dec_seq, _pick_tile(ROW_TILE, m_ctx, dec_seq))

    cond = jnp.concatenate([c_ctx[None, :], c, jnp.zeros((COND_PAD - 1 - dec_batch, d), F32)], axis=0)
    mod = _modulation_all(cond, ada_w, ada_b).reshape(depth, COND_PAD, N_MOD, d)

    cos_t, sin_t = _rope_tables(dec_seq, rows.tm)
    tri = _tri_masks()
    krope_blk = jnp.concatenate([jnp.zeros(cache_krope.shape[:-1] + (MLA_NOPE,), F32), cache_krope,
                                 cache_krope], axis=-1)
    even_wts = _pack_even_weights(cos_t, sin_t, even_w_in, gla_gate_w2, gla_gate_b, mla_qa_g, mla_qb_w,
                                  mla_qn_g, mla_kva_g, mla_kvb_w, mla_kn_g)
    kc, vc = _ctx_kv(cache_ckv, krope_blk, even_wts["kvb_k"], even_wts["kvb_v"], even_wts["kn_g_ctx"])

    ffn1 = (ffn1_wg, ffn1_wu, ffn1_wd)
    ffn2 = (ffn2_wg, ffn2_wu, ffn2_wd)
    w_out_bf = even_w_out.astype(BF16)
    odd_wts = (odd_w_in, odd_v_g[:, None, :], odd_ws.astype(BF16), jnp.swapaxes(odd_bs, 1, 2), odd_w_out)
    norms = norm_g.reshape(depth * 3, 1, d)
    gg = gla_norm_g[:, None, :]

    x = (x_prompt.reshape(m_ctx, d), x_sample.reshape(m_dec, d))
    new_ckv, new_krope, new_gla = [], [], []
    for i in range(depth):
        j = i // 2
        x = _ffn(x, mod, i, 0, (norms, 3 * i), *ffn1, rows)
        g_mix = (norms, 3 * i + 1)
        premix = None
        if i % 2 == 0:
            gq, gk, gv, og, la, qm, km, vm, ckv, kr = _even_in(x, mod, i, g_mix, even_wts, j, rows)
            gla_ctx, st = _gla(gq, gk, gv, la, og, None, j, tri, gg, batch, seq, 0)
            gla_dec, _ = _gla(gq, gk, gv, la, og, state_gla, j, tri, gg, dec_batch, dec_seq, m_ctx)
            mla_ctx = _mla(qm, km, vm, None, None, j, batch, seq, 0, MLA_SHARED_ROWS)
            mla_dec = _mla(qm, km, vm, kc, vc, j, dec_batch, dec_seq, m_ctx, _pick_tile(MLA_QUERY_TILE, dec_seq))
            premix = ((gla_ctx, gla_dec), (mla_ctx, mla_dec), w_out_bf, j)
            new_ckv.append(ckv.reshape(batch, seq, MLA_KV_RANK))
            new_krope.append(kr.reshape(batch, seq, MLA_ROPE))
            new_gla.append(st)
        else:
            x = _odd_mixer(x, mod, i, j, g_mix, *odd_wts, rows)
        x = _ffn(x, mod, i, 6, (norms, 3 * i + 2), *ffn2, rows, split_out=(i == depth - 1),
                 premix=premix)

    y_prompt, y_sample = x
    return (y_prompt.reshape(batch, seq, d), y_sample.reshape(dec_batch, dec_seq, d),
            jnp.stack(new_ckv, axis=1), jnp.stack(new_krope, axis=1), jnp.stack(new_gla, axis=1))
```
